```python
import jax, jax.numpy as jnp
from jax import lax
import numpy as np

D_MODEL = 4096
BATCH = 8
SEQ = 4096
DEPTH = 1

EXPAND = 2
D_MIX = EXPAND * D_MODEL
D_ATTN = D_MIX // 2
D_RNN = D_MIX - D_ATTN
ATTN_HEAD_DIM = 64
N_Q_HEADS = D_ATTN // ATTN_HEAD_DIM
N_KV_HEADS = N_Q_HEADS // 8
GQA_GROUP = N_Q_HEADS // N_KV_HEADS
D_KV = N_KV_HEADS * ATTN_HEAD_DIM
WINDOW = 128
RNN_HEAD_DIM = 128
N_RNN_HEADS = D_RNN // RNN_HEAD_DIM
CHUNK = 64
NORM_EPS = 1e-6

COL_SIZES = [D_ATTN, D_KV, D_KV, D_ATTN, D_RNN, D_RNN, D_RNN, D_RNN]
D_IN = int(sum(COL_SIZES))
SPLIT_POINTS = [int(c) for c in np.cumsum(COL_SIZES)[:-1]]

kernel_name = "hymba_swa_sink_hgrn2_sandwich"


def rms_norm(x, gain):
    xf = x.astype(jnp.float32)
    y = xf * lax.rsqrt(jnp.mean(xf * xf, axis=-1, keepdims=True) + NORM_EPS)
    return (y * gain.astype(jnp.float32)).astype(x.dtype)


def sliding_window_attention(q, k, v, sinks):
    B, S = q.shape[0], q.shape[1]
    nb = S // WINDOW
    q = q.reshape(B, nb, WINDOW, N_KV_HEADS, GQA_GROUP, ATTN_HEAD_DIM)
    k = k.reshape(B, nb, WINDOW, N_KV_HEADS, ATTN_HEAD_DIM)
    v = v.reshape(B, nb, WINDOW, N_KV_HEADS, ATTN_HEAD_DIM)
    pad = ((0, 0), (1, 0), (0, 0), (0, 0), (0, 0))
    kk = jnp.concatenate([jnp.pad(k, pad)[:, :-1], k], axis=2)
    vv = jnp.concatenate([jnp.pad(v, pad)[:, :-1], v], axis=2)
    scale = ATTN_HEAD_DIM ** -0.5
    scores = jnp.einsum('bnqhgd,bnkhd->bnhgqk', q, kk).astype(jnp.float32) * scale
    qi = jnp.arange(WINDOW)[:, None]
    kj = jnp.arange(2 * WINDOW)[None, :]
    band = (kj > qi) & (kj <= qi + WINDOW)
    blk = jnp.arange(nb)[:, None, None]
    valid = band[None] & ((blk > 0) | (kj[None] >= WINDOW))
    scores = jnp.where(valid[None, :, None, None], scores, -jnp.inf)
    sink = sinks.astype(jnp.float32).reshape(N_KV_HEADS, GQA_GROUP)[None, None, :, :, None, None]
    m = jnp.maximum(jnp.max(scores, axis=-1, keepdims=True), sink)
    p = jnp.exp(scores - m)
    denom = jnp.sum(p, axis=-1, keepdims=True) + jnp.exp(sink - m)
    probs = (p / denom).astype(v.dtype)
    out = jnp.einsum('bnhgqk,bnkhd->bnqhgd', probs, vv)
    return out.reshape(B, S, D_ATTN)


def hgrn2_recurrence(q, k, v, g):
    B, S, H, dk = q.shape
    dv = v.shape[-1]
    nc = S // CHUNK

    def to_chunks(t):
        return t.astype(jnp.float32).reshape(B, nc, CHUNK, H, t.shape[-1]).transpose(1, 0, 3, 2, 4)

    qc, kc, vc, gc = to_chunks(q), to_chunks(k), to_chunks(v), to_chunks(g)
    causal = jnp.tril(jnp.ones((CHUNK, CHUNK), dtype=bool))

    def step(state, inp):
        qb, kb, vb, gb = inp
        G = jnp.cumsum(gb, axis=2)
        inter = jnp.einsum('bhtd,bhde->bhte', qb * jnp.exp(G), state)
        diff = G[:, :, :, None, :] - G[:, :, None, :, :]
        decay = jnp.exp(jnp.where(causal[:, :, None], diff, -jnp.inf))
        attn = jnp.einsum('bhtd,bhsd,bhtsd->bhts', qb, kb, decay)
        intra = jnp.einsum('bhts,bhse->bhte', attn, vb)
        G_last = G[:, :, -1:, :]
        new_state = (jnp.exp(G_last[:, :, 0, :])[..., None] * state
                     + jnp.einsum('bhsd,bhse->bhde', kb * jnp.exp(G_last - G), vb))
        return new_state, inter + intra

    init = jnp.zeros((B, H, dk, dv), jnp.float32)
    _, out = lax.scan(step, init, (qc, kc, vc, gc))
    return out.transpose(1, 0, 3, 2, 4).reshape(B, S, H, dv)


def _fwd_setup_inputs(seed: int = 0) -> dict:
    key = jax.random.key(seed)
    ks = jax.random.split(key, 8)
    x = jax.random.normal(ks[0], (BATCH, SEQ, D_MODEL), jnp.float32)
    w_in = jax.random.normal(ks[1], (DEPTH, D_MODEL, D_IN), jnp.float32) * D_MODEL ** -0.5
    attn_sinks = jax.random.normal(ks[2], (DEPTH, N_Q_HEADS), jnp.float32)
    lb_logits = 0.5 * jax.random.normal(ks[3], (DEPTH + 1, D_RNN), jnp.float32)
    rnn_norm = 1.0 + 0.1 * jax.random.normal(ks[4], (DEPTH, D_RNN), jnp.float32)
    w_out = jax.random.normal(ks[5], (DEPTH, D_MIX, D_MODEL), jnp.float32) * D_MIX ** -0.5
    pre_norm = 1.0 + 0.1 * jax.random.normal(ks[6], (DEPTH, D_MODEL), jnp.float32)
    post_norm = 1.0 + 0.1 * jax.random.normal(ks[7], (DEPTH, D_MODEL), jnp.float32)
    return {"x": x, "w_in": w_in, "attn_sinks": attn_sinks, "lb_logits": lb_logits,
            "rnn_norm": rnn_norm, "w_out": w_out, "pre_norm": pre_norm, "post_norm": post_norm}


def _fwd_reference(x, w_in, attn_sinks, lb_logits, rnn_norm, w_out, pre_norm, post_norm):
    B, S, _ = x.shape
    lb_table = jnp.cumsum(jax.nn.softmax(lb_logits.astype(jnp.float32), axis=0), axis=0)
    for layer in range(DEPTH):
        h = rms_norm(x, pre_norm[layer])
        proj = jnp.einsum('bsd,de->bse', h, w_in[layer])
        aq, ak, av, ag, rq, rf, ri, rg = jnp.split(proj, SPLIT_POINTS, axis=-1)

        attn = sliding_window_attention(
            aq.reshape(B, S, N_Q_HEADS, ATTN_HEAD_DIM),
            ak.reshape(B, S, N_KV_HEADS, ATTN_HEAD_DIM),
            av.reshape(B, S, N_KV_HEADS, ATTN_HEAD_DIM),
            attn_sinks[layer])
        attn = attn * jax.nn.silu(ag)

        lb = lb_table[layer]
        f = lb + (1.0 - lb) * jax.nn.sigmoid(rf.astype(jnp.float32))
        g = jnp.log(f)
        k = 1.0 - f
        q = jax.nn.silu(rq)
        shp = (B, S, N_RNN_HEADS, RNN_HEAD_DIM)
        o = hgrn2_recurrence(q.reshape(shp), k.reshape(shp), ri.reshape(shp), g.reshape(shp))
        o = rms_norm(o, rnn_norm[layer].reshape(N_RNN_HEADS, RNN_HEAD_DIM))
        o = o.reshape(B, S, D_RNN).astype(x.dtype) * jax.nn.silu(rg)

        mixed = jnp.concatenate([attn, o], axis=-1)
        y = jnp.einsum('bse,ed->bsd', mixed, w_out[layer])
        x = x + rms_norm(y, post_norm[layer])
    return x


import jax as _jax
import jax.numpy as _jnp

TWIN_FORMAT = 'train_step'
FWD_PARAMS = ['x', 'w_in', 'attn_sinks', 'lb_logits', 'rnn_norm', 'w_out', 'pre_norm', 'post_norm']
TWIN_WEIGHTS = ['w_in', 'attn_sinks', 'lb_logits', 'rnn_norm', 'w_out', 'pre_norm', 'post_norm']
TWIN_DIFF_INPUT = 'x'
TWIN_INPUTS = ['x', 'w_in', 'attn_sinks', 'lb_logits', 'rnn_norm', 'w_out', 'pre_norm', 'post_norm', 'loss_target', 'm_w_in', 'm_attn_sinks', 'm_lb_logits', 'm_rnn_norm', 'm_w_out', 'm_pre_norm', 'm_post_norm', 'v_w_in', 'v_attn_sinks', 'v_lb_logits', 'v_rnn_norm', 'v_w_out', 'v_pre_norm', 'v_post_norm']
TWIN_OUTPUTS = ['loss', 'grad_x', 'grad_w_in', 'grad_attn_sinks', 'grad_lb_logits', 'grad_rnn_norm', 'grad_w_out', 'grad_pre_norm', 'grad_post_norm', 'delta_w_in', 'delta_attn_sinks', 'delta_lb_logits', 'delta_rnn_norm', 'delta_w_out', 'delta_pre_norm', 'delta_post_norm', 'new_m_w_in', 'new_m_attn_sinks', 'new_m_lb_logits', 'new_m_rnn_norm', 'new_m_w_out', 'new_m_pre_norm', 'new_m_post_norm', 'new_v_w_in', 'new_v_attn_sinks', 'new_v_lb_logits', 'new_v_rnn_norm', 'new_v_w_out', 'new_v_pre_norm', 'new_v_post_norm']
TWIN_LEAF_KINDS = {'loss': 'loss', 'grad_x': 'grad_x', 'grad_w_in': 'grad_w', 'grad_attn_sinks': 'grad_w', 'grad_lb_logits': 'grad_w', 'grad_rnn_norm': 'grad_w', 'grad_w_out': 'grad_w', 'grad_pre_norm': 'grad_w', 'grad_post_norm': 'grad_w', 'delta_w_in': 'delta_w', 'delta_attn_sinks': 'delta_w', 'delta_lb_logits': 'delta_w', 'delta_rnn_norm': 'delta_w', 'delta_w_out': 'delta_w', 'delta_pre_norm': 'delta_w', 'delta_post_norm': 'delta_w', 'new_m_w_in': 'new_m', 'new_m_attn_sinks': 'new_m', 'new_m_lb_logits': 'new_m', 'new_m_rnn_norm': 'new_m', 'new_m_w_out': 'new_m', 'new_m_pre_norm': 'new_m', 'new_m_post_norm': 'new_m', 'new_v_w_in': 'new_v', 'new_v_attn_sinks': 'new_v', 'new_v_lb_logits': 'new_v', 'new_v_rnn_norm': 'new_v', 'new_v_w_out': 'new_v', 'new_v_pre_norm': 'new_v', 'new_v_post_norm': 'new_v'}


def _forward(args):
    return _fwd_reference(*[args[k] for k in FWD_PARAMS])


def _output_shape():
    out = _jax.eval_shape(lambda: _forward(_fwd_setup_inputs(0)))
    return out.shape, out.dtype

N_MICROBATCH = 1
ADAM_LR = 0.001
ADAM_B1 = 0.9
ADAM_B2 = 0.999
ADAM_EPS = 1e-08
ADAM_WD = 0.01
ADAM_STEP = 10
PER_EXAMPLE_BATCH_AXIS = {'x': 0, 'loss_target': 0}
SHARED_INPUTS = []
_WEIGHT_DTYPES = {'w_in': _jnp.float32, 'attn_sinks': _jnp.float32, 'lb_logits': _jnp.float32, 'rnn_norm': _jnp.float32, 'w_out': _jnp.float32, 'pre_norm': _jnp.float32, 'post_norm': _jnp.float32}
MOMENT_SCALE = {'w_in': 3.755696e-02, 'attn_sinks': 9.833246e-03, 'lb_logits': 5.402456e-03, 'rnn_norm': 6.800135e-02, 'w_out': 6.649222e-02, 'pre_norm': 9.405106e-02, 'post_norm': 8.039095e+00}


def _to_microbatches(a, axis):
    t = _jnp.moveaxis(a, axis, 0)
    t = t.reshape((N_MICROBATCH, t.shape[0] // N_MICROBATCH) + t.shape[1:])
    return _jnp.moveaxis(t, 1, axis + 1)


def setup_inputs(seed: int = 0) -> dict:
    inp = _fwd_setup_inputs(seed)
    key = _jax.random.fold_in(_jax.random.key(seed), 7919)
    shape, _ = _output_shape()
    out = dict(inp)
    out["loss_target"] = _jax.random.normal(_jax.random.fold_in(key, 0), shape, _jnp.float32)
    for i, name in enumerate(TWIN_WEIGHTS):
        w = inp[name].astype(_jnp.float32)
        if MOMENT_SCALE is None:
            s = _jnp.sqrt(_jnp.mean(_jnp.square(w)) + 1e-30)
        else:
            s = MOMENT_SCALE[name]
        km, kv = _jax.random.split(_jax.random.fold_in(key, i + 1))
        out[name] = w
        out["m_" + name] = s * _jax.random.normal(km, w.shape, _jnp.float32)
        out["v_" + name] = (s * s) * _jax.random.uniform(kv, w.shape, _jnp.float32, 0.5, 1.5)
    if N_MICROBATCH > 1:
        for name, axis in PER_EXAMPLE_BATCH_AXIS.items():
            out[name] = _to_microbatches(out[name], axis)
    return {'x': out['x'], 'w_in': out['w_in'], 'attn_sinks': out['attn_sinks'], 'lb_logits': out['lb_logits'], 'rnn_norm': out['rnn_norm'], 'w_out': out['w_out'], 'pre_norm': out['pre_norm'], 'post_norm': out['post_norm'], 'loss_target': out['loss_target'], 'm_w_in': out['m_w_in'], 'm_attn_sinks': out['m_attn_sinks'], 'm_lb_logits': out['m_lb_logits'], 'm_rnn_norm': out['m_rnn_norm'], 'm_w_out': out['m_w_out'], 'm_pre_norm': out['m_pre_norm'], 'm_post_norm': out['m_post_norm'], 'v_w_in': out['v_w_in'], 'v_attn_sinks': out['v_attn_sinks'], 'v_lb_logits': out['v_lb_logits'], 'v_rnn_norm': out['v_rnn_norm'], 'v_w_out': out['v_w_out'], 'v_pre_norm': out['v_pre_norm'], 'v_post_norm': out['v_post_norm']}


def _loss(weights, diff, rest, loss_target):
    with _jax.named_scope("forward"):
        args = {**rest, TWIN_DIFF_INPUT: diff, **{k: w.astype(_WEIGHT_DTYPES[k]) for k, w in weights.items()}}
        y = _forward(args)
    with _jax.named_scope("loss_head"):
        err = _jnp.square(y.astype(_jnp.float32) - loss_target)
        return 0.5 * _jnp.sum(_jnp.mean(err, axis=-1)) if err.ndim else 0.5 * err


def _adamw(w, g, m, v):
    m = ADAM_B1 * m + (1.0 - ADAM_B1) * g
    v = ADAM_B2 * v + (1.0 - ADAM_B2) * _jnp.square(g)
    m_hat = m / (1.0 - ADAM_B1 ** ADAM_STEP)
    v_hat = v / (1.0 - ADAM_B2 ** ADAM_STEP)
    delta = -ADAM_LR * (m_hat / (_jnp.sqrt(v_hat) + ADAM_EPS) + ADAM_WD * w)
    return delta, m, v


def reference(x, w_in, attn_sinks, lb_logits, rnn_norm, w_out, pre_norm, post_norm, loss_target, m_w_in, m_attn_sinks, m_lb_logits, m_rnn_norm, m_w_out, m_pre_norm, m_post_norm, v_w_in, v_attn_sinks, v_lb_logits, v_rnn_norm, v_w_out, v_pre_norm, v_post_norm):
    given = dict(x=x, w_in=w_in, attn_sinks=attn_sinks, lb_logits=lb_logits, rnn_norm=rnn_norm, w_out=w_out, pre_norm=pre_norm, post_norm=post_norm, loss_target=loss_target, m_w_in=m_w_in, m_attn_sinks=m_attn_sinks, m_lb_logits=m_lb_logits, m_rnn_norm=m_rnn_norm, m_w_out=m_w_out, m_pre_norm=m_pre_norm, m_post_norm=m_post_norm, v_w_in=v_w_in, v_attn_sinks=v_attn_sinks, v_lb_logits=v_lb_logits, v_rnn_norm=v_rnn_norm, v_w_out=v_w_out, v_pre_norm=v_pre_norm, v_post_norm=v_post_norm)
    weights = {n: given[n] for n in TWIN_WEIGHTS}
    shared = {n: given[n] for n in SHARED_INPUTS}
    per_example = {n: given[n] for n in ['x']}
    grad_fn = _jax.value_and_grad(_loss, argnums=(0, 1))

    def one_microbatch(ex, loss_target):
        ex = dict(ex)
        diff = ex.pop(TWIN_DIFF_INPUT)
        return grad_fn(weights, diff, {**shared, **ex}, loss_target)

    if N_MICROBATCH == 1:
        loss, (grad_w, grad_x) = one_microbatch(per_example, given["loss_target"])
    else:
        def body(carry, xs):
            loss_sum, grad_sum = carry
            l_k, (gw_k, gx_k) = one_microbatch(xs[0], xs[1])
            with _jax.named_scope("update"):
                return (loss_sum + l_k, _jax.tree.map(_jnp.add, grad_sum, gw_k)), gx_k

        init = (_jnp.zeros((), _jnp.float32), _jax.tree.map(_jnp.zeros_like, weights))
        (loss, grad_w), grad_x = _jax.lax.scan(body, init, (per_example, given["loss_target"]))
    with _jax.named_scope("update"):
        delta_w, new_m, new_v = {}, {}, {}
        for n in TWIN_WEIGHTS:
            delta_w[n], new_m[n], new_v[n] = _adamw(weights[n], grad_w[n], given["m_" + n], given["v_" + n])
    return (loss, grad_x, *[grad_w[n] for n in TWIN_WEIGHTS], *[delta_w[n] for n in TWIN_WEIGHTS],
            *[new_m[n] for n in TWIN_WEIGHTS], *[new_v[n] for n in TWIN_WEIGHTS])
```

```python
import functools

import jax
import jax.numpy as jnp
from jax import lax
from jax.experimental import pallas as pl
from jax.experimental.pallas import tpu as pltpu

F32 = jnp.float32
BF16 = jnp.bfloat16

ATTN_HEAD = 64
GQA = 8
WINDOW = 128
RNN_HEAD = 128
CHUNK = 64
NORM_EPS = 1e-6
LANES = 128
N_DEV = 8

ADAM_LR = 0.001
ADAM_B1 = 0.9
ADAM_B2 = 0.999
ADAM_EPS = 1e-08
ADAM_WD = 0.01
ADAM_STEP = 10

VMEM_LIMIT = 56 * 1024 * 1024
MESH = pl.DeviceIdType.MESH

NT_DIMS = (((1,), (1,)), ((), ()))
NN_DIMS = (((1,), (0,)), ((), ()))
TN_DIMS = (((0,), (0,)), ((), ()))


def _params(*sem):
    return pltpu.CompilerParams(dimension_semantics=sem, vmem_limit_bytes=VMEM_LIMIT)


def _dot(a, b, dims):
    return lax.dot_general(a, b, dims, preferred_element_type=F32)


def _sigmoid(v):
    return 1.0 / (1.0 + jnp.exp(-v))


def _matmul(a, b, *, grid, a_spec, b_spec, o_spec, out_shape, trans_b, name):
    nk = grid[2]
    dims = NT_DIMS if trans_b else NN_DIMS

    def body(a_ref, b_ref, o_ref, *scratch):
        prod = _dot(a_ref[...], b_ref[...], dims)
        if nk == 1:
            o_ref[...] = prod.astype(o_ref.dtype)
        else:
            acc_ref, = scratch
            k = pl.program_id(2)

            @pl.when(k == 0)
            def _():
                acc_ref[...] = prod

            @pl.when(k > 0)
            def _():
                acc_ref[...] += prod

            @pl.when(k == nk - 1)
            def _():
                o_ref[...] = acc_ref[...].astype(o_ref.dtype)

    scratch = [] if nk == 1 else [pltpu.VMEM(tuple(d for d in o_spec.block_shape if d is not None), F32)]
    return pl.pallas_call(
        body, name=name, grid=grid, in_specs=[a_spec, b_spec], out_specs=o_spec, out_shape=out_shape,
        scratch_shapes=scratch, compiler_params=_params("parallel", "parallel", "arbitrary"),
    )(a, b)


def _pick(n, pref):
    t = min(n, pref)
    assert n % t == 0, (n, pref)
    return t


def _proj_in(h, w_blk):
    T, D = h.shape
    nb, _, wd = w_blk.shape
    tm, tn = _pick(T, 1024), _pick(wd, 640)
    per = wd // tn
    return _matmul(
        h, w_blk, grid=(T // tm, nb * per, 1),
        a_spec=pl.BlockSpec((tm, D), lambda i, j, k: (i, 0)),
        b_spec=pl.BlockSpec((None, D, tn), lambda i, j, k: (j // per, 0, j % per)),
        o_spec=pl.BlockSpec((tm, tn), lambda i, j, k: (i, j)),
        out_shape=jax.ShapeDtypeStruct((T, nb * wd), F32), trans_b=False, name="proj_in")


def _proj_out(mixed, w_out):
    T, E = mixed.shape
    D = w_out.shape[1]
    tm, tn = _pick(T, 512), _pick(D, 512)
    return _matmul(
        mixed, w_out, grid=(T // tm, D // tn, 1),
        a_spec=pl.BlockSpec((tm, E), lambda i, j, k: (i, 0)),
        b_spec=pl.BlockSpec((E, tn), lambda i, j, k: (0, j)),
        o_spec=pl.BlockSpec((tm, tn), lambda i, j, k: (i, j)),
        out_shape=jax.ShapeDtypeStruct((T, D), F32), trans_b=False, name="proj_out")


def _dmixed(dy, w_out):
    T, D = dy.shape
    E = w_out.shape[0]
    tm, tn = _pick(T, 1024), _pick(E, 512)
    return _matmul(
        dy, w_out, grid=(T // tm, E // tn, 1),
        a_spec=pl.BlockSpec((tm, D), lambda i, j, k: (i, 0)),
        b_spec=pl.BlockSpec((tn, D), lambda i, j, k: (j, 0)),
        o_spec=pl.BlockSpec((tm, tn), lambda i, j, k: (i, j)),
        out_shape=jax.ShapeDtypeStruct((T, E), F32), trans_b=True, name="dmixed")


def _dw_out(mixed_t, dy):
    E, T = mixed_t.shape
    D = dy.shape[1]
    tm, tn = _pick(E, 1024), _pick(D, 512)
    return _matmul(
        mixed_t, dy, grid=(E // tm, D // tn, 1),
        a_spec=pl.BlockSpec((tm, T), lambda i, j, k: (i, 0)),
        b_spec=pl.BlockSpec((T, tn), lambda i, j, k: (0, j)),
        o_spec=pl.BlockSpec((tm, tn), lambda i, j, k: (i, j)),
        out_shape=jax.ShapeDtypeStruct((E, D), BF16), trans_b=False, name="dw_out")


def _dw_in(h_t, dproj, nb):
    D, T = h_t.shape
    wd = dproj.shape[1] // nb
    tm, tn = _pick(D, 1024), _pick(wd, 640)
    per = wd // tn
    return _matmul(
        h_t, dproj, grid=(D // tm, nb * per, 1),
        a_spec=pl.BlockSpec((tm, T), lambda i, j, k: (i, 0)),
        b_spec=pl.BlockSpec((T, tn), lambda i, j, k: (0, j)),
        o_spec=pl.BlockSpec((None, tm, tn), lambda i, j, k: (j // per, i, j % per)),
        out_shape=jax.ShapeDtypeStruct((nb, D, wd), BF16), trans_b=False, name="dw_in")


def _dh(dproj, w_blk):
    T = dproj.shape[0]
    nb, D, wd = w_blk.shape
    tm, tn = _pick(T, 1024), _pick(D, 1024)
    return _matmul(
        dproj, w_blk, grid=(T // tm, D // tn, nb),
        a_spec=pl.BlockSpec((tm, wd), lambda i, j, k: (i, k)),
        b_spec=pl.BlockSpec((None, tn, wd), lambda i, j, k: (k, j, 0)),
        o_spec=pl.BlockSpec((tm, tn), lambda i, j, k: (i, j)),
        out_shape=jax.ShapeDtypeStruct((T, D), F32), trans_b=True, name="dh")


def _cast_bf16(w, name):
    R, C = w.shape
    tr = _pick(R, 256)

    def body(w_ref, o_ref):
        o_ref[...] = w_ref[...].astype(BF16)

    return pl.pallas_call(
        body, name=name, grid=(R // tr,), in_specs=[pl.BlockSpec((tr, C), lambda i: (i, 0))],
        out_specs=pl.BlockSpec((tr, C), lambda i: (i, 0)), out_shape=jax.ShapeDtypeStruct((R, C), BF16),
        compiler_params=_params("parallel"))(w)


def _prenorm(x, gain):
    T, D = x.shape
    tm = _pick(T, 256)

    def body(x_ref, g_ref, h_ref, ht_ref):
        xv = x_ref[...]
        r = lax.rsqrt(jnp.mean(xv * xv, axis=-1, keepdims=True) + NORM_EPS)
        h = xv * r * g_ref[...]
        h_ref[...] = h.astype(BF16)
        ht_ref[...] = h.T.astype(BF16)

    return pl.pallas_call(
        body, name="prenorm", grid=(T // tm,),
        in_specs=[pl.BlockSpec((tm, D), lambda i: (i, 0)), pl.BlockSpec((1, D), lambda i: (0, 0))],
        out_specs=[pl.BlockSpec((tm, D), lambda i: (i, 0)), pl.BlockSpec((D, tm), lambda i: (0, i))],
        out_shape=[jax.ShapeDtypeStruct((T, D), BF16), jax.ShapeDtypeStruct((D, T), BF16)],
        compiler_params=_params("parallel"))(x, gain)


def _norm_bwd(u, yn, r):
    return r * (u - yn * jnp.mean(u * yn, axis=-1, keepdims=True))


def _loss_head(y, x, target, gain):
    T, D = y.shape
    tm = _pick(T, 256)

    def body(y_ref, x_ref, t_ref, g_ref, dy_ref, dout_ref, gpost_ref, sq_ref):
        yv = y_ref[...]
        g = g_ref[...]
        r = lax.rsqrt(jnp.mean(yv * yv, axis=-1, keepdims=True) + NORM_EPS)
        yn = yv * r
        err = x_ref[...] + yn * g - t_ref[...]
        dout = err * (1.0 / D)
        dy_ref[...] = _norm_bwd(dout * g, yn, r).astype(BF16)
        dout_ref[...] = dout

        @pl.when(pl.program_id(0) == 0)
        def _():
            gpost_ref[...] = jnp.zeros_like(gpost_ref)
            sq_ref[...] = jnp.zeros_like(sq_ref)

        gpost_ref[...] += jnp.sum(dout * yn, axis=0, keepdims=True)
        sq_ref[...] += jnp.sum(err * err, axis=0, keepdims=True)

    row = pl.BlockSpec((tm, D), lambda i: (i, 0))
    vec = pl.BlockSpec((1, D), lambda i: (0, 0))
    return pl.pallas_call(
        body, name="loss_head", grid=(T // tm,), in_specs=[row, row, row, vec], out_specs=[row, row, vec, vec],
        out_shape=[jax.ShapeDtypeStruct((T, D), BF16), jax.ShapeDtypeStruct((T, D), F32),
                   jax.ShapeDtypeStruct((1, D), F32), jax.ShapeDtypeStruct((1, D), F32)],
        compiler_params=_params("arbitrary"))(y, x, target, gain)


def _prenorm_bwd(x, dh, dout, gain):
    T, D = x.shape
    tm = _pick(T, 256)

    def body(x_ref, dh_ref, dout_ref, g_ref, gx_ref, gpre_ref):
        xv = x_ref[...]
        dhv = dh_ref[...]
        r = lax.rsqrt(jnp.mean(xv * xv, axis=-1, keepdims=True) + NORM_EPS)
        xn = xv * r
        gx_ref[...] = dout_ref[...] + _norm_bwd(dhv * g_ref[...], xn, r)

        @pl.when(pl.program_id(0) == 0)
        def _():
            gpre_ref[...] = jnp.zeros_like(gpre_ref)

        gpre_ref[...] += jnp.sum(dhv * xn, axis=0, keepdims=True)

    row = pl.BlockSpec((tm, D), lambda i: (i, 0))
    vec = pl.BlockSpec((1, D), lambda i: (0, 0))
    return pl.pallas_call(
        body, name="prenorm_bwd", grid=(T // tm,), in_specs=[row, row, row, vec], out_specs=[row, vec],
        out_shape=[jax.ShapeDtypeStruct((T, D), F32), jax.ShapeDtypeStruct((1, D), F32)],
        compiler_params=_params("arbitrary"))(x, dh, dout, gain)


def _attn_masks(n):
    row = lax.broadcasted_iota(jnp.int32, (2 * WINDOW, 2 * WINDOW), 0) % WINDOW
    col = lax.broadcasted_iota(jnp.int32, (2 * WINDOW, 2 * WINDOW), 1)
    valid = (col > row) & (col <= row + WINDOW) & ((n > 0) | (col >= WINDOW))
    low = lax.broadcasted_iota(jnp.int32, (1, LANES), 1) < ATTN_HEAD
    top = lax.broadcasted_iota(jnp.int32, (2 * WINDOW, 1), 0) < WINDOW
    return valid, low, top


def _dup_half(pair, keep):
    return jnp.where(keep, pair, pltpu.roll(pair, ATTN_HEAD, 1))


def _fold_half(v):
    return v + pltpu.roll(v, ATTN_HEAD, 1)


def _attn_probs(qpair, k2, sink_lo, sink_hi, valid, low, top):
    q2 = jnp.concatenate([jnp.where(low, qpair, 0.0), jnp.where(low, 0.0, qpair)], axis=0).astype(BF16)
    s = _dot(q2, k2, NT_DIMS) * (ATTN_HEAD ** -0.5)
    s = jnp.where(valid, s, -jnp.inf)
    sink = jnp.where(top, sink_lo, sink_hi)
    m = jnp.maximum(jnp.max(s, axis=-1, keepdims=True), sink)
    p = jnp.exp(s - m)
    psink = jnp.exp(sink - m)
    inv = 1.0 / (jnp.sum(p, axis=-1, keepdims=True) + psink)
    return q2, p * inv, psink * inv


def _attn_specs(T, D, with_dmix):
    nqb = D // 512
    kb = D // LANES
    vb = kb + D // (8 * LANES)
    gb = (D + D // 4) // 512
    wide = lambda off: [pl.BlockSpec((WINDOW, 512), functools.partial(lambda o, e, jp, n: (n, o + 2 * jp + e), off, e))
                        for e in (0, 1)]
    cur = lambda off: pl.BlockSpec((WINDOW, LANES), functools.partial(lambda o, jp, n: (n, o + jp), off))
    prev = lambda off: pl.BlockSpec((WINDOW, LANES),
                                    functools.partial(lambda o, jp, n: (jnp.maximum(n - 1, 0), o + jp), off))
    del nqb
    return wide(0) + [cur(kb), prev(kb), cur(vb), prev(vb)] + wide(gb)


def _attn_fwd(proj, sinks, D):
    T = proj.shape[0]
    nb, njp = T // WINDOW, D // 1024

    def body(sink_ref, qlo_ref, qhi_ref, kc_ref, kp_ref, vc_ref, vp_ref, glo_ref, ghi_ref, mix_ref, mixt_ref):
        jp, n = pl.program_id(0), pl.program_id(1)
        valid, low, top = _attn_masks(n)
        kk = jnp.concatenate([kp_ref[...], kc_ref[...]], axis=0)
        vv = jnp.concatenate([vp_ref[...], vc_ref[...]], axis=0)
        for hj, (q_ref, g_ref) in enumerate(((qlo_ref, glo_ref), (qhi_ref, ghi_ref))):
            keep = low if hj == 0 else jnp.logical_not(low)
            k2 = _dup_half(kk, keep).astype(BF16)
            v2 = _dup_half(vv, keep).astype(BF16)
            for p in range(4):
                cols = slice(LANES * p, LANES * (p + 1))
                head = (2 * jp + hj) * GQA + 2 * p
                _, probs, _ = _attn_probs(q_ref[:, cols], k2, sink_ref[0, head], sink_ref[0, head + 1],
                                          valid, low, top)
                o2 = _dot(probs.astype(BF16), v2, NN_DIMS)
                opair = jnp.where(low, o2[:WINDOW], o2[WINDOW:])
                g = g_ref[:, cols]
                out = opair * (g * _sigmoid(g))
                oc = slice(512 * hj + LANES * p, 512 * hj + LANES * (p + 1))
                mix_ref[:, oc] = out.astype(BF16)
                mixt_ref[oc, :] = out.T.astype(BF16)

    return pl.pallas_call(
        body, name="attn_fwd", grid=(njp, nb),
        in_specs=[pl.BlockSpec(memory_space=pltpu.SMEM)] + _attn_specs(T, D, False),
        out_specs=[pl.BlockSpec((WINDOW, 1024), lambda jp, n: (n, jp)),
                   pl.BlockSpec((1024, WINDOW), lambda jp, n: (jp, n))],
        out_shape=[jax.ShapeDtypeStruct((T, 2 * D), BF16), jax.ShapeDtypeStruct((2 * D, T), BF16)],
        compiler_params=_params("parallel", "parallel"))(sinks, *([proj] * 8))


def _attn_bwd(proj, sinks, dmix, D):
    T = proj.shape[0]
    nb, njp = T // WINDOW, D // 1024

    def body(sink_ref, qlo_ref, qhi_ref, kc_ref, kp_ref, vc_ref, vp_ref, glo_ref, ghi_ref, dmix_ref,
             dq_ref, dk_ref, dv_ref, dg_ref, dsink_ref, kcarry_ref, vcarry_ref):
        jp, step = pl.program_id(0), pl.program_id(1)
        n = nb - 1 - step
        valid, low, top = _attn_masks(n)
        lane = lax.broadcasted_iota(jnp.int32, (1, LANES), 1)
        kk = jnp.concatenate([kp_ref[...], kc_ref[...]], axis=0)
        vv = jnp.concatenate([vp_ref[...], vc_ref[...]], axis=0)

        @pl.when(step == 0)
        def _():
            kcarry_ref[...] = jnp.zeros_like(kcarry_ref)
            vcarry_ref[...] = jnp.zeros_like(vcarry_ref)
            dsink_ref[...] = jnp.zeros_like(dsink_ref)

        dk_pair = jnp.zeros((2 * WINDOW, LANES), F32)
        dv_pair = jnp.zeros((2 * WINDOW, LANES), F32)
        dsink = jnp.zeros((1, LANES), F32)
        for hj, (q_ref, g_ref) in enumerate(((qlo_ref, glo_ref), (qhi_ref, ghi_ref))):
            keep = low if hj == 0 else jnp.logical_not(low)
            k2 = _dup_half(kk, keep).astype(BF16)
            v2 = _dup_half(vv, keep).astype(BF16)
            dk_head = jnp.zeros((2 * WINDOW, LANES), F32)
            dv_head = jnp.zeros((2 * WINDOW, LANES), F32)
            for p in range(4):
                cols = slice(LANES * p, LANES * (p + 1))
                oc = slice(512 * hj + LANES * p, 512 * hj + LANES * (p + 1))
                head = (2 * jp + hj) * GQA + 2 * p
                q2, probs, psink = _attn_probs(q_ref[:, cols], k2, sink_ref[0, head], sink_ref[0, head + 1],
                                               valid, low, top)
                pb = probs.astype(BF16)
                o2 = _dot(pb, v2, NN_DIMS)
                opair = jnp.where(low, o2[:WINDOW], o2[WINDOW:])
                g = g_ref[:, cols]
                sg = _sigmoid(g)
                dgated = dmix_ref[:, oc]
                dg_ref[:, oc] = (dgated * opair * (sg * (1.0 + g * (1.0 - sg)))).astype(BF16)
                do = dgated * (g * sg)
                do2 = jnp.concatenate([jnp.where(low, do, 0.0), jnp.where(low, 0.0, do)], axis=0).astype(BF16)
                dp = _dot(do2, v2, NT_DIMS)
                delta = jnp.sum(probs * dp, axis=-1, keepdims=True)
                ds = (probs * (dp - delta) * (ATTN_HEAD ** -0.5)).astype(BF16)
                dq2 = _dot(ds, k2, NN_DIMS)
                dq_ref[:, oc] = jnp.where(low, dq2[:WINDOW], dq2[WINDOW:]).astype(BF16)
                dk_head += _dot(ds, q2, TN_DIMS)
                dv_head += _dot(pb, do2, TN_DIMS)
                ps = psink * delta
                local = hj * GQA + 2 * p
                dsink -= jnp.where(lane == local, jnp.sum(ps[:WINDOW], axis=0, keepdims=True), 0.0)
                dsink -= jnp.where(lane == local + 1, jnp.sum(ps[WINDOW:], axis=0, keepdims=True), 0.0)
            dk_pair += jnp.where(keep, _fold_half(dk_head), 0.0)
            dv_pair += jnp.where(keep, _fold_half(dv_head), 0.0)
        dk_ref[...] = (dk_pair[WINDOW:] + kcarry_ref[...]).astype(BF16)
        dv_ref[...] = (dv_pair[WINDOW:] + vcarry_ref[...]).astype(BF16)
        kcarry_ref[...] = dk_pair[:WINDOW]
        vcarry_ref[...] = dv_pair[:WINDOW]
        dsink_ref[...] += dsink

    rev = lambda spec: pl.BlockSpec(spec.block_shape, functools.partial(
        lambda f, jp, s: f(jp, nb - 1 - s), spec.index_map))
    in_specs = [rev(s) for s in _attn_specs(T, D, True)]
    wide_out = pl.BlockSpec((WINDOW, 1024), lambda jp, s: (nb - 1 - s, jp))
    pair_out = pl.BlockSpec((WINDOW, LANES), lambda jp, s: (nb - 1 - s, jp))
    return pl.pallas_call(
        body, name="attn_bwd", grid=(njp, nb),
        in_specs=[pl.BlockSpec(memory_space=pltpu.SMEM)] + in_specs + [wide_out],
        out_specs=[wide_out, pair_out, pair_out, wide_out, pl.BlockSpec((None, 1, LANES), lambda jp, s: (jp, 0, 0))],
        out_shape=[jax.ShapeDtypeStruct((T, D), BF16), jax.ShapeDtypeStruct((T, D // 8), BF16),
                   jax.ShapeDtypeStruct((T, D // 8), BF16), jax.ShapeDtypeStruct((T, D), BF16),
                   jax.ShapeDtypeStruct((njp, 1, LANES), F32)],
        scratch_shapes=[pltpu.VMEM((WINDOW, LANES), F32), pltpu.VMEM((WINDOW, LANES), F32)],
        compiler_params=_params("parallel", "arbitrary"))(sinks, *([proj] * 8), dmix)


RNN_TB = 512


def _split3(v):
    a = v.astype(BF16)
    r = v - a.astype(F32)
    b = r.astype(BF16)
    c = (r - b.astype(F32)).astype(BF16)
    return a, b, c


def _tri_sum(tri, v):
    a, b, c = _split3(v)
    return _dot(tri, a, NN_DIMS) + _dot(tri, b, NN_DIMS) + _dot(tri, c, NN_DIMS)


def _lower_bound(lb_ref):
    l0, l1 = lb_ref[0:1, :], lb_ref[1:2, :]
    m = jnp.maximum(l0, l1)
    e0, e1 = jnp.exp(l0 - m), jnp.exp(l1 - m)
    return e0 / (e0 + e1)


def _rnn_gates(rq, rf, lb):
    sq = _sigmoid(rq)
    sf = _sigmoid(rf)
    f = lb + (1.0 - lb) * sf
    return sq, sf, f


def _rnn_decays(g, tri):
    G = _tri_sum(tri, g)
    last = G[CHUNK - 1:CHUNK, :]
    mid = G[CHUNK // 2 - 1:CHUNK // 2, :]
    return G, jnp.exp(G), jnp.exp(G - mid), jnp.exp(mid - G), jnp.exp(last - G), jnp.exp(last)


def _chunk_masks():
    r = lax.broadcasted_iota(jnp.int32, (CHUNK, CHUNK), 0)
    c = lax.broadcasted_iota(jnp.int32, (CHUNK, CHUNK), 1)
    return r >= c, (r >= c).astype(BF16), (r <= c).astype(BF16)


def _rnn_specs(T, D, tb, rev):
    nt = T // tb
    base = (2 * D + D // 4) // LANES
    t_of = (lambda s: nt - 1 - s) if rev else (lambda s: s)
    cols = [pl.BlockSpec((tb, LANES), functools.partial(lambda o, h, s: (t_of(s), o + h), base + i * (D // LANES)))
            for i in range(4)]
    return cols, t_of


def _rnn_fwd(proj, lb_logits, rnn_norm, mixed, mixed_t, D):
    T = proj.shape[0]
    tb = _pick(T, RNN_TB)
    nt, nh, cpb = T // tb, D // RNN_HEAD, tb // CHUNK
    cols, _ = _rnn_specs(T, D, tb, False)

    def body(rq_ref, rf_ref, ri_ref, rg_ref, lb_ref, gain_ref, mix_in, mixt_in,
             mix_ref, mixt_ref, o_ref, st_ref, state_ref):
        del mix_in, mixt_in
        causal, tri, _ = _chunk_masks()
        lb = _lower_bound(lb_ref)

        @pl.when(pl.program_id(1) == 0)
        def _():
            state_ref[...] = jnp.zeros_like(state_ref)

        def chunk(c, carry):
            rows = pl.ds(pl.multiple_of(c * CHUNK, CHUNK), CHUNK)
            rq, rf, v, rg = rq_ref[rows, :], rf_ref[rows, :], ri_ref[rows, :], rg_ref[rows, :]
            sq, _, f = _rnn_gates(rq, rf, lb)
            q, k = rq * sq, 1.0 - f
            _, eG, eq, ek, ekl, elast = _rnn_decays(jnp.log(f), tri)
            st = state_ref[...]
            st_ref[c] = st
            vb = v.astype(BF16)
            att = jnp.where(causal, _dot((q * eq).astype(BF16), (k * ek).astype(BF16), NT_DIMS), 0.0)
            o = _dot((q * eG).astype(BF16), st.astype(BF16), NT_DIMS) + _dot(att.astype(BF16), vb, NN_DIMS)
            state_ref[...] = st * elast + _dot(vb, (k * ekl).astype(BF16), TN_DIMS)
            o_ref[rows, :] = o
            r = lax.rsqrt(jnp.mean(o * o, axis=-1, keepdims=True) + NORM_EPS)
            out = (o * r * gain_ref[...]) * (rg * _sigmoid(rg))
            mix_ref[rows, :] = out.astype(BF16)
            return carry

        lax.fori_loop(0, cpb, chunk, 0)
        mixt_ref[...] = mix_ref[...].astype(F32).T.astype(BF16)

    vec2 = pl.BlockSpec((2, LANES), lambda h, s: (0, h))
    vec1 = pl.BlockSpec((1, LANES), lambda h, s: (0, h))
    anyspec = pl.BlockSpec(memory_space=pl.ANY)
    return pl.pallas_call(
        body, name="rnn_fwd", grid=(nh, nt),
        in_specs=cols + [vec2, vec1, anyspec, anyspec],
        out_specs=[pl.BlockSpec((tb, LANES), lambda h, s: (s, D // LANES + h)),
                   pl.BlockSpec((LANES, tb), lambda h, s: (D // LANES + h, s)),
                   pl.BlockSpec((tb, LANES), lambda h, s: (s, h)),
                   pl.BlockSpec((None, cpb, RNN_HEAD, RNN_HEAD), lambda h, s: (h, s, 0, 0))],
        out_shape=[jax.ShapeDtypeStruct(mixed.shape, BF16), jax.ShapeDtypeStruct(mixed_t.shape, BF16),
                   jax.ShapeDtypeStruct((T, D), F32),
                   jax.ShapeDtypeStruct((nh, T // CHUNK, RNN_HEAD, RNN_HEAD), F32)],
        scratch_shapes=[pltpu.VMEM((RNN_HEAD, RNN_HEAD), F32)],
        input_output_aliases={6: 0, 7: 1},
        compiler_params=_params("parallel", "arbitrary"))(proj, proj, proj, proj, lb_logits, rnn_norm, mixed, mixed_t)


def _rnn_bwd(proj, lb_logits, rnn_norm, o_raw, states, dmix, D):
    T = proj.shape[0]
    tb = _pick(T, RNN_TB)
    nt, nh, cpb = T // tb, D // RNN_HEAD, tb // CHUNK
    cols, t_of = _rnn_specs(T, D, tb, True)

    def body(rq_ref, rf_ref, ri_ref, rg_ref, lb_ref, gain_ref, o_ref, st_ref, dmix_ref,
             drq_ref, drf_ref, dri_ref, drg_ref, dlb_ref, dgain_ref, dstate_ref):
        causal, tri, tri_t = _chunk_masks()
        lb = _lower_bound(lb_ref)
        gain = gain_ref[...]

        @pl.when(pl.program_id(1) == 0)
        def _():
            dstate_ref[...] = jnp.zeros_like(dstate_ref)
            dlb_ref[...] = jnp.zeros_like(dlb_ref)
            dgain_ref[...] = jnp.zeros_like(dgain_ref)

        def chunk(i, carry):
            c = cpb - 1 - i
            rows = pl.ds(pl.multiple_of(c * CHUNK, CHUNK), CHUNK)
            rq, rf, v, rg = rq_ref[rows, :], rf_ref[rows, :], ri_ref[rows, :], rg_ref[rows, :]
            o, dgated = o_ref[rows, :], dmix_ref[rows, :]
            sgt = _sigmoid(rg)
            r = lax.rsqrt(jnp.mean(o * o, axis=-1, keepdims=True) + NORM_EPS)
            on = o * r
            drg_ref[rows, :] = (dgated * (on * gain) * (sgt * (1.0 + rg * (1.0 - sgt)))).astype(BF16)
            d_on = dgated * (rg * sgt)
            dgain_ref[...] += jnp.sum(d_on * on, axis=0, keepdims=True)
            do = _norm_bwd(d_on * gain, on, r)
            sq, sf, f = _rnn_gates(rq, rf, lb)
            q, k = rq * sq, 1.0 - f
            _, eG, eq, ek, ekl, elast = _rnn_decays(jnp.log(f), tri)
            st, dst = st_ref[c], dstate_ref[...]
            qg, qm, km, kl = q * eG, q * eq, k * ek, k * ekl
            qgb, qmb, kmb, klb = qg.astype(BF16), qm.astype(BF16), km.astype(BF16), kl.astype(BF16)
            dob, vb, dstb = do.astype(BF16), v.astype(BF16), dst.astype(BF16)
            att = jnp.where(causal, _dot(qmb, kmb, NT_DIMS), 0.0).astype(BF16)
            datt = jnp.where(causal, _dot(dob, vb, NT_DIMS), 0.0).astype(BF16)
            dqg = _dot(dob, st.astype(BF16), NN_DIMS)
            dqm = _dot(datt, kmb, NN_DIMS)
            dkm = _dot(datt, qmb, TN_DIMS)
            dkl = _dot(vb, dstb, NN_DIMS)
            dri_ref[rows, :] = (_dot(att, dob, TN_DIMS) + _dot(klb, dstb, NT_DIMS)).astype(BF16)
            dstate_ref[...] = dst * elast + _dot(dob, qgb, TN_DIMS)
            dq = dqg * eG + dqm * eq
            dk = dkm * ek + dkl * ekl
            dG = dqg * qg + dqm * qmb.astype(F32) - dkm * kmb.astype(F32) - dkl * kl
            dlast = jnp.sum(dkl * kl, axis=0, keepdims=True) + jnp.sum(dst * st, axis=0, keepdims=True) * elast
            last_row = lax.broadcasted_iota(jnp.int32, (CHUNK, 1), 0) == CHUNK - 1
            dg = _tri_sum(tri_t, dG + jnp.where(last_row, dlast, 0.0))
            df = dg / f - dk
            drq_ref[rows, :] = (dq * (sq * (1.0 + rq * (1.0 - sq)))).astype(BF16)
            drf_ref[rows, :] = (df * (1.0 - lb) * (sf * (1.0 - sf))).astype(BF16)
            dlb_ref[...] += jnp.sum(df * (1.0 - sf), axis=0, keepdims=True)
            return carry

        lax.fori_loop(0, cpb, chunk, 0)

    vec2 = pl.BlockSpec((2, LANES), lambda h, s: (0, h))
    vec1 = pl.BlockSpec((1, LANES), lambda h, s: (0, h))
    blk = pl.BlockSpec((tb, LANES), lambda h, s: (t_of(s), h))
    return pl.pallas_call(
        body, name="rnn_bwd", grid=(nh, nt),
        in_specs=cols + [vec2, vec1, blk,
                         pl.BlockSpec((None, cpb, RNN_HEAD, RNN_HEAD), lambda h, s: (h, t_of(s), 0, 0)),
                         pl.BlockSpec((tb, LANES), lambda h, s: (t_of(s), D // LANES + h))],
        out_specs=[blk, blk, blk, blk, vec1, vec1],
        out_shape=[jax.ShapeDtypeStruct((T, D), BF16)] * 4 + [jax.ShapeDtypeStruct((1, D), F32)] * 2,
        scratch_shapes=[pltpu.VMEM((RNN_HEAD, RNN_HEAD), F32)],
        compiler_params=_params("parallel", "arbitrary"))(proj, proj, proj, proj, lb_logits, rnn_norm, o_raw, states, dmix)


def _adamw(w, g, m, v):
    m = ADAM_B1 * m + (1.0 - ADAM_B1) * g
    v = ADAM_B2 * v + (1.0 - ADAM_B2) * (g * g)
    m_hat = m / (1.0 - ADAM_B1 ** ADAM_STEP)
    v_hat = v / (1.0 - ADAM_B2 ** ADAM_STEP)
    delta = -ADAM_LR * (m_hat / (jnp.sqrt(v_hat) + ADAM_EPS) + ADAM_WD * w)
    return delta, m, v


def _adamw_big(w, m, v, own, landed, tr, name):
    R, C = w.shape
    n_land = landed.shape[0]

    def body(w_ref, m_ref, v_ref, own_ref, land_ref, g_ref, d_ref, nm_ref, nv_ref):
        g = own_ref[...].astype(F32)
        for k in range(n_land):
            g = g + land_ref[k].astype(F32)
        delta, nm, nv = _adamw(w_ref[...], g, m_ref[...], v_ref[...])
        g_ref[...] = g
        d_ref[...] = delta
        nm_ref[...] = nm
        nv_ref[...] = nv

    blk = pl.BlockSpec((tr, C), lambda i: (i, 0))
    return pl.pallas_call(
        body, name=name, grid=(R // tr,),
        in_specs=[blk, blk, blk, blk, pl.BlockSpec((n_land, tr, C), lambda i: (0, i, 0))],
        out_specs=[blk] * 4, out_shape=[jax.ShapeDtypeStruct((R, C), F32)] * 4,
        compiler_params=_params("parallel"))(w, m, v, own, landed)


def _adamw_small(total, sinks, lb_logits, rnn_norm, pre_norm, post_norm, moments, D):
    params = [sinks, lb_logits, rnn_norm, pre_norm, post_norm]

    def body(tot_ref, *refs):
        p_refs, m_refs, v_refs = refs[0:5], refs[5:10], refs[10:15]
        loss_ref, outs = refs[15], refs[16:]
        tot = tot_ref[...]
        l0, l1 = p_refs[1][0:1, :], p_refs[1][1:2, :]
        mx = jnp.maximum(l0, l1)
        e0, e1 = jnp.exp(l0 - mx), jnp.exp(l1 - mx)
        p0, p1 = e0 / (e0 + e1), e1 / (e0 + e1)
        dlb = tot[0:1, :]
        grads = [tot[5:6, 0:LANES], jnp.concatenate([dlb * p0 * (1.0 - p0), -dlb * p0 * p1], axis=0),
                 tot[1:2, :], tot[2:3, :], tot[3:4, :]]
        loss_ref[...] = 0.5 / D * jnp.sum(tot[4:5, :], axis=-1, keepdims=True)
        for i, g in enumerate(grads):
            delta, nm, nv = _adamw(p_refs[i][...], g, m_refs[i][...], v_refs[i][...])
            outs[4 * i][...] = g
            outs[4 * i + 1][...] = delta
            outs[4 * i + 2][...] = nm
            outs[4 * i + 3][...] = nv

    out_shape = [jax.ShapeDtypeStruct((1, 1), F32)]
    for p in params:
        out_shape += [jax.ShapeDtypeStruct(p.shape, F32)] * 4
    return pl.pallas_call(body, name="adamw_small", out_shape=out_shape)(total, *params, *moments)


def _me():
    return lax.axis_index("x"), lax.axis_index("y"), lax.axis_index("c")


def _flat(px, py, pc):
    return 4 * px + 2 * py + pc


def _gather_weights(wi, wo):
    def body(wi_ref, wo_ref, gi_ref, go_ref, send_sems, recv_sems, local_sems):
        x, y, c = _me()
        me, sibling = (x, y, c), (x, y, 1 - c)
        chips = [(1 - x, y), (x, 1 - y), (1 - x, 1 - y)]

        def copies(k, block, to, own):
            out = []
            for a, (src, dst) in enumerate(((wi_ref, gi_ref), (wo_ref, go_ref))):
                slot = dst.at[_flat(*block)]
                out.append(pltpu.make_async_remote_copy(
                    src_ref=src if own else slot, dst_ref=slot, send_sem=send_sems.at[a, k],
                    recv_sem=recv_sems.at[a, k], device_id=to, device_id_type=MESH))
            return out

        mine = [pltpu.make_async_copy(wi_ref, gi_ref.at[_flat(*me)], local_sems.at[0]),
                pltpu.make_async_copy(wo_ref, go_ref.at[_flat(*me)], local_sems.at[1])]
        for cp in mine:
            cp.start()
        first = copies(0, me, sibling, True)
        for j, chip in enumerate(chips):
            first += copies(1 + j, me, (*chip, c), True)
        for cp in first:
            cp.start()
        passed = []
        for j, chip in enumerate(chips):
            for cp in copies(1 + j, (*chip, c), me, False):
                cp.wait_recv()
            fwd = copies(4 + j, (*chip, c), sibling, False)
            for cp in fwd:
                cp.start()
            passed += fwd
        for cp in copies(0, sibling, me, False):
            cp.wait_recv()
        for j, chip in enumerate(chips):
            for cp in copies(4 + j, (*chip, 1 - c), me, False):
                cp.wait_recv()
        for cp in first + passed:
            cp.wait_send()
        for cp in mine:
            cp.wait()

    anyspec = pl.BlockSpec(memory_space=pl.ANY)
    return pl.pallas_call(
        body, name="gather_weights", in_specs=[anyspec, anyspec], out_specs=[anyspec, anyspec],
        out_shape=[jax.ShapeDtypeStruct((N_DEV,) + wi.shape, BF16), jax.ShapeDtypeStruct((N_DEV,) + wo.shape, BF16)],
        scratch_shapes=[pltpu.SemaphoreType.DMA((2, 7)), pltpu.SemaphoreType.DMA((2, 7)), pltpu.SemaphoreType.DMA((2,))],
    )(wi, wo)


def _peers():
    x, y, c = _me()
    out = []
    for k in range(1, N_DEV):
        fx, fy, fc = (k >> 2) & 1, (k >> 1) & 1, k & 1
        out.append(((1 - x) if fx else x, (1 - y) if fy else y, (1 - c) if fc else c))
    return out


def _scatter_partials(part, name):
    def body(part_ref, land_ref, send_sems, recv_sems):
        copies = []
        for k, peer in enumerate(_peers()):
            copies.append(pltpu.make_async_remote_copy(
                src_ref=part_ref.at[_flat(*peer)], dst_ref=land_ref.at[k], send_sem=send_sems.at[k],
                recv_sem=recv_sems.at[k], device_id=peer, device_id_type=MESH))
        for cp in copies:
            cp.start()
        for cp in copies:
            cp.wait_recv()
        for cp in copies:
            cp.wait_send()

    anyspec = pl.BlockSpec(memory_space=pl.ANY)
    return pl.pallas_call(
        body, name=name, in_specs=[anyspec], out_specs=anyspec,
        out_shape=jax.ShapeDtypeStruct((N_DEV - 1,) + part.shape[1:], part.dtype),
        scratch_shapes=[pltpu.SemaphoreType.DMA((7,)), pltpu.SemaphoreType.DMA((7,))],
    )(part)


def _allreduce_small(rows, D):
    n_rows = len(rows)

    def body(*refs):
        in_refs, out_ref, all_ref = refs[:n_rows], refs[n_rows], refs[n_rows + 1]
        send_sems, recv_sems = refs[n_rows + 2], refs[n_rows + 3]
        me = _flat(*_me())
        mine = all_ref.at[me]
        mine[...] = jnp.zeros((8, D), F32)
        for i, r in enumerate(in_refs):
            mine[i:i + 1, :] = r[...]
        copies = []
        for k, peer in enumerate(_peers()):
            copies.append(pltpu.make_async_remote_copy(
                src_ref=mine, dst_ref=mine, send_sem=send_sems.at[k], recv_sem=recv_sems.at[k],
                device_id=peer, device_id_type=MESH))
        for cp in copies:
            cp.start()
        for cp in copies:
            cp.wait_recv()
        for cp in copies:
            cp.wait_send()
        tot = all_ref[0]
        for d in range(1, N_DEV):
            tot = tot + all_ref[d]
        out_ref[...] = tot

    vm = pl.BlockSpec(memory_space=pltpu.VMEM)
    return pl.pallas_call(
        body, name="allreduce_small", in_specs=[vm] * n_rows, out_specs=vm,
        out_shape=jax.ShapeDtypeStruct((8, D), F32),
        scratch_shapes=[pltpu.VMEM((N_DEV, 8, D), F32), pltpu.SemaphoreType.DMA((7,)), pltpu.SemaphoreType.DMA((7,))],
    )(*rows)


def _local_step(x, target, w_in_blk, w_out_full, sinks, lb_logits, rnn_norm, pre_norm, post_norm):
    T, D = x.shape
    nb = w_in_blk.shape[0]
    h, h_t = _prenorm(x, pre_norm)
    proj = _proj_in(h, w_in_blk)
    mixed, mixed_t = _attn_fwd(proj, sinks, D)
    mixed, mixed_t, o_raw, states = _rnn_fwd(proj, lb_logits, rnn_norm, mixed, mixed_t, D)
    y = _proj_out(mixed, w_out_full)
    dy, dout, g_post, sq_err = _loss_head(y, x, target, post_norm)
    dmix = _dmixed(dy, w_out_full)
    dw_out = _dw_out(mixed_t, dy)
    d_aq, d_ak, d_av, d_ag, d_sink = _attn_bwd(proj, sinks, dmix, D)
    d_rq, d_rf, d_ri, d_rg, d_lb, g_rnn = _rnn_bwd(proj, lb_logits, rnn_norm, o_raw, states, dmix, D)
    dproj = jnp.concatenate([d_aq, d_ak, d_av, d_ag, d_rq, d_rf, d_ri, d_rg], axis=1)
    dw_in = _dw_in(h_t, dproj, nb)
    dh = _dh(dproj, w_in_blk)
    grad_x, g_pre = _prenorm_bwd(x, dh, dout, pre_norm)
    n_q = D // ATTN_HEAD
    sink_row = jnp.pad(d_sink[:, 0, :2 * GQA].reshape(1, n_q), ((0, 0), (0, D - n_q)))
    return grad_x, dw_in, dw_out, [d_lb, g_rnn, g_pre, g_post, sq_err, sink_row]


def kernel(x, w_in, attn_sinks, lb_logits, rnn_norm, w_out, pre_norm, post_norm, loss_target, m_w_in, m_attn_sinks, m_lb_logits, m_rnn_norm, m_w_out, m_pre_norm, m_post_norm, v_w_in, v_attn_sinks, v_lb_logits, v_rnn_norm, v_w_out, v_pre_norm, v_post_norm):
    _, T, D = x.shape
    wd, ro = w_in.shape[2], w_out.shape[1]
    n_q = attn_sinks.shape[1]
    assert lb_logits.shape[0] == 2 and n_q == D // ATTN_HEAD and n_q <= LANES

    w_in_blk, w_out_blk = _gather_weights(_cast_bf16(w_in[0], "cast_w_in"), _cast_bf16(w_out[0], "cast_w_out"))
    grad_x, dw_in, dw_out, small_rows = _local_step(
        x[0], loss_target[0], w_in_blk, w_out_blk.reshape(N_DEV * ro, D), attn_sinks, lb_logits, rnn_norm,
        pre_norm, post_norm)

    me = _flat(*_me())
    land_out = _scatter_partials(dw_out.reshape(N_DEV, ro, D), "scatter_dw_out")
    land_in = _scatter_partials(dw_in, "scatter_dw_in")
    own_out = lax.dynamic_index_in_dim(dw_out.reshape(N_DEV, ro, D), me, 0, keepdims=False)
    own_in = lax.dynamic_index_in_dim(dw_in, me, 0, keepdims=False)
    g_wo, d_wo, nm_wo, nv_wo = _adamw_big(w_out[0], m_w_out[0], v_w_out[0], own_out, land_out, _pick(ro, 64), "adamw_w_out")
    g_wi, d_wi, nm_wi, nv_wi = _adamw_big(w_in[0], m_w_in[0], v_w_in[0], own_in, land_in, _pick(D, 128), "adamw_w_in")

    total = _allreduce_small(small_rows, D)
    pad = lambda a: jnp.pad(a, ((0, 0), (0, LANES - n_q)))
    moments = [pad(m_attn_sinks), m_lb_logits, m_rnn_norm, m_pre_norm, m_post_norm,
               pad(v_attn_sinks), v_lb_logits, v_rnn_norm, v_pre_norm, v_post_norm]
    res = _adamw_small(total, pad(attn_sinks), lb_logits, rnn_norm, pre_norm, post_norm, moments, D)
    loss = res[0][0, 0]
    small = [[res[1 + 4 * i + j] for i in range(5)] for j in range(4)]
    for j in range(4):
        small[j][0] = small[j][0][:, :n_q]

    def assemble(j, wi, wo):
        s = small[j]
        return [wi[None], s[0], s[1], s[2], wo[None], s[3], s[4]]

    return (loss, grad_x[None], *assemble(0, g_wi, g_wo), *assemble(1, d_wi, d_wo),
            *assemble(2, nm_wi, nm_wo), *assemble(3, nv_wi, nv_wo))
```

```python
import functools

import jax
import jax.numpy as jnp
from jax import lax
from jax.experimental import pallas as pl
from jax.experimental.pallas import tpu as pltpu

F32 = jnp.float32
BF16 = jnp.bfloat16

ATTN_HEAD = 64
GQA = 8
WINDOW = 128
RNN_HEAD = 128
CHUNK = 64
NORM_EPS = 1e-6
LANES = 128
N_DEV = 8

ADAM_LR = 0.001
ADAM_B1 = 0.9
ADAM_B2 = 0.999
ADAM_EPS = 1e-08
ADAM_WD = 0.01
ADAM_STEP = 10

VMEM_LIMIT = 56 * 1024 * 1024
MESH = pl.DeviceIdType.MESH

ANY_SPEC = pl.BlockSpec(memory_space=pl.ANY)

NT_DIMS = (((1,), (1,)), ((), ()))
NN_DIMS = (((1,), (0,)), ((), ()))
TN_DIMS = (((0,), (0,)), ((), ()))


def _params(*sem):
    return pltpu.CompilerParams(dimension_semantics=sem, vmem_limit_bytes=VMEM_LIMIT)


def _dot(a, b, dims):
    return lax.dot_general(a, b, dims, preferred_element_type=F32)


def _sigmoid(v):
    return 1.0 / (1.0 + jnp.exp(-v))


def _matmul(a, b, *, grid, a_spec, b_spec, o_spec, out_shape, trans_b, name, after=()):
    nk = grid[2]
    dims = NT_DIMS if trans_b else NN_DIMS
    n_after = len(after)

    def body(a_ref, b_ref, *rest):
        o_ref, scratch = rest[n_after], rest[n_after + 1:]
        prod = _dot(a_ref[...], b_ref[...], dims)
        if nk == 1:
            o_ref[...] = prod.astype(o_ref.dtype)
        else:
            acc_ref, = scratch
            k = pl.program_id(2)

            @pl.when(k == 0)
            def _():
                acc_ref[...] = prod

            @pl.when(k > 0)
            def _():
                acc_ref[...] += prod

            @pl.when(k == nk - 1)
            def _():
                o_ref[...] = acc_ref[...].astype(o_ref.dtype)

    scratch = [] if nk == 1 else [pltpu.VMEM(tuple(d for d in o_spec.block_shape if d is not None), F32)]
    return pl.pallas_call(
        body, name=name, grid=grid, in_specs=[a_spec, b_spec] + [ANY_SPEC] * n_after, out_specs=o_spec,
        out_shape=out_shape, scratch_shapes=scratch, compiler_params=_params("parallel", "parallel", "arbitrary"),
    )(a, b, *after)


def _pick(n, pref):
    t = min(n, pref)
    assert n % t == 0, (n, pref)
    return t


def _proj_in(h, w_blk):
    T, D = h.shape
    nb, _, wd = w_blk.shape
    tm, tn = _pick(T, 1024), _pick(wd, 640)
    per = wd // tn
    return _matmul(
        h, w_blk, grid=(T // tm, nb * per, 1),
        a_spec=pl.BlockSpec((tm, D), lambda i, j, k: (i, 0)),
        b_spec=pl.BlockSpec((None, D, tn), lambda i, j, k: (j // per, 0, j % per)),
        o_spec=pl.BlockSpec((tm, tn), lambda i, j, k: (i, j)),
        out_shape=jax.ShapeDtypeStruct((T, nb * wd), F32), trans_b=False, name="proj_in")


def _proj_out(mixed, w_out):
    T, E = mixed.shape
    D = w_out.shape[1]
    tm, tn = _pick(T, 512), _pick(D, 512)
    return _matmul(
        mixed, w_out, grid=(T // tm, D // tn, 1),
        a_spec=pl.BlockSpec((tm, E), lambda i, j, k: (i, 0)),
        b_spec=pl.BlockSpec((E, tn), lambda i, j, k: (0, j)),
        o_spec=pl.BlockSpec((tm, tn), lambda i, j, k: (i, j)),
        out_shape=jax.ShapeDtypeStruct((T, D), F32), trans_b=False, name="proj_out")


def _dmixed(dy, w_out):
    T, D = dy.shape
    E = w_out.shape[0]
    tm, tn = _pick(T, 1024), _pick(E, 512)
    return _matmul(
        dy, w_out, grid=(T // tm, E // tn, 1),
        a_spec=pl.BlockSpec((tm, D), lambda i, j, k: (i, 0)),
        b_spec=pl.BlockSpec((tn, D), lambda i, j, k: (j, 0)),
        o_spec=pl.BlockSpec((tm, tn), lambda i, j, k: (i, j)),
        out_shape=jax.ShapeDtypeStruct((T, E), F32), trans_b=True, name="dmixed")


def _dw_out(mixed_t, dy):
    E, T = mixed_t.shape
    D = dy.shape[1]
    tm, tn = _pick(E, 1024), _pick(D, 512)
    return _matmul(
        mixed_t, dy, grid=(E // tm, D // tn, 1),
        a_spec=pl.BlockSpec((tm, T), lambda i, j, k: (i, 0)),
        b_spec=pl.BlockSpec((T, tn), lambda i, j, k: (0, j)),
        o_spec=pl.BlockSpec((tm, tn), lambda i, j, k: (i, j)),
        out_shape=jax.ShapeDtypeStruct((E, D), BF16), trans_b=False, name="dw_out")


def _dw_in(h_t, dproj, nb):
    D, T = h_t.shape
    wd = dproj.shape[1] // nb
    tm, tn = _pick(D, 1024), _pick(wd, 640)
    per = wd // tn
    return _matmul(
        h_t, dproj, grid=(D // tm, nb * per, 1),
        a_spec=pl.BlockSpec((tm, T), lambda i, j, k: (i, 0)),
        b_spec=pl.BlockSpec((T, tn), lambda i, j, k: (0, j)),
        o_spec=pl.BlockSpec((None, tm, tn), lambda i, j, k: (j // per, i, j % per)),
        out_shape=jax.ShapeDtypeStruct((nb, D, wd), BF16), trans_b=False, name="dw_in")


def _dh(dproj, w_blk, after=()):
    T = dproj.shape[0]
    nb, D, wd = w_blk.shape
    tm, tn = _pick(T, 1024), _pick(D, 1024)
    return _matmul(
        dproj, w_blk, grid=(T // tm, D // tn, nb),
        a_spec=pl.BlockSpec((tm, wd), lambda i, j, k: (i, k)),
        b_spec=pl.BlockSpec((None, tn, wd), lambda i, j, k: (k, j, 0)),
        o_spec=pl.BlockSpec((tm, tn), lambda i, j, k: (i, j)),
        out_shape=jax.ShapeDtypeStruct((T, D), F32), trans_b=True, name="dh", after=after)


def _cast_bf16(w, name):
    R, C = w.shape
    tr = _pick(R, 256)

    def body(w_ref, o_ref):
        o_ref[...] = w_ref[...].astype(BF16)

    return pl.pallas_call(
        body, name=name, grid=(R // tr,), in_specs=[pl.BlockSpec((tr, C), lambda i: (i, 0))],
        out_specs=pl.BlockSpec((tr, C), lambda i: (i, 0)), out_shape=jax.ShapeDtypeStruct((R, C), BF16),
        compiler_params=_params("parallel"))(w)


def _prenorm(x, gain):
    T, D = x.shape
    tm = _pick(T, 256)

    def body(x_ref, g_ref, h_ref, ht_ref):
        xv = x_ref[...]
        r = lax.rsqrt(jnp.mean(xv * xv, axis=-1, keepdims=True) + NORM_EPS)
        h = xv * r * g_ref[...]
        h_ref[...] = h.astype(BF16)
        ht_ref[...] = h.T.astype(BF16)

    return pl.pallas_call(
        body, name="prenorm", grid=(T // tm,),
        in_specs=[pl.BlockSpec((tm, D), lambda i: (i, 0)), pl.BlockSpec((1, D), lambda i: (0, 0))],
        out_specs=[pl.BlockSpec((tm, D), lambda i: (i, 0)), pl.BlockSpec((D, tm), lambda i: (0, i))],
        out_shape=[jax.ShapeDtypeStruct((T, D), BF16), jax.ShapeDtypeStruct((D, T), BF16)],
        compiler_params=_params("parallel"))(x, gain)


def _norm_bwd(u, yn, r):
    return r * (u - yn * jnp.mean(u * yn, axis=-1, keepdims=True))


def _loss_head(y, x, target, gain):
    T, D = y.shape
    tm = _pick(T, 256)

    def body(y_ref, x_ref, t_ref, g_ref, dy_ref, dout_ref, gpost_ref, sq_ref):
        yv = y_ref[...]
        g = g_ref[...]
        r = lax.rsqrt(jnp.mean(yv * yv, axis=-1, keepdims=True) + NORM_EPS)
        yn = yv * r
        err = x_ref[...] + yn * g - t_ref[...]
        dout = err * (1.0 / D)
        dy_ref[...] = _norm_bwd(dout * g, yn, r).astype(BF16)
        dout_ref[...] = dout

        @pl.when(pl.program_id(0) == 0)
        def _():
            gpost_ref[...] = jnp.zeros_like(gpost_ref)
            sq_ref[...] = jnp.zeros_like(sq_ref)

        gpost_ref[...] += jnp.sum(dout * yn, axis=0, keepdims=True)
        sq_ref[...] += jnp.sum(err * err, axis=0, keepdims=True)

    row = pl.BlockSpec((tm, D), lambda i: (i, 0))
    vec = pl.BlockSpec((1, D), lambda i: (0, 0))
    return pl.pallas_call(
        body, name="loss_head", grid=(T // tm,), in_specs=[row, row, row, vec], out_specs=[row, row, vec, vec],
        out_shape=[jax.ShapeDtypeStruct((T, D), BF16), jax.ShapeDtypeStruct((T, D), F32),
                   jax.ShapeDtypeStruct((1, D), F32), jax.ShapeDtypeStruct((1, D), F32)],
        compiler_params=_params("arbitrary"))(y, x, target, gain)


def _prenorm_bwd(x, dh, dout, gain):
    T, D = x.shape
    tm = _pick(T, 256)

    def body(x_ref, dh_ref, dout_ref, g_ref, gx_ref, gpre_ref):
        xv = x_ref[...]
        dhv = dh_ref[...]
        r = lax.rsqrt(jnp.mean(xv * xv, axis=-1, keepdims=True) + NORM_EPS)
        xn = xv * r
        gx_ref[...] = dout_ref[...] + _norm_bwd(dhv * g_ref[...], xn, r)

        @pl.when(pl.program_id(0) == 0)
        def _():
            gpre_ref[...] = jnp.zeros_like(gpre_ref)

        gpre_ref[...] += jnp.sum(dhv * xn, axis=0, keepdims=True)

    row = pl.BlockSpec((tm, D), lambda i: (i, 0))
    vec = pl.BlockSpec((1, D), lambda i: (0, 0))
    return pl.pallas_call(
        body, name="prenorm_bwd", grid=(T // tm,), in_specs=[row, row, row, vec], out_specs=[row, vec],
        out_shape=[jax.ShapeDtypeStruct((T, D), F32), jax.ShapeDtypeStruct((1, D), F32)],
        compiler_params=_params("arbitrary"))(x, dh, dout, gain)


def _attn_masks(n):
    row = lax.broadcasted_iota(jnp.int32, (2 * WINDOW, 2 * WINDOW), 0) % WINDOW
    col = lax.broadcasted_iota(jnp.int32, (2 * WINDOW, 2 * WINDOW), 1)
    valid = (col > row) & (col <= row + WINDOW) & ((n > 0) | (col >= WINDOW))
    low = lax.broadcasted_iota(jnp.int32, (1, LANES), 1) < ATTN_HEAD
    top = lax.broadcasted_iota(jnp.int32, (2 * WINDOW, 1), 0) < WINDOW
    return valid, low, top


def _dup_half(pair, keep):
    return jnp.where(keep, pair, pltpu.roll(pair, ATTN_HEAD, 1))


def _fold_half(v):
    return v + pltpu.roll(v, ATTN_HEAD, 1)


def _attn_probs(qpair, k2, sink_lo, sink_hi, valid, low, top):
    q2 = jnp.concatenate([jnp.where(low, qpair, 0.0), jnp.where(low, 0.0, qpair)], axis=0).astype(BF16)
    s = _dot(q2, k2, NT_DIMS) * (ATTN_HEAD ** -0.5)
    s = jnp.where(valid, s, -jnp.inf)
    sink = jnp.where(top, sink_lo, sink_hi)
    m = jnp.maximum(jnp.max(s, axis=-1, keepdims=True), sink)
    p = jnp.exp(s - m)
    psink = jnp.exp(sink - m)
    inv = 1.0 / (jnp.sum(p, axis=-1, keepdims=True) + psink)
    return q2, p * inv, psink * inv


def _attn_specs(T, D, with_dmix):
    nqb = D // 512
    kb = D // LANES
    vb = kb + D // (8 * LANES)
    gb = (D + D // 4) // 512
    wide = lambda off: [pl.BlockSpec((WINDOW, 512), functools.partial(lambda o, e, jp, n: (n, o + 2 * jp + e), off, e))
                        for e in (0, 1)]
    cur = lambda off: pl.BlockSpec((WINDOW, LANES), functools.partial(lambda o, jp, n: (n, o + jp), off))
    prev = lambda off: pl.BlockSpec((WINDOW, LANES),
                                    functools.partial(lambda o, jp, n: (jnp.maximum(n - 1, 0), o + jp), off))
    del nqb
    return wide(0) + [cur(kb), prev(kb), cur(vb), prev(vb)] + wide(gb)


def _attn_fwd(proj, sinks, D):
    T = proj.shape[0]
    nb, njp = T // WINDOW, D // 1024

    def body(sink_ref, qlo_ref, qhi_ref, kc_ref, kp_ref, vc_ref, vp_ref, glo_ref, ghi_ref, mix_ref, mixt_ref):
        jp, n = pl.program_id(0), pl.program_id(1)
        valid, low, top = _attn_masks(n)
        kk = jnp.concatenate([kp_ref[...], kc_ref[...]], axis=0)
        vv = jnp.concatenate([vp_ref[...], vc_ref[...]], axis=0)
        for hj, (q_ref, g_ref) in enumerate(((qlo_ref, glo_ref), (qhi_ref, ghi_ref))):
            keep = low if hj == 0 else jnp.logical_not(low)
            k2 = _dup_half(kk, keep).astype(BF16)
            v2 = _dup_half(vv, keep).astype(BF16)
            for p in range(4):
                cols = slice(LANES * p, LANES * (p + 1))
                head = (2 * jp + hj) * GQA + 2 * p
                _, probs, _ = _attn_probs(q_ref[:, cols], k2, sink_ref[0, head], sink_ref[0, head + 1],
                                          valid, low, top)
                o2 = _dot(probs.astype(BF16), v2, NN_DIMS)
                opair = jnp.where(low, o2[:WINDOW], o2[WINDOW:])
                g = g_ref[:, cols]
                out = opair * (g * _sigmoid(g))
                oc = slice(512 * hj + LANES * p, 512 * hj + LANES * (p + 1))
                mix_ref[:, oc] = out.astype(BF16)
                mixt_ref[oc, :] = out.T.astype(BF16)

    return pl.pallas_call(
        body, name="attn_fwd", grid=(njp, nb),
        in_specs=[pl.BlockSpec(memory_space=pltpu.SMEM)] + _attn_specs(T, D, False),
        out_specs=[pl.BlockSpec((WINDOW, 1024), lambda jp, n: (n, jp)),
                   pl.BlockSpec((1024, WINDOW), lambda jp, n: (jp, n))],
        out_shape=[jax.ShapeDtypeStruct((T, 2 * D), BF16), jax.ShapeDtypeStruct((2 * D, T), BF16)],
        compiler_params=_params("parallel", "parallel"))(sinks, *([proj] * 8))


def _attn_bwd(proj, sinks, dmix, D, after=()):
    T = proj.shape[0]
    nb, njp = T // WINDOW, D // 1024

    n_after = len(after)

    def body(sink_ref, qlo_ref, qhi_ref, kc_ref, kp_ref, vc_ref, vp_ref, glo_ref, ghi_ref, dmix_ref, *rest):
        dq_ref, dk_ref, dv_ref, dg_ref, dsink_ref, kcarry_ref, vcarry_ref = rest[n_after:]
        jp, step = pl.program_id(0), pl.program_id(1)
        n = nb - 1 - step
        valid, low, top = _attn_masks(n)
        lane = lax.broadcasted_iota(jnp.int32, (1, LANES), 1)
        kk = jnp.concatenate([kp_ref[...], kc_ref[...]], axis=0)
        vv = jnp.concatenate([vp_ref[...], vc_ref[...]], axis=0)

        @pl.when(step == 0)
        def _():
            kcarry_ref[...] = jnp.zeros_like(kcarry_ref)
            vcarry_ref[...] = jnp.zeros_like(vcarry_ref)
            dsink_ref[...] = jnp.zeros_like(dsink_ref)

        dk_pair = jnp.zeros((2 * WINDOW, LANES), F32)
        dv_pair = jnp.zeros((2 * WINDOW, LANES), F32)
        dsink = jnp.zeros((1, LANES), F32)
        for hj, (q_ref, g_ref) in enumerate(((qlo_ref, glo_ref), (qhi_ref, ghi_ref))):
            keep = low if hj == 0 else jnp.logical_not(low)
            k2 = _dup_half(kk, keep).astype(BF16)
            v2 = _dup_half(vv, keep).astype(BF16)
            dk_head = jnp.zeros((2 * WINDOW, LANES), F32)
            dv_head = jnp.zeros((2 * WINDOW, LANES), F32)
            for p in range(4):
                cols = slice(LANES * p, LANES * (p + 1))
                oc = slice(512 * hj + LANES * p, 512 * hj + LANES * (p + 1))
                head = (2 * jp + hj) * GQA + 2 * p
                q2, probs, psink = _attn_probs(q_ref[:, cols], k2, sink_ref[0, head], sink_ref[0, head + 1],
                                               valid, low, top)
                pb = probs.astype(BF16)
                o2 = _dot(pb, v2, NN_DIMS)
                opair = jnp.where(low, o2[:WINDOW], o2[WINDOW:])
                g = g_ref[:, cols]
                sg = _sigmoid(g)
                dgated = dmix_ref[:, oc]
                dg_ref[:, oc] = (dgated * opair * (sg * (1.0 + g * (1.0 - sg)))).astype(BF16)
                do = dgated * (g * sg)
                do2 = jnp.concatenate([jnp.where(low, do, 0.0), jnp.where(low, 0.0, do)], axis=0).astype(BF16)
                dp = _dot(do2, v2, NT_DIMS)
                delta = jnp.sum(probs * dp, axis=-1, keepdims=True)
                ds = (probs * (dp - delta) * (ATTN_HEAD ** -0.5)).astype(BF16)
                dq2 = _dot(ds, k2, NN_DIMS)
                dq_ref[:, oc] = jnp.where(low, dq2[:WINDOW], dq2[WINDOW:]).astype(BF16)
                dk_head += _dot(ds, q2, TN_DIMS)
                dv_head += _dot(pb, do2, TN_DIMS)
                ps = psink * delta
                local = hj * GQA + 2 * p
                dsink -= jnp.where(lane == local, jnp.sum(ps[:WINDOW], axis=0, keepdims=True), 0.0)
                dsink -= jnp.where(lane == local + 1, jnp.sum(ps[WINDOW:], axis=0, keepdims=True), 0.0)
            dk_pair += jnp.where(keep, _fold_half(dk_head), 0.0)
            dv_pair += jnp.where(keep, _fold_half(dv_head), 0.0)
        dk_ref[...] = (dk_pair[WINDOW:] + kcarry_ref[...]).astype(BF16)
        dv_ref[...] = (dv_pair[WINDOW:] + vcarry_ref[...]).astype(BF16)
        kcarry_ref[...] = dk_pair[:WINDOW]
        vcarry_ref[...] = dv_pair[:WINDOW]
        dsink_ref[...] += dsink

    rev = lambda spec: pl.BlockSpec(spec.block_shape, functools.partial(
        lambda f, jp, s: f(jp, nb - 1 - s), spec.index_map))
    in_specs = [rev(s) for s in _attn_specs(T, D, True)]
    wide_out = pl.BlockSpec((WINDOW, 1024), lambda jp, s: (nb - 1 - s, jp))
    pair_out = pl.BlockSpec((WINDOW, LANES), lambda jp, s: (nb - 1 - s, jp))
    return pl.pallas_call(
        body, name="attn_bwd", grid=(njp, nb),
        in_specs=[pl.BlockSpec(memory_space=pltpu.SMEM)] + in_specs + [wide_out] + [ANY_SPEC] * n_after,
        out_specs=[wide_out, pair_out, pair_out, wide_out, pl.BlockSpec((None, 1, LANES), lambda jp, s: (jp, 0, 0))],
        out_shape=[jax.ShapeDtypeStruct((T, D), BF16), jax.ShapeDtypeStruct((T, D // 8), BF16),
                   jax.ShapeDtypeStruct((T, D // 8), BF16), jax.ShapeDtypeStruct((T, D), BF16),
                   jax.ShapeDtypeStruct((njp, 1, LANES), F32)],
        scratch_shapes=[pltpu.VMEM((WINDOW, LANES), F32), pltpu.VMEM((WINDOW, LANES), F32)],
        compiler_params=_params("parallel", "arbitrary"))(sinks, *([proj] * 8), dmix, *after)


RNN_TB = 512


def _split3(v):
    a = v.astype(BF16)
    r = v - a.astype(F32)
    b = r.astype(BF16)
    c = (r - b.astype(F32)).astype(BF16)
    return a, b, c


def _tri_sum(tri, v):
    a, b, c = _split3(v)
    return _dot(tri, a, NN_DIMS) + _dot(tri, b, NN_DIMS) + _dot(tri, c, NN_DIMS)


def _lower_bound(lb_ref):
    l0, l1 = lb_ref[0:1, :], lb_ref[1:2, :]
    m = jnp.maximum(l0, l1)
    e0, e1 = jnp.exp(l0 - m), jnp.exp(l1 - m)
    return e0 / (e0 + e1)


def _rnn_gates(rq, rf, lb):
    sq = _sigmoid(rq)
    sf = _sigmoid(rf)
    f = lb + (1.0 - lb) * sf
    return sq, sf, f


def _rnn_decays(g, tri):
    G = _tri_sum(tri, g)
    last = G[CHUNK - 1:CHUNK, :]
    mid = G[CHUNK // 2 - 1:CHUNK // 2, :]
    return G, jnp.exp(G), jnp.exp(G - mid), jnp.exp(mid - G), jnp.exp(last - G), jnp.exp(last)


def _chunk_masks():
    r = lax.broadcasted_iota(jnp.int32, (CHUNK, CHUNK), 0)
    c = lax.broadcasted_iota(jnp.int32, (CHUNK, CHUNK), 1)
    return r >= c, (r >= c).astype(BF16), (r <= c).astype(BF16)


def _rnn_specs(T, D, tb, rev):
    nt = T // tb
    base = (2 * D + D // 4) // LANES
    t_of = (lambda s: nt - 1 - s) if rev else (lambda s: s)
    cols = [pl.BlockSpec((tb, LANES), functools.partial(lambda o, h, s: (t_of(s), o + h), base + i * (D // LANES)))
            for i in range(4)]
    return cols, t_of


def _rnn_fwd(proj, lb_logits, rnn_norm, mixed, mixed_t, D):
    T = proj.shape[0]
    tb = _pick(T, RNN_TB)
    nt, nh, cpb = T // tb, D // RNN_HEAD, tb // CHUNK
    cols, _ = _rnn_specs(T, D, tb, False)

    def body(rq_ref, rf_ref, ri_ref, rg_ref, lb_ref, gain_ref, mix_in, mixt_in,
             mix_ref, mixt_ref, o_ref, st_ref, state_ref):
        del mix_in, mixt_in
        causal, tri, _ = _chunk_masks()
        lb = _lower_bound(lb_ref)

        @pl.when(pl.program_id(1) == 0)
        def _():
            state_ref[...] = jnp.zeros_like(state_ref)

        def chunk(c, carry):
            rows = pl.ds(pl.multiple_of(c * CHUNK, CHUNK), CHUNK)
            rq, rf, v, rg = rq_ref[rows, :], rf_ref[rows, :], ri_ref[rows, :], rg_ref[rows, :]
            sq, _, f = _rnn_gates(rq, rf, lb)
            q, k = rq * sq, 1.0 - f
            _, eG, eq, ek, ekl, elast = _rnn_decays(jnp.log(f), tri)
            st = state_ref[...]
            st_ref[c] = st
            vb = v.astype(BF16)
            att = jnp.where(causal, _dot((q * eq).astype(BF16), (k * ek).astype(BF16), NT_DIMS), 0.0)
            o = _dot((q * eG).astype(BF16), st.astype(BF16), NT_DIMS) + _dot(att.astype(BF16), vb, NN_DIMS)
            state_ref[...] = st * elast + _dot(vb, (k * ekl).astype(BF16), TN_DIMS)
            o_ref[rows, :] = o
            r = lax.rsqrt(jnp.mean(o * o, axis=-1, keepdims=True) + NORM_EPS)
            out = (o * r * gain_ref[...]) * (rg * _sigmoid(rg))
            mix_ref[rows, :] = out.astype(BF16)
            return carry

        lax.fori_loop(0, cpb, chunk, 0)
        mixt_ref[...] = mix_ref[...].astype(F32).T.astype(BF16)

    vec2 = pl.BlockSpec((2, LANES), lambda h, s: (0, h))
    vec1 = pl.BlockSpec((1, LANES), lambda h, s: (0, h))
    anyspec = pl.BlockSpec(memory_space=pl.ANY)
    return pl.pallas_call(
        body, name="rnn_fwd", grid=(nh, nt),
        in_specs=cols + [vec2, vec1, anyspec, anyspec],
        out_specs=[pl.BlockSpec((tb, LANES), lambda h, s: (s, D // LANES + h)),
                   pl.BlockSpec((LANES, tb), lambda h, s: (D // LANES + h, s)),
                   pl.BlockSpec((tb, LANES), lambda h, s: (s, h)),
                   pl.BlockSpec((None, cpb, RNN_HEAD, RNN_HEAD), lambda h, s: (h, s, 0, 0))],
        out_shape=[jax.ShapeDtypeStruct(mixed.shape, BF16), jax.ShapeDtypeStruct(mixed_t.shape, BF16),
                   jax.ShapeDtypeStruct((T, D), F32),
                   jax.ShapeDtypeStruct((nh, T // CHUNK, RNN_HEAD, RNN_HEAD), F32)],
        scratch_shapes=[pltpu.VMEM((RNN_HEAD, RNN_HEAD), F32)],
        input_output_aliases={6: 0, 7: 1},
        compiler_params=_params("parallel", "arbitrary"))(proj, proj, proj, proj, lb_logits, rnn_norm, mixed, mixed_t)


def _rnn_bwd(proj, lb_logits, rnn_norm, o_raw, states, dmix, D):
    T = proj.shape[0]
    tb = _pick(T, RNN_TB)
    nt, nh, cpb = T // tb, D // RNN_HEAD, tb // CHUNK
    cols, t_of = _rnn_specs(T, D, tb, True)

    def body(rq_ref, rf_ref, ri_ref, rg_ref, lb_ref, gain_ref, o_ref, st_ref, dmix_ref,
             drq_ref, drf_ref, dri_ref, drg_ref, dlb_ref, dgain_ref, dstate_ref):
        causal, tri, tri_t = _chunk_masks()
        lb = _lower_bound(lb_ref)
        gain = gain_ref[...]

        @pl.when(pl.program_id(1) == 0)
        def _():
            dstate_ref[...] = jnp.zeros_like(dstate_ref)
            dlb_ref[...] = jnp.zeros_like(dlb_ref)
            dgain_ref[...] = jnp.zeros_like(dgain_ref)

        def chunk(i, carry):
            c = cpb - 1 - i
            rows = pl.ds(pl.multiple_of(c * CHUNK, CHUNK), CHUNK)
            rq, rf, v, rg = rq_ref[rows, :], rf_ref[rows, :], ri_ref[rows, :], rg_ref[rows, :]
            o, dgated = o_ref[rows, :], dmix_ref[rows, :]
            sgt = _sigmoid(rg)
            r = lax.rsqrt(jnp.mean(o * o, axis=-1, keepdims=True) + NORM_EPS)
            on = o * r
            drg_ref[rows, :] = (dgated * (on * gain) * (sgt * (1.0 + rg * (1.0 - sgt)))).astype(BF16)
            d_on = dgated * (rg * sgt)
            dgain_ref[...] += jnp.sum(d_on * on, axis=0, keepdims=True)
            do = _norm_bwd(d_on * gain, on, r)
            sq, sf, f = _rnn_gates(rq, rf, lb)
            q, k = rq * sq, 1.0 - f
            _, eG, eq, ek, ekl, elast = _rnn_decays(jnp.log(f), tri)
            st, dst = st_ref[c], dstate_ref[...]
            qg, qm, km, kl = q * eG, q * eq, k * ek, k * ekl
            qgb, qmb, kmb, klb = qg.astype(BF16), qm.astype(BF16), km.astype(BF16), kl.astype(BF16)
            dob, vb, dstb = do.astype(BF16), v.astype(BF16), dst.astype(BF16)
            att = jnp.where(causal, _dot(qmb, kmb, NT_DIMS), 0.0).astype(BF16)
            datt = jnp.where(causal, _dot(dob, vb, NT_DIMS), 0.0).astype(BF16)
            dqg = _dot(dob, st.astype(BF16), NN_DIMS)
            dqm = _dot(datt, kmb, NN_DIMS)
            dkm = _dot(datt, qmb, TN_DIMS)
            dkl = _dot(vb, dstb, NN_DIMS)
            dri_ref[rows, :] = (_dot(att, dob, TN_DIMS) + _dot(klb, dstb, NT_DIMS)).astype(BF16)
            dstate_ref[...] = dst * elast + _dot(dob, qgb, TN_DIMS)
            dq = dqg * eG + dqm * eq
            dk = dkm * ek + dkl * ekl
            dG = dqg * qg + dqm * qmb.astype(F32) - dkm * kmb.astype(F32) - dkl * kl
            dlast = jnp.sum(dkl * kl, axis=0, keepdims=True) + jnp.sum(dst * st, axis=0, keepdims=True) * elast
            last_row = lax.broadcasted_iota(jnp.int32, (CHUNK, 1), 0) == CHUNK - 1
            dg = _tri_sum(tri_t, dG + jnp.where(last_row, dlast, 0.0))
            df = dg / f - dk
            drq_ref[rows, :] = (dq * (sq * (1.0 + rq * (1.0 - sq)))).astype(BF16)
            drf_ref[rows, :] = (df * (1.0 - lb) * (sf * (1.0 - sf))).astype(BF16)
            dlb_ref[...] += jnp.sum(df * (1.0 - sf), axis=0, keepdims=True)
            return carry

        lax.fori_loop(0, cpb, chunk, 0)

    vec2 = pl.BlockSpec((2, LANES), lambda h, s: (0, h))
    vec1 = pl.BlockSpec((1, LANES), lambda h, s: (0, h))
    blk = pl.BlockSpec((tb, LANES), lambda h, s: (t_of(s), h))
    return pl.pallas_call(
        body, name="rnn_bwd", grid=(nh, nt),
        in_specs=cols + [vec2, vec1, blk,
                         pl.BlockSpec((None, cpb, RNN_HEAD, RNN_HEAD), lambda h, s: (h, t_of(s), 0, 0)),
                         pl.BlockSpec((tb, LANES), lambda h, s: (t_of(s), D // LANES + h))],
        out_specs=[blk, blk, blk, blk, vec1, vec1],
        out_shape=[jax.ShapeDtypeStruct((T, D), BF16)] * 4 + [jax.ShapeDtypeStruct((1, D), F32)] * 2,
        scratch_shapes=[pltpu.VMEM((RNN_HEAD, RNN_HEAD), F32)],
        compiler_params=_params("parallel", "arbitrary"))(proj, proj, proj, proj, lb_logits, rnn_norm, o_raw, states, dmix)


def _adamw(w, g, m, v):
    m = ADAM_B1 * m + (1.0 - ADAM_B1) * g
    v = ADAM_B2 * v + (1.0 - ADAM_B2) * (g * g)
    m_hat = m / (1.0 - ADAM_B1 ** ADAM_STEP)
    v_hat = v / (1.0 - ADAM_B2 ** ADAM_STEP)
    delta = -ADAM_LR * (m_hat / (jnp.sqrt(v_hat) + ADAM_EPS) + ADAM_WD * w)
    return delta, m, v


def _adamw_big(w, m, v, own, landed, tr, name):
    R, C = w.shape
    n_land = landed.shape[0]

    def body(w_ref, m_ref, v_ref, own_ref, land_ref, g_ref, d_ref, nm_ref, nv_ref):
        g = own_ref[...].astype(F32)
        for k in range(n_land):
            g = g + land_ref[k].astype(F32)
        delta, nm, nv = _adamw(w_ref[...], g, m_ref[...], v_ref[...])
        g_ref[...] = g
        d_ref[...] = delta
        nm_ref[...] = nm
        nv_ref[...] = nv

    blk = pl.BlockSpec((tr, C), lambda i: (i, 0))
    return pl.pallas_call(
        body, name=name, grid=(R // tr,),
        in_specs=[blk, blk, blk, blk, pl.BlockSpec((n_land, tr, C), lambda i: (0, i, 0))],
        out_specs=[blk] * 4, out_shape=[jax.ShapeDtypeStruct((R, C), F32)] * 4,
        compiler_params=_params("parallel"))(w, m, v, own, landed)


def _adamw_small(total, sinks, lb_logits, rnn_norm, pre_norm, post_norm, moments, D):
    params = [sinks, lb_logits, rnn_norm, pre_norm, post_norm]

    def body(tot_ref, *refs):
        p_refs, m_refs, v_refs = refs[0:5], refs[5:10], refs[10:15]
        loss_ref, outs = refs[15], refs[16:]
        tot = tot_ref[...]
        l0, l1 = p_refs[1][0:1, :], p_refs[1][1:2, :]
        mx = jnp.maximum(l0, l1)
        e0, e1 = jnp.exp(l0 - mx), jnp.exp(l1 - mx)
        p0, p1 = e0 / (e0 + e1), e1 / (e0 + e1)
        dlb = tot[0:1, :]
        grads = [tot[5:6, 0:LANES], jnp.concatenate([dlb * p0 * (1.0 - p0), -dlb * p0 * p1], axis=0),
                 tot[1:2, :], tot[2:3, :], tot[3:4, :]]
        loss_ref[...] = 0.5 / D * jnp.sum(tot[4:5, :], axis=-1, keepdims=True)
        for i, g in enumerate(grads):
            delta, nm, nv = _adamw(p_refs[i][...], g, m_refs[i][...], v_refs[i][...])
            outs[4 * i][...] = g
            outs[4 * i + 1][...] = delta
            outs[4 * i + 2][...] = nm
            outs[4 * i + 3][...] = nv

    out_shape = [jax.ShapeDtypeStruct((1, 1), F32)]
    for p in params:
        out_shape += [jax.ShapeDtypeStruct(p.shape, F32)] * 4
    return pl.pallas_call(body, name="adamw_small", out_shape=out_shape)(total, *params, *moments)


def _me():
    return lax.axis_index("x"), lax.axis_index("y"), lax.axis_index("c")


def _flat(px, py, pc):
    return 4 * px + 2 * py + pc


def _gather_weights(wi, wo):
    def body(wi_ref, wo_ref, gi_ref, go_ref, send_sems, recv_sems, local_sems):
        x, y, c = _me()
        me, sibling = (x, y, c), (x, y, 1 - c)
        chips = [(1 - x, y), (x, 1 - y), (1 - x, 1 - y)]

        def copies(k, block, to, own):
            out = []
            for a, (src, dst) in enumerate(((wi_ref, gi_ref), (wo_ref, go_ref))):
                slot = dst.at[_flat(*block)]
                out.append(pltpu.make_async_remote_copy(
                    src_ref=src if own else slot, dst_ref=slot, send_sem=send_sems.at[a, k],
                    recv_sem=recv_sems.at[a, k], device_id=to, device_id_type=MESH))
            return out

        mine = [pltpu.make_async_copy(wi_ref, gi_ref.at[_flat(*me)], local_sems.at[0]),
                pltpu.make_async_copy(wo_ref, go_ref.at[_flat(*me)], local_sems.at[1])]
        for cp in mine:
            cp.start()
        first = copies(0, me, sibling, True)
        for j, chip in enumerate(chips):
            first += copies(1 + j, me, (*chip, c), True)
        for cp in first:
            cp.start()
        passed = []
        for j, chip in enumerate(chips):
            for cp in copies(1 + j, (*chip, c), me, False):
                cp.wait_recv()
            fwd = copies(4 + j, (*chip, c), sibling, False)
            for cp in fwd:
                cp.start()
            passed += fwd
        for cp in copies(0, sibling, me, False):
            cp.wait_recv()
        for j, chip in enumerate(chips):
            for cp in copies(4 + j, (*chip, 1 - c), me, False):
                cp.wait_recv()
        for cp in first + passed:
            cp.wait_send()
        for cp in mine:
            cp.wait()

    anyspec = pl.BlockSpec(memory_space=pl.ANY)
    return pl.pallas_call(
        body, name="gather_weights", in_specs=[anyspec, anyspec], out_specs=[anyspec, anyspec],
        out_shape=[jax.ShapeDtypeStruct((N_DEV,) + wi.shape, BF16), jax.ShapeDtypeStruct((N_DEV,) + wo.shape, BF16)],
        scratch_shapes=[pltpu.SemaphoreType.DMA((2, 7)), pltpu.SemaphoreType.DMA((2, 7)), pltpu.SemaphoreType.DMA((2,))],
    )(wi, wo)


def _peers():
    x, y, c = _me()
    out = []
    for k in range(1, N_DEV):
        fx, fy, fc = (k >> 2) & 1, (k >> 1) & 1, k & 1
        out.append(((1 - x) if fx else x, (1 - y) if fy else y, (1 - c) if fc else c))
    return out


def _scatter_partials(part, name):
    def body(part_ref, land_ref, send_sems, recv_sems):
        copies = []
        for k, peer in enumerate(_peers()):
            copies.append(pltpu.make_async_remote_copy(
                src_ref=part_ref.at[_flat(*peer)], dst_ref=land_ref.at[k], send_sem=send_sems.at[k],
                recv_sem=recv_sems.at[k], device_id=peer, device_id_type=MESH))
        for cp in copies:
            cp.start()
        for cp in copies:
            cp.wait_recv()
        for cp in copies:
            cp.wait_send()

    anyspec = pl.BlockSpec(memory_space=pl.ANY)
    return pl.pallas_call(
        body, name=name, in_specs=[anyspec], out_specs=anyspec,
        out_shape=jax.ShapeDtypeStruct((N_DEV - 1,) + part.shape[1:], part.dtype),
        scratch_shapes=[pltpu.SemaphoreType.DMA((7,)), pltpu.SemaphoreType.DMA((7,))],
    )(part)


HBM_SPEC = pl.BlockSpec(memory_space=pltpu.HBM)
SEM_SPEC = pl.BlockSpec(memory_space=pltpu.SEMAPHORE)
EFFECT = pltpu.SideEffectType.DATAFLOW_SIDE_EFFECTING


def _scatter_copies(part_ref, land_ref, send_sems, recv_sems):
    return [pltpu.make_async_remote_copy(
        src_ref=part_ref.at[_flat(*peer)], dst_ref=land_ref.at[k], send_sem=send_sems.at[k],
        recv_sem=recv_sems.at[k], device_id=peer, device_id_type=MESH) for k, peer in enumerate(_peers())]


def _scatter_start(part, name):
    def body(part_ref, land_ref, send_sems, recv_sems, part_thru, land_thru, token):
        del part_thru, land_thru
        for cp in _scatter_copies(part_ref, land_ref, send_sems, recv_sems):
            cp.start()
        token[...] = jnp.zeros_like(token)

    land_shape = (N_DEV - 1,) + part.shape[1:]
    return pl.pallas_call(
        body, name=name,
        out_shape=(pltpu.SemaphoreType.DMA((7,)), pltpu.SemaphoreType.DMA((7,)), pltpu.HBM(part.shape, part.dtype),
                   pltpu.HBM(land_shape, part.dtype), jax.ShapeDtypeStruct((8, LANES), F32)),
        in_specs=(HBM_SPEC, HBM_SPEC),
        out_specs=(SEM_SPEC, SEM_SPEC, HBM_SPEC, HBM_SPEC, pl.BlockSpec(memory_space=pltpu.VMEM)),
        input_output_aliases={0: 2, 1: 3},
        compiler_params=pltpu.CompilerParams(has_side_effects=EFFECT),
    )(pltpu.with_memory_space_constraint(part, pltpu.HBM),
      pltpu.with_memory_space_constraint(lax.empty(land_shape, part.dtype), pltpu.HBM))


def _scatter_wait(send_sems, recv_sems, part_thru, land_thru, after, name):
    def body(part_ref, land_ref, send_sems, recv_sems, after_ref, part_out, land_out):
        del after_ref, part_out, land_out
        copies = _scatter_copies(part_ref, land_ref, send_sems, recv_sems)
        for cp in copies:
            cp.wait_send()
        for cp in copies:
            cp.wait_recv()

    return pl.pallas_call(
        body, name=name,
        out_shape=(pltpu.HBM(part_thru.shape, part_thru.dtype), pltpu.HBM(land_thru.shape, land_thru.dtype)),
        in_specs=(HBM_SPEC, HBM_SPEC, SEM_SPEC, SEM_SPEC, pl.BlockSpec(memory_space=pl.ANY)),
        out_specs=(HBM_SPEC, HBM_SPEC), input_output_aliases={0: 0, 1: 1},
        compiler_params=pltpu.CompilerParams(has_side_effects=EFFECT),
    )(part_thru, land_thru, send_sems, recv_sems, after)


def _allreduce_small(rows, D):
    n_rows = len(rows)

    def body(*refs):
        in_refs, out_ref, all_ref = refs[:n_rows], refs[n_rows], refs[n_rows + 1]
        send_sems, recv_sems = refs[n_rows + 2], refs[n_rows + 3]
        me = _flat(*_me())
        mine = all_ref.at[me]
        mine[...] = jnp.zeros((8, D), F32)
        for i, r in enumerate(in_refs):
            mine[i:i + 1, :] = r[...]
        copies = []
        for k, peer in enumerate(_peers()):
            copies.append(pltpu.make_async_remote_copy(
                src_ref=mine, dst_ref=mine, send_sem=send_sems.at[k], recv_sem=recv_sems.at[k],
                device_id=peer, device_id_type=MESH))
        for cp in copies:
            cp.start()
        for cp in copies:
            cp.wait_recv()
        for cp in copies:
            cp.wait_send()
        tot = all_ref[0]
        for d in range(1, N_DEV):
            tot = tot + all_ref[d]
        out_ref[...] = tot

    vm = pl.BlockSpec(memory_space=pltpu.VMEM)
    return pl.pallas_call(
        body, name="allreduce_small", in_specs=[vm] * n_rows, out_specs=vm,
        out_shape=jax.ShapeDtypeStruct((8, D), F32),
        scratch_shapes=[pltpu.VMEM((N_DEV, 8, D), F32), pltpu.SemaphoreType.DMA((7,)), pltpu.SemaphoreType.DMA((7,))],
    )(*rows)


def _local_step(x, target, w_in_blk, w_out_full, sinks, lb_logits, rnn_norm, pre_norm, post_norm,
                send_dw_out=None, send_dw_in=None):
    T, D = x.shape
    nb = w_in_blk.shape[0]
    h, h_t = _prenorm(x, pre_norm)
    proj = _proj_in(h, w_in_blk)
    mixed, mixed_t = _attn_fwd(proj, sinks, D)
    mixed, mixed_t, o_raw, states = _rnn_fwd(proj, lb_logits, rnn_norm, mixed, mixed_t, D)
    y = _proj_out(mixed, w_out_full)
    dy, dout, g_post, sq_err = _loss_head(y, x, target, post_norm)
    dmix = _dmixed(dy, w_out_full)
    dw_out, after = _dw_out(mixed_t, dy), ()
    if send_dw_out is not None:
        dw_out, token = send_dw_out(dw_out)
        after = (token,)
    d_aq, d_ak, d_av, d_ag, d_sink = _attn_bwd(proj, sinks, dmix, D, after)
    d_rq, d_rf, d_ri, d_rg, d_lb, g_rnn = _rnn_bwd(proj, lb_logits, rnn_norm, o_raw, states, dmix, D)
    dproj = jnp.concatenate([d_aq, d_ak, d_av, d_ag, d_rq, d_rf, d_ri, d_rg], axis=1)
    dw_in, after = _dw_in(h_t, dproj, nb), ()
    if send_dw_in is not None:
        dw_in, token = send_dw_in(dw_in)
        after = (token,)
    dh = _dh(dproj, w_in_blk, after)
    grad_x, g_pre = _prenorm_bwd(x, dh, dout, pre_norm)
    n_q = D // ATTN_HEAD
    sink_row = jnp.pad(d_sink[:, 0, :2 * GQA].reshape(1, n_q), ((0, 0), (0, D - n_q)))
    return grad_x, dw_in, dw_out, [d_lb, g_rnn, g_pre, g_post, sq_err, sink_row]


def kernel(x, w_in, attn_sinks, lb_logits, rnn_norm, w_out, pre_norm, post_norm, loss_target, m_w_in, m_attn_sinks, m_lb_logits, m_rnn_norm, m_w_out, m_pre_norm, m_post_norm, v_w_in, v_attn_sinks, v_lb_logits, v_rnn_norm, v_w_out, v_pre_norm, v_post_norm):
    _, T, D = x.shape
    wd, ro = w_in.shape[2], w_out.shape[1]
    n_q = attn_sinks.shape[1]
    assert lb_logits.shape[0] == 2 and n_q == D // ATTN_HEAD and n_q <= LANES

    def send_dw_out(dw):
        *state, token = _scatter_start(dw.reshape(N_DEV, ro, D), "scatter_dw_out_start")
        return state, token

    def send_dw_in(dw):
        *state, token = _scatter_start(dw, "scatter_dw_in_start")
        return state, token

    w_in_blk, w_out_blk = _gather_weights(_cast_bf16(w_in[0], "cast_w_in"), _cast_bf16(w_out[0], "cast_w_out"))
    grad_x, sent_in, sent_out, small_rows = _local_step(
        x[0], loss_target[0], w_in_blk, w_out_blk.reshape(N_DEV * ro, D), attn_sinks, lb_logits, rnn_norm,
        pre_norm, post_norm, send_dw_out, send_dw_in)

    total = _allreduce_small(small_rows, D)
    pad = lambda a: jnp.pad(a, ((0, 0), (0, LANES - n_q)))
    moments = [pad(m_attn_sinks), m_lb_logits, m_rnn_norm, m_pre_norm, m_post_norm,
               pad(v_attn_sinks), v_lb_logits, v_rnn_norm, v_pre_norm, v_post_norm]
    res = _adamw_small(total, pad(attn_sinks), lb_logits, rnn_norm, pre_norm, post_norm, moments, D)

    me = _flat(*_me())
    dw_out, land_out = _scatter_wait(*sent_out, grad_x, "scatter_dw_out_wait")
    own_out = lax.dynamic_index_in_dim(dw_out, me, 0, keepdims=False)
    g_wo, d_wo, nm_wo, nv_wo = _adamw_big(w_out[0], m_w_out[0], v_w_out[0], own_out, land_out, _pick(ro, 64), "adamw_w_out")
    dw_in, land_in = _scatter_wait(*sent_in, g_wo, "scatter_dw_in_wait")
    own_in = lax.dynamic_index_in_dim(dw_in, me, 0, keepdims=False)
    g_wi, d_wi, nm_wi, nv_wi = _adamw_big(w_in[0], m_w_in[0], v_w_in[0], own_in, land_in, _pick(D, 128), "adamw_w_in")
    loss = res[0][0, 0]
    small = [[res[1 + 4 * i + j] for i in range(5)] for j in range(4)]
    for j in range(4):
        small[j][0] = small[j][0][:, :n_q]

    def assemble(j, wi, wo):
        s = small[j]
        return [wi[None], s[0], s[1], s[2], wo[None], s[3], s[4]]

    return (loss, grad_x[None], *assemble(0, g_wi, g_wo), *assemble(1, d_wi, d_wo),
            *assemble(2, nm_wi, nm_wo), *assemble(3, nv_wi, nv_wo))
```

```python
import functools

import jax
import jax.numpy as jnp
from jax import lax
from jax.experimental import pallas as pl
from jax.experimental.pallas import tpu as pltpu

F32 = jnp.float32
BF16 = jnp.bfloat16

ATTN_HEAD = 64
GQA = 8
WINDOW = 128
RNN_HEAD = 128
CHUNK = 64
NORM_EPS = 1e-6
LANES = 128
N_DEV = 8

ADAM_LR = 0.001
ADAM_B1 = 0.9
ADAM_B2 = 0.999
ADAM_EPS = 1e-08
ADAM_WD = 0.01
ADAM_STEP = 10

VMEM_LIMIT = 56 * 1024 * 1024
MESH = pl.DeviceIdType.MESH
ANY_SPEC = pl.BlockSpec(memory_space=pl.ANY)
HBM_SPEC = pl.BlockSpec(memory_space=pltpu.HBM)
SEM_SPEC = pl.BlockSpec(memory_space=pltpu.SEMAPHORE)
EFFECT = pltpu.SideEffectType.DATAFLOW_SIDE_EFFECTING

NT_DIMS = (((1,), (1,)), ((), ()))
NN_DIMS = (((1,), (0,)), ((), ()))
TN_DIMS = (((0,), (0,)), ((), ()))


def _params(*sem):
    return pltpu.CompilerParams(dimension_semantics=sem, vmem_limit_bytes=VMEM_LIMIT)


def _dot(a, b, dims):
    return lax.dot_general(a, b, dims, preferred_element_type=F32)


def _sigmoid(v):
    return 1.0 / (1.0 + jnp.exp(-v))


def _pick(n, pref):
    t = min(n, pref)
    assert n % t == 0, (n, pref)
    return t


def _matmul(ids, a, b, *, grid, a_spec, b_spec, o_spec, out_shape, trans_b, name, after=(), prev=None):
    nk = grid[2]
    dims = NT_DIMS if trans_b else NN_DIMS
    n_skip = len(after) + (prev is not None)

    def body(ids_ref, a_ref, b_ref, *rest):
        del ids_ref
        o_ref, scratch = rest[n_skip], rest[n_skip + 1:]
        prod = _dot(a_ref[...], b_ref[...], dims)
        if nk == 1:
            o_ref[...] = prod.astype(o_ref.dtype)
        else:
            acc_ref, = scratch
            k = pl.program_id(2)

            @pl.when(k == 0)
            def _():
                acc_ref[...] = prod

            @pl.when(k > 0)
            def _():
                acc_ref[...] += prod

            @pl.when(k == nk - 1)
            def _():
                o_ref[...] = acc_ref[...].astype(o_ref.dtype)

    scratch = [] if nk == 1 else [pltpu.VMEM(tuple(d for d in o_spec.block_shape if d is not None), F32)]
    extra = list(after) + ([prev] if prev is not None else [])
    aliases = {3 + len(after): 0} if prev is not None else {}
    return pl.pallas_call(
        body, name=name, out_shape=out_shape, input_output_aliases=aliases,
        grid_spec=pltpu.PrefetchScalarGridSpec(
            num_scalar_prefetch=1, grid=grid, in_specs=[a_spec, b_spec] + [ANY_SPEC] * len(extra),
            out_specs=o_spec, scratch_shapes=scratch),
        compiler_params=_params("parallel", "parallel", "arbitrary"),
    )(ids, a, b, *extra)


def _proj_in(ids, h, w_roles, r0, nr, prev, name, after=()):
    T, D = h.shape
    n_roles, _, wd = w_roles.shape
    tm, tn = _pick(T, 1024), _pick(wd, 640)
    per = wd // tn
    return _matmul(
        ids, h, w_roles, grid=(T // tm, nr * per, 1),
        a_spec=pl.BlockSpec((tm, D), lambda i, j, k, ids: (i, 0)),
        b_spec=pl.BlockSpec((None, D, tn), lambda i, j, k, ids: (r0 + j // per, 0, j % per)),
        o_spec=pl.BlockSpec((tm, tn), lambda i, j, k, ids: (i, ids[r0 + j // per] * per + j % per)),
        out_shape=jax.ShapeDtypeStruct((T, n_roles * wd), F32), trans_b=False, name=name, after=after, prev=prev)


def _proj_out(ids, mixed, wo_roles):
    T = mixed.shape[0]
    n_roles, R, D = wo_roles.shape
    tm, tn = _pick(T, 2048), _pick(D, 1024)
    return _matmul(
        ids, mixed, wo_roles, grid=(T // tm, D // tn, n_roles),
        a_spec=pl.BlockSpec((tm, R), lambda i, j, k, ids: (i, ids[k])),
        b_spec=pl.BlockSpec((None, R, tn), lambda i, j, k, ids: (k, 0, j)),
        o_spec=pl.BlockSpec((tm, tn), lambda i, j, k, ids: (i, j)),
        out_shape=jax.ShapeDtypeStruct((T, D), F32), trans_b=False, name="proj_out")


def _dmixed(ids, dy, wo_roles):
    T, D = dy.shape
    n_roles, R, _ = wo_roles.shape
    tm = _pick(T, 1024)
    return _matmul(
        ids, dy, wo_roles, grid=(T // tm, n_roles, 1),
        a_spec=pl.BlockSpec((tm, D), lambda i, j, k, ids: (i, 0)),
        b_spec=pl.BlockSpec((None, R, D), lambda i, j, k, ids: (j, 0, 0)),
        o_spec=pl.BlockSpec((tm, R), lambda i, j, k, ids: (i, ids[j])),
        out_shape=jax.ShapeDtypeStruct((T, n_roles * R), F32), trans_b=True, name="dmixed")


def _dw_out(ids, mixed_t, dy, n_roles):
    E, T = mixed_t.shape
    D = dy.shape[1]
    R = E // n_roles
    tn = _pick(D, 512)
    return _matmul(
        ids, mixed_t, dy, grid=(n_roles, D // tn, 1),
        a_spec=pl.BlockSpec((R, T), lambda i, j, k, ids: (ids[i], 0)),
        b_spec=pl.BlockSpec((T, tn), lambda i, j, k, ids: (0, j)),
        o_spec=pl.BlockSpec((None, R, tn), lambda i, j, k, ids: (i, 0, j)),
        out_shape=jax.ShapeDtypeStruct((n_roles, R, D), BF16), trans_b=False, name="dw_out")


def _dw_in(ids, h_t, dproj, n_roles, r0, nr, name, after=()):
    D, T = h_t.shape
    wd = dproj.shape[1] // n_roles
    tm, tn = _pick(D, 1024), _pick(wd, 640)
    per = wd // tn
    return _matmul(
        ids, h_t, dproj, grid=(D // tm, nr * per, 1),
        a_spec=pl.BlockSpec((tm, T), lambda i, j, k, ids: (i, 0)),
        b_spec=pl.BlockSpec((T, tn), lambda i, j, k, ids: (0, ids[r0 + j // per] * per + j % per)),
        o_spec=pl.BlockSpec((None, tm, tn), lambda i, j, k, ids: (j // per, i, j % per)),
        out_shape=jax.ShapeDtypeStruct((nr, D, wd), BF16), trans_b=False, name=name, after=after)


def _dh(ids, dproj, w_roles, after=()):
    T = dproj.shape[0]
    n_roles, D, wd = w_roles.shape
    tm, tn = _pick(T, 1024), _pick(D, 1024)
    return _matmul(
        ids, dproj, w_roles, grid=(T // tm, D // tn, n_roles),
        a_spec=pl.BlockSpec((tm, wd), lambda i, j, k, ids: (i, ids[k])),
        b_spec=pl.BlockSpec((None, tn, wd), lambda i, j, k, ids: (k, j, 0)),
        o_spec=pl.BlockSpec((tm, tn), lambda i, j, k, ids: (i, j)),
        out_shape=jax.ShapeDtypeStruct((T, D), F32), trans_b=True, name="dh", after=after)


def _cast_slot0(w, n_roles, name):
    R, C = w.shape
    tr = _pick(R, 256)

    def body(w_ref, o_ref):
        o_ref[...] = w_ref[...].astype(BF16)

    return pl.pallas_call(
        body, name=name, grid=(R // tr,), in_specs=[pl.BlockSpec((tr, C), lambda i: (i, 0))],
        out_specs=pl.BlockSpec((None, tr, C), lambda i: (0, i, 0)),
        out_shape=jax.ShapeDtypeStruct((n_roles, R, C), BF16), compiler_params=_params("parallel"))(w)


def _prenorm(x, gain):
    T, D = x.shape
    tm = _pick(T, 256)

    def body(x_ref, g_ref, h_ref, ht_ref):
        xv = x_ref[...]
        r = lax.rsqrt(jnp.mean(xv * xv, axis=-1, keepdims=True) + NORM_EPS)
        h = xv * r * g_ref[...]
        h_ref[...] = h.astype(BF16)
        ht_ref[...] = h.T.astype(BF16)

    return pl.pallas_call(
        body, name="prenorm", grid=(T // tm,),
        in_specs=[pl.BlockSpec((tm, D), lambda i: (i, 0)), pl.BlockSpec((1, D), lambda i: (0, 0))],
        out_specs=[pl.BlockSpec((tm, D), lambda i: (i, 0)), pl.BlockSpec((D, tm), lambda i: (0, i))],
        out_shape=[jax.ShapeDtypeStruct((T, D), BF16), jax.ShapeDtypeStruct((D, T), BF16)],
        compiler_params=_params("parallel"))(x, gain)


def _norm_bwd(u, yn, r):
    return r * (u - yn * jnp.mean(u * yn, axis=-1, keepdims=True))


def _loss_head(y, x, target, gain):
    T, D = y.shape
    tm = _pick(T, 256)

    def body(y_ref, x_ref, t_ref, g_ref, dy_ref, dout_ref, gpost_ref, sq_ref):
        yv = y_ref[...]
        g = g_ref[...]
        r = lax.rsqrt(jnp.mean(yv * yv, axis=-1, keepdims=True) + NORM_EPS)
        yn = yv * r
        err = x_ref[...] + yn * g - t_ref[...]
        dout = err * (1.0 / D)
        dy_ref[...] = _norm_bwd(dout * g, yn, r).astype(BF16)
        dout_ref[...] = dout

        @pl.when(pl.program_id(0) == 0)
        def _():
            gpost_ref[...] = jnp.zeros_like(gpost_ref)
            sq_ref[...] = jnp.zeros_like(sq_ref)

        gpost_ref[...] += jnp.sum(dout * yn, axis=0, keepdims=True)
        sq_ref[...] += jnp.sum(err * err, axis=0, keepdims=True)

    row = pl.BlockSpec((tm, D), lambda i: (i, 0))
    vec = pl.BlockSpec((1, D), lambda i: (0, 0))
    return pl.pallas_call(
        body, name="loss_head", grid=(T // tm,), in_specs=[row, row, row, vec], out_specs=[row, row, vec, vec],
        out_shape=[jax.ShapeDtypeStruct((T, D), BF16), jax.ShapeDtypeStruct((T, D), F32),
                   jax.ShapeDtypeStruct((1, D), F32), jax.ShapeDtypeStruct((1, D), F32)],
        compiler_params=_params("arbitrary"))(y, x, target, gain)


def _prenorm_bwd(x, dh, dout, gain):
    T, D = x.shape
    tm = _pick(T, 256)

    def body(x_ref, dh_ref, dout_ref, g_ref, gx_ref, gpre_ref):
        xv = x_ref[...]
        dhv = dh_ref[...]
        r = lax.rsqrt(jnp.mean(xv * xv, axis=-1, keepdims=True) + NORM_EPS)
        xn = xv * r
        gx_ref[...] = dout_ref[...] + _norm_bwd(dhv * g_ref[...], xn, r)

        @pl.when(pl.program_id(0) == 0)
        def _():
            gpre_ref[...] = jnp.zeros_like(gpre_ref)

        gpre_ref[...] += jnp.sum(dhv * xn, axis=0, keepdims=True)

    row = pl.BlockSpec((tm, D), lambda i: (i, 0))
    vec = pl.BlockSpec((1, D), lambda i: (0, 0))
    return pl.pallas_call(
        body, name="prenorm_bwd", grid=(T // tm,), in_specs=[row, row, row, vec], out_specs=[row, vec],
        out_shape=[jax.ShapeDtypeStruct((T, D), F32), jax.ShapeDtypeStruct((1, D), F32)],
        compiler_params=_params("arbitrary"))(x, dh, dout, gain)


def _attn_masks(n):
    row = lax.broadcasted_iota(jnp.int32, (2 * WINDOW, 2 * WINDOW), 0) % WINDOW
    col = lax.broadcasted_iota(jnp.int32, (2 * WINDOW, 2 * WINDOW), 1)
    valid = (col > row) & (col <= row + WINDOW) & ((n > 0) | (col >= WINDOW))
    low = lax.broadcasted_iota(jnp.int32, (1, LANES), 1) < ATTN_HEAD
    top = lax.broadcasted_iota(jnp.int32, (2 * WINDOW, 1), 0) < WINDOW
    return valid, low, top


def _dup_half(pair, keep):
    return jnp.where(keep, pair, pltpu.roll(pair, ATTN_HEAD, 1))


def _fold_half(v):
    return v + pltpu.roll(v, ATTN_HEAD, 1)


def _attn_probs(qpair, k2, sink_lo, sink_hi, valid, low, top):
    q2 = jnp.concatenate([jnp.where(low, qpair, 0.0), jnp.where(low, 0.0, qpair)], axis=0).astype(BF16)
    s = _dot(q2, k2, NT_DIMS) * (ATTN_HEAD ** -0.5)
    s = jnp.where(valid, s, -jnp.inf)
    sink = jnp.where(top, sink_lo, sink_hi)
    m = jnp.maximum(jnp.max(s, axis=-1, keepdims=True), sink)
    p = jnp.exp(s - m)
    psink = jnp.exp(sink - m)
    inv = 1.0 / (jnp.sum(p, axis=-1, keepdims=True) + psink)
    return q2, p * inv, psink * inv


def _attn_specs(D):
    kb = D // LANES
    vb = kb + D // (8 * LANES)
    gb = (D + D // 4) // 512
    wide = lambda off: [pl.BlockSpec((WINDOW, 512), functools.partial(lambda o, e, jp, n: (n, o + 2 * jp + e), off, e))
                        for e in (0, 1)]
    cur = lambda off: pl.BlockSpec((WINDOW, LANES), functools.partial(lambda o, jp, n: (n, o + jp), off))
    prev = lambda off: pl.BlockSpec((WINDOW, LANES),
                                    functools.partial(lambda o, jp, n: (jnp.maximum(n - 1, 0), o + jp), off))
    return wide(0) + [cur(kb), prev(kb), cur(vb), prev(vb)] + wide(gb)


def _attn_fwd(proj, sinks, D):
    T = proj.shape[0]
    nb, njp = T // WINDOW, D // 1024

    def body(sink_ref, qlo_ref, qhi_ref, kc_ref, kp_ref, vc_ref, vp_ref, glo_ref, ghi_ref, mix_ref, mixt_ref):
        jp, n = pl.program_id(0), pl.program_id(1)
        valid, low, top = _attn_masks(n)
        kk = jnp.concatenate([kp_ref[...], kc_ref[...]], axis=0)
        vv = jnp.concatenate([vp_ref[...], vc_ref[...]], axis=0)
        for hj, (q_ref, g_ref) in enumerate(((qlo_ref, glo_ref), (qhi_ref, ghi_ref))):
            keep = low if hj == 0 else jnp.logical_not(low)
            k2 = _dup_half(kk, keep).astype(BF16)
            v2 = _dup_half(vv, keep).astype(BF16)
            for p in range(4):
                cols = slice(LANES * p, LANES * (p + 1))
                head = (2 * jp + hj) * GQA + 2 * p
                _, probs, _ = _attn_probs(q_ref[:, cols], k2, sink_ref[0, head], sink_ref[0, head + 1],
                                          valid, low, top)
                o2 = _dot(probs.astype(BF16), v2, NN_DIMS)
                opair = jnp.where(low, o2[:WINDOW], o2[WINDOW:])
                g = g_ref[:, cols]
                out = opair * (g * _sigmoid(g))
                oc = slice(512 * hj + LANES * p, 512 * hj + LANES * (p + 1))
                mix_ref[:, oc] = out.astype(BF16)
                mixt_ref[oc, :] = out.T.astype(BF16)

    return pl.pallas_call(
        body, name="attn_fwd", grid=(njp, nb),
        in_specs=[pl.BlockSpec(memory_space=pltpu.SMEM)] + _attn_specs(D),
        out_specs=[pl.BlockSpec((WINDOW, 1024), lambda jp, n: (n, jp)),
                   pl.BlockSpec((1024, WINDOW), lambda jp, n: (jp, n))],
        out_shape=[jax.ShapeDtypeStruct((T, 2 * D), BF16), jax.ShapeDtypeStruct((2 * D, T), BF16)],
        compiler_params=_params("parallel", "parallel"))(sinks, *([proj] * 8))


def _attn_bwd(proj, sinks, dmix, D, after=()):
    T = proj.shape[0]
    nb, njp = T // WINDOW, D // 1024
    n_after = len(after)

    def body(sink_ref, qlo_ref, qhi_ref, kc_ref, kp_ref, vc_ref, vp_ref, glo_ref, ghi_ref, dmix_ref, *rest):
        dq_ref, dk_ref, dv_ref, dg_ref, dsink_ref, kcarry_ref, vcarry_ref = rest[n_after:]
        jp, step = pl.program_id(0), pl.program_id(1)
        n = nb - 1 - step
        valid, low, top = _attn_masks(n)
        lane = lax.broadcasted_iota(jnp.int32, (1, LANES), 1)
        kk = jnp.concatenate([kp_ref[...], kc_ref[...]], axis=0)
        vv = jnp.concatenate([vp_ref[...], vc_ref[...]], axis=0)

        @pl.when(step == 0)
        def _():
            kcarry_ref[...] = jnp.zeros_like(kcarry_ref)
            vcarry_ref[...] = jnp.zeros_like(vcarry_ref)
            dsink_ref[...] = jnp.zeros_like(dsink_ref)

        dk_pair = jnp.zeros((2 * WINDOW, LANES), F32)
        dv_pair = jnp.zeros((2 * WINDOW, LANES), F32)
        dsink = jnp.zeros((1, LANES), F32)
        for hj, (q_ref, g_ref) in enumerate(((qlo_ref, glo_ref), (qhi_ref, ghi_ref))):
            keep = low if hj == 0 else jnp.logical_not(low)
            k2 = _dup_half(kk, keep).astype(BF16)
            v2 = _dup_half(vv, keep).astype(BF16)
            dk_head = jnp.zeros((2 * WINDOW, LANES), F32)
            dv_head = jnp.zeros((2 * WINDOW, LANES), F32)
            for p in range(4):
                cols = slice(LANES * p, LANES * (p + 1))
                oc = slice(512 * hj + LANES * p, 512 * hj + LANES * (p + 1))
                head = (2 * jp + hj) * GQA + 2 * p
                q2, probs, psink = _attn_probs(q_ref[:, cols], k2, sink_ref[0, head], sink_ref[0, head + 1],
                                               valid, low, top)
                pb = probs.astype(BF16)
                o2 = _dot(pb, v2, NN_DIMS)
                opair = jnp.where(low, o2[:WINDOW], o2[WINDOW:])
                g = g_ref[:, cols]
                sg = _sigmoid(g)
                dgated = dmix_ref[:, oc]
                dg_ref[:, oc] = (dgated * opair * (sg * (1.0 + g * (1.0 - sg)))).astype(BF16)
                do = dgated * (g * sg)
                do2 = jnp.concatenate([jnp.where(low, do, 0.0), jnp.where(low, 0.0, do)], axis=0).astype(BF16)
                dp = _dot(do2, v2, NT_DIMS)
                delta = jnp.sum(probs * dp, axis=-1, keepdims=True)
                ds = (probs * (dp - delta) * (ATTN_HEAD ** -0.5)).astype(BF16)
                dq2 = _dot(ds, k2, NN_DIMS)
                dq_ref[:, oc] = jnp.where(low, dq2[:WINDOW], dq2[WINDOW:]).astype(BF16)
                dk_head += _dot(ds, q2, TN_DIMS)
                dv_head += _dot(pb, do2, TN_DIMS)
                ps = psink * delta
                local = hj * GQA + 2 * p
                dsink -= jnp.where(lane == local, jnp.sum(ps[:WINDOW], axis=0, keepdims=True), 0.0)
                dsink -= jnp.where(lane == local + 1, jnp.sum(ps[WINDOW:], axis=0, keepdims=True), 0.0)
            dk_pair += jnp.where(keep, _fold_half(dk_head), 0.0)
            dv_pair += jnp.where(keep, _fold_half(dv_head), 0.0)
        dk_ref[...] = (dk_pair[WINDOW:] + kcarry_ref[...]).astype(BF16)
        dv_ref[...] = (dv_pair[WINDOW:] + vcarry_ref[...]).astype(BF16)
        kcarry_ref[...] = dk_pair[:WINDOW]
        vcarry_ref[...] = dv_pair[:WINDOW]
        dsink_ref[...] += dsink

    rev = lambda spec: pl.BlockSpec(spec.block_shape, functools.partial(
        lambda f, jp, s: f(jp, nb - 1 - s), spec.index_map))
    in_specs = [rev(s) for s in _attn_specs(D)]
    wide_out = pl.BlockSpec((WINDOW, 1024), lambda jp, s: (nb - 1 - s, jp))
    pair_out = pl.BlockSpec((WINDOW, LANES), lambda jp, s: (nb - 1 - s, jp))
    return pl.pallas_call(
        body, name="attn_bwd", grid=(njp, nb),
        in_specs=[pl.BlockSpec(memory_space=pltpu.SMEM)] + in_specs + [wide_out] + [ANY_SPEC] * n_after,
        out_specs=[wide_out, pair_out, pair_out, wide_out, pl.BlockSpec((None, 1, LANES), lambda jp, s: (jp, 0, 0))],
        out_shape=[jax.ShapeDtypeStruct((T, D), BF16), jax.ShapeDtypeStruct((T, D // 8), BF16),
                   jax.ShapeDtypeStruct((T, D // 8), BF16), jax.ShapeDtypeStruct((T, D), BF16),
                   jax.ShapeDtypeStruct((njp, 1, LANES), F32)],
        scratch_shapes=[pltpu.VMEM((WINDOW, LANES), F32), pltpu.VMEM((WINDOW, LANES), F32)],
        compiler_params=_params("parallel", "arbitrary"))(sinks, *([proj] * 8), dmix, *after)


RNN_TB = 512


def _split3(v):
    a = v.astype(BF16)
    r = v - a.astype(F32)
    b = r.astype(BF16)
    c = (r - b.astype(F32)).astype(BF16)
    return a, b, c


def _tri_sum(tri, v):
    a, b, c = _split3(v)
    return _dot(tri, a, NN_DIMS) + _dot(tri, b, NN_DIMS) + _dot(tri, c, NN_DIMS)


def _lower_bound(lb_ref):
    l0, l1 = lb_ref[0:1, :], lb_ref[1:2, :]
    m = jnp.maximum(l0, l1)
    e0, e1 = jnp.exp(l0 - m), jnp.exp(l1 - m)
    return e0 / (e0 + e1)


def _rnn_gates(rq, rf, lb):
    sq = _sigmoid(rq)
    sf = _sigmoid(rf)
    f = lb + (1.0 - lb) * sf
    return sq, sf, f


def _rnn_decays(g, tri):
    G = _tri_sum(tri, g)
    last = G[CHUNK - 1:CHUNK, :]
    mid = G[CHUNK // 2 - 1:CHUNK // 2, :]
    return G, jnp.exp(G), jnp.exp(G - mid), jnp.exp(mid - G), jnp.exp(last - G), jnp.exp(last)


def _chunk_masks():
    r = lax.broadcasted_iota(jnp.int32, (CHUNK, CHUNK), 0)
    c = lax.broadcasted_iota(jnp.int32, (CHUNK, CHUNK), 1)
    return r >= c, (r >= c).astype(BF16), (r <= c).astype(BF16)


def _rnn_specs(T, D, tb, rev):
    nt = T // tb
    base = (2 * D + D // 4) // LANES
    t_of = (lambda s: nt - 1 - s) if rev else (lambda s: s)
    cols = [pl.BlockSpec((tb, LANES), functools.partial(lambda o, h, s: (t_of(s), o + h), base + i * (D // LANES)))
            for i in range(4)]
    return cols, t_of


def _rnn_fwd(proj, lb_logits, rnn_norm, mixed, mixed_t, D):
    T = proj.shape[0]
    tb = _pick(T, RNN_TB)
    nt, nh, cpb = T // tb, D // RNN_HEAD, tb // CHUNK
    cols, _ = _rnn_specs(T, D, tb, False)

    def body(rq_ref, rf_ref, ri_ref, rg_ref, lb_ref, gain_ref, mix_in, mixt_in,
             mix_ref, mixt_ref, o_ref, st_ref, state_ref):
        del mix_in, mixt_in
        causal, tri, _ = _chunk_masks()
        lb = _lower_bound(lb_ref)

        @pl.when(pl.program_id(1) == 0)
        def _():
            state_ref[...] = jnp.zeros_like(state_ref)

        def chunk(c, carry):
            rows = pl.ds(pl.multiple_of(c * CHUNK, CHUNK), CHUNK)
            rq, rf, v, rg = rq_ref[rows, :], rf_ref[rows, :], ri_ref[rows, :], rg_ref[rows, :]
            sq, _, f = _rnn_gates(rq, rf, lb)
            q, k = rq * sq, 1.0 - f
            _, eG, eq, ek, ekl, elast = _rnn_decays(jnp.log(f), tri)
            st = state_ref[...]
            st_ref[c] = st
            vb = v.astype(BF16)
            att = jnp.where(causal, _dot((q * eq).astype(BF16), (k * ek).astype(BF16), NT_DIMS), 0.0)
            o = _dot((q * eG).astype(BF16), st.astype(BF16), NT_DIMS) + _dot(att.astype(BF16), vb, NN_DIMS)
            state_ref[...] = st * elast + _dot(vb, (k * ekl).astype(BF16), TN_DIMS)
            o_ref[rows, :] = o
            r = lax.rsqrt(jnp.mean(o * o, axis=-1, keepdims=True) + NORM_EPS)
            out = (o * r * gain_ref[...]) * (rg * _sigmoid(rg))
            mix_ref[rows, :] = out.astype(BF16)
            return carry

        lax.fori_loop(0, cpb, chunk, 0)
        mixt_ref[...] = mix_ref[...].astype(F32).T.astype(BF16)

    vec2 = pl.BlockSpec((2, LANES), lambda h, s: (0, h))
    vec1 = pl.BlockSpec((1, LANES), lambda h, s: (0, h))
    return pl.pallas_call(
        body, name="rnn_fwd", grid=(nh, nt),
        in_specs=cols + [vec2, vec1, ANY_SPEC, ANY_SPEC],
        out_specs=[pl.BlockSpec((tb, LANES), lambda h, s: (s, D // LANES + h)),
                   pl.BlockSpec((LANES, tb), lambda h, s: (D // LANES + h, s)),
                   pl.BlockSpec((tb, LANES), lambda h, s: (s, h)),
                   pl.BlockSpec((None, cpb, RNN_HEAD, RNN_HEAD), lambda h, s: (h, s, 0, 0))],
        out_shape=[jax.ShapeDtypeStruct(mixed.shape, BF16), jax.ShapeDtypeStruct(mixed_t.shape, BF16),
                   jax.ShapeDtypeStruct((T, D), F32),
                   jax.ShapeDtypeStruct((nh, T // CHUNK, RNN_HEAD, RNN_HEAD), F32)],
        scratch_shapes=[pltpu.VMEM((RNN_HEAD, RNN_HEAD), F32)],
        input_output_aliases={6: 0, 7: 1},
        compiler_params=_params("parallel", "arbitrary"))(proj, proj, proj, proj, lb_logits, rnn_norm, mixed, mixed_t)


def _rnn_bwd(proj, lb_logits, rnn_norm, o_raw, states, dmix, D):
    T = proj.shape[0]
    tb = _pick(T, RNN_TB)
    nt, nh, cpb = T // tb, D // RNN_HEAD, tb // CHUNK
    cols, t_of = _rnn_specs(T, D, tb, True)

    def body(rq_ref, rf_ref, ri_ref, rg_ref, lb_ref, gain_ref, o_ref, st_ref, dmix_ref,
             drq_ref, drf_ref, dri_ref, drg_ref, dlb_ref, dgain_ref, dstate_ref):
        causal, tri, tri_t = _chunk_masks()
        lb = _lower_bound(lb_ref)
        gain = gain_ref[...]

        @pl.when(pl.program_id(1) == 0)
        def _():
            dstate_ref[...] = jnp.zeros_like(dstate_ref)
            dlb_ref[...] = jnp.zeros_like(dlb_ref)
            dgain_ref[...] = jnp.zeros_like(dgain_ref)

        def chunk(i, carry):
            c = cpb - 1 - i
            rows = pl.ds(pl.multiple_of(c * CHUNK, CHUNK), CHUNK)
            rq, rf, v, rg = rq_ref[rows, :], rf_ref[rows, :], ri_ref[rows, :], rg_ref[rows, :]
            o, dgated = o_ref[rows, :], dmix_ref[rows, :]
            sgt = _sigmoid(rg)
            r = lax.rsqrt(jnp.mean(o * o, axis=-1, keepdims=True) + NORM_EPS)
            on = o * r
            drg_ref[rows, :] = (dgated * (on * gain) * (sgt * (1.0 + rg * (1.0 - sgt)))).astype(BF16)
            d_on = dgated * (rg * sgt)
            dgain_ref[...] += jnp.sum(d_on * on, axis=0, keepdims=True)
            do = _norm_bwd(d_on * gain, on, r)
            sq, sf, f = _rnn_gates(rq, rf, lb)
            q, k = rq * sq, 1.0 - f
            _, eG, eq, ek, ekl, elast = _rnn_decays(jnp.log(f), tri)
            st, dst = st_ref[c], dstate_ref[...]
            qg, qm, km, kl = q * eG, q * eq, k * ek, k * ekl
            qgb, qmb, kmb, klb = qg.astype(BF16), qm.astype(BF16), km.astype(BF16), kl.astype(BF16)
            dob, vb, dstb = do.astype(BF16), v.astype(BF16), dst.astype(BF16)
            att = jnp.where(causal, _dot(qmb, kmb, NT_DIMS), 0.0).astype(BF16)
            datt = jnp.where(causal, _dot(dob, vb, NT_DIMS), 0.0).astype(BF16)
            dqg = _dot(dob, st.astype(BF16), NN_DIMS)
            dqm = _dot(datt, kmb, NN_DIMS)
            dkm = _dot(datt, qmb, TN_DIMS)
            dkl = _dot(vb, dstb, NN_DIMS)
            dri_ref[rows, :] = (_dot(att, dob, TN_DIMS) + _dot(klb, dstb, NT_DIMS)).astype(BF16)
            dstate_ref[...] = dst * elast + _dot(dob, qgb, TN_DIMS)
            dq = dqg * eG + dqm * eq
            dk = dkm * ek + dkl * ekl
            dG = dqg * qg + dqm * qmb.astype(F32) - dkm * kmb.astype(F32) - dkl * kl
            dlast = jnp.sum(dkl * kl, axis=0, keepdims=True) + jnp.sum(dst * st, axis=0, keepdims=True) * elast
            last_row = lax.broadcasted_iota(jnp.int32, (CHUNK, 1), 0) == CHUNK - 1
            dg = _tri_sum(tri_t, dG + jnp.where(last_row, dlast, 0.0))
            df = dg / f - dk
            drq_ref[rows, :] = (dq * (sq * (1.0 + rq * (1.0 - sq)))).astype(BF16)
            drf_ref[rows, :] = (df * (1.0 - lb) * (sf * (1.0 - sf))).astype(BF16)
            dlb_ref[...] += jnp.sum(df * (1.0 - sf), axis=0, keepdims=True)
            return carry

        lax.fori_loop(0, cpb, chunk, 0)

    vec2 = pl.BlockSpec((2, LANES), lambda h, s: (0, h))
    vec1 = pl.BlockSpec((1, LANES), lambda h, s: (0, h))
    blk = pl.BlockSpec((tb, LANES), lambda h, s: (t_of(s), h))
    return pl.pallas_call(
        body, name="rnn_bwd", grid=(nh, nt),
        in_specs=cols + [vec2, vec1, blk,
                         pl.BlockSpec((None, cpb, RNN_HEAD, RNN_HEAD), lambda h, s: (h, t_of(s), 0, 0)),
                         pl.BlockSpec((tb, LANES), lambda h, s: (t_of(s), D // LANES + h))],
        out_specs=[blk, blk, blk, blk, vec1, vec1],
        out_shape=[jax.ShapeDtypeStruct((T, D), BF16)] * 4 + [jax.ShapeDtypeStruct((1, D), F32)] * 2,
        scratch_shapes=[pltpu.VMEM((RNN_HEAD, RNN_HEAD), F32)],
        compiler_params=_params("parallel", "arbitrary"))(proj, proj, proj, proj, lb_logits, rnn_norm, o_raw, states, dmix)


def _adamw(w, g, m, v):
    m = ADAM_B1 * m + (1.0 - ADAM_B1) * g
    v = ADAM_B2 * v + (1.0 - ADAM_B2) * (g * g)
    m_hat = m / (1.0 - ADAM_B1 ** ADAM_STEP)
    v_hat = v / (1.0 - ADAM_B2 ** ADAM_STEP)
    delta = -ADAM_LR * (m_hat / (jnp.sqrt(v_hat) + ADAM_EPS) + ADAM_WD * w)
    return delta, m, v


def _adamw_big(w, m, v, parts, tr, name):
    R, C = w.shape
    n_parts = len(parts)

    def body(w_ref, m_ref, v_ref, *rest):
        part_refs = rest[:n_parts]
        g_ref, d_ref, nm_ref, nv_ref = rest[n_parts:]
        g = part_refs[0][...].astype(F32)
        for p_ref in part_refs[1:]:
            g = g + p_ref[...].astype(F32)
        delta, nm, nv = _adamw(w_ref[...], g, m_ref[...], v_ref[...])
        g_ref[...] = g
        d_ref[...] = delta
        nm_ref[...] = nm
        nv_ref[...] = nv

    blk = pl.BlockSpec((tr, C), lambda i: (i, 0))
    part_specs = [pl.BlockSpec((None, tr, C), functools.partial(lambda s, i: (s, i, 0), slot)) for _, slot in parts]
    return pl.pallas_call(
        body, name=name, grid=(R // tr,), in_specs=[blk, blk, blk] + part_specs,
        out_specs=[blk] * 4, out_shape=[jax.ShapeDtypeStruct((R, C), F32)] * 4,
        compiler_params=_params("parallel"))(w, m, v, *[a for a, _ in parts])


def _pair_sum(pa, qa):
    n, R, C = qa.shape
    tr = _pick(R, 256)

    def body(p_ref, q_ref, r_ref):
        r_ref[...] = (p_ref[...].astype(F32) + q_ref[...].astype(F32)).astype(BF16)

    return pl.pallas_call(
        body, name="pair_sum", grid=(n, R // tr),
        in_specs=[pl.BlockSpec((None, tr, C), lambda j, i: (2 * j, i, 0)), pl.BlockSpec((None, tr, C), lambda j, i: (j, i, 0))],
        out_specs=pl.BlockSpec((None, tr, C), lambda j, i: (j, i, 0)),
        out_shape=jax.ShapeDtypeStruct((n, R, C), BF16), compiler_params=_params("parallel", "parallel"))(pa, qa)


def _adamw_small(total, sinks, lb_logits, rnn_norm, pre_norm, post_norm, moments, D):
    params = [sinks, lb_logits, rnn_norm, pre_norm, post_norm]

    def body(tot_ref, *refs):
        p_refs, m_refs, v_refs = refs[0:5], refs[5:10], refs[10:15]
        loss_ref, outs = refs[15], refs[16:]
        tot = tot_ref[...]
        l0, l1 = p_refs[1][0:1, :], p_refs[1][1:2, :]
        mx = jnp.maximum(l0, l1)
        e0, e1 = jnp.exp(l0 - mx), jnp.exp(l1 - mx)
        p0, p1 = e0 / (e0 + e1), e1 / (e0 + e1)
        dlb = tot[0:1, :]
        grads = [tot[5:6, 0:LANES], jnp.concatenate([dlb * p0 * (1.0 - p0), -dlb * p0 * p1], axis=0),
                 tot[1:2, :], tot[2:3, :], tot[3:4, :]]
        loss_ref[...] = 0.5 / D * jnp.sum(tot[4:5, :], axis=-1, keepdims=True)
        for i, g in enumerate(grads):
            delta, nm, nv = _adamw(p_refs[i][...], g, m_refs[i][...], v_refs[i][...])
            outs[4 * i][...] = g
            outs[4 * i + 1][...] = delta
            outs[4 * i + 2][...] = nm
            outs[4 * i + 3][...] = nv

    out_shape = [jax.ShapeDtypeStruct((1, 1), F32)]
    for p in params:
        out_shape += [jax.ShapeDtypeStruct(p.shape, F32)] * 4
    return pl.pallas_call(body, name="adamw_small", out_shape=out_shape)(total, *params, *moments)


SIBLING = 1


def _me():
    return lax.axis_index("x"), lax.axis_index("y"), lax.axis_index("c")


def _flat(px, py, pc):
    return 4 * px + 2 * py + pc


def _role_peer(role, x, y, c):
    if role < 2:
        return (x, y, (1 - c) if role else c)
    j, other = (role - 2) // 2, (role - 2) % 2
    px = (1 - x) if j in (0, 2) else x
    py = (1 - y) if j in (1, 2) else y
    return (px, py, (1 - c) if other else c)


def _role_ids():
    x, y, c = _me()
    return jnp.stack([_flat(*_role_peer(r, x, y, c)) for r in range(N_DEV)]).astype(jnp.int32)


def _comm_call(name, bufs, waits=(), starts=(), after=()):
    n_buf, n_wait, n_start, n_after = len(bufs), len(waits), len(starts), len(after)

    def body(*refs):
        buf_refs = refs[:n_buf]
        sem_refs = refs[n_buf:n_buf + 2 * n_wait]
        outs = refs[n_buf + 2 * n_wait + n_after:]
        new_sems, token = outs[:2 * n_start], outs[-1]
        x, y, c = _me()
        for w, (_, kind, like) in enumerate(waits):
            shape_ref = buf_refs[like].at[0]
            cp = pltpu.make_async_remote_copy(
                src_ref=shape_ref, dst_ref=shape_ref, send_sem=sem_refs[2 * w], recv_sem=sem_refs[2 * w + 1],
                device_id=(x, y, c), device_id_type=MESH)
            if kind == "send":
                cp.wait_send()
            else:
                cp.wait_recv()
        for s, (sb, ss, db, ds, role) in enumerate(starts):
            pltpu.make_async_remote_copy(
                src_ref=buf_refs[sb].at[ss], dst_ref=buf_refs[db].at[ds], send_sem=new_sems[2 * s],
                recv_sem=new_sems[2 * s + 1], device_id=_role_peer(role, x, y, c), device_id_type=MESH).start()
        token[...] = jnp.zeros_like(token)

    sems = [s for flight, _, _ in waits for s in flight]
    out = pl.pallas_call(
        body, name=name,
        out_shape=tuple([pltpu.SemaphoreType.DMA(())] * (2 * n_start) + [pltpu.HBM(b.shape, b.dtype) for b in bufs]
                        + [jax.ShapeDtypeStruct((8, LANES), F32)]),
        in_specs=tuple([HBM_SPEC] * n_buf + [SEM_SPEC] * (2 * n_wait) + [ANY_SPEC] * n_after),
        out_specs=tuple([SEM_SPEC] * (2 * n_start) + [HBM_SPEC] * n_buf + [pl.BlockSpec(memory_space=pltpu.VMEM)]),
        input_output_aliases={i: 2 * n_start + i for i in range(n_buf)},
        compiler_params=pltpu.CompilerParams(has_side_effects=EFFECT),
    )(*[pltpu.with_memory_space_constraint(b, pltpu.HBM) for b in bufs], *sems, *after)
    flights = [(out[2 * s], out[2 * s + 1]) for s in range(n_start)]
    return list(out[2 * n_start:2 * n_start + n_buf]), flights, out[-1]


def _landing(shape):
    return lax.empty(shape, BF16)


def _both(flights, like):
    return [(f, kind, like) for f in flights for kind in ("send", "recv")]


def _allreduce_small(rows, D):
    n_rows = len(rows)

    def body(*refs):
        in_refs, out_ref, all_ref = refs[:n_rows], refs[n_rows], refs[n_rows + 1]
        send_sems, recv_sems = refs[n_rows + 2], refs[n_rows + 3]
        x, y, c = _me()
        mine = all_ref.at[_flat(x, y, c)]
        mine[...] = jnp.zeros((8, D), F32)
        for i, r in enumerate(in_refs):
            mine[i:i + 1, :] = r[...]
        copies = []
        for k in range(N_DEV - 1):
            copies.append(pltpu.make_async_remote_copy(
                src_ref=mine, dst_ref=mine, send_sem=send_sems.at[k], recv_sem=recv_sems.at[k],
                device_id=_role_peer(k + 1, x, y, c), device_id_type=MESH))
        for cp in copies:
            cp.start()
        for cp in copies:
            cp.wait_recv()
        for cp in copies:
            cp.wait_send()
        tot = all_ref[0]
        for d in range(1, N_DEV):
            tot = tot + all_ref[d]
        out_ref[...] = tot

    vm = pl.BlockSpec(memory_space=pltpu.VMEM)
    return pl.pallas_call(
        body, name="allreduce_small", in_specs=[vm] * n_rows, out_specs=vm,
        out_shape=jax.ShapeDtypeStruct((8, D), F32),
        scratch_shapes=[pltpu.VMEM((N_DEV, 8, D), F32), pltpu.SemaphoreType.DMA((7,)), pltpu.SemaphoreType.DMA((7,))],
    )(*rows)


def _step(ids, x, target, g_in, g_out, sinks, lb_logits, rnn_norm, pre_norm, post_norm, dist):
    T, D = x.shape
    n_roles, _, wd = g_in.shape
    ro = g_out.shape[1]
    assert n_roles % 2 == 0 and (not dist or n_roles == N_DEV)
    n_chips = n_roles // 2 - 1

    if dist:
        (g_in,), fl_in, _ = _comm_call(
            "gather_in_start", [g_in], starts=[(0, 0, 0, SIBLING, SIBLING)] + [(0, 0, 0, 2 + 2 * j, 2 + 2 * j) for j in range(3)])
    h, h_t = _prenorm(x, pre_norm)
    if dist:
        (g_in,), _, _ = _comm_call("gather_in_sibling", [g_in], waits=[(fl_in[0], "recv", 0)], after=(h,))
    proj = _proj_in(ids, h, g_in, 0, 2, None, "proj_in_0")
    fl_fwd = []

    def pass_on(i, g):
        (g,), fl, _ = _comm_call(
            f"gather_in_pass_{i}", [g], waits=[(fl_in[1 + i], "recv", 0)],
            starts=[(0, 2 + 2 * i, 0, 3 + 2 * i, SIBLING)], after=(proj,))
        fl_fwd.extend(fl)
        return g

    if dist:
        g_in = pass_on(1, pass_on(0, g_in))
    for j in range(n_chips):
        after = ()
        if dist:
            if j == 2:
                g_in = pass_on(2, g_in)
            (g_in,), _, _ = _comm_call(f"gather_in_passed_{j}", [g_in], waits=[(fl_fwd[j], "recv", 0)], after=(proj,))
            if j == 2:
                (g_in, g_out), fl_out, token = _comm_call(
                    "gather_out_start", [g_in, g_out],
                    starts=[(1, 0, 1, SIBLING, SIBLING)] + [(1, 0, 1, 2 + 2 * i, 2 + 2 * i) for i in range(3)], after=(proj,))
                after = (token,)
        proj = _proj_in(ids, h, g_in, 2 + 2 * j, 2, proj, f"proj_in_{j + 1}", after)
    mixed, mixed_t = _attn_fwd(proj, sinks, D)
    if dist:
        (g_in, g_out), fl_out_fwd, _ = _comm_call(
            "gather_out_pass", [g_in, g_out],
            waits=[(f, "send", 0) for f in fl_in + fl_fwd] + [(fl_out[1 + i], "recv", 1) for i in range(3)],
            starts=[(1, 2 + 2 * i, 1, 3 + 2 * i, SIBLING) for i in range(3)], after=(mixed,))
    mixed, mixed_t, o_raw, states = _rnn_fwd(proj, lb_logits, rnn_norm, mixed, mixed_t, D)
    if dist:
        (g_out,), _, _ = _comm_call(
            "gather_out_done", [g_out],
            waits=[(fl_out[0], "recv", 0)] + [(f, "recv", 0) for f in fl_out_fwd]
            + [(f, "send", 0) for f in fl_out + fl_out_fwd], after=(states,))
    y = _proj_out(ids, mixed, g_out)
    dy, dout, g_post, sq_err = _loss_head(y, x, target, post_norm)

    dmix = _dmixed(ids, dy, g_out)
    p_out, after = _dw_out(ids, mixed_t, dy, n_roles), ()
    if dist:
        (p_out, l_out), fl_so, token = _comm_call(
            "scatter_out_start", [p_out, _landing((n_roles - 1, ro, D))],
            starts=[(0, r, 1, r - 1, r) for r in range(1, n_roles)])
        after = (token,)
    d_aq, d_ak, d_av, d_ag, d_sink = _attn_bwd(proj, sinks, dmix, D, after)
    d_rq, d_rf, d_ri, d_rg, d_lb, g_rnn = _rnn_bwd(proj, lb_logits, rnn_norm, o_raw, states, dmix, D)
    dproj = jnp.concatenate([d_aq, d_ak, d_av, d_ag, d_rq, d_rf, d_ri, d_rg], axis=1)

    p_far = _dw_in(ids, h_t, dproj, n_roles, 2, n_roles - 2, "dw_in_far")
    if dist:
        (p_far, q_far), fl_pair, token = _comm_call(
            "scatter_in_pair_start", [p_far, _landing((3, D, wd))],
            starts=[(0, 1 + 2 * j, 1, j, SIBLING) for j in range(3)])
        p_near = _dw_in(ids, h_t, dproj, n_roles, 0, 2, "dw_in_near", after=(token,))
        (p_far, q_far), _, _ = _comm_call("scatter_in_pair_wait", [p_far, q_far], waits=_both(fl_pair, 1), after=(p_near,))
        chip_sum = _pair_sum(p_far, q_far)
        (chip_sum, z_far), fl_chip, token = _comm_call(
            "scatter_in_chip_start", [chip_sum, _landing((3, D, wd))],
            starts=[(0, j, 1, j, 2 + 2 * j) for j in range(3)])
        (p_near, q_near), fl_sib, token = _comm_call(
            "scatter_in_sibling_start", [p_near, _landing((1, D, wd))], starts=[(0, 1, 1, 0, SIBLING)], after=(token,))
        after = (token,)
    else:
        p_near, after = _dw_in(ids, h_t, dproj, n_roles, 0, 2, "dw_in_near"), ()
    dh = _dh(ids, dproj, g_in, after)
    grad_x, g_pre = _prenorm_bwd(x, dh, dout, pre_norm)
    n_q = D // ATTN_HEAD
    sink_row = jnp.pad(d_sink[:, 0, :2 * GQA].reshape(1, n_q), ((0, 0), (0, D - n_q)))
    rows = [d_lb, g_rnn, g_pre, g_post, sq_err, sink_row]
    if not dist:
        return grad_x, [(p_near, r) for r in range(2)] + [(p_far, r) for r in range(n_roles - 2)], \
            [(p_out, r) for r in range(n_roles)], rows

    (p_out, l_out), _, _ = _comm_call("scatter_out_wait", [p_out, l_out], waits=_both(fl_so, 1), after=(grad_x,))
    (chip_sum, z_far, p_near, q_near), _, _ = _comm_call(
        "scatter_in_wait", [chip_sum, z_far, p_near, q_near], waits=_both(fl_chip, 1) + _both(fl_sib, 3), after=(p_out,))
    parts_in = [(p_near, 0), (q_near, 0)] + [(z_far, j) for j in range(3)]
    parts_out = [(p_out, 0)] + [(l_out, k) for k in range(n_roles - 1)]
    return grad_x, parts_in, parts_out, rows


def kernel(x, w_in, attn_sinks, lb_logits, rnn_norm, w_out, pre_norm, post_norm, loss_target, m_w_in, m_attn_sinks, m_lb_logits, m_rnn_norm, m_w_out, m_pre_norm, m_post_norm, v_w_in, v_attn_sinks, v_lb_logits, v_rnn_norm, v_w_out, v_pre_norm, v_post_norm):
    _, T, D = x.shape
    ro = w_out.shape[1]
    n_q = attn_sinks.shape[1]
    assert lb_logits.shape[0] == 2 and n_q == D // ATTN_HEAD and n_q <= LANES

    grad_x, parts_in, parts_out, small_rows = _step(
        _role_ids(), x[0], loss_target[0], _cast_slot0(w_in[0], N_DEV, "cast_w_in"),
        _cast_slot0(w_out[0], N_DEV, "cast_w_out"), attn_sinks, lb_logits, rnn_norm, pre_norm, post_norm, True)

    g_wo, d_wo, nm_wo, nv_wo = _adamw_big(w_out[0], m_w_out[0], v_w_out[0], parts_out, _pick(ro, 64), "adamw_w_out")
    g_wi, d_wi, nm_wi, nv_wi = _adamw_big(w_in[0], m_w_in[0], v_w_in[0], parts_in, _pick(D, 128), "adamw_w_in")

    total = _allreduce_small(small_rows, D)
    pad = lambda a: jnp.pad(a, ((0, 0), (0, LANES - n_q)))
    moments = [pad(m_attn_sinks), m_lb_logits, m_rnn_norm, m_pre_norm, m_post_norm,
               pad(v_attn_sinks), v_lb_logits, v_rnn_norm, v_pre_norm, v_post_norm]
    res = _adamw_small(total, pad(attn_sinks), lb_logits, rnn_norm, pre_norm, post_norm, moments, D)
    loss = res[0][0, 0]
    small = [[res[1 + 4 * i + j] for i in range(5)] for j in range(4)]
    for j in range(4):
        small[j][0] = small[j][0][:, :n_q]

    def assemble(j, wi, wo):
        s = small[j]
        return [wi[None], s[0], s[1], s[2], wo[None], s[3], s[4]]

    return (loss, grad_x[None], *assemble(0, g_wi, g_wo), *assemble(1, d_wi, d_wo),
            *assemble(2, nm_wi, nm_wo), *assemble(3, nv_wi, nv_wo))
```

```python
import functools

import jax
import jax.numpy as jnp
from jax import lax
from jax.experimental import pallas as pl
from jax.experimental.pallas import tpu as pltpu

F32 = jnp.float32
BF16 = jnp.bfloat16

ATTN_HEAD = 64
GQA = 8
WINDOW = 128
RNN_HEAD = 128
CHUNK = 64
NORM_EPS = 1e-6
LANES = 128
N_DEV = 8

ADAM_LR = 0.001
ADAM_B1 = 0.9
ADAM_B2 = 0.999
ADAM_EPS = 1e-08
ADAM_WD = 0.01
ADAM_STEP = 10

VMEM_LIMIT = 56 * 1024 * 1024
MESH = pl.DeviceIdType.MESH
ANY_SPEC = pl.BlockSpec(memory_space=pl.ANY)
HBM_SPEC = pl.BlockSpec(memory_space=pltpu.HBM)
SEM_SPEC = pl.BlockSpec(memory_space=pltpu.SEMAPHORE)
EFFECT = pltpu.SideEffectType.DATAFLOW_SIDE_EFFECTING

NT_DIMS = (((1,), (1,)), ((), ()))
NN_DIMS = (((1,), (0,)), ((), ()))
TN_DIMS = (((0,), (0,)), ((), ()))


def _params(*sem):
    return pltpu.CompilerParams(dimension_semantics=sem, vmem_limit_bytes=VMEM_LIMIT)


def _dot(a, b, dims):
    return lax.dot_general(a, b, dims, preferred_element_type=F32)


def _sigmoid(v):
    return 1.0 / (1.0 + jnp.exp(-v))


def _pick(n, pref):
    t = min(n, pref)
    assert n % t == 0, (n, pref)
    return t


def _matmul(ids, a, b, *, grid, a_spec, b_spec, o_spec, out_shape, trans_b, name, after=(), prev=None):
    nk = grid[2]
    dims = NT_DIMS if trans_b else NN_DIMS
    n_skip = len(after) + (prev is not None)

    def body(ids_ref, a_ref, b_ref, *rest):
        del ids_ref
        o_ref, scratch = rest[n_skip], rest[n_skip + 1:]
        prod = _dot(a_ref[...], b_ref[...], dims)
        if nk == 1:
            o_ref[...] = prod.astype(o_ref.dtype)
        else:
            acc_ref, = scratch
            k = pl.program_id(2)

            @pl.when(k == 0)
            def _():
                acc_ref[...] = prod

            @pl.when(k > 0)
            def _():
                acc_ref[...] += prod

            @pl.when(k == nk - 1)
            def _():
                o_ref[...] = acc_ref[...].astype(o_ref.dtype)

    scratch = [] if nk == 1 else [pltpu.VMEM(tuple(d for d in o_spec.block_shape if d is not None), F32)]
    extra = list(after) + ([prev] if prev is not None else [])
    aliases = {3 + len(after): 0} if prev is not None else {}
    return pl.pallas_call(
        body, name=name, out_shape=out_shape, input_output_aliases=aliases,
        grid_spec=pltpu.PrefetchScalarGridSpec(
            num_scalar_prefetch=1, grid=grid, in_specs=[a_spec, b_spec] + [ANY_SPEC] * len(extra),
            out_specs=o_spec, scratch_shapes=scratch),
        compiler_params=_params("parallel", "parallel", "arbitrary"),
    )(ids, a, b, *extra)


def _proj_in(ids, h, w_roles, r0, nr, prev, name, after=()):
    T, D = h.shape
    n_roles, _, wd = w_roles.shape
    tm, tn = _pick(T, 1024), _pick(wd, 640)
    per = wd // tn
    return _matmul(
        ids, h, w_roles, grid=(T // tm, nr * per, 1),
        a_spec=pl.BlockSpec((tm, D), lambda i, j, k, ids: (i, 0)),
        b_spec=pl.BlockSpec((None, D, tn), lambda i, j, k, ids: (r0 + j // per, 0, j % per)),
        o_spec=pl.BlockSpec((tm, tn), lambda i, j, k, ids: (i, ids[r0 + j // per] * per + j % per)),
        out_shape=jax.ShapeDtypeStruct((T, n_roles * wd), F32), trans_b=False, name=name, after=after, prev=prev)


def _proj_out(ids, mixed, wo_roles):
    T = mixed.shape[0]
    n_roles, R, D = wo_roles.shape
    tm, tn = _pick(T, 2048), _pick(D, 1024)
    return _matmul(
        ids, mixed, wo_roles, grid=(T // tm, D // tn, n_roles),
        a_spec=pl.BlockSpec((tm, R), lambda i, j, k, ids: (i, ids[k])),
        b_spec=pl.BlockSpec((None, R, tn), lambda i, j, k, ids: (k, 0, j)),
        o_spec=pl.BlockSpec((tm, tn), lambda i, j, k, ids: (i, j)),
        out_shape=jax.ShapeDtypeStruct((T, D), F32), trans_b=False, name="proj_out")


def _dmixed(ids, dy, wo_roles):
    T, D = dy.shape
    n_roles, R, _ = wo_roles.shape
    tm = _pick(T, 1024)
    return _matmul(
        ids, dy, wo_roles, grid=(T // tm, n_roles, 1),
        a_spec=pl.BlockSpec((tm, D), lambda i, j, k, ids: (i, 0)),
        b_spec=pl.BlockSpec((None, R, D), lambda i, j, k, ids: (j, 0, 0)),
        o_spec=pl.BlockSpec((tm, R), lambda i, j, k, ids: (i, ids[j])),
        out_shape=jax.ShapeDtypeStruct((T, n_roles * R), F32), trans_b=True, name="dmixed")


def _dw_out(ids, mixed_t, dy, n_roles):
    E, T = mixed_t.shape
    D = dy.shape[1]
    R = E // n_roles
    tn = _pick(D, 512)
    return _matmul(
        ids, mixed_t, dy, grid=(n_roles, D // tn, 1),
        a_spec=pl.BlockSpec((R, T), lambda i, j, k, ids: (ids[i], 0)),
        b_spec=pl.BlockSpec((T, tn), lambda i, j, k, ids: (0, j)),
        o_spec=pl.BlockSpec((None, R, tn), lambda i, j, k, ids: (i, 0, j)),
        out_shape=jax.ShapeDtypeStruct((n_roles, R, D), BF16), trans_b=False, name="dw_out")


def _dw_in(ids, h_t, dproj, n_roles, r0, nr, name, after=()):
    D, T = h_t.shape
    wd = dproj.shape[1] // n_roles
    tm, tn = _pick(D, 1024), _pick(wd, 640)
    per = wd // tn
    return _matmul(
        ids, h_t, dproj, grid=(D // tm, nr * per, 1),
        a_spec=pl.BlockSpec((tm, T), lambda i, j, k, ids: (i, 0)),
        b_spec=pl.BlockSpec((T, tn), lambda i, j, k, ids: (0, ids[r0 + j // per] * per + j % per)),
        o_spec=pl.BlockSpec((None, tm, tn), lambda i, j, k, ids: (j // per, i, j % per)),
        out_shape=jax.ShapeDtypeStruct((nr, D, wd), BF16), trans_b=False, name=name, after=after)


def _dh(ids, dproj, w_roles, after=()):
    T = dproj.shape[0]
    n_roles, D, wd = w_roles.shape
    tm, tn = _pick(T, 1024), _pick(D, 1024)
    return _matmul(
        ids, dproj, w_roles, grid=(T // tm, D // tn, n_roles),
        a_spec=pl.BlockSpec((tm, wd), lambda i, j, k, ids: (i, ids[k])),
        b_spec=pl.BlockSpec((None, tn, wd), lambda i, j, k, ids: (k, j, 0)),
        o_spec=pl.BlockSpec((tm, tn), lambda i, j, k, ids: (i, j)),
        out_shape=jax.ShapeDtypeStruct((T, D), F32), trans_b=True, name="dh", after=after)


def _cast_slot0(w, n_roles, name):
    R, C = w.shape
    tr = _pick(R, 256)

    def body(w_ref, o_ref):
        o_ref[...] = w_ref[...].astype(BF16)

    return pl.pallas_call(
        body, name=name, grid=(R // tr,), in_specs=[pl.BlockSpec((tr, C), lambda i: (i, 0))],
        out_specs=pl.BlockSpec((None, tr, C), lambda i: (0, i, 0)),
        out_shape=jax.ShapeDtypeStruct((n_roles, R, C), BF16), compiler_params=_params("parallel"))(w)


def _prenorm(x, gain):
    T, D = x.shape
    tm = _pick(T, 256)

    def body(x_ref, g_ref, h_ref, ht_ref):
        xv = x_ref[...]
        r = lax.rsqrt(jnp.mean(xv * xv, axis=-1, keepdims=True) + NORM_EPS)
        h = xv * r * g_ref[...]
        h_ref[...] = h.astype(BF16)
        ht_ref[...] = h.T.astype(BF16)

    return pl.pallas_call(
        body, name="prenorm", grid=(T // tm,),
        in_specs=[pl.BlockSpec((tm, D), lambda i: (i, 0)), pl.BlockSpec((1, D), lambda i: (0, 0))],
        out_specs=[pl.BlockSpec((tm, D), lambda i: (i, 0)), pl.BlockSpec((D, tm), lambda i: (0, i))],
        out_shape=[jax.ShapeDtypeStruct((T, D), BF16), jax.ShapeDtypeStruct((D, T), BF16)],
        compiler_params=_params("parallel"))(x, gain)


def _norm_bwd(u, yn, r):
    return r * (u - yn * jnp.mean(u * yn, axis=-1, keepdims=True))


def _loss_head(y, x, target, gain):
    T, D = y.shape
    tm = _pick(T, 256)

    def body(y_ref, x_ref, t_ref, g_ref, dy_ref, dout_ref, gpost_ref, sq_ref):
        yv = y_ref[...]
        g = g_ref[...]
        r = lax.rsqrt(jnp.mean(yv * yv, axis=-1, keepdims=True) + NORM_EPS)
        yn = yv * r
        err = x_ref[...] + yn * g - t_ref[...]
        dout = err * (1.0 / D)
        dy_ref[...] = _norm_bwd(dout * g, yn, r).astype(BF16)
        dout_ref[...] = dout

        @pl.when(pl.program_id(0) == 0)
        def _():
            gpost_ref[...] = jnp.zeros_like(gpost_ref)
            sq_ref[...] = jnp.zeros_like(sq_ref)

        gpost_ref[...] += jnp.sum(dout * yn, axis=0, keepdims=True)
        sq_ref[...] += jnp.sum(err * err, axis=0, keepdims=True)

    row = pl.BlockSpec((tm, D), lambda i: (i, 0))
    vec = pl.BlockSpec((1, D), lambda i: (0, 0))
    return pl.pallas_call(
        body, name="loss_head", grid=(T // tm,), in_specs=[row, row, row, vec], out_specs=[row, row, vec, vec],
        out_shape=[jax.ShapeDtypeStruct((T, D), BF16), jax.ShapeDtypeStruct((T, D), F32),
                   jax.ShapeDtypeStruct((1, D), F32), jax.ShapeDtypeStruct((1, D), F32)],
        compiler_params=_params("arbitrary"))(y, x, target, gain)


def _prenorm_bwd(x, dh, dout, gain):
    T, D = x.shape
    tm = _pick(T, 256)

    def body(x_ref, dh_ref, dout_ref, g_ref, gx_ref, gpre_ref):
        xv = x_ref[...]
        dhv = dh_ref[...]
        r = lax.rsqrt(jnp.mean(xv * xv, axis=-1, keepdims=True) + NORM_EPS)
        xn = xv * r
        gx_ref[...] = dout_ref[...] + _norm_bwd(dhv * g_ref[...], xn, r)

        @pl.when(pl.program_id(0) == 0)
        def _():
            gpre_ref[...] = jnp.zeros_like(gpre_ref)

        gpre_ref[...] += jnp.sum(dhv * xn, axis=0, keepdims=True)

    row = pl.BlockSpec((tm, D), lambda i: (i, 0))
    vec = pl.BlockSpec((1, D), lambda i: (0, 0))
    return pl.pallas_call(
        body, name="prenorm_bwd", grid=(T // tm,), in_specs=[row, row, row, vec], out_specs=[row, vec],
        out_shape=[jax.ShapeDtypeStruct((T, D), F32), jax.ShapeDtypeStruct((1, D), F32)],
        compiler_params=_params("arbitrary"))(x, dh, dout, gain)


def _attn_masks(n):
    row = lax.broadcasted_iota(jnp.int32, (2 * WINDOW, 2 * WINDOW), 0) % WINDOW
    col = lax.broadcasted_iota(jnp.int32, (2 * WINDOW, 2 * WINDOW), 1)
    valid = (col > row) & (col <= row + WINDOW) & ((n > 0) | (col >= WINDOW))
    low = lax.broadcasted_iota(jnp.int32, (1, LANES), 1) < ATTN_HEAD
    top = lax.broadcasted_iota(jnp.int32, (2 * WINDOW, 1), 0) < WINDOW
    return valid, low, top


def _dup_half(pair, keep):
    return jnp.where(keep, pair, pltpu.roll(pair, ATTN_HEAD, 1))


def _fold_half(v):
    return v + pltpu.roll(v, ATTN_HEAD, 1)


def _attn_probs(qpair, k2, sink_lo, sink_hi, valid, low, top):
    q2 = jnp.concatenate([jnp.where(low, qpair, 0.0), jnp.where(low, 0.0, qpair)], axis=0).astype(BF16)
    s = _dot(q2, k2, NT_DIMS) * (ATTN_HEAD ** -0.5)
    s = jnp.where(valid, s, -jnp.inf)
    sink = jnp.where(top, sink_lo, sink_hi)
    m = jnp.maximum(jnp.max(s, axis=-1, keepdims=True), sink)
    p = jnp.exp(s - m)
    psink = jnp.exp(sink - m)
    inv = 1.0 / (jnp.sum(p, axis=-1, keepdims=True) + psink)
    return q2, p * inv, psink * inv


def _attn_specs(D):
    kb = D // LANES
    vb = kb + D // (8 * LANES)
    gb = (D + D // 4) // 512
    wide = lambda off: [pl.BlockSpec((WINDOW, 512), functools.partial(lambda o, e, jp, n: (n, o + 2 * jp + e), off, e))
                        for e in (0, 1)]
    cur = lambda off: pl.BlockSpec((WINDOW, LANES), functools.partial(lambda o, jp, n: (n, o + jp), off))
    prev = lambda off: pl.BlockSpec((WINDOW, LANES),
                                    functools.partial(lambda o, jp, n: (jnp.maximum(n - 1, 0), o + jp), off))
    return wide(0) + [cur(kb), prev(kb), cur(vb), prev(vb)] + wide(gb)


def _attn_fwd(proj, sinks, D):
    T = proj.shape[0]
    nb, njp = T // WINDOW, D // 1024

    def body(sink_ref, qlo_ref, qhi_ref, kc_ref, kp_ref, vc_ref, vp_ref, glo_ref, ghi_ref, mix_ref, mixt_ref):
        jp, n = pl.program_id(0), pl.program_id(1)
        valid, low, top = _attn_masks(n)
        kk = jnp.concatenate([kp_ref[...], kc_ref[...]], axis=0)
        vv = jnp.concatenate([vp_ref[...], vc_ref[...]], axis=0)
        for hj, (q_ref, g_ref) in enumerate(((qlo_ref, glo_ref), (qhi_ref, ghi_ref))):
            keep = low if hj == 0 else jnp.logical_not(low)
            k2 = _dup_half(kk, keep).astype(BF16)
            v2 = _dup_half(vv, keep).astype(BF16)
            for p in range(4):
                cols = slice(LANES * p, LANES * (p + 1))
                head = (2 * jp + hj) * GQA + 2 * p
                _, probs, _ = _attn_probs(q_ref[:, cols], k2, sink_ref[0, head], sink_ref[0, head + 1],
                                          valid, low, top)
                o2 = _dot(probs.astype(BF16), v2, NN_DIMS)
                opair = jnp.where(low, o2[:WINDOW], o2[WINDOW:])
                g = g_ref[:, cols]
                out = opair * (g * _sigmoid(g))
                oc = slice(512 * hj + LANES * p, 512 * hj + LANES * (p + 1))
                mix_ref[:, oc] = out.astype(BF16)
                mixt_ref[oc, :] = out.T.astype(BF16)

    return pl.pallas_call(
        body, name="attn_fwd", grid=(njp, nb),
        in_specs=[pl.BlockSpec(memory_space=pltpu.SMEM)] + _attn_specs(D),
        out_specs=[pl.BlockSpec((WINDOW, 1024), lambda jp, n: (n, jp)),
                   pl.BlockSpec((1024, WINDOW), lambda jp, n: (jp, n))],
        out_shape=[jax.ShapeDtypeStruct((T, 2 * D), BF16), jax.ShapeDtypeStruct((2 * D, T), BF16)],
        compiler_params=_params("parallel", "parallel"))(sinks, *([proj] * 8))


def _attn_bwd(proj, sinks, dmix, D, after=()):
    T = proj.shape[0]
    nb, njp = T // WINDOW, D // 1024
    n_after = len(after)

    def body(sink_ref, qlo_ref, qhi_ref, kc_ref, kp_ref, vc_ref, vp_ref, glo_ref, ghi_ref, dmix_ref, *rest):
        dq_ref, dk_ref, dv_ref, dg_ref, dsink_ref, kcarry_ref, vcarry_ref = rest[n_after:]
        jp, step = pl.program_id(0), pl.program_id(1)
        n = nb - 1 - step
        valid, low, top = _attn_masks(n)
        lane = lax.broadcasted_iota(jnp.int32, (1, LANES), 1)
        kk = jnp.concatenate([kp_ref[...], kc_ref[...]], axis=0)
        vv = jnp.concatenate([vp_ref[...], vc_ref[...]], axis=0)

        @pl.when(step == 0)
        def _():
            kcarry_ref[...] = jnp.zeros_like(kcarry_ref)
            vcarry_ref[...] = jnp.zeros_like(vcarry_ref)
            dsink_ref[...] = jnp.zeros_like(dsink_ref)

        dk_pair = jnp.zeros((2 * WINDOW, LANES), F32)
        dv_pair = jnp.zeros((2 * WINDOW, LANES), F32)
        dsink = jnp.zeros((1, LANES), F32)
        for hj, (q_ref, g_ref) in enumerate(((qlo_ref, glo_ref), (qhi_ref, ghi_ref))):
            keep = low if hj == 0 else jnp.logical_not(low)
            k2 = _dup_half(kk, keep).astype(BF16)
            v2 = _dup_half(vv, keep).astype(BF16)
            dk_head = jnp.zeros((2 * WINDOW, LANES), F32)
            dv_head = jnp.zeros((2 * WINDOW, LANES), F32)
            for p in range(4):
                cols = slice(LANES * p, LANES * (p + 1))
                oc = slice(512 * hj + LANES * p, 512 * hj + LANES * (p + 1))
                head = (2 * jp + hj) * GQA + 2 * p
                q2, probs, psink = _attn_probs(q_ref[:, cols], k2, sink_ref[0, head], sink_ref[0, head + 1],
                                               valid, low, top)
                pb = probs.astype(BF16)
                o2 = _dot(pb, v2, NN_DIMS)
                opair = jnp.where(low, o2[:WINDOW], o2[WINDOW:])
                g = g_ref[:, cols]
                sg = _sigmoid(g)
                dgated = dmix_ref[:, oc]
                dg_ref[:, oc] = (dgated * opair * (sg * (1.0 + g * (1.0 - sg)))).astype(BF16)
                do = dgated * (g * sg)
                do2 = jnp.concatenate([jnp.where(low, do, 0.0), jnp.where(low, 0.0, do)], axis=0).astype(BF16)
                dp = _dot(do2, v2, NT_DIMS)
                delta = jnp.sum(probs * dp, axis=-1, keepdims=True)
                ds = (probs * (dp - delta) * (ATTN_HEAD ** -0.5)).astype(BF16)
                dq2 = _dot(ds, k2, NN_DIMS)
                dq_ref[:, oc] = jnp.where(low, dq2[:WINDOW], dq2[WINDOW:]).astype(BF16)
                dk_head += _dot(ds, q2, TN_DIMS)
                dv_head += _dot(pb, do2, TN_DIMS)
                ps = psink * delta
                local = hj * GQA + 2 * p
                dsink -= jnp.where(lane == local, jnp.sum(ps[:WINDOW], axis=0, keepdims=True), 0.0)
                dsink -= jnp.where(lane == local + 1, jnp.sum(ps[WINDOW:], axis=0, keepdims=True), 0.0)
            dk_pair += jnp.where(keep, _fold_half(dk_head), 0.0)
            dv_pair += jnp.where(keep, _fold_half(dv_head), 0.0)
        dk_ref[...] = (dk_pair[WINDOW:] + kcarry_ref[...]).astype(BF16)
        dv_ref[...] = (dv_pair[WINDOW:] + vcarry_ref[...]).astype(BF16)
        kcarry_ref[...] = dk_pair[:WINDOW]
        vcarry_ref[...] = dv_pair[:WINDOW]
        dsink_ref[...] += dsink

    rev = lambda spec: pl.BlockSpec(spec.block_shape, functools.partial(
        lambda f, jp, s: f(jp, nb - 1 - s), spec.index_map))
    in_specs = [rev(s) for s in _attn_specs(D)]
    wide_out = pl.BlockSpec((WINDOW, 1024), lambda jp, s: (nb - 1 - s, jp))
    pair_out = pl.BlockSpec((WINDOW, LANES), lambda jp, s: (nb - 1 - s, jp))
    return pl.pallas_call(
        body, name="attn_bwd", grid=(njp, nb),
        in_specs=[pl.BlockSpec(memory_space=pltpu.SMEM)] + in_specs + [wide_out] + [ANY_SPEC] * n_after,
        out_specs=[wide_out, pair_out, pair_out, wide_out, pl.BlockSpec((None, 1, LANES), lambda jp, s: (jp, 0, 0))],
        out_shape=[jax.ShapeDtypeStruct((T, D), BF16), jax.ShapeDtypeStruct((T, D // 8), BF16),
                   jax.ShapeDtypeStruct((T, D // 8), BF16), jax.ShapeDtypeStruct((T, D), BF16),
                   jax.ShapeDtypeStruct((njp, 1, LANES), F32)],
        scratch_shapes=[pltpu.VMEM((WINDOW, LANES), F32), pltpu.VMEM((WINDOW, LANES), F32)],
        compiler_params=_params("parallel", "arbitrary"))(sinks, *([proj] * 8), dmix, *after)


RNN_TB = 512
RNN_HB = 4
RNN_WIDE = RNN_HB * RNN_HEAD


def _split3(v):
    a = v.astype(BF16)
    r = v - a.astype(F32)
    b = r.astype(BF16)
    c = (r - b.astype(F32)).astype(BF16)
    return a, b, c


def _tri_sum(tri, v):
    a, b, c = _split3(v)
    return _dot(tri, a, NN_DIMS) + _dot(tri, b, NN_DIMS) + _dot(tri, c, NN_DIMS)


def _lower_bound(lb_ref):
    l0, l1 = lb_ref[0:1, :], lb_ref[1:2, :]
    m = jnp.maximum(l0, l1)
    e0, e1 = jnp.exp(l0 - m), jnp.exp(l1 - m)
    return e0 / (e0 + e1)


def _rnn_gates(rq, rf, lb):
    sq = _sigmoid(rq)
    sf = _sigmoid(rf)
    f = lb + (1.0 - lb) * sf
    return sq, sf, f


def _rnn_decays(g, tri):
    G = _tri_sum(tri, g)
    last = G[CHUNK - 1:CHUNK, :]
    mid = G[CHUNK // 2 - 1:CHUNK // 2, :]
    return G, jnp.exp(G), jnp.exp(G - mid), jnp.exp(mid - G), jnp.exp(last - G), jnp.exp(last)


def _chunk_masks():
    r = lax.broadcasted_iota(jnp.int32, (CHUNK, CHUNK), 0)
    c = lax.broadcasted_iota(jnp.int32, (CHUNK, CHUNK), 1)
    return r >= c, (r >= c).astype(BF16), (r <= c).astype(BF16)


def _rnn_specs(T, D, tb, rev):
    nt = T // tb
    base = (2 * D + D // 4) // LANES
    t_of = (lambda s: nt - 1 - s) if rev else (lambda s: s)
    assert base % RNN_HB == 0 and (D // LANES) % RNN_HB == 0
    cols = [pl.BlockSpec((tb, RNN_WIDE), functools.partial(lambda o, h, s: (t_of(s), o + h),
                                                            (base + i * (D // LANES)) // RNN_HB))
            for i in range(4)]
    return cols, t_of


def _rnn_fwd(proj, lb_logits, rnn_norm, mixed, mixed_t, D):
    T = proj.shape[0]
    tb = _pick(T, RNN_TB)
    nt, nh, cpb = T // tb, D // RNN_HEAD, tb // CHUNK
    cols, _ = _rnn_specs(T, D, tb, False)

    def body(rq_ref, rf_ref, ri_ref, rg_ref, lb_ref, gain_ref, mix_in, mixt_in,
             mix_ref, mixt_ref, o_ref, st_ref, state_ref):
        del mix_in, mixt_in
        causal, tri, _ = _chunk_masks()
        lb = _lower_bound(lb_ref)

        @pl.when(pl.program_id(1) == 0)
        def _():
            state_ref[...] = jnp.zeros_like(state_ref)

        def chunk(c, carry):
            rows = pl.ds(pl.multiple_of(c * CHUNK, CHUNK), CHUNK)
            for hh in range(RNN_HB):
                ln = slice(RNN_HEAD * hh, RNN_HEAD * (hh + 1))
                rq, rf, v, rg = rq_ref[rows, ln], rf_ref[rows, ln], ri_ref[rows, ln], rg_ref[rows, ln]
                sq, _, f = _rnn_gates(rq, rf, lb[:, ln])
                q, k = rq * sq, 1.0 - f
                _, eG, eq, ek, ekl, elast = _rnn_decays(jnp.log(f), tri)
                st = state_ref[hh]
                st_ref[hh, c] = st
                vb = v.astype(BF16)
                att = jnp.where(causal, _dot((q * eq).astype(BF16), (k * ek).astype(BF16), NT_DIMS), 0.0)
                o = _dot((q * eG).astype(BF16), st.astype(BF16), NT_DIMS) + _dot(att.astype(BF16), vb, NN_DIMS)
                state_ref[hh] = st * elast + _dot(vb, (k * ekl).astype(BF16), TN_DIMS)
                o_ref[rows, ln] = o
                r = lax.rsqrt(jnp.mean(o * o, axis=-1, keepdims=True) + NORM_EPS)
                out = (o * r * gain_ref[:, ln]) * (rg * _sigmoid(rg))
                mix_ref[rows, ln] = out.astype(BF16)
            return carry

        lax.fori_loop(0, cpb, chunk, 0)
        mixt_ref[...] = mix_ref[...].astype(F32).T.astype(BF16)

    nh //= RNN_HB
    vec2 = pl.BlockSpec((2, RNN_WIDE), lambda h, s: (0, h))
    vec1 = pl.BlockSpec((1, RNN_WIDE), lambda h, s: (0, h))
    return pl.pallas_call(
        body, name="rnn_fwd", grid=(nh, nt),
        in_specs=cols + [vec2, vec1, ANY_SPEC, ANY_SPEC],
        out_specs=[pl.BlockSpec((tb, RNN_WIDE), lambda h, s: (s, D // RNN_WIDE + h)),
                   pl.BlockSpec((RNN_WIDE, tb), lambda h, s: (D // RNN_WIDE + h, s)),
                   pl.BlockSpec((tb, RNN_WIDE), lambda h, s: (s, h)),
                   pl.BlockSpec((RNN_HB, cpb, RNN_HEAD, RNN_HEAD), lambda h, s: (h, s, 0, 0))],
        out_shape=[jax.ShapeDtypeStruct(mixed.shape, BF16), jax.ShapeDtypeStruct(mixed_t.shape, BF16),
                   jax.ShapeDtypeStruct((T, D), F32),
                   jax.ShapeDtypeStruct((nh * RNN_HB, T // CHUNK, RNN_HEAD, RNN_HEAD), F32)],
        scratch_shapes=[pltpu.VMEM((RNN_HB, RNN_HEAD, RNN_HEAD), F32)],
        input_output_aliases={6: 0, 7: 1},
        compiler_params=_params("parallel", "arbitrary"))(proj, proj, proj, proj, lb_logits, rnn_norm, mixed, mixed_t)


def _rnn_bwd(proj, lb_logits, rnn_norm, o_raw, states, dmix, D):
    T = proj.shape[0]
    tb = _pick(T, RNN_TB)
    nt, nh, cpb = T // tb, D // RNN_HEAD, tb // CHUNK
    cols, t_of = _rnn_specs(T, D, tb, True)

    def body(rq_ref, rf_ref, ri_ref, rg_ref, lb_ref, gain_ref, o_ref, st_ref, dmix_ref,
             drq_ref, drf_ref, dri_ref, drg_ref, dlb_ref, dgain_ref, dstate_ref):
        causal, tri, tri_t = _chunk_masks()
        lb = _lower_bound(lb_ref)
        gain = gain_ref[...]

        @pl.when(pl.program_id(1) == 0)
        def _():
            dstate_ref[...] = jnp.zeros_like(dstate_ref)
            dlb_ref[...] = jnp.zeros_like(dlb_ref)
            dgain_ref[...] = jnp.zeros_like(dgain_ref)

        def head(c, rows, hh):
            ln = slice(RNN_HEAD * hh, RNN_HEAD * (hh + 1))
            rq, rf, v, rg = rq_ref[rows, ln], rf_ref[rows, ln], ri_ref[rows, ln], rg_ref[rows, ln]
            o, dgated = o_ref[rows, ln], dmix_ref[rows, ln]
            lbh, gainh = lb[:, ln], gain[:, ln]
            sgt = _sigmoid(rg)
            r = lax.rsqrt(jnp.mean(o * o, axis=-1, keepdims=True) + NORM_EPS)
            on = o * r
            drg_ref[rows, ln] = (dgated * (on * gainh) * (sgt * (1.0 + rg * (1.0 - sgt)))).astype(BF16)
            d_on = dgated * (rg * sgt)
            dgain_ref[:, ln] += jnp.sum(d_on * on, axis=0, keepdims=True)
            do = _norm_bwd(d_on * gainh, on, r)
            sq, sf, f = _rnn_gates(rq, rf, lbh)
            q, k = rq * sq, 1.0 - f
            _, eG, eq, ek, ekl, elast = _rnn_decays(jnp.log(f), tri)
            st, dst = st_ref[hh, c], dstate_ref[hh]
            qg, qm, km, kl = q * eG, q * eq, k * ek, k * ekl
            qgb, qmb, kmb, klb = qg.astype(BF16), qm.astype(BF16), km.astype(BF16), kl.astype(BF16)
            dob, vb, dstb = do.astype(BF16), v.astype(BF16), dst.astype(BF16)
            att = jnp.where(causal, _dot(qmb, kmb, NT_DIMS), 0.0).astype(BF16)
            datt = jnp.where(causal, _dot(dob, vb, NT_DIMS), 0.0).astype(BF16)
            dqg = _dot(dob, st.astype(BF16), NN_DIMS)
            dqm = _dot(datt, kmb, NN_DIMS)
            dkm = _dot(datt, qmb, TN_DIMS)
            dkl = _dot(vb, dstb, NN_DIMS)
            dri_ref[rows, ln] = (_dot(att, dob, TN_DIMS) + _dot(klb, dstb, NT_DIMS)).astype(BF16)
            dstate_ref[hh] = dst * elast + _dot(dob, qgb, TN_DIMS)
            dq = dqg * eG + dqm * eq
            dk = dkm * ek + dkl * ekl
            dG = dqg * qg + dqm * qmb.astype(F32) - dkm * kmb.astype(F32) - dkl * kl
            dlast = jnp.sum(dkl * kl, axis=0, keepdims=True) + jnp.sum(dst * st, axis=0, keepdims=True) * elast
            last_row = lax.broadcasted_iota(jnp.int32, (CHUNK, 1), 0) == CHUNK - 1
            dg = _tri_sum(tri_t, dG + jnp.where(last_row, dlast, 0.0))
            df = dg / f - dk
            drq_ref[rows, ln] = (dq * (sq * (1.0 + rq * (1.0 - sq)))).astype(BF16)
            drf_ref[rows, ln] = (df * (1.0 - lbh) * (sf * (1.0 - sf))).astype(BF16)
            dlb_ref[:, ln] += jnp.sum(df * (1.0 - sf), axis=0, keepdims=True)

        def chunk(i, carry):
            c = cpb - 1 - i
            rows = pl.ds(pl.multiple_of(c * CHUNK, CHUNK), CHUNK)
            for hh in range(RNN_HB):
                head(c, rows, hh)
            return carry

        lax.fori_loop(0, cpb, chunk, 0)

    nh //= RNN_HB
    vec2 = pl.BlockSpec((2, RNN_WIDE), lambda h, s: (0, h))
    vec1 = pl.BlockSpec((1, RNN_WIDE), lambda h, s: (0, h))
    blk = pl.BlockSpec((tb, RNN_WIDE), lambda h, s: (t_of(s), h))
    return pl.pallas_call(
        body, name="rnn_bwd", grid=(nh, nt),
        in_specs=cols + [vec2, vec1, blk,
                         pl.BlockSpec((RNN_HB, cpb, RNN_HEAD, RNN_HEAD), lambda h, s: (h, t_of(s), 0, 0)),
                         pl.BlockSpec((tb, RNN_WIDE), lambda h, s: (t_of(s), D // RNN_WIDE + h))],
        out_specs=[blk, blk, blk, blk, vec1, vec1],
        out_shape=[jax.ShapeDtypeStruct((T, D), BF16)] * 4 + [jax.ShapeDtypeStruct((1, D), F32)] * 2,
        scratch_shapes=[pltpu.VMEM((RNN_HB, RNN_HEAD, RNN_HEAD), F32)],
        compiler_params=_params("parallel", "arbitrary"))(proj, proj, proj, proj, lb_logits, rnn_norm, o_raw, states, dmix)


def _adamw(w, g, m, v):
    m = ADAM_B1 * m + (1.0 - ADAM_B1) * g
    v = ADAM_B2 * v + (1.0 - ADAM_B2) * (g * g)
    m_hat = m / (1.0 - ADAM_B1 ** ADAM_STEP)
    v_hat = v / (1.0 - ADAM_B2 ** ADAM_STEP)
    delta = -ADAM_LR * (m_hat / (jnp.sqrt(v_hat) + ADAM_EPS) + ADAM_WD * w)
    return delta, m, v


def _adamw_big(w, m, v, parts, tr, name):
    R, C = w.shape
    n_parts = len(parts)

    def body(w_ref, m_ref, v_ref, *rest):
        part_refs = rest[:n_parts]
        g_ref, d_ref, nm_ref, nv_ref = rest[n_parts:]
        g = part_refs[0][...].astype(F32)
        for p_ref in part_refs[1:]:
            g = g + p_ref[...].astype(F32)
        delta, nm, nv = _adamw(w_ref[...], g, m_ref[...], v_ref[...])
        g_ref[...] = g
        d_ref[...] = delta
        nm_ref[...] = nm
        nv_ref[...] = nv

    blk = pl.BlockSpec((tr, C), lambda i: (i, 0))
    part_specs = [pl.BlockSpec((None, tr, C), functools.partial(lambda s, i: (s, i, 0), slot)) for _, slot in parts]
    return pl.pallas_call(
        body, name=name, grid=(R // tr,), in_specs=[blk, blk, blk] + part_specs,
        out_specs=[blk] * 4, out_shape=[jax.ShapeDtypeStruct((R, C), F32)] * 4,
        compiler_params=_params("parallel"))(w, m, v, *[a for a, _ in parts])


def _pair_sum(pa, qa):
    n, R, C = qa.shape
    tr = _pick(R, 256)

    def body(p_ref, q_ref, r_ref):
        r_ref[...] = (p_ref[...].astype(F32) + q_ref[...].astype(F32)).astype(BF16)

    return pl.pallas_call(
        body, name="pair_sum", grid=(n, R // tr),
        in_specs=[pl.BlockSpec((None, tr, C), lambda j, i: (2 * j, i, 0)), pl.BlockSpec((None, tr, C), lambda j, i: (j, i, 0))],
        out_specs=pl.BlockSpec((None, tr, C), lambda j, i: (j, i, 0)),
        out_shape=jax.ShapeDtypeStruct((n, R, C), BF16), compiler_params=_params("parallel", "parallel"))(pa, qa)


def _adamw_small(total, sinks, lb_logits, rnn_norm, pre_norm, post_norm, moments, D):
    params = [sinks, lb_logits, rnn_norm, pre_norm, post_norm]

    def body(tot_ref, *refs):
        p_refs, m_refs, v_refs = refs[0:5], refs[5:10], refs[10:15]
        loss_ref, outs = refs[15], refs[16:]
        tot = tot_ref[...]
        l0, l1 = p_refs[1][0:1, :], p_refs[1][1:2, :]
        mx = jnp.maximum(l0, l1)
        e0, e1 = jnp.exp(l0 - mx), jnp.exp(l1 - mx)
        p0, p1 = e0 / (e0 + e1), e1 / (e0 + e1)
        dlb = tot[0:1, :]
        grads = [tot[5:6, 0:LANES], jnp.concatenate([dlb * p0 * (1.0 - p0), -dlb * p0 * p1], axis=0),
                 tot[1:2, :], tot[2:3, :], tot[3:4, :]]
        loss_ref[...] = 0.5 / D * jnp.sum(tot[4:5, :], axis=-1, keepdims=True)
        for i, g in enumerate(grads):
            delta, nm, nv = _adamw(p_refs[i][...], g, m_refs[i][...], v_refs[i][...])
            outs[4 * i][...] = g
            outs[4 * i + 1][...] = delta
            outs[4 * i + 2][...] = nm
            outs[4 * i + 3][...] = nv

    out_shape = [jax.ShapeDtypeStruct((1, 1), F32)]
    for p in params:
        out_shape += [jax.ShapeDtypeStruct(p.shape, F32)] * 4
    return pl.pallas_call(body, name="adamw_small", out_shape=out_shape)(total, *params, *moments)


SIBLING = 1


def _me():
    return lax.axis_index("x"), lax.axis_index("y"), lax.axis_index("c")


def _flat(px, py, pc):
    return 4 * px + 2 * py + pc


def _role_peer(role, x, y, c):
    if role < 2:
        return (x, y, (1 - c) if role else c)
    j, other = (role - 2) // 2, (role - 2) % 2
    px = (1 - x) if j in (0, 2) else x
    py = (1 - y) if j in (1, 2) else y
    return (px, py, (1 - c) if other else c)


def _role_ids():
    x, y, c = _me()
    return jnp.stack([_flat(*_role_peer(r, x, y, c)) for r in range(N_DEV)]).astype(jnp.int32)


def _comm_call(name, bufs, waits=(), starts=(), after=()):
    n_buf, n_wait, n_start, n_after = len(bufs), len(waits), len(starts), len(after)

    def body(*refs):
        buf_refs = refs[:n_buf]
        sem_refs = refs[n_buf:n_buf + 2 * n_wait]
        outs = refs[n_buf + 2 * n_wait + n_after:]
        new_sems, token = outs[:2 * n_start], outs[-1]
        x, y, c = _me()
        for w, (_, kind, like) in enumerate(waits):
            shape_ref = buf_refs[like].at[0]
            cp = pltpu.make_async_remote_copy(
                src_ref=shape_ref, dst_ref=shape_ref, send_sem=sem_refs[2 * w], recv_sem=sem_refs[2 * w + 1],
                device_id=(x, y, c), device_id_type=MESH)
            if kind == "send":
                cp.wait_send()
            else:
                cp.wait_recv()
        for s, (sb, ss, db, ds, role) in enumerate(starts):
            pltpu.make_async_remote_copy(
                src_ref=buf_refs[sb].at[ss], dst_ref=buf_refs[db].at[ds], send_sem=new_sems[2 * s],
                recv_sem=new_sems[2 * s + 1], device_id=_role_peer(role, x, y, c), device_id_type=MESH).start()
        token[...] = jnp.zeros_like(token)

    sems = [s for flight, _, _ in waits for s in flight]
    out = pl.pallas_call(
        body, name=name,
        out_shape=tuple([pltpu.SemaphoreType.DMA(())] * (2 * n_start) + [pltpu.HBM(b.shape, b.dtype) for b in bufs]
                        + [jax.ShapeDtypeStruct((8, LANES), F32)]),
        in_specs=tuple([HBM_SPEC] * n_buf + [SEM_SPEC] * (2 * n_wait) + [ANY_SPEC] * n_after),
        out_specs=tuple([SEM_SPEC] * (2 * n_start) + [HBM_SPEC] * n_buf + [pl.BlockSpec(memory_space=pltpu.VMEM)]),
        input_output_aliases={i: 2 * n_start + i for i in range(n_buf)},
        compiler_params=pltpu.CompilerParams(has_side_effects=EFFECT),
    )(*[pltpu.with_memory_space_constraint(b, pltpu.HBM) for b in bufs], *sems, *after)
    flights = [(out[2 * s], out[2 * s + 1]) for s in range(n_start)]
    return list(out[2 * n_start:2 * n_start + n_buf]), flights, out[-1]


def _landing(shape):
    return lax.empty(shape, BF16)


def _both(flights, like):
    return [(f, kind, like) for f in flights for kind in ("send", "recv")]


def _allreduce_small(rows, D):
    n_rows = len(rows)

    def body(*refs):
        in_refs, out_ref, all_ref = refs[:n_rows], refs[n_rows], refs[n_rows + 1]
        send_sems, recv_sems = refs[n_rows + 2], refs[n_rows + 3]
        x, y, c = _me()
        mine = all_ref.at[_flat(x, y, c)]
        mine[...] = jnp.zeros((8, D), F32)
        for i, r in enumerate(in_refs):
            mine[i:i + 1, :] = r[...]
        copies = []
        for k in range(N_DEV - 1):
            copies.append(pltpu.make_async_remote_copy(
                src_ref=mine, dst_ref=mine, send_sem=send_sems.at[k], recv_sem=recv_sems.at[k],
                device_id=_role_peer(k + 1, x, y, c), device_id_type=MESH))
        for cp in copies:
            cp.start()
        for cp in copies:
            cp.wait_recv()
        for cp in copies:
            cp.wait_send()
        tot = all_ref[0]
        for d in range(1, N_DEV):
            tot = tot + all_ref[d]
        out_ref[...] = tot

    vm = pl.BlockSpec(memory_space=pltpu.VMEM)
    return pl.pallas_call(
        body, name="allreduce_small", in_specs=[vm] * n_rows, out_specs=vm,
        out_shape=jax.ShapeDtypeStruct((8, D), F32),
        scratch_shapes=[pltpu.VMEM((N_DEV, 8, D), F32), pltpu.SemaphoreType.DMA((7,)), pltpu.SemaphoreType.DMA((7,))],
    )(*rows)


def _step(ids, x, target, g_in, g_out, sinks, lb_logits, rnn_norm, pre_norm, post_norm, dist):
    T, D = x.shape
    n_roles, _, wd = g_in.shape
    ro = g_out.shape[1]
    assert n_roles % 2 == 0 and (not dist or n_roles == N_DEV)
    n_chips = n_roles // 2 - 1

    if dist:
        (g_in,), fl_in, _ = _comm_call(
            "gather_in_start", [g_in], starts=[(0, 0, 0, SIBLING, SIBLING)] + [(0, 0, 0, 2 + 2 * j, 2 + 2 * j) for j in range(3)])
    h, h_t = _prenorm(x, pre_norm)
    if dist:
        (g_in,), _, _ = _comm_call("gather_in_sibling", [g_in], waits=[(fl_in[0], "recv", 0)], after=(h,))
    proj = _proj_in(ids, h, g_in, 0, 2, None, "proj_in_0")
    fl_fwd = []

    def pass_on(i, g):
        (g,), fl, _ = _comm_call(
            f"gather_in_pass_{i}", [g], waits=[(fl_in[1 + i], "recv", 0)],
            starts=[(0, 2 + 2 * i, 0, 3 + 2 * i, SIBLING)], after=(proj,))
        fl_fwd.extend(fl)
        return g

    if dist:
        g_in = pass_on(1, pass_on(0, g_in))
    for j in range(n_chips):
        after = ()
        if dist:
            if j == 2:
                g_in = pass_on(2, g_in)
            (g_in,), _, _ = _comm_call(f"gather_in_passed_{j}", [g_in], waits=[(fl_fwd[j], "recv", 0)], after=(proj,))
            if j == 2:
                (g_in, g_out), fl_out, token = _comm_call(
                    "gather_out_start", [g_in, g_out],
                    starts=[(1, 0, 1, SIBLING, SIBLING)] + [(1, 0, 1, 2 + 2 * i, 2 + 2 * i) for i in range(3)], after=(proj,))
                after = (token,)
        proj = _proj_in(ids, h, g_in, 2 + 2 * j, 2, proj, f"proj_in_{j + 1}", after)
    mixed, mixed_t = _attn_fwd(proj, sinks, D)
    if dist:
        (g_in, g_out), fl_out_fwd, _ = _comm_call(
            "gather_out_pass", [g_in, g_out],
            waits=[(f, "send", 0) for f in fl_in + fl_fwd] + [(fl_out[1 + i], "recv", 1) for i in range(3)],
            starts=[(1, 2 + 2 * i, 1, 3 + 2 * i, SIBLING) for i in range(3)], after=(mixed,))
    mixed, mixed_t, o_raw, states = _rnn_fwd(proj, lb_logits, rnn_norm, mixed, mixed_t, D)
    if dist:
        (g_out,), _, _ = _comm_call(
            "gather_out_done", [g_out],
            waits=[(fl_out[0], "recv", 0)] + [(f, "recv", 0) for f in fl_out_fwd]
            + [(f, "send", 0) for f in fl_out + fl_out_fwd], after=(states,))
    y = _proj_out(ids, mixed, g_out)
    dy, dout, g_post, sq_err = _loss_head(y, x, target, post_norm)

    dmix = _dmixed(ids, dy, g_out)
    p_out, after = _dw_out(ids, mixed_t, dy, n_roles), ()
    if dist:
        (p_out, l_out), fl_so, token = _comm_call(
            "scatter_out_start", [p_out, _landing((n_roles - 1, ro, D))],
            starts=[(0, r, 1, r - 1, r) for r in range(1, n_roles)])
        after = (token,)
    d_aq, d_ak, d_av, d_ag, d_sink = _attn_bwd(proj, sinks, dmix, D, after)
    d_rq, d_rf, d_ri, d_rg, d_lb, g_rnn = _rnn_bwd(proj, lb_logits, rnn_norm, o_raw, states, dmix, D)
    dproj = jnp.concatenate([d_aq, d_ak, d_av, d_ag, d_rq, d_rf, d_ri, d_rg], axis=1)

    p_far = _dw_in(ids, h_t, dproj, n_roles, 2, n_roles - 2, "dw_in_far")
    if dist:
        (p_far, q_far), fl_pair, token = _comm_call(
            "scatter_in_pair_start", [p_far, _landing((3, D, wd))],
            starts=[(0, 1 + 2 * j, 1, j, SIBLING) for j in range(3)])
        p_near = _dw_in(ids, h_t, dproj, n_roles, 0, 2, "dw_in_near", after=(token,))
        (p_far, q_far), _, _ = _comm_call("scatter_in_pair_wait", [p_far, q_far], waits=_both(fl_pair, 1), after=(p_near,))
        chip_sum = _pair_sum(p_far, q_far)
        (chip_sum, z_far), fl_chip, token = _comm_call(
            "scatter_in_chip_start", [chip_sum, _landing((3, D, wd))],
            starts=[(0, j, 1, j, 2 + 2 * j) for j in range(3)])
        (p_near, q_near), fl_sib, token = _comm_call(
            "scatter_in_sibling_start", [p_near, _landing((1, D, wd))], starts=[(0, 1, 1, 0, SIBLING)], after=(token,))
        after = (token,)
    else:
        p_near, after = _dw_in(ids, h_t, dproj, n_roles, 0, 2, "dw_in_near"), ()
    dh = _dh(ids, dproj, g_in, after)
    grad_x, g_pre = _prenorm_bwd(x, dh, dout, pre_norm)
    n_q = D // ATTN_HEAD
    sink_row = jnp.pad(d_sink[:, 0, :2 * GQA].reshape(1, n_q), ((0, 0), (0, D - n_q)))
    rows = [d_lb, g_rnn, g_pre, g_post, sq_err, sink_row]
    if not dist:
        return grad_x, [(p_near, r) for r in range(2)] + [(p_far, r) for r in range(n_roles - 2)], \
            [(p_out, r) for r in range(n_roles)], rows

    (p_out, l_out), _, _ = _comm_call("scatter_out_wait", [p_out, l_out], waits=_both(fl_so, 1), after=(grad_x,))
    (chip_sum, z_far, p_near, q_near), _, _ = _comm_call(
        "scatter_in_wait", [chip_sum, z_far, p_near, q_near], waits=_both(fl_chip, 1) + _both(fl_sib, 3), after=(p_out,))
    parts_in = [(p_near, 0), (q_near, 0)] + [(z_far, j) for j in range(3)]
    parts_out = [(p_out, 0)] + [(l_out, k) for k in range(n_roles - 1)]
    return grad_x, parts_in, parts_out, rows


def kernel(x, w_in, attn_sinks, lb_logits, rnn_norm, w_out, pre_norm, post_norm, loss_target, m_w_in, m_attn_sinks, m_lb_logits, m_rnn_norm, m_w_out, m_pre_norm, m_post_norm, v_w_in, v_attn_sinks, v_lb_logits, v_rnn_norm, v_w_out, v_pre_norm, v_post_norm):
    _, T, D = x.shape
    ro = w_out.shape[1]
    n_q = attn_sinks.shape[1]
    assert lb_logits.shape[0] == 2 and n_q == D // ATTN_HEAD and n_q <= LANES

    grad_x, parts_in, parts_out, small_rows = _step(
        _role_ids(), x[0], loss_target[0], _cast_slot0(w_in[0], N_DEV, "cast_w_in"),
        _cast_slot0(w_out[0], N_DEV, "cast_w_out"), attn_sinks, lb_logits, rnn_norm, pre_norm, post_norm, True)

    g_wo, d_wo, nm_wo, nv_wo = _adamw_big(w_out[0], m_w_out[0], v_w_out[0], parts_out, _pick(ro, 64), "adamw_w_out")
    g_wi, d_wi, nm_wi, nv_wi = _adamw_big(w_in[0], m_w_in[0], v_w_in[0], parts_in, _pick(D, 128), "adamw_w_in")

    total = _allreduce_small(small_rows, D)
    pad = lambda a: jnp.pad(a, ((0, 0), (0, LANES - n_q)))
    moments = [pad(m_attn_sinks), m_lb_logits, m_rnn_norm, m_pre_norm, m_post_norm,
               pad(v_attn_sinks), v_lb_logits, v_rnn_norm, v_pre_norm, v_post_norm]
    res = _adamw_small(total, pad(attn_sinks), lb_logits, rnn_norm, pre_norm, post_norm, moments, D)
    loss = res[0][0, 0]
    small = [[res[1 + 4 * i + j] for i in range(5)] for j in range(4)]
    for j in range(4):
        small[j][0] = small[j][0][:, :n_q]

    def assemble(j, wi, wo):
        s = small[j]
        return [wi[None], s[0], s[1], s[2], wo[None], s[3], s[4]]

    return (loss, grad_x[None], *assemble(0, g_wi, g_wo), *assemble(1, d_wi, d_wo),
            *assemble(2, nm_wi, nm_wo), *assemble(3, nv_wi, nv_wo))
```

```python
import functools

import jax
import jax.numpy as jnp
from jax import lax
from jax.experimental import pallas as pl
from jax.experimental.pallas import tpu as pltpu

F32 = jnp.float32
BF16 = jnp.bfloat16

ATTN_HEAD = 64
GQA = 8
WINDOW = 128
RNN_HEAD = 128
CHUNK = 64
NORM_EPS = 1e-6
LANES = 128
N_DEV = 8

ADAM_LR = 0.001
ADAM_B1 = 0.9
ADAM_B2 = 0.999
ADAM_EPS = 1e-08
ADAM_WD = 0.01
ADAM_STEP = 10

VMEM_LIMIT = 56 * 1024 * 1024
MESH = pl.DeviceIdType.MESH
ANY_SPEC = pl.BlockSpec(memory_space=pl.ANY)
HBM_SPEC = pl.BlockSpec(memory_space=pltpu.HBM)
SEM_SPEC = pl.BlockSpec(memory_space=pltpu.SEMAPHORE)
EFFECT = pltpu.SideEffectType.DATAFLOW_SIDE_EFFECTING

NT_DIMS = (((1,), (1,)), ((), ()))
NN_DIMS = (((1,), (0,)), ((), ()))
TN_DIMS = (((0,), (0,)), ((), ()))


def _params(*sem):
    return pltpu.CompilerParams(dimension_semantics=sem, vmem_limit_bytes=VMEM_LIMIT)


def _dot(a, b, dims):
    return lax.dot_general(a, b, dims, preferred_element_type=F32)


def _sigmoid(v):
    return 1.0 / (1.0 + jnp.exp(-v))


def _pick(n, pref):
    t = min(n, pref)
    assert n % t == 0, (n, pref)
    return t


def _matmul(ids, a, b, *, grid, a_spec, b_spec, o_spec, out_shape, trans_b, name, after=(), prev=None):
    nk = grid[2]
    dims = NT_DIMS if trans_b else NN_DIMS
    n_skip = len(after) + (prev is not None)

    def body(ids_ref, a_ref, b_ref, *rest):
        del ids_ref
        o_ref, scratch = rest[n_skip], rest[n_skip + 1:]
        prod = _dot(a_ref[...], b_ref[...], dims)
        if nk == 1:
            o_ref[...] = prod.astype(o_ref.dtype)
        else:
            acc_ref, = scratch
            k = pl.program_id(2)

            @pl.when(k == 0)
            def _():
                acc_ref[...] = prod

            @pl.when(k > 0)
            def _():
                acc_ref[...] += prod

            @pl.when(k == nk - 1)
            def _():
                o_ref[...] = acc_ref[...].astype(o_ref.dtype)

    scratch = [] if nk == 1 else [pltpu.VMEM(tuple(d for d in o_spec.block_shape if d is not None), F32)]
    extra = list(after) + ([prev] if prev is not None else [])
    aliases = {3 + len(after): 0} if prev is not None else {}
    return pl.pallas_call(
        body, name=name, out_shape=out_shape, input_output_aliases=aliases,
        grid_spec=pltpu.PrefetchScalarGridSpec(
            num_scalar_prefetch=1, grid=grid, in_specs=[a_spec, b_spec] + [ANY_SPEC] * len(extra),
            out_specs=o_spec, scratch_shapes=scratch),
        compiler_params=_params("parallel", "parallel", "arbitrary"),
    )(ids, a, b, *extra)


def _proj_in(ids, h, w_roles, r0, nr, prev, name, after=()):
    T, D = h.shape
    n_roles, _, wd = w_roles.shape
    tm, tn = _pick(T, 1024), _pick(wd, 640)
    per = wd // tn
    return _matmul(
        ids, h, w_roles, grid=(T // tm, nr * per, 1),
        a_spec=pl.BlockSpec((tm, D), lambda i, j, k, ids: (i, 0)),
        b_spec=pl.BlockSpec((None, D, tn), lambda i, j, k, ids: (r0 + j // per, 0, j % per)),
        o_spec=pl.BlockSpec((tm, tn), lambda i, j, k, ids: (i, ids[r0 + j // per] * per + j % per)),
        out_shape=jax.ShapeDtypeStruct((T, n_roles * wd), F32), trans_b=False, name=name, after=after, prev=prev)


def _proj_out(ids, mixed, wo_roles):
    T = mixed.shape[0]
    n_roles, R, D = wo_roles.shape
    tm, tn = _pick(T, 2048), _pick(D, 1024)
    return _matmul(
        ids, mixed, wo_roles, grid=(T // tm, D // tn, n_roles),
        a_spec=pl.BlockSpec((tm, R), lambda i, j, k, ids: (i, ids[k])),
        b_spec=pl.BlockSpec((None, R, tn), lambda i, j, k, ids: (k, 0, j)),
        o_spec=pl.BlockSpec((tm, tn), lambda i, j, k, ids: (i, j)),
        out_shape=jax.ShapeDtypeStruct((T, D), F32), trans_b=False, name="proj_out")


def _dmixed(ids, dy, wo_roles):
    T, D = dy.shape
    n_roles, R, _ = wo_roles.shape
    tm = _pick(T, 1024)
    return _matmul(
        ids, dy, wo_roles, grid=(T // tm, n_roles, 1),
        a_spec=pl.BlockSpec((tm, D), lambda i, j, k, ids: (i, 0)),
        b_spec=pl.BlockSpec((None, R, D), lambda i, j, k, ids: (j, 0, 0)),
        o_spec=pl.BlockSpec((tm, R), lambda i, j, k, ids: (i, ids[j])),
        out_shape=jax.ShapeDtypeStruct((T, n_roles * R), F32), trans_b=True, name="dmixed")


def _dw_out(ids, mixed_t, dy, n_roles):
    E, T = mixed_t.shape
    D = dy.shape[1]
    R = E // n_roles
    tn = _pick(D, 512)
    return _matmul(
        ids, mixed_t, dy, grid=(n_roles, D // tn, 1),
        a_spec=pl.BlockSpec((R, T), lambda i, j, k, ids: (ids[i], 0)),
        b_spec=pl.BlockSpec((T, tn), lambda i, j, k, ids: (0, j)),
        o_spec=pl.BlockSpec((None, R, tn), lambda i, j, k, ids: (i, 0, j)),
        out_shape=jax.ShapeDtypeStruct((n_roles, R, D), BF16), trans_b=False, name="dw_out")


def _dw_in(ids, h_t, dproj, n_roles, r0, nr, name, after=()):
    D, T = h_t.shape
    wd = dproj.shape[1] // n_roles
    tm, tn = _pick(D, 1024), _pick(wd, 640)
    per = wd // tn
    return _matmul(
        ids, h_t, dproj, grid=(D // tm, nr * per, 1),
        a_spec=pl.BlockSpec((tm, T), lambda i, j, k, ids: (i, 0)),
        b_spec=pl.BlockSpec((T, tn), lambda i, j, k, ids: (0, ids[r0 + j // per] * per + j % per)),
        o_spec=pl.BlockSpec((None, tm, tn), lambda i, j, k, ids: (j // per, i, j % per)),
        out_shape=jax.ShapeDtypeStruct((nr, D, wd), BF16), trans_b=False, name=name, after=after)


def _dh(ids, dproj, w_roles, after=()):
    T = dproj.shape[0]
    n_roles, D, wd = w_roles.shape
    tm, tn = _pick(T, 1024), _pick(D, 1024)
    return _matmul(
        ids, dproj, w_roles, grid=(T // tm, D // tn, n_roles),
        a_spec=pl.BlockSpec((tm, wd), lambda i, j, k, ids: (i, ids[k])),
        b_spec=pl.BlockSpec((None, tn, wd), lambda i, j, k, ids: (k, j, 0)),
        o_spec=pl.BlockSpec((tm, tn), lambda i, j, k, ids: (i, j)),
        out_shape=jax.ShapeDtypeStruct((T, D), F32), trans_b=True, name="dh", after=after)


def _cast_slot0(w, n_roles, name):
    R, C = w.shape
    tr = _pick(R, 256)

    def body(w_ref, o_ref):
        o_ref[...] = w_ref[...].astype(BF16)

    return pl.pallas_call(
        body, name=name, grid=(R // tr,), in_specs=[pl.BlockSpec((tr, C), lambda i: (i, 0))],
        out_specs=pl.BlockSpec((None, tr, C), lambda i: (0, i, 0)),
        out_shape=jax.ShapeDtypeStruct((n_roles, R, C), BF16), compiler_params=_params("parallel"))(w)


def _prenorm(x, gain):
    T, D = x.shape
    tm = _pick(T, 256)

    def body(x_ref, g_ref, h_ref, ht_ref):
        xv = x_ref[...]
        r = lax.rsqrt(jnp.mean(xv * xv, axis=-1, keepdims=True) + NORM_EPS)
        h = xv * r * g_ref[...]
        h_ref[...] = h.astype(BF16)
        ht_ref[...] = h.T.astype(BF16)

    return pl.pallas_call(
        body, name="prenorm", grid=(T // tm,),
        in_specs=[pl.BlockSpec((tm, D), lambda i: (i, 0)), pl.BlockSpec((1, D), lambda i: (0, 0))],
        out_specs=[pl.BlockSpec((tm, D), lambda i: (i, 0)), pl.BlockSpec((D, tm), lambda i: (0, i))],
        out_shape=[jax.ShapeDtypeStruct((T, D), BF16), jax.ShapeDtypeStruct((D, T), BF16)],
        compiler_params=_params("parallel"))(x, gain)


def _norm_bwd(u, yn, r):
    return r * (u - yn * jnp.mean(u * yn, axis=-1, keepdims=True))


def _loss_head(y, x, target, gain):
    T, D = y.shape
    tm = _pick(T, 256)

    def body(y_ref, x_ref, t_ref, g_ref, dy_ref, dout_ref, gpost_ref, sq_ref):
        yv = y_ref[...]
        g = g_ref[...]
        r = lax.rsqrt(jnp.mean(yv * yv, axis=-1, keepdims=True) + NORM_EPS)
        yn = yv * r
        err = x_ref[...] + yn * g - t_ref[...]
        dout = err * (1.0 / D)
        dy_ref[...] = _norm_bwd(dout * g, yn, r).astype(BF16)
        dout_ref[...] = dout

        @pl.when(pl.program_id(0) == 0)
        def _():
            gpost_ref[...] = jnp.zeros_like(gpost_ref)
            sq_ref[...] = jnp.zeros_like(sq_ref)

        gpost_ref[...] += jnp.sum(dout * yn, axis=0, keepdims=True)
        sq_ref[...] += jnp.sum(err * err, axis=0, keepdims=True)

    row = pl.BlockSpec((tm, D), lambda i: (i, 0))
    vec = pl.BlockSpec((1, D), lambda i: (0, 0))
    return pl.pallas_call(
        body, name="loss_head", grid=(T // tm,), in_specs=[row, row, row, vec], out_specs=[row, row, vec, vec],
        out_shape=[jax.ShapeDtypeStruct((T, D), BF16), jax.ShapeDtypeStruct((T, D), F32),
                   jax.ShapeDtypeStruct((1, D), F32), jax.ShapeDtypeStruct((1, D), F32)],
        compiler_params=_params("arbitrary"))(y, x, target, gain)


def _prenorm_bwd(x, dh, dout, gain):
    T, D = x.shape
    tm = _pick(T, 256)

    def body(x_ref, dh_ref, dout_ref, g_ref, gx_ref, gpre_ref):
        xv = x_ref[...]
        dhv = dh_ref[...]
        r = lax.rsqrt(jnp.mean(xv * xv, axis=-1, keepdims=True) + NORM_EPS)
        xn = xv * r
        gx_ref[...] = dout_ref[...] + _norm_bwd(dhv * g_ref[...], xn, r)

        @pl.when(pl.program_id(0) == 0)
        def _():
            gpre_ref[...] = jnp.zeros_like(gpre_ref)

        gpre_ref[...] += jnp.sum(dhv * xn, axis=0, keepdims=True)

    row = pl.BlockSpec((tm, D), lambda i: (i, 0))
    vec = pl.BlockSpec((1, D), lambda i: (0, 0))
    return pl.pallas_call(
        body, name="prenorm_bwd", grid=(T // tm,), in_specs=[row, row, row, vec], out_specs=[row, vec],
        out_shape=[jax.ShapeDtypeStruct((T, D), F32), jax.ShapeDtypeStruct((1, D), F32)],
        compiler_params=_params("arbitrary"))(x, dh, dout, gain)


def _attn_masks(n):
    row = lax.broadcasted_iota(jnp.int32, (2 * WINDOW, 2 * WINDOW), 0) % WINDOW
    col = lax.broadcasted_iota(jnp.int32, (2 * WINDOW, 2 * WINDOW), 1)
    valid = (col > row) & (col <= row + WINDOW) & ((n > 0) | (col >= WINDOW))
    low = lax.broadcasted_iota(jnp.int32, (1, LANES), 1) < ATTN_HEAD
    top = lax.broadcasted_iota(jnp.int32, (2 * WINDOW, 1), 0) < WINDOW
    return valid, low, top


def _dup_half(pair, keep):
    return jnp.where(keep, pair, pltpu.roll(pair, ATTN_HEAD, 1))


def _fold_half(v):
    return v + pltpu.roll(v, ATTN_HEAD, 1)


def _attn_probs(qpair, k2, sink_lo, sink_hi, valid, low, top):
    q2 = jnp.concatenate([jnp.where(low, qpair, 0.0), jnp.where(low, 0.0, qpair)], axis=0).astype(BF16)
    s = _dot(q2, k2, NT_DIMS) * (ATTN_HEAD ** -0.5)
    s = jnp.where(valid, s, -jnp.inf)
    sink = jnp.where(top, sink_lo, sink_hi)
    m = jnp.maximum(jnp.max(s, axis=-1, keepdims=True), sink)
    p = jnp.exp(s - m)
    psink = jnp.exp(sink - m)
    inv = 1.0 / (jnp.sum(p, axis=-1, keepdims=True) + psink)
    return q2, p * inv, psink * inv


def _attn_specs(D):
    kb = D // LANES
    vb = kb + D // (8 * LANES)
    gb = (D + D // 4) // 512
    wide = lambda off: [pl.BlockSpec((WINDOW, 512), functools.partial(lambda o, e, jp, n: (n, o + 2 * jp + e), off, e))
                        for e in (0, 1)]
    cur = lambda off: pl.BlockSpec((WINDOW, LANES), functools.partial(lambda o, jp, n: (n, o + jp), off))
    prev = lambda off: pl.BlockSpec((WINDOW, LANES),
                                    functools.partial(lambda o, jp, n: (jnp.maximum(n - 1, 0), o + jp), off))
    return wide(0) + [cur(kb), prev(kb), cur(vb), prev(vb)] + wide(gb)


def _attn_fwd(proj, sinks, D):
    T = proj.shape[0]
    nb, njp = T // WINDOW, D // 1024

    def body(sink_ref, qlo_ref, qhi_ref, kc_ref, kp_ref, vc_ref, vp_ref, glo_ref, ghi_ref, mix_ref, mixt_ref):
        jp, n = pl.program_id(0), pl.program_id(1)
        valid, low, top = _attn_masks(n)
        kk = jnp.concatenate([kp_ref[...], kc_ref[...]], axis=0)
        vv = jnp.concatenate([vp_ref[...], vc_ref[...]], axis=0)
        for hj, (q_ref, g_ref) in enumerate(((qlo_ref, glo_ref), (qhi_ref, ghi_ref))):
            keep = low if hj == 0 else jnp.logical_not(low)
            k2 = _dup_half(kk, keep).astype(BF16)
            v2 = _dup_half(vv, keep).astype(BF16)
            for p in range(4):
                cols = slice(LANES * p, LANES * (p + 1))
                head = (2 * jp + hj) * GQA + 2 * p
                _, probs, _ = _attn_probs(q_ref[:, cols], k2, sink_ref[0, head], sink_ref[0, head + 1],
                                          valid, low, top)
                o2 = _dot(probs.astype(BF16), v2, NN_DIMS)
                opair = jnp.where(low, o2[:WINDOW], o2[WINDOW:])
                g = g_ref[:, cols]
                out = opair * (g * _sigmoid(g))
                oc = slice(512 * hj + LANES * p, 512 * hj + LANES * (p + 1))
                mix_ref[:, oc] = out.astype(BF16)
                mixt_ref[oc, :] = out.T.astype(BF16)

    return pl.pallas_call(
        body, name="attn_fwd", grid=(njp, nb),
        in_specs=[pl.BlockSpec(memory_space=pltpu.SMEM)] + _attn_specs(D),
        out_specs=[pl.BlockSpec((WINDOW, 1024), lambda jp, n: (n, jp)),
                   pl.BlockSpec((1024, WINDOW), lambda jp, n: (jp, n))],
        out_shape=[jax.ShapeDtypeStruct((T, 2 * D), BF16), jax.ShapeDtypeStruct((2 * D, T), BF16)],
        compiler_params=_params("parallel", "parallel"))(sinks, *([proj] * 8))


def _attn_bwd(proj, sinks, dmix, D, after=()):
    T = proj.shape[0]
    nb, njp = T // WINDOW, D // 1024
    n_after = len(after)

    def body(sink_ref, qlo_ref, qhi_ref, kc_ref, kp_ref, vc_ref, vp_ref, glo_ref, ghi_ref, dmix_ref, *rest):
        dq_ref, dk_ref, dv_ref, dg_ref, dsink_ref, kcarry_ref, vcarry_ref = rest[n_after:]
        jp, step = pl.program_id(0), pl.program_id(1)
        n = nb - 1 - step
        valid, low, top = _attn_masks(n)
        lane = lax.broadcasted_iota(jnp.int32, (1, LANES), 1)
        kk = jnp.concatenate([kp_ref[...], kc_ref[...]], axis=0)
        vv = jnp.concatenate([vp_ref[...], vc_ref[...]], axis=0)

        @pl.when(step == 0)
        def _():
            kcarry_ref[...] = jnp.zeros_like(kcarry_ref)
            vcarry_ref[...] = jnp.zeros_like(vcarry_ref)
            dsink_ref[...] = jnp.zeros_like(dsink_ref)

        dk_pair = jnp.zeros((2 * WINDOW, LANES), F32)
        dv_pair = jnp.zeros((2 * WINDOW, LANES), F32)
        dsink = jnp.zeros((1, LANES), F32)
        for hj, (q_ref, g_ref) in enumerate(((qlo_ref, glo_ref), (qhi_ref, ghi_ref))):
            keep = low if hj == 0 else jnp.logical_not(low)
            k2 = _dup_half(kk, keep).astype(BF16)
            v2 = _dup_half(vv, keep).astype(BF16)
            dk_head = jnp.zeros((2 * WINDOW, LANES), F32)
            dv_head = jnp.zeros((2 * WINDOW, LANES), F32)
            for p in range(4):
                cols = slice(LANES * p, LANES * (p + 1))
                oc = slice(512 * hj + LANES * p, 512 * hj + LANES * (p + 1))
                head = (2 * jp + hj) * GQA + 2 * p
                q2, probs, psink = _attn_probs(q_ref[:, cols], k2, sink_ref[0, head], sink_ref[0, head + 1],
                                               valid, low, top)
                pb = probs.astype(BF16)
                o2 = _dot(pb, v2, NN_DIMS)
                opair = jnp.where(low, o2[:WINDOW], o2[WINDOW:])
                g = g_ref[:, cols]
                sg = _sigmoid(g)
                dgated = dmix_ref[:, oc]
                dg_ref[:, oc] = (dgated * opair * (sg * (1.0 + g * (1.0 - sg)))).astype(BF16)
                do = dgated * (g * sg)
                do2 = jnp.concatenate([jnp.where(low, do, 0.0), jnp.where(low, 0.0, do)], axis=0).astype(BF16)
                dp = _dot(do2, v2, NT_DIMS)
                delta = jnp.sum(probs * dp, axis=-1, keepdims=True)
                ds = (probs * (dp - delta) * (ATTN_HEAD ** -0.5)).astype(BF16)
                dq2 = _dot(ds, k2, NN_DIMS)
                dq_ref[:, oc] = jnp.where(low, dq2[:WINDOW], dq2[WINDOW:]).astype(BF16)
                dk_head += _dot(ds, q2, TN_DIMS)
                dv_head += _dot(pb, do2, TN_DIMS)
                ps = psink * delta
                local = hj * GQA + 2 * p
                dsink -= jnp.where(lane == local, jnp.sum(ps[:WINDOW], axis=0, keepdims=True), 0.0)
                dsink -= jnp.where(lane == local + 1, jnp.sum(ps[WINDOW:], axis=0, keepdims=True), 0.0)
            dk_pair += jnp.where(keep, _fold_half(dk_head), 0.0)
            dv_pair += jnp.where(keep, _fold_half(dv_head), 0.0)
        dk_ref[...] = (dk_pair[WINDOW:] + kcarry_ref[...]).astype(BF16)
        dv_ref[...] = (dv_pair[WINDOW:] + vcarry_ref[...]).astype(BF16)
        kcarry_ref[...] = dk_pair[:WINDOW]
        vcarry_ref[...] = dv_pair[:WINDOW]
        dsink_ref[...] += dsink

    rev = lambda spec: pl.BlockSpec(spec.block_shape, functools.partial(
        lambda f, jp, s: f(jp, nb - 1 - s), spec.index_map))
    in_specs = [rev(s) for s in _attn_specs(D)]
    wide_out = pl.BlockSpec((WINDOW, 1024), lambda jp, s: (nb - 1 - s, jp))
    pair_out = pl.BlockSpec((WINDOW, LANES), lambda jp, s: (nb - 1 - s, jp))
    return pl.pallas_call(
        body, name="attn_bwd", grid=(njp, nb),
        in_specs=[pl.BlockSpec(memory_space=pltpu.SMEM)] + in_specs + [wide_out] + [ANY_SPEC] * n_after,
        out_specs=[wide_out, pair_out, pair_out, wide_out, pl.BlockSpec((None, 1, LANES), lambda jp, s: (jp, 0, 0))],
        out_shape=[jax.ShapeDtypeStruct((T, D), BF16), jax.ShapeDtypeStruct((T, D // 8), BF16),
                   jax.ShapeDtypeStruct((T, D // 8), BF16), jax.ShapeDtypeStruct((T, D), BF16),
                   jax.ShapeDtypeStruct((njp, 1, LANES), F32)],
        scratch_shapes=[pltpu.VMEM((WINDOW, LANES), F32), pltpu.VMEM((WINDOW, LANES), F32)],
        compiler_params=_params("parallel", "arbitrary"))(sinks, *([proj] * 8), dmix, *after)


RNN_TB = 512
RNN_HB = 8
RNN_WIDE = RNN_HB * RNN_HEAD


def _split3(v):
    a = v.astype(BF16)
    r = v - a.astype(F32)
    b = r.astype(BF16)
    c = (r - b.astype(F32)).astype(BF16)
    return a, b, c


def _tri_sum(tri, v):
    a, b, c = _split3(v)
    return _dot(tri, a, NN_DIMS) + _dot(tri, b, NN_DIMS) + _dot(tri, c, NN_DIMS)


def _lower_bound(lb_ref):
    l0, l1 = lb_ref[0:1, :], lb_ref[1:2, :]
    m = jnp.maximum(l0, l1)
    e0, e1 = jnp.exp(l0 - m), jnp.exp(l1 - m)
    return e0 / (e0 + e1)


def _rnn_gates(rq, rf, lb):
    sq = _sigmoid(rq)
    sf = _sigmoid(rf)
    f = lb + (1.0 - lb) * sf
    return sq, sf, f


def _rnn_decays(g, tri):
    return _rnn_factors(_tri_sum(tri, g))


def _rnn_factors(G):
    last = G[CHUNK - 1:CHUNK, :]
    mid = G[CHUNK // 2 - 1:CHUNK // 2, :]
    return G, jnp.exp(G), jnp.exp(G - mid), jnp.exp(mid - G), jnp.exp(last - G), jnp.exp(last)


def _chunk_masks():
    r = lax.broadcasted_iota(jnp.int32, (CHUNK, CHUNK), 0)
    c = lax.broadcasted_iota(jnp.int32, (CHUNK, CHUNK), 1)
    return r >= c, (r >= c).astype(BF16), (r <= c).astype(BF16)


def _rnn_specs(T, D, tb, rev):
    nt = T // tb
    base = (2 * D + D // 4) // LANES
    t_of = (lambda s: nt - 1 - s) if rev else (lambda s: s)
    assert base % RNN_HB == 0 and (D // LANES) % RNN_HB == 0
    cols = [pl.BlockSpec((tb, RNN_WIDE), functools.partial(lambda o, h, s: (t_of(s), o + h),
                                                            (base + i * (D // LANES)) // RNN_HB))
            for i in range(4)]
    return cols, t_of


def _rnn_fwd(proj, lb_logits, rnn_norm, mixed, mixed_t, D):
    T = proj.shape[0]
    tb = _pick(T, RNN_TB)
    nt, nh, cpb = T // tb, D // RNN_HEAD, tb // CHUNK
    cols, _ = _rnn_specs(T, D, tb, False)

    def body(rq_ref, rf_ref, ri_ref, rg_ref, lb_ref, gain_ref, mix_in, mixt_in,
             mix_ref, mixt_ref, o_ref, st_ref, state_ref):
        del mix_in, mixt_in
        causal, tri, _ = _chunk_masks()
        lb = _lower_bound(lb_ref)

        @pl.when(pl.program_id(1) == 0)
        def _():
            state_ref[...] = jnp.zeros_like(state_ref)

        def chunk(c, carry):
            rows = pl.ds(pl.multiple_of(c * CHUNK, CHUNK), CHUNK)
            heads = range(RNN_HB)
            lns = [slice(RNN_HEAD * hh, RNN_HEAD * (hh + 1)) for hh in heads]
            qs, ks, Gs = [], [], []
            for ln in lns:
                rq, rf = rq_ref[rows, ln], rf_ref[rows, ln]
                sq, _, f = _rnn_gates(rq, rf, lb[:, ln])
                qs.append(rq * sq)
                ks.append(1.0 - f)
                Gs.append(_tri_sum(tri, jnp.log(f)))
            atts, inters, vbs = [], [], []
            for hh, ln in enumerate(lns):
                _, eG, eq, ek, ekl, elast = _rnn_factors(Gs[hh])
                q, k = qs[hh], ks[hh]
                st = state_ref[hh]
                st_ref[hh, c] = st
                vb = ri_ref[rows, ln].astype(BF16)
                vbs.append(vb)
                atts.append(_dot((q * eq).astype(BF16), (k * ek).astype(BF16), NT_DIMS))
                inters.append(_dot((q * eG).astype(BF16), st.astype(BF16), NT_DIMS))
                state_ref[hh] = st * elast + _dot(vb, (k * ekl).astype(BF16), TN_DIMS)
            intras = [_dot(jnp.where(causal, atts[hh], 0.0).astype(BF16), vbs[hh], NN_DIMS) for hh in heads]
            for hh, ln in enumerate(lns):
                o = inters[hh] + intras[hh]
                o_ref[rows, ln] = o
                rg = rg_ref[rows, ln]
                r = lax.rsqrt(jnp.mean(o * o, axis=-1, keepdims=True) + NORM_EPS)
                out = (o * r * gain_ref[:, ln]) * (rg * _sigmoid(rg))
                mix_ref[rows, ln] = out.astype(BF16)
            return carry

        lax.fori_loop(0, cpb, chunk, 0)
        mixt_ref[...] = mix_ref[...].astype(F32).T.astype(BF16)

    nh //= RNN_HB
    vec2 = pl.BlockSpec((2, RNN_WIDE), lambda h, s: (0, h))
    vec1 = pl.BlockSpec((1, RNN_WIDE), lambda h, s: (0, h))
    return pl.pallas_call(
        body, name="rnn_fwd", grid=(nh, nt),
        in_specs=cols + [vec2, vec1, ANY_SPEC, ANY_SPEC],
        out_specs=[pl.BlockSpec((tb, RNN_WIDE), lambda h, s: (s, D // RNN_WIDE + h)),
                   pl.BlockSpec((RNN_WIDE, tb), lambda h, s: (D // RNN_WIDE + h, s)),
                   pl.BlockSpec((tb, RNN_WIDE), lambda h, s: (s, h)),
                   pl.BlockSpec((RNN_HB, cpb, RNN_HEAD, RNN_HEAD), lambda h, s: (h, s, 0, 0))],
        out_shape=[jax.ShapeDtypeStruct(mixed.shape, BF16), jax.ShapeDtypeStruct(mixed_t.shape, BF16),
                   jax.ShapeDtypeStruct((T, D), F32),
                   jax.ShapeDtypeStruct((nh * RNN_HB, T // CHUNK, RNN_HEAD, RNN_HEAD), F32)],
        scratch_shapes=[pltpu.VMEM((RNN_HB, RNN_HEAD, RNN_HEAD), F32)],
        input_output_aliases={6: 0, 7: 1},
        compiler_params=_params("parallel", "arbitrary"))(proj, proj, proj, proj, lb_logits, rnn_norm, mixed, mixed_t)


def _rnn_bwd(proj, lb_logits, rnn_norm, o_raw, states, dmix, D):
    T = proj.shape[0]
    tb = _pick(T, RNN_TB)
    nt, nh, cpb = T // tb, D // RNN_HEAD, tb // CHUNK
    cols, t_of = _rnn_specs(T, D, tb, True)

    def body(rq_ref, rf_ref, ri_ref, rg_ref, lb_ref, gain_ref, o_ref, st_ref, dmix_ref,
             drq_ref, drf_ref, dri_ref, drg_ref, dlb_ref, dgain_ref, dstate_ref):
        causal, tri, tri_t = _chunk_masks()
        lb = _lower_bound(lb_ref)
        gain = gain_ref[...]

        @pl.when(pl.program_id(1) == 0)
        def _():
            dstate_ref[...] = jnp.zeros_like(dstate_ref)
            dlb_ref[...] = jnp.zeros_like(dlb_ref)
            dgain_ref[...] = jnp.zeros_like(dgain_ref)

        def chunk(i, carry):
            c = cpb - 1 - i
            rows = pl.ds(pl.multiple_of(c * CHUNK, CHUNK), CHUNK)
            heads = range(RNN_HB)
            lns = [slice(RNN_HEAD * hh, RNN_HEAD * (hh + 1)) for hh in heads]
            last_row = lax.broadcasted_iota(jnp.int32, (CHUNK, 1), 0) == CHUNK - 1
            A = []
            for ln in lns:
                rq, rf, rg = rq_ref[rows, ln], rf_ref[rows, ln], rg_ref[rows, ln]
                o, dgated = o_ref[rows, ln], dmix_ref[rows, ln]
                gainh = gain[:, ln]
                sgt = _sigmoid(rg)
                r = lax.rsqrt(jnp.mean(o * o, axis=-1, keepdims=True) + NORM_EPS)
                on = o * r
                drg_ref[rows, ln] = (dgated * (on * gainh) * (sgt * (1.0 + rg * (1.0 - sgt)))).astype(BF16)
                d_on = dgated * (rg * sgt)
                dgain_ref[:, ln] += jnp.sum(d_on * on, axis=0, keepdims=True)
                dob = _norm_bwd(d_on * gainh, on, r).astype(BF16)
                sq, sf, f = _rnn_gates(rq, rf, lb[:, ln])
                A.append(dict(rq=rq, sq=sq, sf=sf, f=f, dob=dob, G=_tri_sum(tri, jnp.log(f))))
            for hh, ln in enumerate(lns):
                a = A[hh]
                _, eG, eq, ek, ekl, elast = _rnn_factors(a.pop("G"))
                q, k = a["rq"] * a["sq"], 1.0 - a["f"]
                st, dst = st_ref[hh, c], dstate_ref[hh]
                qg, kl = q * eG, k * ekl
                qmb, kmb = (q * eq).astype(BF16), (k * ek).astype(BF16)
                dob, vb, dstb = a["dob"], ri_ref[rows, ln].astype(BF16), dst.astype(BF16)
                a.update(eG=eG, eq=eq, ek=ek, ekl=ekl, qg=qg, kl=kl, qmb=qmb, kmb=kmb,
                         att=_dot(qmb, kmb, NT_DIMS), datt=_dot(dob, vb, NT_DIMS),
                         dqg=_dot(dob, st.astype(BF16), NN_DIMS), dkl=_dot(vb, dstb, NN_DIMS),
                         dri=_dot(kl.astype(BF16), dstb, NT_DIMS),
                         dlast=jnp.sum(dst * st, axis=0, keepdims=True) * elast)
                dstate_ref[hh] = dst * elast + _dot(dob, qg.astype(BF16), TN_DIMS)
            for hh, ln in enumerate(lns):
                a = A[hh]
                att = jnp.where(causal, a.pop("att"), 0.0).astype(BF16)
                datt = jnp.where(causal, a.pop("datt"), 0.0).astype(BF16)
                dqm = _dot(datt, a["kmb"], NN_DIMS)
                dkm = _dot(datt, a["qmb"], TN_DIMS)
                dri_ref[rows, ln] = (_dot(att, a["dob"], TN_DIMS) + a.pop("dri")).astype(BF16)
                dqg, dkl, kl = a.pop("dqg"), a.pop("dkl"), a.pop("kl")
                a["dq"] = dqg * a.pop("eG") + dqm * a.pop("eq")
                a["dk"] = dkm * a.pop("ek") + dkl * a.pop("ekl")
                dG = dqg * a.pop("qg") + dqm * a.pop("qmb").astype(F32) - dkm * a.pop("kmb").astype(F32) - dkl * kl
                dlast = jnp.sum(dkl * kl, axis=0, keepdims=True) + a.pop("dlast")
                a["dg"] = _tri_sum(tri_t, dG + jnp.where(last_row, dlast, 0.0))
            for hh, ln in enumerate(lns):
                a = A[hh]
                rq, sq, sf = a["rq"], a["sq"], a["sf"]
                df = a["dg"] / a["f"] - a["dk"]
                drq_ref[rows, ln] = (a["dq"] * (sq * (1.0 + rq * (1.0 - sq)))).astype(BF16)
                drf_ref[rows, ln] = (df * (1.0 - lb[:, ln]) * (sf * (1.0 - sf))).astype(BF16)
                dlb_ref[:, ln] += jnp.sum(df * (1.0 - sf), axis=0, keepdims=True)
            return carry

        lax.fori_loop(0, cpb, chunk, 0)

    nh //= RNN_HB
    vec2 = pl.BlockSpec((2, RNN_WIDE), lambda h, s: (0, h))
    vec1 = pl.BlockSpec((1, RNN_WIDE), lambda h, s: (0, h))
    blk = pl.BlockSpec((tb, RNN_WIDE), lambda h, s: (t_of(s), h))
    return pl.pallas_call(
        body, name="rnn_bwd", grid=(nh, nt),
        in_specs=cols + [vec2, vec1, blk,
                         pl.BlockSpec((RNN_HB, cpb, RNN_HEAD, RNN_HEAD), lambda h, s: (h, t_of(s), 0, 0)),
                         pl.BlockSpec((tb, RNN_WIDE), lambda h, s: (t_of(s), D // RNN_WIDE + h))],
        out_specs=[blk, blk, blk, blk, vec1, vec1],
        out_shape=[jax.ShapeDtypeStruct((T, D), BF16)] * 4 + [jax.ShapeDtypeStruct((1, D), F32)] * 2,
        scratch_shapes=[pltpu.VMEM((RNN_HB, RNN_HEAD, RNN_HEAD), F32)],
        compiler_params=_params("parallel", "arbitrary"))(proj, proj, proj, proj, lb_logits, rnn_norm, o_raw, states, dmix)


def _adamw(w, g, m, v):
    m = ADAM_B1 * m + (1.0 - ADAM_B1) * g
    v = ADAM_B2 * v + (1.0 - ADAM_B2) * (g * g)
    m_hat = m / (1.0 - ADAM_B1 ** ADAM_STEP)
    v_hat = v / (1.0 - ADAM_B2 ** ADAM_STEP)
    delta = -ADAM_LR * (m_hat / (jnp.sqrt(v_hat) + ADAM_EPS) + ADAM_WD * w)
    return delta, m, v


def _adamw_big(w, m, v, parts, tr, name):
    R, C = w.shape
    n_parts = len(parts)

    def body(w_ref, m_ref, v_ref, *rest):
        part_refs = rest[:n_parts]
        g_ref, d_ref, nm_ref, nv_ref = rest[n_parts:]
        g = part_refs[0][...].astype(F32)
        for p_ref in part_refs[1:]:
            g = g + p_ref[...].astype(F32)
        delta, nm, nv = _adamw(w_ref[...], g, m_ref[...], v_ref[...])
        g_ref[...] = g
        d_ref[...] = delta
        nm_ref[...] = nm
        nv_ref[...] = nv

    blk = pl.BlockSpec((tr, C), lambda i: (i, 0))
    part_specs = [pl.BlockSpec((None, tr, C), functools.partial(lambda s, i: (s, i, 0), slot)) for _, slot in parts]
    return pl.pallas_call(
        body, name=name, grid=(R // tr,), in_specs=[blk, blk, blk] + part_specs,
        out_specs=[blk] * 4, out_shape=[jax.ShapeDtypeStruct((R, C), F32)] * 4,
        compiler_params=_params("parallel"))(w, m, v, *[a for a, _ in parts])


def _pair_sum(pa, qa):
    n, R, C = qa.shape
    tr = _pick(R, 256)

    def body(p_ref, q_ref, r_ref):
        r_ref[...] = (p_ref[...].astype(F32) + q_ref[...].astype(F32)).astype(BF16)

    return pl.pallas_call(
        body, name="pair_sum", grid=(n, R // tr),
        in_specs=[pl.BlockSpec((None, tr, C), lambda j, i: (2 * j, i, 0)), pl.BlockSpec((None, tr, C), lambda j, i: (j, i, 0))],
        out_specs=pl.BlockSpec((None, tr, C), lambda j, i: (j, i, 0)),
        out_shape=jax.ShapeDtypeStruct((n, R, C), BF16), compiler_params=_params("parallel", "parallel"))(pa, qa)


def _adamw_small(total, sinks, lb_logits, rnn_norm, pre_norm, post_norm, moments, D):
    params = [sinks, lb_logits, rnn_norm, pre_norm, post_norm]

    def body(tot_ref, *refs):
        p_refs, m_refs, v_refs = refs[0:5], refs[5:10], refs[10:15]
        loss_ref, outs = refs[15], refs[16:]
        tot = tot_ref[...]
        l0, l1 = p_refs[1][0:1, :], p_refs[1][1:2, :]
        mx = jnp.maximum(l0, l1)
        e0, e1 = jnp.exp(l0 - mx), jnp.exp(l1 - mx)
        p0, p1 = e0 / (e0 + e1), e1 / (e0 + e1)
        dlb = tot[0:1, :]
        grads = [tot[5:6, 0:LANES], jnp.concatenate([dlb * p0 * (1.0 - p0), -dlb * p0 * p1], axis=0),
                 tot[1:2, :], tot[2:3, :], tot[3:4, :]]
        loss_ref[...] = 0.5 / D * jnp.sum(tot[4:5, :], axis=-1, keepdims=True)
        for i, g in enumerate(grads):
            delta, nm, nv = _adamw(p_refs[i][...], g, m_refs[i][...], v_refs[i][...])
            outs[4 * i][...] = g
            outs[4 * i + 1][...] = delta
            outs[4 * i + 2][...] = nm
            outs[4 * i + 3][...] = nv

    out_shape = [jax.ShapeDtypeStruct((1, 1), F32)]
    for p in params:
        out_shape += [jax.ShapeDtypeStruct(p.shape, F32)] * 4
    return pl.pallas_call(body, name="adamw_small", out_shape=out_shape)(total, *params, *moments)


SIBLING = 1


def _me():
    return lax.axis_index("x"), lax.axis_index("y"), lax.axis_index("c")


def _flat(px, py, pc):
    return 4 * px + 2 * py + pc


def _role_peer(role, x, y, c):
    if role < 2:
        return (x, y, (1 - c) if role else c)
    j, other = (role - 2) // 2, (role - 2) % 2
    px = (1 - x) if j in (0, 2) else x
    py = (1 - y) if j in (1, 2) else y
    return (px, py, (1 - c) if other else c)


def _role_ids():
    x, y, c = _me()
    return jnp.stack([_flat(*_role_peer(r, x, y, c)) for r in range(N_DEV)]).astype(jnp.int32)


def _comm_call(name, bufs, waits=(), starts=(), after=()):
    n_buf, n_wait, n_start, n_after = len(bufs), len(waits), len(starts), len(after)

    def body(*refs):
        buf_refs = refs[:n_buf]
        sem_refs = refs[n_buf:n_buf + 2 * n_wait]
        outs = refs[n_buf + 2 * n_wait + n_after:]
        new_sems, token = outs[:2 * n_start], outs[-1]
        x, y, c = _me()
        def block(ref, slot, rows):
            return ref.at[slot] if rows is None else ref.at[slot, pl.ds(rows[0], rows[1])]

        for w, (_, kind, like, *rows) in enumerate(waits):
            shape_ref = block(buf_refs[like], 0, (0, rows[0][1]) if rows else None)
            cp = pltpu.make_async_remote_copy(
                src_ref=shape_ref, dst_ref=shape_ref, send_sem=sem_refs[2 * w], recv_sem=sem_refs[2 * w + 1],
                device_id=(x, y, c), device_id_type=MESH)
            if kind == "send":
                cp.wait_send()
            else:
                cp.wait_recv()
        for s, (sb, ss, db, ds, role, *rows) in enumerate(starts):
            rows = rows[0] if rows else None
            pltpu.make_async_remote_copy(
                src_ref=block(buf_refs[sb], ss, rows), dst_ref=block(buf_refs[db], ds, rows), send_sem=new_sems[2 * s],
                recv_sem=new_sems[2 * s + 1], device_id=_role_peer(role, x, y, c), device_id_type=MESH).start()
        token[...] = jnp.zeros_like(token)

    sems = [s for flight, *_ in waits for s in flight]
    out = pl.pallas_call(
        body, name=name,
        out_shape=tuple([pltpu.SemaphoreType.DMA(())] * (2 * n_start) + [pltpu.HBM(b.shape, b.dtype) for b in bufs]
                        + [jax.ShapeDtypeStruct((8, LANES), F32)]),
        in_specs=tuple([HBM_SPEC] * n_buf + [SEM_SPEC] * (2 * n_wait) + [ANY_SPEC] * n_after),
        out_specs=tuple([SEM_SPEC] * (2 * n_start) + [HBM_SPEC] * n_buf + [pl.BlockSpec(memory_space=pltpu.VMEM)]),
        input_output_aliases={i: 2 * n_start + i for i in range(n_buf)},
        compiler_params=pltpu.CompilerParams(has_side_effects=EFFECT),
    )(*[pltpu.with_memory_space_constraint(b, pltpu.HBM) for b in bufs], *sems, *after)
    flights = [(out[2 * s], out[2 * s + 1]) for s in range(n_start)]
    return list(out[2 * n_start:2 * n_start + n_buf]), flights, out[-1]


def _landing(shape):
    return lax.empty(shape, BF16)


def _both(flights, like):
    return [(f, kind, like) for f in flights for kind in ("send", "recv")]


def _allreduce_small(rows, D):
    n_rows = len(rows)

    def body(*refs):
        in_refs, out_ref, all_ref = refs[:n_rows], refs[n_rows], refs[n_rows + 1]
        send_sems, recv_sems = refs[n_rows + 2], refs[n_rows + 3]
        x, y, c = _me()
        mine = all_ref.at[_flat(x, y, c)]
        mine[...] = jnp.zeros((8, D), F32)
        for i, r in enumerate(in_refs):
            mine[i:i + 1, :] = r[...]
        copies = []
        for k in range(N_DEV - 1):
            copies.append(pltpu.make_async_remote_copy(
                src_ref=mine, dst_ref=mine, send_sem=send_sems.at[k], recv_sem=recv_sems.at[k],
                device_id=_role_peer(k + 1, x, y, c), device_id_type=MESH))
        for cp in copies:
            cp.start()
        for cp in copies:
            cp.wait_recv()
        for cp in copies:
            cp.wait_send()
        tot = all_ref[0]
        for d in range(1, N_DEV):
            tot = tot + all_ref[d]
        out_ref[...] = tot

    vm = pl.BlockSpec(memory_space=pltpu.VMEM)
    return pl.pallas_call(
        body, name="allreduce_small", in_specs=[vm] * n_rows, out_specs=vm,
        out_shape=jax.ShapeDtypeStruct((8, D), F32),
        scratch_shapes=[pltpu.VMEM((N_DEV, 8, D), F32), pltpu.SemaphoreType.DMA((7,)), pltpu.SemaphoreType.DMA((7,))],
    )(*rows)


def _step(ids, x, target, g_in, g_out, sinks, lb_logits, rnn_norm, pre_norm, post_norm, dist):
    T, D = x.shape
    n_roles, _, wd = g_in.shape
    ro = g_out.shape[1]
    assert n_roles % 2 == 0 and (not dist or n_roles == N_DEV)
    n_chips = n_roles // 2 - 1

    if dist:
        (g_in,), fl_in, _ = _comm_call(
            "gather_in_start", [g_in], starts=[(0, 0, 0, SIBLING, SIBLING)] + [(0, 0, 0, 2 + 2 * j, 2 + 2 * j) for j in range(2)])
    h, h_t = _prenorm(x, pre_norm)
    if dist:
        (g_in,), _, _ = _comm_call("gather_in_sibling", [g_in], waits=[(fl_in[0], "recv", 0)], after=(h,))
    proj = _proj_in(ids, h, g_in, 0, 2, None, "proj_in_0")
    fl_fwd, fl_half = [], []

    def pass_on(i, g):
        slot = 2 + 2 * i
        starts = [(0, slot, 0, slot + 1, SIBLING)]
        if i < 2:
            waits = [(fl_in[1 + i], "recv", 0)]
            starts.append((0, slot, 0, 6, 4 - 2 * i, (i * (D // 2), D // 2)))
        else:
            waits = [(f, "recv", 0, (0, D // 2)) for f in fl_half]
        (g,), fl, _ = _comm_call(f"gather_in_pass_{i}", [g], waits=waits, starts=starts, after=(proj,))
        fl_fwd.append(fl[0])
        fl_half.extend(fl[1:])
        return g

    if dist:
        g_in = pass_on(1, pass_on(0, g_in))
    for j in range(n_chips):
        after = ()
        if dist:
            if j == 2:
                g_in = pass_on(2, g_in)
            (g_in,), _, _ = _comm_call(f"gather_in_passed_{j}", [g_in], waits=[(fl_fwd[j], "recv", 0)], after=(proj,))
            if j == 2:
                (g_in, g_out), fl_out, token = _comm_call(
                    "gather_out_start", [g_in, g_out],
                    starts=[(1, 0, 1, SIBLING, SIBLING)] + [(1, 0, 1, 2 + 2 * i, 2 + 2 * i) for i in range(3)], after=(proj,))
                after = (token,)
        proj = _proj_in(ids, h, g_in, 2 + 2 * j, 2, proj, f"proj_in_{j + 1}", after)
    mixed, mixed_t = _attn_fwd(proj, sinks, D)
    if dist:
        (g_in, g_out), fl_out_fwd, _ = _comm_call(
            "gather_out_pass", [g_in, g_out],
            waits=[(f, "send", 0) for f in fl_in + fl_fwd] + [(f, "send", 0, (0, D // 2)) for f in fl_half]
            + [(fl_out[1 + i], "recv", 1) for i in range(3)],
            starts=[(1, 2 + 2 * i, 1, 3 + 2 * i, SIBLING) for i in range(3)], after=(mixed,))
    mixed, mixed_t, o_raw, states = _rnn_fwd(proj, lb_logits, rnn_norm, mixed, mixed_t, D)
    if dist:
        (g_out,), _, _ = _comm_call(
            "gather_out_done", [g_out],
            waits=[(fl_out[0], "recv", 0)] + [(f, "recv", 0) for f in fl_out_fwd]
            + [(f, "send", 0) for f in fl_out + fl_out_fwd], after=(states,))
    y = _proj_out(ids, mixed, g_out)
    dy, dout, g_post, sq_err = _loss_head(y, x, target, post_norm)

    dmix = _dmixed(ids, dy, g_out)
    p_out, after = _dw_out(ids, mixed_t, dy, n_roles), ()
    if dist:
        (p_out, l_out), fl_so, token = _comm_call(
            "scatter_out_start", [p_out, _landing((n_roles - 1, ro, D))],
            starts=[(0, r, 1, r - 1, r) for r in range(1, n_roles)])
        after = (token,)
    d_aq, d_ak, d_av, d_ag, d_sink = _attn_bwd(proj, sinks, dmix, D, after)
    d_rq, d_rf, d_ri, d_rg, d_lb, g_rnn = _rnn_bwd(proj, lb_logits, rnn_norm, o_raw, states, dmix, D)
    dproj = jnp.concatenate([d_aq, d_ak, d_av, d_ag, d_rq, d_rf, d_ri, d_rg], axis=1)

    p_far = _dw_in(ids, h_t, dproj, n_roles, 2, n_roles - 2, "dw_in_far")
    if dist:
        (p_far, q_far), fl_pair, token = _comm_call(
            "scatter_in_pair_start", [p_far, _landing((3, D, wd))],
            starts=[(0, 1 + 2 * j, 1, j, SIBLING) for j in range(3)])
        p_near = _dw_in(ids, h_t, dproj, n_roles, 0, 2, "dw_in_near", after=(token,))
        (p_far, q_far), _, _ = _comm_call("scatter_in_pair_wait", [p_far, q_far], waits=_both(fl_pair, 1), after=(p_near,))
        chip_sum = _pair_sum(p_far, q_far)
        (chip_sum, z_far), fl_chip, token = _comm_call(
            "scatter_in_chip_start", [chip_sum, _landing((3, D, wd))],
            starts=[(0, j, 1, j, 2 + 2 * j) for j in range(3)])
        (p_near, q_near), fl_sib, token = _comm_call(
            "scatter_in_sibling_start", [p_near, _landing((1, D, wd))], starts=[(0, 1, 1, 0, SIBLING)], after=(token,))
        after = (token,)
    else:
        p_near, after = _dw_in(ids, h_t, dproj, n_roles, 0, 2, "dw_in_near"), ()
    dh = _dh(ids, dproj, g_in, after)
    grad_x, g_pre = _prenorm_bwd(x, dh, dout, pre_norm)
    n_q = D // ATTN_HEAD
    sink_row = jnp.pad(d_sink[:, 0, :2 * GQA].reshape(1, n_q), ((0, 0), (0, D - n_q)))
    rows = [d_lb, g_rnn, g_pre, g_post, sq_err, sink_row]
    if not dist:
        return grad_x, [(p_near, r) for r in range(2)] + [(p_far, r) for r in range(n_roles - 2)], \
            [(p_out, r) for r in range(n_roles)], rows

    (p_out, l_out), _, _ = _comm_call("scatter_out_wait", [p_out, l_out], waits=_both(fl_so, 1), after=(grad_x,))
    (chip_sum, z_far, p_near, q_near), _, _ = _comm_call(
        "scatter_in_wait", [chip_sum, z_far, p_near, q_near], waits=_both(fl_chip, 1) + _both(fl_sib, 3), after=(p_out,))
    parts_in = [(p_near, 0), (q_near, 0)] + [(z_far, j) for j in range(3)]
    parts_out = [(p_out, 0)] + [(l_out, k) for k in range(n_roles - 1)]
    return grad_x, parts_in, parts_out, rows


def kernel(x, w_in, attn_sinks, lb_logits, rnn_norm, w_out, pre_norm, post_norm, loss_target, m_w_in, m_attn_sinks, m_lb_logits, m_rnn_norm, m_w_out, m_pre_norm, m_post_norm, v_w_in, v_attn_sinks, v_lb_logits, v_rnn_norm, v_w_out, v_pre_norm, v_post_norm):
    _, T, D = x.shape
    ro = w_out.shape[1]
    n_q = attn_sinks.shape[1]
    assert lb_logits.shape[0] == 2 and n_q == D // ATTN_HEAD and n_q <= LANES

    grad_x, parts_in, parts_out, small_rows = _step(
        _role_ids(), x[0], loss_target[0], _cast_slot0(w_in[0], N_DEV, "cast_w_in"),
        _cast_slot0(w_out[0], N_DEV, "cast_w_out"), attn_sinks, lb_logits, rnn_norm, pre_norm, post_norm, True)

    g_wo, d_wo, nm_wo, nv_wo = _adamw_big(w_out[0], m_w_out[0], v_w_out[0], parts_out, _pick(ro, 64), "adamw_w_out")
    g_wi, d_wi, nm_wi, nv_wi = _adamw_big(w_in[0], m_w_in[0], v_w_in[0], parts_in, _pick(D, 128), "adamw_w_in")

    total = _allreduce_small(small_rows, D)
    pad = lambda a: jnp.pad(a, ((0, 0), (0, LANES - n_q)))
    moments = [pad(m_attn_sinks), m_lb_logits, m_rnn_norm, m_pre_norm, m_post_norm,
               pad(v_attn_sinks), v_lb_logits, v_rnn_norm, v_pre_norm, v_post_norm]
    res = _adamw_small(total, pad(attn_sinks), lb_logits, rnn_norm, pre_norm, post_norm, moments, D)
    loss = res[0][0, 0]
    small = [[res[1 + 4 * i + j] for i in range(5)] for j in range(4)]
    for j in range(4):
        small[j][0] = small[j][0][:, :n_q]

    def assemble(j, wi, wo):
        s = small[j]
        return [wi[None], s[0], s[1], s[2], wo[None], s[3], s[4]]

    return (loss, grad_x[None], *assemble(0, g_wi, g_wo), *assemble(1, d_wi, d_wo),
            *assemble(2, nm_wi, nm_wo), *assemble(3, nv_wi, nv_wo))
```

```python
import functools

import jax
import jax.numpy as jnp
from jax import lax
from jax.experimental import pallas as pl
from jax.experimental.pallas import tpu as pltpu

F32 = jnp.float32
BF16 = jnp.bfloat16

ATTN_HEAD = 64
GQA = 8
WINDOW = 128
RNN_HEAD = 128
CHUNK = 64
NORM_EPS = 1e-6
LANES = 128
N_DEV = 8

ADAM_LR = 0.001
ADAM_B1 = 0.9
ADAM_B2 = 0.999
ADAM_EPS = 1e-08
ADAM_WD = 0.01
ADAM_STEP = 10

VMEM_LIMIT = 56 * 1024 * 1024
MESH = pl.DeviceIdType.MESH
ANY_SPEC = pl.BlockSpec(memory_space=pl.ANY)
HBM_SPEC = pl.BlockSpec(memory_space=pltpu.HBM)
SEM_SPEC = pl.BlockSpec(memory_space=pltpu.SEMAPHORE)
EFFECT = pltpu.SideEffectType.DATAFLOW_SIDE_EFFECTING

NT_DIMS = (((1,), (1,)), ((), ()))
NN_DIMS = (((1,), (0,)), ((), ()))
TN_DIMS = (((0,), (0,)), ((), ()))


def _params(*sem):
    return pltpu.CompilerParams(dimension_semantics=sem, vmem_limit_bytes=VMEM_LIMIT)


def _dot(a, b, dims):
    return lax.dot_general(a, b, dims, preferred_element_type=F32)


def _sigmoid(v):
    return 1.0 / (1.0 + jnp.exp(-v))


def _pick(n, pref):
    t = min(n, pref)
    assert n % t == 0, (n, pref)
    return t


MXU_COLS = 256
WIDE_TILE = 5 * MXU_COLS


def _col_tiles(wd):
    wide = wd // WIDE_TILE
    rest = wd - wide * WIDE_TILE
    assert rest % LANES == 0 and (rest == 0 or (wide * WIDE_TILE) % rest == 0), wd
    return [(WIDE_TILE, 0, wide)] * (wide > 0) + [(rest, wide * WIDE_TILE, 1)] * (rest > 0)


def _matmul(ids, a, b, *, grid, a_spec, b_spec, o_spec, out_shape, trans_b, name, after=(), prev=None):
    nk = grid[2]
    dims = NT_DIMS if trans_b else NN_DIMS
    n_skip = len(after) + (prev is not None)

    def body(ids_ref, a_ref, b_ref, *rest):
        del ids_ref
        o_ref, scratch = rest[n_skip], rest[n_skip + 1:]
        prod = _dot(a_ref[...], b_ref[...], dims)
        if nk == 1:
            o_ref[...] = prod.astype(o_ref.dtype)
        else:
            acc_ref, = scratch
            k = pl.program_id(2)

            @pl.when(k == 0)
            def _():
                acc_ref[...] = prod

            @pl.when(k > 0)
            def _():
                acc_ref[...] += prod

            @pl.when(k == nk - 1)
            def _():
                o_ref[...] = acc_ref[...].astype(o_ref.dtype)

    scratch = [] if nk == 1 else [pltpu.VMEM(tuple(d for d in o_spec.block_shape if d is not None), F32)]
    extra = list(after) + ([prev] if prev is not None else [])
    aliases = {3 + len(after): 0} if prev is not None else {}
    return pl.pallas_call(
        body, name=name, out_shape=out_shape, input_output_aliases=aliases,
        grid_spec=pltpu.PrefetchScalarGridSpec(
            num_scalar_prefetch=1, grid=grid, in_specs=[a_spec, b_spec] + [ANY_SPEC] * len(extra),
            out_specs=o_spec, scratch_shapes=scratch),
        compiler_params=_params("parallel", "parallel", "arbitrary"),
    )(ids, a, b, *extra)


def _proj_in(ids, h, w_roles, r0, step, prev, name, after=()):
    T, D = h.shape
    n_roles, _, wd = w_roles.shape
    tm = _pick(T, 1024)
    for t, (tn, first, count) in enumerate(_col_tiles(wd)):
        prev = _matmul(
            ids, h, w_roles, grid=(T // tm, 2 * count, 1),
            a_spec=pl.BlockSpec((tm, D), lambda i, j, k, ids: (i, 0)),
            b_spec=pl.BlockSpec((None, D, tn), functools.partial(
                lambda tn, first, count, i, j, k, ids: (r0 + step * (j // count), 0, first // tn + j % count),
                tn, first, count)),
            o_spec=pl.BlockSpec((pl.Element(tm), pl.Element(tn)), functools.partial(
                lambda tn, first, count, i, j, k, ids: (
                    i * tm, pl.multiple_of(ids[r0 + step * (j // count)] * wd + first + (j % count) * tn, LANES)),
                tn, first, count)),
            out_shape=jax.ShapeDtypeStruct((T, n_roles * wd), F32), trans_b=False, name=f"{name}_{t}",
            after=after, prev=prev)
    return prev


def _proj_out(ids, mixed, wo_roles):
    T, E = mixed.shape
    n_roles, R, D = wo_roles.shape
    tm, tn = _pick(T, 512), _pick(D, 512)

    def body(ids_ref, a_ref, b_ref, o_ref, a_roles):
        @pl.when(pl.program_id(1) == 0)
        def _():
            for k in range(n_roles):
                col = pl.multiple_of(ids_ref[k] * R, R)
                a_roles[:, k * R:(k + 1) * R] = a_ref[:, pl.ds(col, R)]

        o_ref[...] = _dot(a_roles[...], b_ref[...].reshape(E, tn), NN_DIMS)

    return pl.pallas_call(
        body, name="proj_out", out_shape=jax.ShapeDtypeStruct((T, D), F32),
        grid_spec=pltpu.PrefetchScalarGridSpec(
            num_scalar_prefetch=1, grid=(T // tm, D // tn),
            in_specs=[pl.BlockSpec((tm, E), lambda i, j, ids: (i, 0)),
                      pl.BlockSpec((n_roles, R, tn), lambda i, j, ids: (0, 0, j))],
            out_specs=pl.BlockSpec((tm, tn), lambda i, j, ids: (i, j)),
            scratch_shapes=[pltpu.VMEM((tm, E), BF16)]),
        compiler_params=_params("parallel", "arbitrary"))(ids, mixed, wo_roles)


def _dmixed(ids, dy, wo_roles):
    T, D = dy.shape
    n_roles, R, _ = wo_roles.shape
    tm = _pick(T, 1024)
    return _matmul(
        ids, dy, wo_roles, grid=(T // tm, n_roles, 1),
        a_spec=pl.BlockSpec((tm, D), lambda i, j, k, ids: (i, 0)),
        b_spec=pl.BlockSpec((None, R, D), lambda i, j, k, ids: (j, 0, 0)),
        o_spec=pl.BlockSpec((tm, R), lambda i, j, k, ids: (i, ids[j])),
        out_shape=jax.ShapeDtypeStruct((T, n_roles * R), F32), trans_b=True, name="dmixed")


def _dw_out(ids, mixed_t, dy, n_roles):
    E, T = mixed_t.shape
    D = dy.shape[1]
    R = E // n_roles
    tn = _pick(D, 512)
    return _matmul(
        ids, mixed_t, dy, grid=(n_roles, D // tn, 1),
        a_spec=pl.BlockSpec((R, T), lambda i, j, k, ids: (ids[i], 0)),
        b_spec=pl.BlockSpec((T, tn), lambda i, j, k, ids: (0, j)),
        o_spec=pl.BlockSpec((None, R, tn), lambda i, j, k, ids: (i, 0, j)),
        out_shape=jax.ShapeDtypeStruct((n_roles, R, D), BF16), trans_b=False, name="dw_out")


def _dw_in(ids, h_t, dproj, n_roles, r0, nr, name, after=()):
    D, T = h_t.shape
    wd = dproj.shape[1] // n_roles
    tm, out = _pick(D, 1024), None
    for t, (tn, first, count) in enumerate(_col_tiles(wd)):
        out = _matmul(
            ids, h_t, dproj, grid=(D // tm, nr * count, 1),
            a_spec=pl.BlockSpec((tm, T), lambda i, j, k, ids: (i, 0)),
            b_spec=pl.BlockSpec((pl.Element(T), pl.Element(tn)), functools.partial(
                lambda tn, first, count, i, j, k, ids: (
                    0, pl.multiple_of(ids[r0 + j // count] * wd + first + (j % count) * tn, LANES)),
                tn, first, count)),
            o_spec=pl.BlockSpec((None, tm, tn), functools.partial(
                lambda tn, first, count, i, j, k, ids: (j // count, i, first // tn + j % count), tn, first, count)),
            out_shape=jax.ShapeDtypeStruct((nr, D, wd), BF16), trans_b=False, name=f"{name}_{t}", after=after, prev=out)
    return out


def _dh(ids, dproj, w_roles, after=()):
    T = dproj.shape[0]
    n_roles, D, wd = w_roles.shape
    tm, tn = _pick(T, 1024), _pick(D, 1024)
    return _matmul(
        ids, dproj, w_roles, grid=(T // tm, D // tn, n_roles),
        a_spec=pl.BlockSpec((tm, wd), lambda i, j, k, ids: (i, ids[k])),
        b_spec=pl.BlockSpec((None, tn, wd), lambda i, j, k, ids: (k, j, 0)),
        o_spec=pl.BlockSpec((tm, tn), lambda i, j, k, ids: (i, j)),
        out_shape=jax.ShapeDtypeStruct((T, D), F32), trans_b=True, name="dh", after=after)


def _cast_slot0(w, n_roles, name):
    R, C = w.shape
    tr = _pick(R, 256)

    def body(w_ref, o_ref):
        o_ref[...] = w_ref[...].astype(BF16)

    return pl.pallas_call(
        body, name=name, grid=(R // tr,), in_specs=[pl.BlockSpec((tr, C), lambda i: (i, 0))],
        out_specs=pl.BlockSpec((None, tr, C), lambda i: (0, i, 0)),
        out_shape=jax.ShapeDtypeStruct((n_roles, R, C), BF16), compiler_params=_params("parallel"))(w)


def _prenorm(x, gain):
    T, D = x.shape
    tm = _pick(T, 256)

    def body(x_ref, g_ref, h_ref, ht_ref):
        xv = x_ref[...]
        r = lax.rsqrt(jnp.mean(xv * xv, axis=-1, keepdims=True) + NORM_EPS)
        h = xv * r * g_ref[...]
        h_ref[...] = h.astype(BF16)
        ht_ref[...] = h.T.astype(BF16)

    return pl.pallas_call(
        body, name="prenorm", grid=(T // tm,),
        in_specs=[pl.BlockSpec((tm, D), lambda i: (i, 0)), pl.BlockSpec((1, D), lambda i: (0, 0))],
        out_specs=[pl.BlockSpec((tm, D), lambda i: (i, 0)), pl.BlockSpec((D, tm), lambda i: (0, i))],
        out_shape=[jax.ShapeDtypeStruct((T, D), BF16), jax.ShapeDtypeStruct((D, T), BF16)],
        compiler_params=_params("parallel"))(x, gain)


def _norm_bwd(u, yn, r):
    return r * (u - yn * jnp.mean(u * yn, axis=-1, keepdims=True))


def _loss_head(y, x, target, gain):
    T, D = y.shape
    tm = _pick(T, 256)

    def body(y_ref, x_ref, t_ref, g_ref, dy_ref, dout_ref, gpost_ref, sq_ref):
        yv = y_ref[...]
        g = g_ref[...]
        r = lax.rsqrt(jnp.mean(yv * yv, axis=-1, keepdims=True) + NORM_EPS)
        yn = yv * r
        err = x_ref[...] + yn * g - t_ref[...]
        dout = err * (1.0 / D)
        dy_ref[...] = _norm_bwd(dout * g, yn, r).astype(BF16)
        dout_ref[...] = dout

        @pl.when(pl.program_id(0) == 0)
        def _():
            gpost_ref[...] = jnp.zeros_like(gpost_ref)
            sq_ref[...] = jnp.zeros_like(sq_ref)

        gpost_ref[...] += jnp.sum(dout * yn, axis=0, keepdims=True)
        sq_ref[...] += jnp.sum(err * err, axis=0, keepdims=True)

    row = pl.BlockSpec((tm, D), lambda i: (i, 0))
    vec = pl.BlockSpec((1, D), lambda i: (0, 0))
    return pl.pallas_call(
        body, name="loss_head", grid=(T // tm,), in_specs=[row, row, row, vec], out_specs=[row, row, vec, vec],
        out_shape=[jax.ShapeDtypeStruct((T, D), BF16), jax.ShapeDtypeStruct((T, D), F32),
                   jax.ShapeDtypeStruct((1, D), F32), jax.ShapeDtypeStruct((1, D), F32)],
        compiler_params=_params("arbitrary"))(y, x, target, gain)


def _prenorm_bwd(x, dh, dout, gain):
    T, D = x.shape
    tm = _pick(T, 256)

    def body(x_ref, dh_ref, dout_ref, g_ref, gx_ref, gpre_ref):
        xv = x_ref[...]
        dhv = dh_ref[...]
        r = lax.rsqrt(jnp.mean(xv * xv, axis=-1, keepdims=True) + NORM_EPS)
        xn = xv * r
        gx_ref[...] = dout_ref[...] + _norm_bwd(dhv * g_ref[...], xn, r)

        @pl.when(pl.program_id(0) == 0)
        def _():
            gpre_ref[...] = jnp.zeros_like(gpre_ref)

        gpre_ref[...] += jnp.sum(dhv * xn, axis=0, keepdims=True)

    row = pl.BlockSpec((tm, D), lambda i: (i, 0))
    vec = pl.BlockSpec((1, D), lambda i: (0, 0))
    return pl.pallas_call(
        body, name="prenorm_bwd", grid=(T // tm,), in_specs=[row, row, row, vec], out_specs=[row, vec],
        out_shape=[jax.ShapeDtypeStruct((T, D), F32), jax.ShapeDtypeStruct((1, D), F32)],
        compiler_params=_params("arbitrary"))(x, dh, dout, gain)


def _attn_masks(n):
    row = lax.broadcasted_iota(jnp.int32, (2 * WINDOW, 2 * WINDOW), 0) % WINDOW
    col = lax.broadcasted_iota(jnp.int32, (2 * WINDOW, 2 * WINDOW), 1)
    valid = (col > row) & (col <= row + WINDOW) & ((n > 0) | (col >= WINDOW))
    low = lax.broadcasted_iota(jnp.int32, (1, LANES), 1) < ATTN_HEAD
    top = lax.broadcasted_iota(jnp.int32, (2 * WINDOW, 1), 0) < WINDOW
    return valid, low, top


def _dup_half(pair, keep):
    return jnp.where(keep, pair, pltpu.roll(pair, ATTN_HEAD, 1))


def _fold_half(v):
    return v + pltpu.roll(v, ATTN_HEAD, 1)


def _attn_probs(qpair, k2, sink_lo, sink_hi, valid, low, top):
    q2 = jnp.concatenate([jnp.where(low, qpair, 0.0), jnp.where(low, 0.0, qpair)], axis=0).astype(BF16)
    s = _dot(q2, k2, NT_DIMS) * (ATTN_HEAD ** -0.5)
    s = jnp.where(valid, s, -jnp.inf)
    sink = jnp.where(top, sink_lo, sink_hi)
    m = jnp.maximum(jnp.max(s, axis=-1, keepdims=True), sink)
    p = jnp.exp(s - m)
    psink = jnp.exp(sink - m)
    inv = 1.0 / (jnp.sum(p, axis=-1, keepdims=True) + psink)
    return q2, p * inv, psink * inv


def _attn_specs(D):
    kb = D // LANES
    vb = kb + D // (8 * LANES)
    gb = (D + D // 4) // 512
    wide = lambda off: [pl.BlockSpec((WINDOW, 512), functools.partial(lambda o, e, jp, n: (n, o + 2 * jp + e), off, e))
                        for e in (0, 1)]
    cur = lambda off: pl.BlockSpec((WINDOW, LANES), functools.partial(lambda o, jp, n: (n, o + jp), off))
    prev = lambda off: pl.BlockSpec((WINDOW, LANES),
                                    functools.partial(lambda o, jp, n: (jnp.maximum(n - 1, 0), o + jp), off))
    return wide(0) + [cur(kb), prev(kb), cur(vb), prev(vb)] + wide(gb)


def _attn_fwd(proj, sinks, D):
    T = proj.shape[0]
    nb, njp = T // WINDOW, D // 1024

    def body(sink_ref, qlo_ref, qhi_ref, kc_ref, kp_ref, vc_ref, vp_ref, glo_ref, ghi_ref, mix_ref, mixt_ref):
        jp, n = pl.program_id(0), pl.program_id(1)
        valid, low, top = _attn_masks(n)
        kk = jnp.concatenate([kp_ref[...], kc_ref[...]], axis=0)
        vv = jnp.concatenate([vp_ref[...], vc_ref[...]], axis=0)
        for hj, (q_ref, g_ref) in enumerate(((qlo_ref, glo_ref), (qhi_ref, ghi_ref))):
            keep = low if hj == 0 else jnp.logical_not(low)
            k2 = _dup_half(kk, keep).astype(BF16)
            v2 = _dup_half(vv, keep).astype(BF16)
            for p in range(4):
                cols = slice(LANES * p, LANES * (p + 1))
                head = (2 * jp + hj) * GQA + 2 * p
                _, probs, _ = _attn_probs(q_ref[:, cols], k2, sink_ref[0, head], sink_ref[0, head + 1],
                                          valid, low, top)
                o2 = _dot(probs.astype(BF16), v2, NN_DIMS)
                opair = jnp.where(low, o2[:WINDOW], o2[WINDOW:])
                g = g_ref[:, cols]
                out = opair * (g * _sigmoid(g))
                oc = slice(512 * hj + LANES * p, 512 * hj + LANES * (p + 1))
                mix_ref[:, oc] = out.astype(BF16)
                mixt_ref[oc, :] = out.T.astype(BF16)

    return pl.pallas_call(
        body, name="attn_fwd", grid=(njp, nb),
        in_specs=[pl.BlockSpec(memory_space=pltpu.SMEM)] + _attn_specs(D),
        out_specs=[pl.BlockSpec((WINDOW, 1024), lambda jp, n: (n, jp)),
                   pl.BlockSpec((1024, WINDOW), lambda jp, n: (jp, n))],
        out_shape=[jax.ShapeDtypeStruct((T, 2 * D), BF16), jax.ShapeDtypeStruct((2 * D, T), BF16)],
        compiler_params=_params("parallel", "parallel"))(sinks, *([proj] * 8))


def _attn_bwd(proj, sinks, dmix, D, after=()):
    T = proj.shape[0]
    nb, njp = T // WINDOW, D // 1024
    n_after = len(after)

    def body(sink_ref, qlo_ref, qhi_ref, kc_ref, kp_ref, vc_ref, vp_ref, glo_ref, ghi_ref, dmix_ref, *rest):
        dq_ref, dk_ref, dv_ref, dg_ref, dsink_ref, kcarry_ref, vcarry_ref = rest[n_after:]
        jp, step = pl.program_id(0), pl.program_id(1)
        n = nb - 1 - step
        valid, low, top = _attn_masks(n)
        lane = lax.broadcasted_iota(jnp.int32, (1, LANES), 1)
        kk = jnp.concatenate([kp_ref[...], kc_ref[...]], axis=0)
        vv = jnp.concatenate([vp_ref[...], vc_ref[...]], axis=0)

        @pl.when(step == 0)
        def _():
            kcarry_ref[...] = jnp.zeros_like(kcarry_ref)
            vcarry_ref[...] = jnp.zeros_like(vcarry_ref)
            dsink_ref[...] = jnp.zeros_like(dsink_ref)

        dk_pair = jnp.zeros((2 * WINDOW, LANES), F32)
        dv_pair = jnp.zeros((2 * WINDOW, LANES), F32)
        dsink = jnp.zeros((1, LANES), F32)
        for hj, (q_ref, g_ref) in enumerate(((qlo_ref, glo_ref), (qhi_ref, ghi_ref))):
            keep = low if hj == 0 else jnp.logical_not(low)
            k2 = _dup_half(kk, keep).astype(BF16)
            v2 = _dup_half(vv, keep).astype(BF16)
            dk_head = jnp.zeros((2 * WINDOW, LANES), F32)
            dv_head = jnp.zeros((2 * WINDOW, LANES), F32)
            for p in range(4):
                cols = slice(LANES * p, LANES * (p + 1))
                oc = slice(512 * hj + LANES * p, 512 * hj + LANES * (p + 1))
                head = (2 * jp + hj) * GQA + 2 * p
                q2, probs, psink = _attn_probs(q_ref[:, cols], k2, sink_ref[0, head], sink_ref[0, head + 1],
                                               valid, low, top)
                pb = probs.astype(BF16)
                o2 = _dot(pb, v2, NN_DIMS)
                opair = jnp.where(low, o2[:WINDOW], o2[WINDOW:])
                g = g_ref[:, cols]
                sg = _sigmoid(g)
                dgated = dmix_ref[:, oc]
                dg_ref[:, oc] = (dgated * opair * (sg * (1.0 + g * (1.0 - sg)))).astype(BF16)
                do = dgated * (g * sg)
                do2 = jnp.concatenate([jnp.where(low, do, 0.0), jnp.where(low, 0.0, do)], axis=0).astype(BF16)
                dp = _dot(do2, v2, NT_DIMS)
                delta = jnp.sum(probs * dp, axis=-1, keepdims=True)
                ds = (probs * (dp - delta) * (ATTN_HEAD ** -0.5)).astype(BF16)
                dq2 = _dot(ds, k2, NN_DIMS)
                dq_ref[:, oc] = jnp.where(low, dq2[:WINDOW], dq2[WINDOW:]).astype(BF16)
                dk_head += _dot(ds, q2, TN_DIMS)
                dv_head += _dot(pb, do2, TN_DIMS)
                ps = psink * delta
                local = hj * GQA + 2 * p
                dsink -= jnp.where(lane == local, jnp.sum(ps[:WINDOW], axis=0, keepdims=True), 0.0)
                dsink -= jnp.where(lane == local + 1, jnp.sum(ps[WINDOW:], axis=0, keepdims=True), 0.0)
            dk_pair += jnp.where(keep, _fold_half(dk_head), 0.0)
            dv_pair += jnp.where(keep, _fold_half(dv_head), 0.0)
        dk_ref[...] = (dk_pair[WINDOW:] + kcarry_ref[...]).astype(BF16)
        dv_ref[...] = (dv_pair[WINDOW:] + vcarry_ref[...]).astype(BF16)
        kcarry_ref[...] = dk_pair[:WINDOW]
        vcarry_ref[...] = dv_pair[:WINDOW]
        dsink_ref[...] += dsink

    rev = lambda spec: pl.BlockSpec(spec.block_shape, functools.partial(
        lambda f, jp, s: f(jp, nb - 1 - s), spec.index_map))
    in_specs = [rev(s) for s in _attn_specs(D)]
    wide_out = pl.BlockSpec((WINDOW, 1024), lambda jp, s: (nb - 1 - s, jp))
    pair_out = pl.BlockSpec((WINDOW, LANES), lambda jp, s: (nb - 1 - s, jp))
    return pl.pallas_call(
        body, name="attn_bwd", grid=(njp, nb),
        in_specs=[pl.BlockSpec(memory_space=pltpu.SMEM)] + in_specs + [wide_out] + [ANY_SPEC] * n_after,
        out_specs=[wide_out, pair_out, pair_out, wide_out, pl.BlockSpec((None, 1, LANES), lambda jp, s: (jp, 0, 0))],
        out_shape=[jax.ShapeDtypeStruct((T, D), BF16), jax.ShapeDtypeStruct((T, D // 8), BF16),
                   jax.ShapeDtypeStruct((T, D // 8), BF16), jax.ShapeDtypeStruct((T, D), BF16),
                   jax.ShapeDtypeStruct((njp, 1, LANES), F32)],
        scratch_shapes=[pltpu.VMEM((WINDOW, LANES), F32), pltpu.VMEM((WINDOW, LANES), F32)],
        compiler_params=_params("parallel", "arbitrary"))(sinks, *([proj] * 8), dmix, *after)


RNN_TB = 512
RNN_HB = 8
RNN_WIDE = RNN_HB * RNN_HEAD


def _split3(v):
    a = v.astype(BF16)
    r = v - a.astype(F32)
    b = r.astype(BF16)
    c = (r - b.astype(F32)).astype(BF16)
    return a, b, c


def _tri_sum(tri, v):
    a, b, c = _split3(v)
    return _dot(tri, a, NN_DIMS) + _dot(tri, b, NN_DIMS) + _dot(tri, c, NN_DIMS)


def _lower_bound(lb_ref):
    l0, l1 = lb_ref[0:1, :], lb_ref[1:2, :]
    m = jnp.maximum(l0, l1)
    e0, e1 = jnp.exp(l0 - m), jnp.exp(l1 - m)
    return e0 / (e0 + e1)


def _rnn_gates(rq, rf, lb):
    sq = _sigmoid(rq)
    sf = _sigmoid(rf)
    f = lb + (1.0 - lb) * sf
    return sq, sf, f


def _rnn_decays(g, tri):
    return _rnn_factors(_tri_sum(tri, g))


def _rnn_factors(G):
    last = G[CHUNK - 1:CHUNK, :]
    mid = G[CHUNK // 2 - 1:CHUNK // 2, :]
    return G, jnp.exp(G), jnp.exp(G - mid), jnp.exp(mid - G), jnp.exp(last - G), jnp.exp(last)


def _chunk_masks():
    r = lax.broadcasted_iota(jnp.int32, (CHUNK, CHUNK), 0)
    c = lax.broadcasted_iota(jnp.int32, (CHUNK, CHUNK), 1)
    return r >= c, (r >= c).astype(BF16), (r <= c).astype(BF16)


def _rnn_specs(T, D, tb, rev):
    nt = T // tb
    base = (2 * D + D // 4) // LANES
    t_of = (lambda s: nt - 1 - s) if rev else (lambda s: s)
    assert base % RNN_HB == 0 and (D // LANES) % RNN_HB == 0
    cols = [pl.BlockSpec((tb, RNN_WIDE), functools.partial(lambda o, h, s: (t_of(s), o + h),
                                                            (base + i * (D // LANES)) // RNN_HB))
            for i in range(4)]
    return cols, t_of


def _rnn_fwd(proj, lb_logits, rnn_norm, mixed, mixed_t, D):
    T = proj.shape[0]
    tb = _pick(T, RNN_TB)
    nt, nh, cpb = T // tb, D // RNN_HEAD, tb // CHUNK
    cols, _ = _rnn_specs(T, D, tb, False)

    def body(rq_ref, rf_ref, ri_ref, rg_ref, lb_ref, gain_ref, mix_in, mixt_in,
             mix_ref, mixt_ref, o_ref, st_ref, state_ref):
        del mix_in, mixt_in
        causal, tri, _ = _chunk_masks()
        lb = _lower_bound(lb_ref)

        @pl.when(pl.program_id(1) == 0)
        def _():
            state_ref[...] = jnp.zeros_like(state_ref)

        def chunk(c, carry):
            rows = pl.ds(pl.multiple_of(c * CHUNK, CHUNK), CHUNK)
            heads = range(RNN_HB)
            lns = [slice(RNN_HEAD * hh, RNN_HEAD * (hh + 1)) for hh in heads]
            qs, ks, Gs = [], [], []
            for ln in lns:
                rq, rf = rq_ref[rows, ln], rf_ref[rows, ln]
                sq, _, f = _rnn_gates(rq, rf, lb[:, ln])
                qs.append(rq * sq)
                ks.append(1.0 - f)
                Gs.append(_tri_sum(tri, jnp.log(f)))
            atts, inters, vbs = [], [], []
            for hh, ln in enumerate(lns):
                _, eG, eq, ek, ekl, elast = _rnn_factors(Gs[hh])
                q, k = qs[hh], ks[hh]
                st = state_ref[hh]
                st_ref[hh, c] = st
                vb = ri_ref[rows, ln].astype(BF16)
                vbs.append(vb)
                atts.append(_dot((q * eq).astype(BF16), (k * ek).astype(BF16), NT_DIMS))
                inters.append(_dot((q * eG).astype(BF16), st.astype(BF16), NT_DIMS))
                state_ref[hh] = st * elast + _dot(vb, (k * ekl).astype(BF16), TN_DIMS)
            intras = [_dot(jnp.where(causal, atts[hh], 0.0).astype(BF16), vbs[hh], NN_DIMS) for hh in heads]
            for hh, ln in enumerate(lns):
                o = inters[hh] + intras[hh]
                o_ref[rows, ln] = o
                rg = rg_ref[rows, ln]
                r = lax.rsqrt(jnp.mean(o * o, axis=-1, keepdims=True) + NORM_EPS)
                out = (o * r * gain_ref[:, ln]) * (rg * _sigmoid(rg))
                mix_ref[rows, ln] = out.astype(BF16)
            return carry

        lax.fori_loop(0, cpb, chunk, 0)
        mixt_ref[...] = mix_ref[...].astype(F32).T.astype(BF16)

    nh //= RNN_HB
    vec2 = pl.BlockSpec((2, RNN_WIDE), lambda h, s: (0, h))
    vec1 = pl.BlockSpec((1, RNN_WIDE), lambda h, s: (0, h))
    return pl.pallas_call(
        body, name="rnn_fwd", grid=(nh, nt),
        in_specs=cols + [vec2, vec1, ANY_SPEC, ANY_SPEC],
        out_specs=[pl.BlockSpec((tb, RNN_WIDE), lambda h, s: (s, D // RNN_WIDE + h)),
                   pl.BlockSpec((RNN_WIDE, tb), lambda h, s: (D // RNN_WIDE + h, s)),
                   pl.BlockSpec((tb, RNN_WIDE), lambda h, s: (s, h)),
                   pl.BlockSpec((RNN_HB, cpb, RNN_HEAD, RNN_HEAD), lambda h, s: (h, s, 0, 0))],
        out_shape=[jax.ShapeDtypeStruct(mixed.shape, BF16), jax.ShapeDtypeStruct(mixed_t.shape, BF16),
                   jax.ShapeDtypeStruct((T, D), F32),
                   jax.ShapeDtypeStruct((nh * RNN_HB, T // CHUNK, RNN_HEAD, RNN_HEAD), F32)],
        scratch_shapes=[pltpu.VMEM((RNN_HB, RNN_HEAD, RNN_HEAD), F32)],
        input_output_aliases={6: 0, 7: 1},
        compiler_params=_params("parallel", "arbitrary"))(proj, proj, proj, proj, lb_logits, rnn_norm, mixed, mixed_t)


def _rnn_bwd(proj, lb_logits, rnn_norm, o_raw, states, dmix, D):
    T = proj.shape[0]
    tb = _pick(T, RNN_TB)
    nt, nh, cpb = T // tb, D // RNN_HEAD, tb // CHUNK
    cols, t_of = _rnn_specs(T, D, tb, True)

    def body(rq_ref, rf_ref, ri_ref, rg_ref, lb_ref, gain_ref, o_ref, st_ref, dmix_ref,
             drq_ref, drf_ref, dri_ref, drg_ref, dlb_ref, dgain_ref, dstate_ref):
        causal, tri, tri_t = _chunk_masks()
        lb = _lower_bound(lb_ref)
        gain = gain_ref[...]

        @pl.when(pl.program_id(1) == 0)
        def _():
            dstate_ref[...] = jnp.zeros_like(dstate_ref)
            dlb_ref[...] = jnp.zeros_like(dlb_ref)
            dgain_ref[...] = jnp.zeros_like(dgain_ref)

        def chunk(i, carry):
            c = cpb - 1 - i
            rows = pl.ds(pl.multiple_of(c * CHUNK, CHUNK), CHUNK)
            heads = range(RNN_HB)
            lns = [slice(RNN_HEAD * hh, RNN_HEAD * (hh + 1)) for hh in heads]
            last_row = lax.broadcasted_iota(jnp.int32, (CHUNK, 1), 0) == CHUNK - 1
            A = []
            for ln in lns:
                rq, rf, rg = rq_ref[rows, ln], rf_ref[rows, ln], rg_ref[rows, ln]
                o, dgated = o_ref[rows, ln], dmix_ref[rows, ln]
                gainh = gain[:, ln]
                sgt = _sigmoid(rg)
                r = lax.rsqrt(jnp.mean(o * o, axis=-1, keepdims=True) + NORM_EPS)
                on = o * r
                drg_ref[rows, ln] = (dgated * (on * gainh) * (sgt * (1.0 + rg * (1.0 - sgt)))).astype(BF16)
                d_on = dgated * (rg * sgt)
                dgain_ref[:, ln] += jnp.sum(d_on * on, axis=0, keepdims=True)
                dob = _norm_bwd(d_on * gainh, on, r).astype(BF16)
                sq, sf, f = _rnn_gates(rq, rf, lb[:, ln])
                A.append(dict(rq=rq, sq=sq, sf=sf, f=f, dob=dob, G=_tri_sum(tri, jnp.log(f))))
            for hh, ln in enumerate(lns):
                a = A[hh]
                _, eG, eq, ek, ekl, elast = _rnn_factors(a.pop("G"))
                q, k = a["rq"] * a["sq"], 1.0 - a["f"]
                st, dst = st_ref[hh, c], dstate_ref[hh]
                qg, kl = q * eG, k * ekl
                qmb, kmb = (q * eq).astype(BF16), (k * ek).astype(BF16)
                dob, vb, dstb = a["dob"], ri_ref[rows, ln].astype(BF16), dst.astype(BF16)
                a.update(eG=eG, eq=eq, ek=ek, ekl=ekl, qg=qg, kl=kl, qmb=qmb, kmb=kmb,
                         att=_dot(qmb, kmb, NT_DIMS), datt=_dot(dob, vb, NT_DIMS),
                         dqg=_dot(dob, st.astype(BF16), NN_DIMS), dkl=_dot(vb, dstb, NN_DIMS),
                         dri=_dot(kl.astype(BF16), dstb, NT_DIMS),
                         dlast=jnp.sum(dst * st, axis=0, keepdims=True) * elast)
                dstate_ref[hh] = dst * elast + _dot(dob, qg.astype(BF16), TN_DIMS)
            for hh, ln in enumerate(lns):
                a = A[hh]
                att = jnp.where(causal, a.pop("att"), 0.0).astype(BF16)
                datt = jnp.where(causal, a.pop("datt"), 0.0).astype(BF16)
                dqm = _dot(datt, a["kmb"], NN_DIMS)
                dkm = _dot(datt, a["qmb"], TN_DIMS)
                dri_ref[rows, ln] = (_dot(att, a["dob"], TN_DIMS) + a.pop("dri")).astype(BF16)
                dqg, dkl, kl = a.pop("dqg"), a.pop("dkl"), a.pop("kl")
                a["dq"] = dqg * a.pop("eG") + dqm * a.pop("eq")
                a["dk"] = dkm * a.pop("ek") + dkl * a.pop("ekl")
                dG = dqg * a.pop("qg") + dqm * a.pop("qmb").astype(F32) - dkm * a.pop("kmb").astype(F32) - dkl * kl
                dlast = jnp.sum(dkl * kl, axis=0, keepdims=True) + a.pop("dlast")
                a["dg"] = _tri_sum(tri_t, dG + jnp.where(last_row, dlast, 0.0))
            for hh, ln in enumerate(lns):
                a = A[hh]
                rq, sq, sf = a["rq"], a["sq"], a["sf"]
                df = a["dg"] / a["f"] - a["dk"]
                drq_ref[rows, ln] = (a["dq"] * (sq * (1.0 + rq * (1.0 - sq)))).astype(BF16)
                drf_ref[rows, ln] = (df * (1.0 - lb[:, ln]) * (sf * (1.0 - sf))).astype(BF16)
                dlb_ref[:, ln] += jnp.sum(df * (1.0 - sf), axis=0, keepdims=True)
            return carry

        lax.fori_loop(0, cpb, chunk, 0)

    nh //= RNN_HB
    vec2 = pl.BlockSpec((2, RNN_WIDE), lambda h, s: (0, h))
    vec1 = pl.BlockSpec((1, RNN_WIDE), lambda h, s: (0, h))
    blk = pl.BlockSpec((tb, RNN_WIDE), lambda h, s: (t_of(s), h))
    return pl.pallas_call(
        body, name="rnn_bwd", grid=(nh, nt),
        in_specs=cols + [vec2, vec1, blk,
                         pl.BlockSpec((RNN_HB, cpb, RNN_HEAD, RNN_HEAD), lambda h, s: (h, t_of(s), 0, 0)),
                         pl.BlockSpec((tb, RNN_WIDE), lambda h, s: (t_of(s), D // RNN_WIDE + h))],
        out_specs=[blk, blk, blk, blk, vec1, vec1],
        out_shape=[jax.ShapeDtypeStruct((T, D), BF16)] * 4 + [jax.ShapeDtypeStruct((1, D), F32)] * 2,
        scratch_shapes=[pltpu.VMEM((RNN_HB, RNN_HEAD, RNN_HEAD), F32)],
        compiler_params=_params("parallel", "arbitrary"))(proj, proj, proj, proj, lb_logits, rnn_norm, o_raw, states, dmix)


def _adamw(w, g, m, v):
    m = ADAM_B1 * m + (1.0 - ADAM_B1) * g
    v = ADAM_B2 * v + (1.0 - ADAM_B2) * (g * g)
    m_hat = m / (1.0 - ADAM_B1 ** ADAM_STEP)
    v_hat = v / (1.0 - ADAM_B2 ** ADAM_STEP)
    delta = -ADAM_LR * (m_hat / (jnp.sqrt(v_hat) + ADAM_EPS) + ADAM_WD * w)
    return delta, m, v


def _adamw_big(w, m, v, parts, tr, name):
    R, C = w.shape
    n_parts = len(parts)

    def body(w_ref, m_ref, v_ref, *rest):
        part_refs = rest[:n_parts]
        g_ref, d_ref, nm_ref, nv_ref = rest[n_parts:]
        g = part_refs[0][...].astype(F32)
        for p_ref in part_refs[1:]:
            g = g + p_ref[...].astype(F32)
        delta, nm, nv = _adamw(w_ref[...], g, m_ref[...], v_ref[...])
        g_ref[...] = g
        d_ref[...] = delta
        nm_ref[...] = nm
        nv_ref[...] = nv

    blk = pl.BlockSpec((tr, C), lambda i: (i, 0))
    part_specs = [pl.BlockSpec((None, tr, C), functools.partial(lambda s, i: (s, i, 0), slot)) for _, slot in parts]
    return pl.pallas_call(
        body, name=name, grid=(R // tr,), in_specs=[blk, blk, blk] + part_specs,
        out_specs=[blk] * 4, out_shape=[jax.ShapeDtypeStruct((R, C), F32)] * 4,
        compiler_params=_params("parallel"))(w, m, v, *[a for a, _ in parts])


def _pair_sum(pa, qa):
    n, R, C = qa.shape
    tr = _pick(R, 256)

    def body(p_ref, q_ref, r_ref):
        r_ref[...] = (p_ref[...].astype(F32) + q_ref[...].astype(F32)).astype(BF16)

    return pl.pallas_call(
        body, name="pair_sum", grid=(n, R // tr),
        in_specs=[pl.BlockSpec((None, tr, C), lambda j, i: (2 * j, i, 0)), pl.BlockSpec((None, tr, C), lambda j, i: (j, i, 0))],
        out_specs=pl.BlockSpec((None, tr, C), lambda j, i: (j, i, 0)),
        out_shape=jax.ShapeDtypeStruct((n, R, C), BF16), compiler_params=_params("parallel", "parallel"))(pa, qa)


def _adamw_small(total, sinks, lb_logits, rnn_norm, pre_norm, post_norm, moments, D):
    params = [sinks, lb_logits, rnn_norm, pre_norm, post_norm]

    def body(tot_ref, *refs):
        p_refs, m_refs, v_refs = refs[0:5], refs[5:10], refs[10:15]
        loss_ref, outs = refs[15], refs[16:]
        tot = tot_ref[...]
        l0, l1 = p_refs[1][0:1, :], p_refs[1][1:2, :]
        mx = jnp.maximum(l0, l1)
        e0, e1 = jnp.exp(l0 - mx), jnp.exp(l1 - mx)
        p0, p1 = e0 / (e0 + e1), e1 / (e0 + e1)
        dlb = tot[0:1, :]
        grads = [tot[5:6, 0:LANES], jnp.concatenate([dlb * p0 * (1.0 - p0), -dlb * p0 * p1], axis=0),
                 tot[1:2, :], tot[2:3, :], tot[3:4, :]]
        loss_ref[...] = 0.5 / D * jnp.sum(tot[4:5, :], axis=-1, keepdims=True)
        for i, g in enumerate(grads):
            delta, nm, nv = _adamw(p_refs[i][...], g, m_refs[i][...], v_refs[i][...])
            outs[4 * i][...] = g
            outs[4 * i + 1][...] = delta
            outs[4 * i + 2][...] = nm
            outs[4 * i + 3][...] = nv

    out_shape = [jax.ShapeDtypeStruct((1, 1), F32)]
    for p in params:
        out_shape += [jax.ShapeDtypeStruct(p.shape, F32)] * 4
    return pl.pallas_call(body, name="adamw_small", out_shape=out_shape)(total, *params, *moments)


SIBLING = 1


def _me():
    return lax.axis_index("x"), lax.axis_index("y"), lax.axis_index("c")


def _flat(px, py, pc):
    return 4 * px + 2 * py + pc


def _role_peer(role, x, y, c):
    if role < 2:
        return (x, y, (1 - c) if role else c)
    j, other = (role - 2) // 2, (role - 2) % 2
    px = (1 - x) if j in (0, 2) else x
    py = (1 - y) if j in (1, 2) else y
    return (px, py, (1 - c) if other else c)


def _role_ids():
    x, y, c = _me()
    return jnp.stack([_flat(*_role_peer(r, x, y, c)) for r in range(N_DEV)]).astype(jnp.int32)


def _comm_call(name, bufs, waits=(), starts=(), after=()):
    n_buf, n_wait, n_start, n_after = len(bufs), len(waits), len(starts), len(after)

    def body(*refs):
        buf_refs = refs[:n_buf]
        sem_refs = refs[n_buf:n_buf + 2 * n_wait]
        outs = refs[n_buf + 2 * n_wait + n_after:]
        new_sems, token = outs[:2 * n_start], outs[-1]
        x, y, c = _me()
        def block(ref, slot, rows):
            return ref.at[slot] if rows is None else ref.at[slot, pl.ds(rows[0], rows[1])]

        for w, (_, kind, like, *rows) in enumerate(waits):
            shape_ref = block(buf_refs[like], 0, (0, rows[0][1]) if rows else None)
            cp = pltpu.make_async_remote_copy(
                src_ref=shape_ref, dst_ref=shape_ref, send_sem=sem_refs[2 * w], recv_sem=sem_refs[2 * w + 1],
                device_id=(x, y, c), device_id_type=MESH)
            if kind == "send":
                cp.wait_send()
            else:
                cp.wait_recv()
        for s, (sb, ss, db, ds, role, *rows) in enumerate(starts):
            rows = rows[0] if rows else None
            pltpu.make_async_remote_copy(
                src_ref=block(buf_refs[sb], ss, rows), dst_ref=block(buf_refs[db], ds, rows), send_sem=new_sems[2 * s],
                recv_sem=new_sems[2 * s + 1], device_id=_role_peer(role, x, y, c), device_id_type=MESH).start()
        token[...] = jnp.zeros_like(token)

    sems = [s for flight, *_ in waits for s in flight]
    out = pl.pallas_call(
        body, name=name,
        out_shape=tuple([pltpu.SemaphoreType.DMA(())] * (2 * n_start) + [pltpu.HBM(b.shape, b.dtype) for b in bufs]
                        + [jax.ShapeDtypeStruct((8, LANES), F32)]),
        in_specs=tuple([HBM_SPEC] * n_buf + [SEM_SPEC] * (2 * n_wait) + [ANY_SPEC] * n_after),
        out_specs=tuple([SEM_SPEC] * (2 * n_start) + [HBM_SPEC] * n_buf + [pl.BlockSpec(memory_space=pltpu.VMEM)]),
        input_output_aliases={i: 2 * n_start + i for i in range(n_buf)},
        compiler_params=pltpu.CompilerParams(has_side_effects=EFFECT),
    )(*[pltpu.with_memory_space_constraint(b, pltpu.HBM) for b in bufs], *sems, *after)
    flights = [(out[2 * s], out[2 * s + 1]) for s in range(n_start)]
    return list(out[2 * n_start:2 * n_start + n_buf]), flights, out[-1]


def _landing(shape):
    return lax.empty(shape, BF16)


def _both(flights, like):
    return [(f, kind, like) for f in flights for kind in ("send", "recv")]


def _allreduce_small(rows, D):
    n_rows = len(rows)

    def body(*refs):
        in_refs, out_ref, all_ref = refs[:n_rows], refs[n_rows], refs[n_rows + 1]
        send_sems, recv_sems = refs[n_rows + 2], refs[n_rows + 3]
        x, y, c = _me()
        mine = all_ref.at[_flat(x, y, c)]
        mine[...] = jnp.zeros((8, D), F32)
        for i, r in enumerate(in_refs):
            mine[i:i + 1, :] = r[...]
        copies = []
        for k in range(N_DEV - 1):
            copies.append(pltpu.make_async_remote_copy(
                src_ref=mine, dst_ref=mine, send_sem=send_sems.at[k], recv_sem=recv_sems.at[k],
                device_id=_role_peer(k + 1, x, y, c), device_id_type=MESH))
        for cp in copies:
            cp.start()
        for cp in copies:
            cp.wait_recv()
        for cp in copies:
            cp.wait_send()
        tot = all_ref[0]
        for d in range(1, N_DEV):
            tot = tot + all_ref[d]
        out_ref[...] = tot

    vm = pl.BlockSpec(memory_space=pltpu.VMEM)
    return pl.pallas_call(
        body, name="allreduce_small", in_specs=[vm] * n_rows, out_specs=vm,
        out_shape=jax.ShapeDtypeStruct((8, D), F32),
        scratch_shapes=[pltpu.VMEM((N_DEV, 8, D), F32), pltpu.SemaphoreType.DMA((7,)), pltpu.SemaphoreType.DMA((7,))],
    )(*rows)


def _step(ids, x, target, g_in, g_out, sinks, lb_logits, rnn_norm, pre_norm, post_norm, dist):
    T, D = x.shape
    n_roles, _, wd = g_in.shape
    ro = g_out.shape[1]
    assert n_roles % 2 == 0 and (not dist or n_roles == N_DEV)
    n_chips = n_roles // 2 - 1

    if dist:
        (g_in,), fl_in, _ = _comm_call(
            "gather_in_start", [g_in], starts=[(0, 0, 0, SIBLING, SIBLING)] + [(0, 0, 0, 2 + 2 * j, 2 + 2 * j) for j in range(2)])
    h, h_t = _prenorm(x, pre_norm)
    if dist:
        (g_in,), _, _ = _comm_call("gather_in_sibling", [g_in], waits=[(fl_in[0], "recv", 0)], after=(h,))
    proj = _proj_in(ids, h, g_in, 0, 1, None, "proj_in_0")
    fl_fwd, fl_half = [], []

    def pass_on(i, g):
        slot = 2 + 2 * i
        starts = [(0, slot, 0, slot + 1, SIBLING)]
        if i < 2:
            waits = [(fl_in[1 + i], "recv", 0)]
            starts.append((0, slot, 0, 6, 4 - 2 * i, (i * (D // 2), D // 2)))
        else:
            waits = [(f, "recv", 0, (0, D // 2)) for f in fl_half]
        (g,), fl, _ = _comm_call(f"gather_in_pass_{i}", [g], waits=waits, starts=starts, after=(proj,))
        fl_fwd.append(fl[0])
        fl_half.extend(fl[1:])
        return g

    if not dist:
        for j in range(n_chips):
            proj = _proj_in(ids, h, g_in, 2 + 2 * j, 1, proj, f"proj_in_{j + 1}")
    else:
        g_in = pass_on(1, pass_on(0, g_in))
        proj = _proj_in(ids, h, g_in, 2, 2, proj, "proj_in_1")
        (g_in,), _, _ = _comm_call(
            "gather_in_passed_01", [g_in], waits=[(fl_fwd[0], "recv", 0), (fl_fwd[1], "recv", 0)], after=(proj,))
        g_in = pass_on(2, g_in)
        proj = _proj_in(ids, h, g_in, 3, 2, proj, "proj_in_2")
        (g_in,), _, _ = _comm_call("gather_in_passed_2", [g_in], waits=[(fl_fwd[2], "recv", 0)], after=(proj,))
        (g_in, g_out), fl_out, token = _comm_call(
            "gather_out_start", [g_in, g_out],
            starts=[(1, 0, 1, SIBLING, SIBLING)] + [(1, 0, 1, 2 + 2 * i, 2 + 2 * i) for i in range(3)], after=(proj,))
        proj = _proj_in(ids, h, g_in, 6, 1, proj, "proj_in_3", (token,))
    mixed, mixed_t = _attn_fwd(proj, sinks, D)
    if dist:
        (g_in, g_out), fl_out_fwd, _ = _comm_call(
            "gather_out_pass", [g_in, g_out],
            waits=[(f, "send", 0) for f in fl_in + fl_fwd] + [(f, "send", 0, (0, D // 2)) for f in fl_half]
            + [(fl_out[1 + i], "recv", 1) for i in range(3)],
            starts=[(1, 2 + 2 * i, 1, 3 + 2 * i, SIBLING) for i in range(3)], after=(mixed,))
    mixed, mixed_t, o_raw, states = _rnn_fwd(proj, lb_logits, rnn_norm, mixed, mixed_t, D)
    if dist:
        (g_out,), _, _ = _comm_call(
            "gather_out_done", [g_out],
            waits=[(fl_out[0], "recv", 0)] + [(f, "recv", 0) for f in fl_out_fwd]
            + [(f, "send", 0) for f in fl_out + fl_out_fwd], after=(states,))
    y = _proj_out(ids, mixed, g_out)
    dy, dout, g_post, sq_err = _loss_head(y, x, target, post_norm)

    dmix = _dmixed(ids, dy, g_out)
    p_out, after = _dw_out(ids, mixed_t, dy, n_roles), ()
    if dist:
        (p_out, l_out), fl_so, token = _comm_call(
            "scatter_out_start", [p_out, _landing((n_roles - 1, ro, D))],
            starts=[(0, r, 1, r - 1, r) for r in range(1, n_roles)])
        after = (token,)
    d_aq, d_ak, d_av, d_ag, d_sink = _attn_bwd(proj, sinks, dmix, D, after)
    d_rq, d_rf, d_ri, d_rg, d_lb, g_rnn = _rnn_bwd(proj, lb_logits, rnn_norm, o_raw, states, dmix, D)
    dproj = jnp.concatenate([d_aq, d_ak, d_av, d_ag, d_rq, d_rf, d_ri, d_rg], axis=1)

    p_far = _dw_in(ids, h_t, dproj, n_roles, 2, n_roles - 2, "dw_in_far")
    if dist:
        (p_far, q_far), fl_pair, token = _comm_call(
            "scatter_in_pair_start", [p_far, _landing((3, D, wd))],
            starts=[(0, 1 + 2 * j, 1, j, SIBLING) for j in range(3)])
        p_near = _dw_in(ids, h_t, dproj, n_roles, 0, 2, "dw_in_near", after=(token,))
        (p_far, q_far), _, _ = _comm_call("scatter_in_pair_wait", [p_far, q_far], waits=_both(fl_pair, 1), after=(p_near,))
        chip_sum = _pair_sum(p_far, q_far)
        (chip_sum, z_far), fl_chip, token = _comm_call(
            "scatter_in_chip_start", [chip_sum, _landing((3, D, wd))],
            starts=[(0, j, 1, j, 2 + 2 * j) for j in range(3)])
        (p_near, q_near), fl_sib, token = _comm_call(
            "scatter_in_sibling_start", [p_near, _landing((1, D, wd))], starts=[(0, 1, 1, 0, SIBLING)], after=(token,))
        after = (token,)
    else:
        p_near, after = _dw_in(ids, h_t, dproj, n_roles, 0, 2, "dw_in_near"), ()
    dh = _dh(ids, dproj, g_in, after)
    grad_x, g_pre = _prenorm_bwd(x, dh, dout, pre_norm)
    n_q = D // ATTN_HEAD
    sink_row = jnp.pad(d_sink[:, 0, :2 * GQA].reshape(1, n_q), ((0, 0), (0, D - n_q)))
    rows = [d_lb, g_rnn, g_pre, g_post, sq_err, sink_row]
    if not dist:
        return grad_x, [(p_near, r) for r in range(2)] + [(p_far, r) for r in range(n_roles - 2)], \
            [(p_out, r) for r in range(n_roles)], rows

    (p_out, l_out), _, _ = _comm_call("scatter_out_wait", [p_out, l_out], waits=_both(fl_so, 1), after=(grad_x,))
    (chip_sum, z_far, p_near, q_near), _, _ = _comm_call(
        "scatter_in_wait", [chip_sum, z_far, p_near, q_near], waits=_both(fl_chip, 1) + _both(fl_sib, 3), after=(p_out,))
    parts_in = [(p_near, 0), (q_near, 0)] + [(z_far, j) for j in range(3)]
    parts_out = [(p_out, 0)] + [(l_out, k) for k in range(n_roles - 1)]
    return grad_x, parts_in, parts_out, rows


def kernel(x, w_in, attn_sinks, lb_logits, rnn_norm, w_out, pre_norm, post_norm, loss_target, m_w_in, m_attn_sinks, m_lb_logits, m_rnn_norm, m_w_out, m_pre_norm, m_post_norm, v_w_in, v_attn_sinks, v_lb_logits, v_rnn_norm, v_w_out, v_pre_norm, v_post_norm):
    _, T, D = x.shape
    ro = w_out.shape[1]
    n_q = attn_sinks.shape[1]
    assert lb_logits.shape[0] == 2 and n_q == D // ATTN_HEAD and n_q <= LANES

    grad_x, parts_in, parts_out, small_rows = _step(
        _role_ids(), x[0], loss_target[0], _cast_slot0(w_in[0], N_DEV, "cast_w_in"),
        _cast_slot0(w_out[0], N_DEV, "cast_w_out"), attn_sinks, lb_logits, rnn_norm, pre_norm, post_norm, True)

    g_wo, d_wo, nm_wo, nv_wo = _adamw_big(w_out[0], m_w_out[0], v_w_out[0], parts_out, _pick(ro, 64), "adamw_w_out")
    g_wi, d_wi, nm_wi, nv_wi = _adamw_big(w_in[0], m_w_in[0], v_w_in[0], parts_in, _pick(D, 128), "adamw_w_in")

    total = _allreduce_small(small_rows, D)
    pad = lambda a: jnp.pad(a, ((0, 0), (0, LANES - n_q)))
    moments = [pad(m_attn_sinks), m_lb_logits, m_rnn_norm, m_pre_norm, m_post_norm,
               pad(v_attn_sinks), v_lb_logits, v_rnn_norm, v_pre_norm, v_post_norm]
    res = _adamw_small(total, pad(attn_sinks), lb_logits, rnn_norm, pre_norm, post_norm, moments, D)
    loss = res[0][0, 0]
    small = [[res[1 + 4 * i + j] for i in range(5)] for j in range(4)]
    for j in range(4):
        small[j][0] = small[j][0][:, :n_q]

    def assemble(j, wi, wo):
        s = small[j]
        return [wi[None], s[0], s[1], s[2], wo[None], s[3], s[4]]

    return (loss, grad_x[None], *assemble(0, g_wi, g_wo), *assemble(1, d_wi, d_wo),
            *assemble(2, nm_wi, nm_wo), *assemble(3, nv_wi, nv_wo))
```

```python
import functools

import jax
import jax.numpy as jnp
from jax import lax
from jax.experimental import pallas as pl
from jax.experimental.pallas import tpu as pltpu

F32 = jnp.float32
BF16 = jnp.bfloat16

ATTN_HEAD = 64
GQA = 8
WINDOW = 128
RNN_HEAD = 128
CHUNK = 64
NORM_EPS = 1e-6
LANES = 128
N_DEV = 8

ADAM_LR = 0.001
ADAM_B1 = 0.9
ADAM_B2 = 0.999
ADAM_EPS = 1e-08
ADAM_WD = 0.01
ADAM_STEP = 10

VMEM_LIMIT = 56 * 1024 * 1024
MESH = pl.DeviceIdType.MESH
ANY_SPEC = pl.BlockSpec(memory_space=pl.ANY)
HBM_SPEC = pl.BlockSpec(memory_space=pltpu.HBM)
SEM_SPEC = pl.BlockSpec(memory_space=pltpu.SEMAPHORE)
EFFECT = pltpu.SideEffectType.DATAFLOW_SIDE_EFFECTING

NT_DIMS = (((1,), (1,)), ((), ()))
NN_DIMS = (((1,), (0,)), ((), ()))
TN_DIMS = (((0,), (0,)), ((), ()))


def _params(*sem):
    return pltpu.CompilerParams(dimension_semantics=sem, vmem_limit_bytes=VMEM_LIMIT)


def _dot(a, b, dims):
    return lax.dot_general(a, b, dims, preferred_element_type=F32)


def _sigmoid(v):
    return 1.0 / (1.0 + jnp.exp(-v))


def _pick(n, pref):
    t = min(n, pref)
    assert n % t == 0, (n, pref)
    return t


MXU_COLS = 256
WIDE_TILE = 5 * MXU_COLS


def _col_tiles(wd):
    wide = wd // WIDE_TILE
    rest = wd - wide * WIDE_TILE
    assert rest % LANES == 0 and (rest == 0 or (wide * WIDE_TILE) % rest == 0), wd
    return [(WIDE_TILE, 0, wide)] * (wide > 0) + [(rest, wide * WIDE_TILE, 1)] * (rest > 0)


def _matmul(ids, a, b, *, grid, a_spec, b_spec, o_spec, out_shape, trans_b, name, after=(), prev=None):
    nk = grid[2]
    dims = NT_DIMS if trans_b else NN_DIMS
    n_skip = len(after) + (prev is not None)

    def body(ids_ref, a_ref, b_ref, *rest):
        del ids_ref
        o_ref, scratch = rest[n_skip], rest[n_skip + 1:]
        prod = _dot(a_ref[...], b_ref[...], dims)
        if nk == 1:
            o_ref[...] = prod.astype(o_ref.dtype)
        else:
            acc_ref, = scratch
            k = pl.program_id(2)

            @pl.when(k == 0)
            def _():
                acc_ref[...] = prod

            @pl.when(k > 0)
            def _():
                acc_ref[...] += prod

            @pl.when(k == nk - 1)
            def _():
                o_ref[...] = acc_ref[...].astype(o_ref.dtype)

    scratch = [] if nk == 1 else [pltpu.VMEM(tuple(d for d in o_spec.block_shape if d is not None), F32)]
    extra = list(after) + ([prev] if prev is not None else [])
    aliases = {3 + len(after): 0} if prev is not None else {}
    return pl.pallas_call(
        body, name=name, out_shape=out_shape, input_output_aliases=aliases,
        grid_spec=pltpu.PrefetchScalarGridSpec(
            num_scalar_prefetch=1, grid=grid, in_specs=[a_spec, b_spec] + [ANY_SPEC] * len(extra),
            out_specs=o_spec, scratch_shapes=scratch),
        compiler_params=_params("parallel", "parallel", "arbitrary"),
    )(ids, a, b, *extra)


def _proj_in(ids, h, w_roles, r0, step, prev, name, after=()):
    T, D = h.shape
    n_roles, _, wd = w_roles.shape
    tm = _pick(T, 1024)
    for t, (tn, first, count) in enumerate(_col_tiles(wd)):
        prev = _matmul(
            ids, h, w_roles, grid=(T // tm, 2 * count, 1),
            a_spec=pl.BlockSpec((tm, D), lambda i, j, k, ids: (i, 0)),
            b_spec=pl.BlockSpec((None, D, tn), functools.partial(
                lambda tn, first, count, i, j, k, ids: (r0 + step * (j // count), 0, first // tn + j % count),
                tn, first, count)),
            o_spec=pl.BlockSpec((pl.Element(tm), pl.Element(tn)), functools.partial(
                lambda tn, first, count, i, j, k, ids: (
                    i * tm, pl.multiple_of(ids[r0 + step * (j // count)] * wd + first + (j % count) * tn, LANES)),
                tn, first, count)),
            out_shape=jax.ShapeDtypeStruct((T, n_roles * wd), F32), trans_b=False, name=f"{name}_{t}",
            after=after, prev=prev)
    return prev


def _proj_out(ids, mixed, wo_roles):
    T, E = mixed.shape
    n_roles, R, D = wo_roles.shape
    tm, tn = _pick(T, 512), _pick(D, 512)

    def body(ids_ref, a_ref, b_ref, o_ref, a_roles):
        @pl.when(pl.program_id(1) == 0)
        def _():
            for k in range(n_roles):
                col = pl.multiple_of(ids_ref[k] * R, R)
                a_roles[:, k * R:(k + 1) * R] = a_ref[:, pl.ds(col, R)]

        o_ref[...] = _dot(a_roles[...], b_ref[...].reshape(E, tn), NN_DIMS)

    return pl.pallas_call(
        body, name="proj_out", out_shape=jax.ShapeDtypeStruct((T, D), F32),
        grid_spec=pltpu.PrefetchScalarGridSpec(
            num_scalar_prefetch=1, grid=(T // tm, D // tn),
            in_specs=[pl.BlockSpec((tm, E), lambda i, j, ids: (i, 0)),
                      pl.BlockSpec((n_roles, R, tn), lambda i, j, ids: (0, 0, j))],
            out_specs=pl.BlockSpec((tm, tn), lambda i, j, ids: (i, j)),
            scratch_shapes=[pltpu.VMEM((tm, E), BF16)]),
        compiler_params=_params("parallel", "arbitrary"))(ids, mixed, wo_roles)


def _dmixed(ids, dy, wo_roles):
    T, D = dy.shape
    n_roles, R, _ = wo_roles.shape
    tm = _pick(T, 1024)
    return _matmul(
        ids, dy, wo_roles, grid=(T // tm, n_roles, 1),
        a_spec=pl.BlockSpec((tm, D), lambda i, j, k, ids: (i, 0)),
        b_spec=pl.BlockSpec((None, R, D), lambda i, j, k, ids: (j, 0, 0)),
        o_spec=pl.BlockSpec((tm, R), lambda i, j, k, ids: (i, ids[j])),
        out_shape=jax.ShapeDtypeStruct((T, n_roles * R), F32), trans_b=True, name="dmixed")


def _dw_out(ids, mixed_t, dy, n_roles):
    E, T = mixed_t.shape
    D = dy.shape[1]
    R = E // n_roles
    tn = _pick(D, 512)
    return _matmul(
        ids, mixed_t, dy, grid=(n_roles, D // tn, 1),
        a_spec=pl.BlockSpec((R, T), lambda i, j, k, ids: (ids[i], 0)),
        b_spec=pl.BlockSpec((T, tn), lambda i, j, k, ids: (0, j)),
        o_spec=pl.BlockSpec((None, R, tn), lambda i, j, k, ids: (i, 0, j)),
        out_shape=jax.ShapeDtypeStruct((n_roles, R, D), BF16), trans_b=False, name="dw_out")


def _dw_in(ids, h_t, dproj, n_roles, r0, nr, name, after=()):
    D, T = h_t.shape
    wd = dproj.shape[1] // n_roles
    tm, out = _pick(D, 1024), None
    for t, (tn, first, count) in enumerate(_col_tiles(wd)):
        out = _matmul(
            ids, h_t, dproj, grid=(D // tm, nr * count, 1),
            a_spec=pl.BlockSpec((tm, T), lambda i, j, k, ids: (i, 0)),
            b_spec=pl.BlockSpec((pl.Element(T), pl.Element(tn)), functools.partial(
                lambda tn, first, count, i, j, k, ids: (
                    0, pl.multiple_of(ids[r0 + j // count] * wd + first + (j % count) * tn, LANES)),
                tn, first, count)),
            o_spec=pl.BlockSpec((None, tm, tn), functools.partial(
                lambda tn, first, count, i, j, k, ids: (j // count, i, first // tn + j % count), tn, first, count)),
            out_shape=jax.ShapeDtypeStruct((nr, D, wd), BF16), trans_b=False, name=f"{name}_{t}", after=after, prev=out)
    return out


def _dh(ids, dproj, w_roles, after=()):
    T = dproj.shape[0]
    n_roles, D, wd = w_roles.shape
    tm, tn = _pick(T, 1024), _pick(D, 1024)
    return _matmul(
        ids, dproj, w_roles, grid=(T // tm, D // tn, n_roles),
        a_spec=pl.BlockSpec((tm, wd), lambda i, j, k, ids: (i, ids[k])),
        b_spec=pl.BlockSpec((None, tn, wd), lambda i, j, k, ids: (k, j, 0)),
        o_spec=pl.BlockSpec((tm, tn), lambda i, j, k, ids: (i, j)),
        out_shape=jax.ShapeDtypeStruct((T, D), F32), trans_b=True, name="dh", after=after)


def _cast_slot0(w, n_roles, name):
    R, C = w.shape
    tr = _pick(R, 256)

    def body(w_ref, o_ref):
        o_ref[...] = w_ref[...].astype(BF16)

    return pl.pallas_call(
        body, name=name, grid=(R // tr,), in_specs=[pl.BlockSpec((tr, C), lambda i: (i, 0))],
        out_specs=pl.BlockSpec((None, tr, C), lambda i: (0, i, 0)),
        out_shape=jax.ShapeDtypeStruct((n_roles, R, C), BF16), compiler_params=_params("parallel"))(w)


def _prenorm(x, gain):
    T, D = x.shape
    tm = _pick(T, 256)

    def body(x_ref, g_ref, h_ref, ht_ref):
        xv = x_ref[...]
        r = lax.rsqrt(jnp.mean(xv * xv, axis=-1, keepdims=True) + NORM_EPS)
        h = xv * r * g_ref[...]
        h_ref[...] = h.astype(BF16)
        ht_ref[...] = h.T.astype(BF16)

    return pl.pallas_call(
        body, name="prenorm", grid=(T // tm,),
        in_specs=[pl.BlockSpec((tm, D), lambda i: (i, 0)), pl.BlockSpec((1, D), lambda i: (0, 0))],
        out_specs=[pl.BlockSpec((tm, D), lambda i: (i, 0)), pl.BlockSpec((D, tm), lambda i: (0, i))],
        out_shape=[jax.ShapeDtypeStruct((T, D), BF16), jax.ShapeDtypeStruct((D, T), BF16)],
        compiler_params=_params("parallel"))(x, gain)


def _norm_bwd(u, yn, r):
    return r * (u - yn * jnp.mean(u * yn, axis=-1, keepdims=True))


def _loss_head(y, x, target, gain):
    T, D = y.shape
    tm = _pick(T, 256)

    def body(y_ref, x_ref, t_ref, g_ref, dy_ref, dout_ref, gpost_ref, sq_ref):
        yv = y_ref[...]
        g = g_ref[...]
        r = lax.rsqrt(jnp.mean(yv * yv, axis=-1, keepdims=True) + NORM_EPS)
        yn = yv * r
        err = x_ref[...] + yn * g - t_ref[...]
        dout = err * (1.0 / D)
        dy_ref[...] = _norm_bwd(dout * g, yn, r).astype(BF16)
        dout_ref[...] = dout

        @pl.when(pl.program_id(0) == 0)
        def _():
            gpost_ref[...] = jnp.zeros_like(gpost_ref)
            sq_ref[...] = jnp.zeros_like(sq_ref)

        gpost_ref[...] += jnp.sum(dout * yn, axis=0, keepdims=True)
        sq_ref[...] += jnp.sum(err * err, axis=0, keepdims=True)

    row = pl.BlockSpec((tm, D), lambda i: (i, 0))
    vec = pl.BlockSpec((1, D), lambda i: (0, 0))
    return pl.pallas_call(
        body, name="loss_head", grid=(T // tm,), in_specs=[row, row, row, vec], out_specs=[row, row, vec, vec],
        out_shape=[jax.ShapeDtypeStruct((T, D), BF16), jax.ShapeDtypeStruct((T, D), F32),
                   jax.ShapeDtypeStruct((1, D), F32), jax.ShapeDtypeStruct((1, D), F32)],
        compiler_params=_params("arbitrary"))(y, x, target, gain)


def _prenorm_bwd(x, dh, dout, gain):
    T, D = x.shape
    tm = _pick(T, 256)

    def body(x_ref, dh_ref, dout_ref, g_ref, gx_ref, gpre_ref):
        xv = x_ref[...]
        dhv = dh_ref[...]
        r = lax.rsqrt(jnp.mean(xv * xv, axis=-1, keepdims=True) + NORM_EPS)
        xn = xv * r
        gx_ref[...] = dout_ref[...] + _norm_bwd(dhv * g_ref[...], xn, r)

        @pl.when(pl.program_id(0) == 0)
        def _():
            gpre_ref[...] = jnp.zeros_like(gpre_ref)

        gpre_ref[...] += jnp.sum(dhv * xn, axis=0, keepdims=True)

    row = pl.BlockSpec((tm, D), lambda i: (i, 0))
    vec = pl.BlockSpec((1, D), lambda i: (0, 0))
    return pl.pallas_call(
        body, name="prenorm_bwd", grid=(T // tm,), in_specs=[row, row, row, vec], out_specs=[row, vec],
        out_shape=[jax.ShapeDtypeStruct((T, D), F32), jax.ShapeDtypeStruct((1, D), F32)],
        compiler_params=_params("arbitrary"))(x, dh, dout, gain)


def _attn_masks(n):
    row = lax.broadcasted_iota(jnp.int32, (2 * WINDOW, 2 * WINDOW), 0) % WINDOW
    col = lax.broadcasted_iota(jnp.int32, (2 * WINDOW, 2 * WINDOW), 1)
    valid = (col > row) & (col <= row + WINDOW) & ((n > 0) | (col >= WINDOW))
    low = lax.broadcasted_iota(jnp.int32, (1, LANES), 1) < ATTN_HEAD
    top = lax.broadcasted_iota(jnp.int32, (2 * WINDOW, 1), 0) < WINDOW
    return valid, low, top


def _dup_half(pair, keep):
    return jnp.where(keep, pair, pltpu.roll(pair, ATTN_HEAD, 1))


def _fold_half(v):
    return v + pltpu.roll(v, ATTN_HEAD, 1)


def _attn_scores(qpair, k2, low):
    q2 = jnp.concatenate([jnp.where(low, qpair, 0.0), jnp.where(low, 0.0, qpair)], axis=0).astype(BF16)
    return q2, _dot(q2, k2, NT_DIMS)


def _attn_softmax(raw, sink_lo, sink_hi, valid, top):
    s = jnp.where(valid, raw * (ATTN_HEAD ** -0.5), -jnp.inf)
    sink = jnp.where(top, sink_lo, sink_hi)
    m = jnp.maximum(jnp.max(s, axis=-1, keepdims=True), sink)
    p = jnp.exp(s - m)
    psink = jnp.exp(sink - m)
    inv = 1.0 / (jnp.sum(p, axis=-1, keepdims=True) + psink)
    return p * inv, psink * inv


def _attn_specs(D):
    kb = D // LANES
    vb = kb + D // (8 * LANES)
    gb = (D + D // 4) // 512
    wide = lambda off: [pl.BlockSpec((WINDOW, 512), functools.partial(lambda o, e, jp, n: (n, o + 2 * jp + e), off, e))
                        for e in (0, 1)]
    cur = lambda off: pl.BlockSpec((WINDOW, LANES), functools.partial(lambda o, jp, n: (n, o + jp), off))
    prev = lambda off: pl.BlockSpec((WINDOW, LANES),
                                    functools.partial(lambda o, jp, n: (jnp.maximum(n - 1, 0), o + jp), off))
    return wide(0) + [cur(kb), prev(kb), cur(vb), prev(vb)] + wide(gb)


def _attn_fwd(proj, sinks, D):
    T = proj.shape[0]
    nb, njp = T // WINDOW, D // 1024

    def body(sink_ref, qlo_ref, qhi_ref, kc_ref, kp_ref, vc_ref, vp_ref, glo_ref, ghi_ref, mix_ref, mixt_ref):
        jp, n = pl.program_id(0), pl.program_id(1)
        valid, low, top = _attn_masks(n)
        kk = jnp.concatenate([kp_ref[...], kc_ref[...]], axis=0)
        vv = jnp.concatenate([vp_ref[...], vc_ref[...]], axis=0)
        pairs = range(4)
        for hj, (q_ref, g_ref) in enumerate(((qlo_ref, glo_ref), (qhi_ref, ghi_ref))):
            keep = low if hj == 0 else jnp.logical_not(low)
            k2 = _dup_half(kk, keep).astype(BF16)
            v2 = _dup_half(vv, keep).astype(BF16)
            cols = [slice(LANES * p, LANES * (p + 1)) for p in pairs]
            scores = [_attn_scores(q_ref[:, cols[p]], k2, low)[1] for p in pairs]
            probs = []
            for p in pairs:
                head = (2 * jp + hj) * GQA + 2 * p
                probs.append(_attn_softmax(scores[p], sink_ref[0, head], sink_ref[0, head + 1], valid, top)[0])
            o2s = [_dot(probs[p].astype(BF16), v2, NN_DIMS) for p in pairs]
            for p in pairs:
                opair = jnp.where(low, o2s[p][:WINDOW], o2s[p][WINDOW:])
                g = g_ref[:, cols[p]]
                out = opair * (g * _sigmoid(g))
                oc = slice(512 * hj + LANES * p, 512 * hj + LANES * (p + 1))
                mix_ref[:, oc] = out.astype(BF16)
                mixt_ref[oc, :] = out.T.astype(BF16)

    return pl.pallas_call(
        body, name="attn_fwd", grid=(njp, nb),
        in_specs=[pl.BlockSpec(memory_space=pltpu.SMEM)] + _attn_specs(D),
        out_specs=[pl.BlockSpec((WINDOW, 1024), lambda jp, n: (n, jp)),
                   pl.BlockSpec((1024, WINDOW), lambda jp, n: (jp, n))],
        out_shape=[jax.ShapeDtypeStruct((T, 2 * D), BF16), jax.ShapeDtypeStruct((2 * D, T), BF16)],
        compiler_params=_params("parallel", "parallel"))(sinks, *([proj] * 8))


def _attn_bwd(proj, sinks, dmix, D, after=()):
    T = proj.shape[0]
    nb, njp = T // WINDOW, D // 1024
    n_after = len(after)

    def body(sink_ref, qlo_ref, qhi_ref, kc_ref, kp_ref, vc_ref, vp_ref, glo_ref, ghi_ref, dmix_ref, *rest):
        dq_ref, dk_ref, dv_ref, dg_ref, dsink_ref, kcarry_ref, vcarry_ref = rest[n_after:]
        jp, step = pl.program_id(0), pl.program_id(1)
        n = nb - 1 - step
        valid, low, top = _attn_masks(n)
        lane = lax.broadcasted_iota(jnp.int32, (1, LANES), 1)
        kk = jnp.concatenate([kp_ref[...], kc_ref[...]], axis=0)
        vv = jnp.concatenate([vp_ref[...], vc_ref[...]], axis=0)

        @pl.when(step == 0)
        def _():
            kcarry_ref[...] = jnp.zeros_like(kcarry_ref)
            vcarry_ref[...] = jnp.zeros_like(vcarry_ref)
            dsink_ref[...] = jnp.zeros_like(dsink_ref)

        dk_pair = jnp.zeros((2 * WINDOW, LANES), F32)
        dv_pair = jnp.zeros((2 * WINDOW, LANES), F32)
        dsink = jnp.zeros((1, LANES), F32)
        for hj, (q_ref, g_ref) in enumerate(((qlo_ref, glo_ref), (qhi_ref, ghi_ref))):
            keep = low if hj == 0 else jnp.logical_not(low)
            k2 = _dup_half(kk, keep).astype(BF16)
            v2 = _dup_half(vv, keep).astype(BF16)
            pairs = range(4)
            cols = [slice(LANES * p, LANES * (p + 1)) for p in pairs]
            ocs = [slice(512 * hj + LANES * p, 512 * hj + LANES * (p + 1)) for p in pairs]
            q2s, scores = zip(*[_attn_scores(q_ref[:, cols[p]], k2, low) for p in pairs])
            probs, psinks, do2s = [], [], []
            for p in pairs:
                head = (2 * jp + hj) * GQA + 2 * p
                pr, ps = _attn_softmax(scores[p], sink_ref[0, head], sink_ref[0, head + 1], valid, top)
                probs.append(pr)
                psinks.append(ps)
                g = g_ref[:, cols[p]]
                do = dmix_ref[:, ocs[p]] * (g * _sigmoid(g))
                do2s.append(jnp.concatenate([jnp.where(low, do, 0.0), jnp.where(low, 0.0, do)], axis=0).astype(BF16))
            pbs = [pr.astype(BF16) for pr in probs]
            o2s = [_dot(pbs[p], v2, NN_DIMS) for p in pairs]
            dps = [_dot(do2s[p], v2, NT_DIMS) for p in pairs]
            dss = []
            for p in pairs:
                opair = jnp.where(low, o2s[p][:WINDOW], o2s[p][WINDOW:])
                g = g_ref[:, cols[p]]
                sg = _sigmoid(g)
                dg_ref[:, ocs[p]] = (dmix_ref[:, ocs[p]] * opair * (sg * (1.0 + g * (1.0 - sg)))).astype(BF16)
                delta = jnp.sum(probs[p] * dps[p], axis=-1, keepdims=True)
                dss.append((probs[p] * (dps[p] - delta) * (ATTN_HEAD ** -0.5)).astype(BF16))
                ps = psinks[p] * delta
                local = hj * GQA + 2 * p
                dsink -= jnp.where(lane == local, jnp.sum(ps[:WINDOW], axis=0, keepdims=True), 0.0)
                dsink -= jnp.where(lane == local + 1, jnp.sum(ps[WINDOW:], axis=0, keepdims=True), 0.0)
            dq2s = [_dot(dss[p], k2, NN_DIMS) for p in pairs]
            dk_head = sum(_dot(dss[p], q2s[p], TN_DIMS) for p in pairs)
            dv_head = sum(_dot(pbs[p], do2s[p], TN_DIMS) for p in pairs)
            for p in pairs:
                dq_ref[:, ocs[p]] = jnp.where(low, dq2s[p][:WINDOW], dq2s[p][WINDOW:]).astype(BF16)
            dk_pair += jnp.where(keep, _fold_half(dk_head), 0.0)
            dv_pair += jnp.where(keep, _fold_half(dv_head), 0.0)
        dk_ref[...] = (dk_pair[WINDOW:] + kcarry_ref[...]).astype(BF16)
        dv_ref[...] = (dv_pair[WINDOW:] + vcarry_ref[...]).astype(BF16)
        kcarry_ref[...] = dk_pair[:WINDOW]
        vcarry_ref[...] = dv_pair[:WINDOW]
        dsink_ref[...] += dsink

    rev = lambda spec: pl.BlockSpec(spec.block_shape, functools.partial(
        lambda f, jp, s: f(jp, nb - 1 - s), spec.index_map))
    in_specs = [rev(s) for s in _attn_specs(D)]
    wide_out = pl.BlockSpec((WINDOW, 1024), lambda jp, s: (nb - 1 - s, jp))
    pair_out = pl.BlockSpec((WINDOW, LANES), lambda jp, s: (nb - 1 - s, jp))
    return pl.pallas_call(
        body, name="attn_bwd", grid=(njp, nb),
        in_specs=[pl.BlockSpec(memory_space=pltpu.SMEM)] + in_specs + [wide_out] + [ANY_SPEC] * n_after,
        out_specs=[wide_out, pair_out, pair_out, wide_out, pl.BlockSpec((None, 1, LANES), lambda jp, s: (jp, 0, 0))],
        out_shape=[jax.ShapeDtypeStruct((T, D), BF16), jax.ShapeDtypeStruct((T, D // 8), BF16),
                   jax.ShapeDtypeStruct((T, D // 8), BF16), jax.ShapeDtypeStruct((T, D), BF16),
                   jax.ShapeDtypeStruct((njp, 1, LANES), F32)],
        scratch_shapes=[pltpu.VMEM((WINDOW, LANES), F32), pltpu.VMEM((WINDOW, LANES), F32)],
        compiler_params=_params("parallel", "arbitrary"))(sinks, *([proj] * 8), dmix, *after)


RNN_TB = 512
RNN_HB = 8
RNN_WIDE = RNN_HB * RNN_HEAD


def _split3(v):
    a = v.astype(BF16)
    r = v - a.astype(F32)
    b = r.astype(BF16)
    c = (r - b.astype(F32)).astype(BF16)
    return a, b, c


def _tri_sum(tri, v):
    a, b, c = _split3(v)
    return _dot(tri, a, NN_DIMS) + _dot(tri, b, NN_DIMS) + _dot(tri, c, NN_DIMS)


def _lower_bound(lb_ref):
    l0, l1 = lb_ref[0:1, :], lb_ref[1:2, :]
    m = jnp.maximum(l0, l1)
    e0, e1 = jnp.exp(l0 - m), jnp.exp(l1 - m)
    return e0 / (e0 + e1)


def _rnn_gates(rq, rf, lb):
    sq = _sigmoid(rq)
    sf = _sigmoid(rf)
    f = lb + (1.0 - lb) * sf
    return sq, sf, f


def _rnn_decays(g, tri):
    return _rnn_factors(_tri_sum(tri, g))


def _rnn_factors(G):
    last = G[CHUNK - 1:CHUNK, :]
    mid = G[CHUNK // 2 - 1:CHUNK // 2, :]
    return G, jnp.exp(G), jnp.exp(G - mid), jnp.exp(mid - G), jnp.exp(last - G), jnp.exp(last)


def _chunk_masks():
    r = lax.broadcasted_iota(jnp.int32, (CHUNK, CHUNK), 0)
    c = lax.broadcasted_iota(jnp.int32, (CHUNK, CHUNK), 1)
    return r >= c, (r >= c).astype(BF16), (r <= c).astype(BF16)


def _rnn_specs(T, D, tb, rev):
    nt = T // tb
    base = (2 * D + D // 4) // LANES
    t_of = (lambda s: nt - 1 - s) if rev else (lambda s: s)
    assert base % RNN_HB == 0 and (D // LANES) % RNN_HB == 0
    cols = [pl.BlockSpec((tb, RNN_WIDE), functools.partial(lambda o, h, s: (t_of(s), o + h),
                                                            (base + i * (D // LANES)) // RNN_HB))
            for i in range(4)]
    return cols, t_of


def _rnn_fwd(proj, lb_logits, rnn_norm, mixed, mixed_t, D):
    T = proj.shape[0]
    tb = _pick(T, RNN_TB)
    nt, nh, cpb = T // tb, D // RNN_HEAD, tb // CHUNK
    cols, _ = _rnn_specs(T, D, tb, False)

    def body(rq_ref, rf_ref, ri_ref, rg_ref, lb_ref, gain_ref, mix_in, mixt_in,
             mix_ref, mixt_ref, o_ref, st_ref, state_ref):
        del mix_in, mixt_in
        causal, tri, _ = _chunk_masks()
        lb = _lower_bound(lb_ref)

        @pl.when(pl.program_id(1) == 0)
        def _():
            state_ref[...] = jnp.zeros_like(state_ref)

        def chunk(c, carry):
            rows = pl.ds(pl.multiple_of(c * CHUNK, CHUNK), CHUNK)
            heads = range(RNN_HB)
            lns = [slice(RNN_HEAD * hh, RNN_HEAD * (hh + 1)) for hh in heads]
            qs, ks, Gs = [], [], []
            for ln in lns:
                rq, rf = rq_ref[rows, ln], rf_ref[rows, ln]
                sq, _, f = _rnn_gates(rq, rf, lb[:, ln])
                qs.append(rq * sq)
                ks.append(1.0 - f)
                Gs.append(_tri_sum(tri, jnp.log(f)))
            atts, inters, vbs = [], [], []
            for hh, ln in enumerate(lns):
                _, eG, eq, ek, ekl, elast = _rnn_factors(Gs[hh])
                q, k = qs[hh], ks[hh]
                st = state_ref[hh]
                st_ref[hh, c] = st
                vb = ri_ref[rows, ln].astype(BF16)
                vbs.append(vb)
                atts.append(_dot((q * eq).astype(BF16), (k * ek).astype(BF16), NT_DIMS))
                inters.append(_dot((q * eG).astype(BF16), st.astype(BF16), NT_DIMS))
                state_ref[hh] = st * elast + _dot(vb, (k * ekl).astype(BF16), TN_DIMS)
            intras = [_dot(jnp.where(causal, atts[hh], 0.0).astype(BF16), vbs[hh], NN_DIMS) for hh in heads]
            for hh, ln in enumerate(lns):
                o = inters[hh] + intras[hh]
                o_ref[rows, ln] = o
                rg = rg_ref[rows, ln]
                r = lax.rsqrt(jnp.mean(o * o, axis=-1, keepdims=True) + NORM_EPS)
                out = (o * r * gain_ref[:, ln]) * (rg * _sigmoid(rg))
                mix_ref[rows, ln] = out.astype(BF16)
            return carry

        lax.fori_loop(0, cpb, chunk, 0)
        mixt_ref[...] = mix_ref[...].astype(F32).T.astype(BF16)

    nh //= RNN_HB
    vec2 = pl.BlockSpec((2, RNN_WIDE), lambda h, s: (0, h))
    vec1 = pl.BlockSpec((1, RNN_WIDE), lambda h, s: (0, h))
    return pl.pallas_call(
        body, name="rnn_fwd", grid=(nh, nt),
        in_specs=cols + [vec2, vec1, ANY_SPEC, ANY_SPEC],
        out_specs=[pl.BlockSpec((tb, RNN_WIDE), lambda h, s: (s, D // RNN_WIDE + h)),
                   pl.BlockSpec((RNN_WIDE, tb), lambda h, s: (D // RNN_WIDE + h, s)),
                   pl.BlockSpec((tb, RNN_WIDE), lambda h, s: (s, h)),
                   pl.BlockSpec((RNN_HB, cpb, RNN_HEAD, RNN_HEAD), lambda h, s: (h, s, 0, 0))],
        out_shape=[jax.ShapeDtypeStruct(mixed.shape, BF16), jax.ShapeDtypeStruct(mixed_t.shape, BF16),
                   jax.ShapeDtypeStruct((T, D), F32),
                   jax.ShapeDtypeStruct((nh * RNN_HB, T // CHUNK, RNN_HEAD, RNN_HEAD), F32)],
        scratch_shapes=[pltpu.VMEM((RNN_HB, RNN_HEAD, RNN_HEAD), F32)],
        input_output_aliases={6: 0, 7: 1},
        compiler_params=_params("parallel", "arbitrary"))(proj, proj, proj, proj, lb_logits, rnn_norm, mixed, mixed_t)


def _rnn_bwd(proj, lb_logits, rnn_norm, o_raw, states, dmix, D):
    T = proj.shape[0]
    tb = _pick(T, RNN_TB)
    nt, nh, cpb = T // tb, D // RNN_HEAD, tb // CHUNK
    cols, t_of = _rnn_specs(T, D, tb, True)

    def body(rq_ref, rf_ref, ri_ref, rg_ref, lb_ref, gain_ref, o_ref, st_ref, dmix_ref,
             drq_ref, drf_ref, dri_ref, drg_ref, dlb_ref, dgain_ref, dstate_ref):
        causal, tri, tri_t = _chunk_masks()
        lb = _lower_bound(lb_ref)
        gain = gain_ref[...]

        @pl.when(pl.program_id(1) == 0)
        def _():
            dstate_ref[...] = jnp.zeros_like(dstate_ref)
            dlb_ref[...] = jnp.zeros_like(dlb_ref)
            dgain_ref[...] = jnp.zeros_like(dgain_ref)

        def chunk(i, carry):
            c = cpb - 1 - i
            rows = pl.ds(pl.multiple_of(c * CHUNK, CHUNK), CHUNK)
            heads = range(RNN_HB)
            lns = [slice(RNN_HEAD * hh, RNN_HEAD * (hh + 1)) for hh in heads]
            last_row = lax.broadcasted_iota(jnp.int32, (CHUNK, 1), 0) == CHUNK - 1
            A = []
            for ln in lns:
                rq, rf, rg = rq_ref[rows, ln], rf_ref[rows, ln], rg_ref[rows, ln]
                o, dgated = o_ref[rows, ln], dmix_ref[rows, ln]
                gainh = gain[:, ln]
                sgt = _sigmoid(rg)
                r = lax.rsqrt(jnp.mean(o * o, axis=-1, keepdims=True) + NORM_EPS)
                on = o * r
                drg_ref[rows, ln] = (dgated * (on * gainh) * (sgt * (1.0 + rg * (1.0 - sgt)))).astype(BF16)
                d_on = dgated * (rg * sgt)
                dgain_ref[:, ln] += jnp.sum(d_on * on, axis=0, keepdims=True)
                dob = _norm_bwd(d_on * gainh, on, r).astype(BF16)
                sq, sf, f = _rnn_gates(rq, rf, lb[:, ln])
                A.append(dict(rq=rq, sq=sq, sf=sf, f=f, dob=dob, G=_tri_sum(tri, jnp.log(f))))
            for hh, ln in enumerate(lns):
                a = A[hh]
                _, eG, eq, ek, ekl, elast = _rnn_factors(a.pop("G"))
                q, k = a["rq"] * a["sq"], 1.0 - a["f"]
                st, dst = st_ref[hh, c], dstate_ref[hh]
                qg, kl = q * eG, k * ekl
                qmb, kmb = (q * eq).astype(BF16), (k * ek).astype(BF16)
                dob, vb, dstb = a["dob"], ri_ref[rows, ln].astype(BF16), dst.astype(BF16)
                a.update(eG=eG, eq=eq, ek=ek, ekl=ekl, qg=qg, kl=kl, qmb=qmb, kmb=kmb,
                         att=_dot(qmb, kmb, NT_DIMS), datt=_dot(dob, vb, NT_DIMS),
                         dqg=_dot(dob, st.astype(BF16), NN_DIMS), dkl=_dot(vb, dstb, NN_DIMS),
                         dri=_dot(kl.astype(BF16), dstb, NT_DIMS),
                         dlast=jnp.sum(dst * st, axis=0, keepdims=True) * elast)
                dstate_ref[hh] = dst * elast + _dot(dob, qg.astype(BF16), TN_DIMS)
            for hh, ln in enumerate(lns):
                a = A[hh]
                att = jnp.where(causal, a.pop("att"), 0.0).astype(BF16)
                datt = jnp.where(causal, a.pop("datt"), 0.0).astype(BF16)
                dqm = _dot(datt, a["kmb"], NN_DIMS)
                dkm = _dot(datt, a["qmb"], TN_DIMS)
                dri_ref[rows, ln] = (_dot(att, a["dob"], TN_DIMS) + a.pop("dri")).astype(BF16)
                dqg, dkl, kl = a.pop("dqg"), a.pop("dkl"), a.pop("kl")
                a["dq"] = dqg * a.pop("eG") + dqm * a.pop("eq")
                a["dk"] = dkm * a.pop("ek") + dkl * a.pop("ekl")
                dG = dqg * a.pop("qg") + dqm * a.pop("qmb").astype(F32) - dkm * a.pop("kmb").astype(F32) - dkl * kl
                dlast = jnp.sum(dkl * kl, axis=0, keepdims=True) + a.pop("dlast")
                a["dg"] = _tri_sum(tri_t, dG + jnp.where(last_row, dlast, 0.0))
            for hh, ln in enumerate(lns):
                a = A[hh]
                rq, sq, sf = a["rq"], a["sq"], a["sf"]
                df = a["dg"] / a["f"] - a["dk"]
                drq_ref[rows, ln] = (a["dq"] * (sq * (1.0 + rq * (1.0 - sq)))).astype(BF16)
                drf_ref[rows, ln] = (df * (1.0 - lb[:, ln]) * (sf * (1.0 - sf))).astype(BF16)
                dlb_ref[:, ln] += jnp.sum(df * (1.0 - sf), axis=0, keepdims=True)
            return carry

        lax.fori_loop(0, cpb, chunk, 0)

    nh //= RNN_HB
    vec2 = pl.BlockSpec((2, RNN_WIDE), lambda h, s: (0, h))
    vec1 = pl.BlockSpec((1, RNN_WIDE), lambda h, s: (0, h))
    blk = pl.BlockSpec((tb, RNN_WIDE), lambda h, s: (t_of(s), h))
    return pl.pallas_call(
        body, name="rnn_bwd", grid=(nh, nt),
        in_specs=cols + [vec2, vec1, blk,
                         pl.BlockSpec((RNN_HB, cpb, RNN_HEAD, RNN_HEAD), lambda h, s: (h, t_of(s), 0, 0)),
                         pl.BlockSpec((tb, RNN_WIDE), lambda h, s: (t_of(s), D // RNN_WIDE + h))],
        out_specs=[blk, blk, blk, blk, vec1, vec1],
        out_shape=[jax.ShapeDtypeStruct((T, D), BF16)] * 4 + [jax.ShapeDtypeStruct((1, D), F32)] * 2,
        scratch_shapes=[pltpu.VMEM((RNN_HB, RNN_HEAD, RNN_HEAD), F32)],
        compiler_params=_params("parallel", "arbitrary"))(proj, proj, proj, proj, lb_logits, rnn_norm, o_raw, states, dmix)


def _adamw(w, g, m, v):
    m = ADAM_B1 * m + (1.0 - ADAM_B1) * g
    v = ADAM_B2 * v + (1.0 - ADAM_B2) * (g * g)
    m_hat = m / (1.0 - ADAM_B1 ** ADAM_STEP)
    v_hat = v / (1.0 - ADAM_B2 ** ADAM_STEP)
    delta = -ADAM_LR * (m_hat / (jnp.sqrt(v_hat) + ADAM_EPS) + ADAM_WD * w)
    return delta, m, v


def _adamw_big(w, m, v, parts, tr, name):
    R, C = w.shape
    n_parts = len(parts)

    def body(w_ref, m_ref, v_ref, *rest):
        part_refs = rest[:n_parts]
        g_ref, d_ref, nm_ref, nv_ref = rest[n_parts:]
        g = part_refs[0][...].astype(F32)
        for p_ref in part_refs[1:]:
            g = g + p_ref[...].astype(F32)
        delta, nm, nv = _adamw(w_ref[...], g, m_ref[...], v_ref[...])
        g_ref[...] = g
        d_ref[...] = delta
        nm_ref[...] = nm
        nv_ref[...] = nv

    blk = pl.BlockSpec((tr, C), lambda i: (i, 0))
    part_specs = [pl.BlockSpec((None, tr, C), functools.partial(lambda s, i: (s, i, 0), slot)) for _, slot in parts]
    return pl.pallas_call(
        body, name=name, grid=(R // tr,), in_specs=[blk, blk, blk] + part_specs,
        out_specs=[blk] * 4, out_shape=[jax.ShapeDtypeStruct((R, C), F32)] * 4,
        compiler_params=_params("parallel"))(w, m, v, *[a for a, _ in parts])


def _pair_sum(pa, qa):
    n, R, C = qa.shape
    tr = _pick(R, 256)

    def body(p_ref, q_ref, r_ref):
        r_ref[...] = (p_ref[...].astype(F32) + q_ref[...].astype(F32)).astype(BF16)

    return pl.pallas_call(
        body, name="pair_sum", grid=(n, R // tr),
        in_specs=[pl.BlockSpec((None, tr, C), lambda j, i: (2 * j, i, 0)), pl.BlockSpec((None, tr, C), lambda j, i: (j, i, 0))],
        out_specs=pl.BlockSpec((None, tr, C), lambda j, i: (j, i, 0)),
        out_shape=jax.ShapeDtypeStruct((n, R, C), BF16), compiler_params=_params("parallel", "parallel"))(pa, qa)


def _adamw_small(total, sinks, lb_logits, rnn_norm, pre_norm, post_norm, moments, D):
    params = [sinks, lb_logits, rnn_norm, pre_norm, post_norm]

    def body(tot_ref, *refs):
        p_refs, m_refs, v_refs = refs[0:5], refs[5:10], refs[10:15]
        loss_ref, outs = refs[15], refs[16:]
        tot = tot_ref[...]
        l0, l1 = p_refs[1][0:1, :], p_refs[1][1:2, :]
        mx = jnp.maximum(l0, l1)
        e0, e1 = jnp.exp(l0 - mx), jnp.exp(l1 - mx)
        p0, p1 = e0 / (e0 + e1), e1 / (e0 + e1)
        dlb = tot[0:1, :]
        grads = [tot[5:6, 0:LANES], jnp.concatenate([dlb * p0 * (1.0 - p0), -dlb * p0 * p1], axis=0),
                 tot[1:2, :], tot[2:3, :], tot[3:4, :]]
        loss_ref[...] = 0.5 / D * jnp.sum(tot[4:5, :], axis=-1, keepdims=True)
        for i, g in enumerate(grads):
            delta, nm, nv = _adamw(p_refs[i][...], g, m_refs[i][...], v_refs[i][...])
            outs[4 * i][...] = g
            outs[4 * i + 1][...] = delta
            outs[4 * i + 2][...] = nm
            outs[4 * i + 3][...] = nv

    out_shape = [jax.ShapeDtypeStruct((1, 1), F32)]
    for p in params:
        out_shape += [jax.ShapeDtypeStruct(p.shape, F32)] * 4
    return pl.pallas_call(body, name="adamw_small", out_shape=out_shape)(total, *params, *moments)


SIBLING = 1


def _me():
    return lax.axis_index("x"), lax.axis_index("y"), lax.axis_index("c")


def _flat(px, py, pc):
    return 4 * px + 2 * py + pc


def _role_peer(role, x, y, c):
    if role < 2:
        return (x, y, (1 - c) if role else c)
    j, other = (role - 2) // 2, (role - 2) % 2
    px = (1 - x) if j in (0, 2) else x
    py = (1 - y) if j in (1, 2) else y
    return (px, py, (1 - c) if other else c)


def _role_ids():
    x, y, c = _me()
    return jnp.stack([_flat(*_role_peer(r, x, y, c)) for r in range(N_DEV)]).astype(jnp.int32)


def _comm_call(name, bufs, waits=(), starts=(), after=()):
    n_buf, n_wait, n_start, n_after = len(bufs), len(waits), len(starts), len(after)

    def body(*refs):
        buf_refs = refs[:n_buf]
        sem_refs = refs[n_buf:n_buf + 2 * n_wait]
        outs = refs[n_buf + 2 * n_wait + n_after:]
        new_sems, token = outs[:2 * n_start], outs[-1]
        x, y, c = _me()
        def block(ref, slot, rows):
            return ref.at[slot] if rows is None else ref.at[slot, pl.ds(rows[0], rows[1])]

        for w, (_, kind, like, *rows) in enumerate(waits):
            shape_ref = block(buf_refs[like], 0, (0, rows[0][1]) if rows else None)
            cp = pltpu.make_async_remote_copy(
                src_ref=shape_ref, dst_ref=shape_ref, send_sem=sem_refs[2 * w], recv_sem=sem_refs[2 * w + 1],
                device_id=(x, y, c), device_id_type=MESH)
            if kind == "send":
                cp.wait_send()
            else:
                cp.wait_recv()
        for s, (sb, ss, db, ds, role, *rows) in enumerate(starts):
            rows = rows[0] if rows else None
            pltpu.make_async_remote_copy(
                src_ref=block(buf_refs[sb], ss, rows), dst_ref=block(buf_refs[db], ds, rows), send_sem=new_sems[2 * s],
                recv_sem=new_sems[2 * s + 1], device_id=_role_peer(role, x, y, c), device_id_type=MESH).start()
        token[...] = jnp.zeros_like(token)

    sems = [s for flight, *_ in waits for s in flight]
    out = pl.pallas_call(
        body, name=name,
        out_shape=tuple([pltpu.SemaphoreType.DMA(())] * (2 * n_start) + [pltpu.HBM(b.shape, b.dtype) for b in bufs]
                        + [jax.ShapeDtypeStruct((8, LANES), F32)]),
        in_specs=tuple([HBM_SPEC] * n_buf + [SEM_SPEC] * (2 * n_wait) + [ANY_SPEC] * n_after),
        out_specs=tuple([SEM_SPEC] * (2 * n_start) + [HBM_SPEC] * n_buf + [pl.BlockSpec(memory_space=pltpu.VMEM)]),
        input_output_aliases={i: 2 * n_start + i for i in range(n_buf)},
        compiler_params=pltpu.CompilerParams(has_side_effects=EFFECT),
    )(*[pltpu.with_memory_space_constraint(b, pltpu.HBM) for b in bufs], *sems, *after)
    flights = [(out[2 * s], out[2 * s + 1]) for s in range(n_start)]
    return list(out[2 * n_start:2 * n_start + n_buf]), flights, out[-1]


def _landing(shape):
    return lax.empty(shape, BF16)


def _both(flights, like):
    return [(f, kind, like) for f in flights for kind in ("send", "recv")]


def _allreduce_small(rows, D):
    n_rows = len(rows)

    def body(*refs):
        in_refs, out_ref, all_ref = refs[:n_rows], refs[n_rows], refs[n_rows + 1]
        send_sems, recv_sems = refs[n_rows + 2], refs[n_rows + 3]
        x, y, c = _me()
        mine = all_ref.at[_flat(x, y, c)]
        mine[...] = jnp.zeros((8, D), F32)
        for i, r in enumerate(in_refs):
            mine[i:i + 1, :] = r[...]
        copies = []
        for k in range(N_DEV - 1):
            copies.append(pltpu.make_async_remote_copy(
                src_ref=mine, dst_ref=mine, send_sem=send_sems.at[k], recv_sem=recv_sems.at[k],
                device_id=_role_peer(k + 1, x, y, c), device_id_type=MESH))
        for cp in copies:
            cp.start()
        for cp in copies:
            cp.wait_recv()
        for cp in copies:
            cp.wait_send()
        tot = all_ref[0]
        for d in range(1, N_DEV):
            tot = tot + all_ref[d]
        out_ref[...] = tot

    vm = pl.BlockSpec(memory_space=pltpu.VMEM)
    return pl.pallas_call(
        body, name="allreduce_small", in_specs=[vm] * n_rows, out_specs=vm,
        out_shape=jax.ShapeDtypeStruct((8, D), F32),
        scratch_shapes=[pltpu.VMEM((N_DEV, 8, D), F32), pltpu.SemaphoreType.DMA((7,)), pltpu.SemaphoreType.DMA((7,))],
    )(*rows)


def _step(ids, x, target, g_in, g_out, sinks, lb_logits, rnn_norm, pre_norm, post_norm, dist):
    T, D = x.shape
    n_roles, _, wd = g_in.shape
    ro = g_out.shape[1]
    assert n_roles % 2 == 0 and (not dist or n_roles == N_DEV)
    n_chips = n_roles // 2 - 1

    if dist:
        (g_in,), fl_in, _ = _comm_call(
            "gather_in_start", [g_in], starts=[(0, 0, 0, SIBLING, SIBLING)] + [(0, 0, 0, 2 + 2 * j, 2 + 2 * j) for j in range(2)])
    h, h_t = _prenorm(x, pre_norm)
    if dist:
        (g_in,), _, _ = _comm_call("gather_in_sibling", [g_in], waits=[(fl_in[0], "recv", 0)], after=(h,))
    proj = _proj_in(ids, h, g_in, 0, 1, None, "proj_in_0")
    fl_fwd, fl_half = [], []

    def pass_on(i, g):
        slot = 2 + 2 * i
        starts = [(0, slot, 0, slot + 1, SIBLING)]
        if i < 2:
            waits = [(fl_in[1 + i], "recv", 0)]
            starts.append((0, slot, 0, 6, 4 - 2 * i, (i * (D // 2), D // 2)))
        else:
            waits = [(f, "recv", 0, (0, D // 2)) for f in fl_half]
        (g,), fl, _ = _comm_call(f"gather_in_pass_{i}", [g], waits=waits, starts=starts, after=(proj,))
        fl_fwd.append(fl[0])
        fl_half.extend(fl[1:])
        return g

    if not dist:
        for j in range(n_chips):
            proj = _proj_in(ids, h, g_in, 2 + 2 * j, 1, proj, f"proj_in_{j + 1}")
    else:
        g_in = pass_on(1, pass_on(0, g_in))
        proj = _proj_in(ids, h, g_in, 2, 2, proj, "proj_in_1")
        (g_in,), _, _ = _comm_call(
            "gather_in_passed_01", [g_in], waits=[(fl_fwd[0], "recv", 0), (fl_fwd[1], "recv", 0)], after=(proj,))
        g_in = pass_on(2, g_in)
        proj = _proj_in(ids, h, g_in, 3, 2, proj, "proj_in_2")
        (g_in,), _, _ = _comm_call("gather_in_passed_2", [g_in], waits=[(fl_fwd[2], "recv", 0)], after=(proj,))
        (g_in, g_out), fl_out, token = _comm_call(
            "gather_out_start", [g_in, g_out],
            starts=[(1, 0, 1, SIBLING, SIBLING)] + [(1, 0, 1, 2 + 2 * i, 2 + 2 * i) for i in range(3)], after=(proj,))
        proj = _proj_in(ids, h, g_in, 6, 1, proj, "proj_in_3", (token,))
    mixed, mixed_t = _attn_fwd(proj, sinks, D)
    if dist:
        (g_in, g_out), fl_out_fwd, _ = _comm_call(
            "gather_out_pass", [g_in, g_out],
            waits=[(f, "send", 0) for f in fl_in + fl_fwd] + [(f, "send", 0, (0, D // 2)) for f in fl_half]
            + [(fl_out[1 + i], "recv", 1) for i in range(3)],
            starts=[(1, 2 + 2 * i, 1, 3 + 2 * i, SIBLING) for i in range(3)], after=(mixed,))
    mixed, mixed_t, o_raw, states = _rnn_fwd(proj, lb_logits, rnn_norm, mixed, mixed_t, D)
    if dist:
        (g_out,), _, _ = _comm_call(
            "gather_out_done", [g_out],
            waits=[(fl_out[0], "recv", 0)] + [(f, "recv", 0) for f in fl_out_fwd]
            + [(f, "send", 0) for f in fl_out + fl_out_fwd], after=(states,))
    y = _proj_out(ids, mixed, g_out)
    dy, dout, g_post, sq_err = _loss_head(y, x, target, post_norm)

    dmix = _dmixed(ids, dy, g_out)
    p_out, after = _dw_out(ids, mixed_t, dy, n_roles), ()
    if dist:
        (p_out, l_out), fl_so, token = _comm_call(
            "scatter_out_start", [p_out, _landing((n_roles - 1, ro, D))],
            starts=[(0, r, 1, r - 1, r) for r in range(1, n_roles)])
        after = (token,)
    d_aq, d_ak, d_av, d_ag, d_sink = _attn_bwd(proj, sinks, dmix, D, after)
    d_rq, d_rf, d_ri, d_rg, d_lb, g_rnn = _rnn_bwd(proj, lb_logits, rnn_norm, o_raw, states, dmix, D)
    dproj = jnp.concatenate([d_aq, d_ak, d_av, d_ag, d_rq, d_rf, d_ri, d_rg], axis=1)

    p_far = _dw_in(ids, h_t, dproj, n_roles, 2, n_roles - 2, "dw_in_far")
    if dist:
        (p_far, q_far), fl_pair, token = _comm_call(
            "scatter_in_pair_start", [p_far, _landing((3, D, wd))],
            starts=[(0, 1 + 2 * j, 1, j, SIBLING) for j in range(3)])
        p_near = _dw_in(ids, h_t, dproj, n_roles, 0, 2, "dw_in_near", after=(token,))
        (p_far, q_far), _, _ = _comm_call("scatter_in_pair_wait", [p_far, q_far], waits=_both(fl_pair, 1), after=(p_near,))
        chip_sum = _pair_sum(p_far, q_far)
        (chip_sum, z_far), fl_chip, token = _comm_call(
            "scatter_in_chip_start", [chip_sum, _landing((3, D, wd))],
            starts=[(0, j, 1, j, 2 + 2 * j) for j in range(3)])
        (p_near, q_near), fl_sib, token = _comm_call(
            "scatter_in_sibling_start", [p_near, _landing((1, D, wd))], starts=[(0, 1, 1, 0, SIBLING)], after=(token,))
        after = (token,)
    else:
        p_near, after = _dw_in(ids, h_t, dproj, n_roles, 0, 2, "dw_in_near"), ()
    dh = _dh(ids, dproj, g_in, after)
    grad_x, g_pre = _prenorm_bwd(x, dh, dout, pre_norm)
    n_q = D // ATTN_HEAD
    sink_row = jnp.pad(d_sink[:, 0, :2 * GQA].reshape(1, n_q), ((0, 0), (0, D - n_q)))
    rows = [d_lb, g_rnn, g_pre, g_post, sq_err, sink_row]
    if not dist:
        return grad_x, [(p_near, r) for r in range(2)] + [(p_far, r) for r in range(n_roles - 2)], \
            [(p_out, r) for r in range(n_roles)], rows

    (p_out, l_out), _, _ = _comm_call("scatter_out_wait", [p_out, l_out], waits=_both(fl_so, 1), after=(grad_x,))
    (chip_sum, z_far, p_near, q_near), _, _ = _comm_call(
        "scatter_in_wait", [chip_sum, z_far, p_near, q_near], waits=_both(fl_chip, 1) + _both(fl_sib, 3), after=(p_out,))
    parts_in = [(p_near, 0), (q_near, 0)] + [(z_far, j) for j in range(3)]
    parts_out = [(p_out, 0)] + [(l_out, k) for k in range(n_roles - 1)]
    return grad_x, parts_in, parts_out, rows


def kernel(x, w_in, attn_sinks, lb_logits, rnn_norm, w_out, pre_norm, post_norm, loss_target, m_w_in, m_attn_sinks, m_lb_logits, m_rnn_norm, m_w_out, m_pre_norm, m_post_norm, v_w_in, v_attn_sinks, v_lb_logits, v_rnn_norm, v_w_out, v_pre_norm, v_post_norm):
    _, T, D = x.shape
    ro = w_out.shape[1]
    n_q = attn_sinks.shape[1]
    assert lb_logits.shape[0] == 2 and n_q == D // ATTN_HEAD and n_q <= LANES

    grad_x, parts_in, parts_out, small_rows = _step(
        _role_ids(), x[0], loss_target[0], _cast_slot0(w_in[0], N_DEV, "cast_w_in"),
        _cast_slot0(w_out[0], N_DEV, "cast_w_out"), attn_sinks, lb_logits, rnn_norm, pre_norm, post_norm, True)

    g_wo, d_wo, nm_wo, nv_wo = _adamw_big(w_out[0], m_w_out[0], v_w_out[0], parts_out, _pick(ro, 64), "adamw_w_out")
    g_wi, d_wi, nm_wi, nv_wi = _adamw_big(w_in[0], m_w_in[0], v_w_in[0], parts_in, _pick(D, 128), "adamw_w_in")

    total = _allreduce_small(small_rows, D)
    pad = lambda a: jnp.pad(a, ((0, 0), (0, LANES - n_q)))
    moments = [pad(m_attn_sinks), m_lb_logits, m_rnn_norm, m_pre_norm, m_post_norm,
               pad(v_attn_sinks), v_lb_logits, v_rnn_norm, v_pre_norm, v_post_norm]
    res = _adamw_small(total, pad(attn_sinks), lb_logits, rnn_norm, pre_norm, post_norm, moments, D)
    loss = res[0][0, 0]
    small = [[res[1 + 4 * i + j] for i in range(5)] for j in range(4)]
    for j in range(4):
        small[j][0] = small[j][0][:, :n_q]

    def assemble(j, wi, wo):
        s = small[j]
        return [wi[None], s[0], s[1], s[2], wo[None], s[3], s[4]]

    return (loss, grad_x[None], *assemble(0, g_wi, g_wo), *assemble(1, d_wi, d_wo),
            *assemble(2, nm_wi, nm_wo), *assemble(3, nv_wi, nv_wo))
```

```python
import functools

import jax
import jax.numpy as jnp
from jax import lax
from jax.experimental import pallas as pl
from jax.experimental.pallas import tpu as pltpu

F32 = jnp.float32
BF16 = jnp.bfloat16

ATTN_HEAD = 64
GQA = 8
WINDOW = 128
RNN_HEAD = 128
CHUNK = 64
NORM_EPS = 1e-6
LANES = 128
N_DEV = 8

ADAM_LR = 0.001
ADAM_B1 = 0.9
ADAM_B2 = 0.999
ADAM_EPS = 1e-08
ADAM_WD = 0.01
ADAM_STEP = 10

VMEM_LIMIT = 56 * 1024 * 1024
MESH = pl.DeviceIdType.MESH
ANY_SPEC = pl.BlockSpec(memory_space=pl.ANY)
HBM_SPEC = pl.BlockSpec(memory_space=pltpu.HBM)
SEM_SPEC = pl.BlockSpec(memory_space=pltpu.SEMAPHORE)
EFFECT = pltpu.SideEffectType.DATAFLOW_SIDE_EFFECTING

NT_DIMS = (((1,), (1,)), ((), ()))
NN_DIMS = (((1,), (0,)), ((), ()))
TN_DIMS = (((0,), (0,)), ((), ()))


def _params(*sem):
    return pltpu.CompilerParams(dimension_semantics=sem, vmem_limit_bytes=VMEM_LIMIT)


def _dot(a, b, dims):
    return lax.dot_general(a, b, dims, preferred_element_type=F32)


def _sigmoid(v):
    return 1.0 / (1.0 + jnp.exp(-v))


def _pick(n, pref):
    t = min(n, pref)
    assert n % t == 0, (n, pref)
    return t


MXU_COLS = 256
WIDE_TILE = 5 * MXU_COLS


def _col_tiles(wd):
    wide = wd // WIDE_TILE
    rest = wd - wide * WIDE_TILE
    assert rest % LANES == 0 and (rest == 0 or (wide * WIDE_TILE) % rest == 0), wd
    return [(WIDE_TILE, 0, wide)] * (wide > 0) + [(rest, wide * WIDE_TILE, 1)] * (rest > 0)


def _matmul(ids, a, b, *, grid, a_spec, b_spec, o_spec, out_shape, trans_b, name, after=(), prev=None):
    nk = grid[2]
    dims = NT_DIMS if trans_b else NN_DIMS
    n_skip = len(after) + (prev is not None)

    def body(ids_ref, a_ref, b_ref, *rest):
        del ids_ref
        o_ref, scratch = rest[n_skip], rest[n_skip + 1:]
        prod = _dot(a_ref[...], b_ref[...], dims)
        if nk == 1:
            o_ref[...] = prod.astype(o_ref.dtype)
        else:
            acc_ref, = scratch
            k = pl.program_id(2)

            @pl.when(k == 0)
            def _():
                acc_ref[...] = prod

            @pl.when(k > 0)
            def _():
                acc_ref[...] += prod

            @pl.when(k == nk - 1)
            def _():
                o_ref[...] = acc_ref[...].astype(o_ref.dtype)

    scratch = [] if nk == 1 else [pltpu.VMEM(tuple(d for d in o_spec.block_shape if d is not None), F32)]
    extra = list(after) + ([prev] if prev is not None else [])
    aliases = {3 + len(after): 0} if prev is not None else {}
    return pl.pallas_call(
        body, name=name, out_shape=out_shape, input_output_aliases=aliases,
        grid_spec=pltpu.PrefetchScalarGridSpec(
            num_scalar_prefetch=1, grid=grid, in_specs=[a_spec, b_spec] + [ANY_SPEC] * len(extra),
            out_specs=o_spec, scratch_shapes=scratch),
        compiler_params=_params("parallel", "parallel", "arbitrary"),
    )(ids, a, b, *extra)


def _proj_in(ids, h, w_roles, pos, n, prev, name, after=()):
    T, D = h.shape
    n_roles, _, wd = w_roles.shape
    tm = _pick(T, 1024)
    for t, (tn, first, count) in enumerate(_col_tiles(wd)):
        prev = _matmul(
            ids, h, w_roles, grid=(T // tm, n * count, 1),
            a_spec=pl.BlockSpec((tm, D), lambda i, j, k, ids: (i, 0)),
            b_spec=pl.BlockSpec((None, D, tn), functools.partial(
                lambda tn, first, count, i, j, k, ids: (ids[n_roles + pos + j // count], 0, first // tn + j % count),
                tn, first, count)),
            o_spec=pl.BlockSpec((pl.Element(tm), pl.Element(tn)), functools.partial(
                lambda tn, first, count, i, j, k, ids: (
                    i * tm, pl.multiple_of(ids[ids[n_roles + pos + j // count]] * wd + first + (j % count) * tn, LANES)),
                tn, first, count)),
            out_shape=jax.ShapeDtypeStruct((T, n_roles * wd), F32), trans_b=False, name=f"{name}_{t}",
            after=after, prev=prev)
    return prev


def _proj_out(ids, mixed, wo_roles):
    T, E = mixed.shape
    n_roles, R, D = wo_roles.shape
    tm, tn = _pick(T, 512), _pick(D, 512)

    def body(ids_ref, a_ref, b_ref, o_ref, a_roles):
        @pl.when(pl.program_id(1) == 0)
        def _():
            for k in range(n_roles):
                col = pl.multiple_of(ids_ref[k] * R, R)
                a_roles[:, k * R:(k + 1) * R] = a_ref[:, pl.ds(col, R)]

        o_ref[...] = _dot(a_roles[...], b_ref[...].reshape(E, tn), NN_DIMS)

    return pl.pallas_call(
        body, name="proj_out", out_shape=jax.ShapeDtypeStruct((T, D), F32),
        grid_spec=pltpu.PrefetchScalarGridSpec(
            num_scalar_prefetch=1, grid=(T // tm, D // tn),
            in_specs=[pl.BlockSpec((tm, E), lambda i, j, ids: (i, 0)),
                      pl.BlockSpec((n_roles, R, tn), lambda i, j, ids: (0, 0, j))],
            out_specs=pl.BlockSpec((tm, tn), lambda i, j, ids: (i, j)),
            scratch_shapes=[pltpu.VMEM((tm, E), BF16)]),
        compiler_params=_params("parallel", "arbitrary"))(ids, mixed, wo_roles)


def _dmixed(ids, dy, wo_roles):
    T, D = dy.shape
    n_roles, R, _ = wo_roles.shape
    tm = _pick(T, 1024)
    return _matmul(
        ids, dy, wo_roles, grid=(T // tm, n_roles, 1),
        a_spec=pl.BlockSpec((tm, D), lambda i, j, k, ids: (i, 0)),
        b_spec=pl.BlockSpec((None, R, D), lambda i, j, k, ids: (j, 0, 0)),
        o_spec=pl.BlockSpec((tm, R), lambda i, j, k, ids: (i, ids[j])),
        out_shape=jax.ShapeDtypeStruct((T, n_roles * R), F32), trans_b=True, name="dmixed")


def _dw_out(ids, mixed_t, dy, n_roles):
    E, T = mixed_t.shape
    D = dy.shape[1]
    R = E // n_roles
    tn = _pick(D, 512)
    return _matmul(
        ids, mixed_t, dy, grid=(n_roles, D // tn, 1),
        a_spec=pl.BlockSpec((R, T), lambda i, j, k, ids: (ids[i], 0)),
        b_spec=pl.BlockSpec((T, tn), lambda i, j, k, ids: (0, j)),
        o_spec=pl.BlockSpec((None, R, tn), lambda i, j, k, ids: (i, 0, j)),
        out_shape=jax.ShapeDtypeStruct((n_roles, R, D), BF16), trans_b=False, name="dw_out")


def _dw_in(ids, h_t, dproj, n_roles, r0, nr, name, after=()):
    D, T = h_t.shape
    wd = dproj.shape[1] // n_roles
    tm, out = _pick(D, 1024), None
    for t, (tn, first, count) in enumerate(_col_tiles(wd)):
        out = _matmul(
            ids, h_t, dproj, grid=(D // tm, nr * count, 1),
            a_spec=pl.BlockSpec((tm, T), lambda i, j, k, ids: (i, 0)),
            b_spec=pl.BlockSpec((pl.Element(T), pl.Element(tn)), functools.partial(
                lambda tn, first, count, i, j, k, ids: (
                    0, pl.multiple_of(ids[r0 + j // count] * wd + first + (j % count) * tn, LANES)),
                tn, first, count)),
            o_spec=pl.BlockSpec((None, tm, tn), functools.partial(
                lambda tn, first, count, i, j, k, ids: (j // count, i, first // tn + j % count), tn, first, count)),
            out_shape=jax.ShapeDtypeStruct((nr, D, wd), BF16), trans_b=False, name=f"{name}_{t}", after=after, prev=out)
    return out


def _dh(ids, dproj, w_roles, after=()):
    T = dproj.shape[0]
    n_roles, D, wd = w_roles.shape
    tm, tn = _pick(T, 1024), _pick(D, 1024)
    return _matmul(
        ids, dproj, w_roles, grid=(T // tm, D // tn, n_roles),
        a_spec=pl.BlockSpec((tm, wd), lambda i, j, k, ids: (i, ids[k])),
        b_spec=pl.BlockSpec((None, tn, wd), lambda i, j, k, ids: (k, j, 0)),
        o_spec=pl.BlockSpec((tm, tn), lambda i, j, k, ids: (i, j)),
        out_shape=jax.ShapeDtypeStruct((T, D), F32), trans_b=True, name="dh", after=after)


def _cast_slot0(w, n_roles, name):
    R, C = w.shape
    tr = _pick(R, 256)

    def body(w_ref, o_ref):
        o_ref[...] = w_ref[...].astype(BF16)

    return pl.pallas_call(
        body, name=name, grid=(R // tr,), in_specs=[pl.BlockSpec((tr, C), lambda i: (i, 0))],
        out_specs=pl.BlockSpec((None, tr, C), lambda i: (0, i, 0)),
        out_shape=jax.ShapeDtypeStruct((n_roles, R, C), BF16), compiler_params=_params("parallel"))(w)


def _prenorm(x, gain):
    T, D = x.shape
    tm = _pick(T, 256)

    def body(x_ref, g_ref, h_ref, ht_ref):
        xv = x_ref[...]
        r = lax.rsqrt(jnp.mean(xv * xv, axis=-1, keepdims=True) + NORM_EPS)
        h = xv * r * g_ref[...]
        h_ref[...] = h.astype(BF16)
        ht_ref[...] = h.T.astype(BF16)

    return pl.pallas_call(
        body, name="prenorm", grid=(T // tm,),
        in_specs=[pl.BlockSpec((tm, D), lambda i: (i, 0)), pl.BlockSpec((1, D), lambda i: (0, 0))],
        out_specs=[pl.BlockSpec((tm, D), lambda i: (i, 0)), pl.BlockSpec((D, tm), lambda i: (0, i))],
        out_shape=[jax.ShapeDtypeStruct((T, D), BF16), jax.ShapeDtypeStruct((D, T), BF16)],
        compiler_params=_params("parallel"))(x, gain)


def _norm_bwd(u, yn, r):
    return r * (u - yn * jnp.mean(u * yn, axis=-1, keepdims=True))


def _loss_head(y, x, target, gain):
    T, D = y.shape
    tm = _pick(T, 256)

    def body(y_ref, x_ref, t_ref, g_ref, dy_ref, dout_ref, gpost_ref, sq_ref):
        yv = y_ref[...]
        g = g_ref[...]
        r = lax.rsqrt(jnp.mean(yv * yv, axis=-1, keepdims=True) + NORM_EPS)
        yn = yv * r
        err = x_ref[...] + yn * g - t_ref[...]
        dout = err * (1.0 / D)
        dy_ref[...] = _norm_bwd(dout * g, yn, r).astype(BF16)
        dout_ref[...] = dout

        @pl.when(pl.program_id(0) == 0)
        def _():
            gpost_ref[...] = jnp.zeros_like(gpost_ref)
            sq_ref[...] = jnp.zeros_like(sq_ref)

        gpost_ref[...] += jnp.sum(dout * yn, axis=0, keepdims=True)
        sq_ref[...] += jnp.sum(err * err, axis=0, keepdims=True)

    row = pl.BlockSpec((tm, D), lambda i: (i, 0))
    vec = pl.BlockSpec((1, D), lambda i: (0, 0))
    return pl.pallas_call(
        body, name="loss_head", grid=(T // tm,), in_specs=[row, row, row, vec], out_specs=[row, row, vec, vec],
        out_shape=[jax.ShapeDtypeStruct((T, D), BF16), jax.ShapeDtypeStruct((T, D), F32),
                   jax.ShapeDtypeStruct((1, D), F32), jax.ShapeDtypeStruct((1, D), F32)],
        compiler_params=_params("arbitrary"))(y, x, target, gain)


def _prenorm_bwd(x, dh, dout, gain):
    T, D = x.shape
    tm = _pick(T, 256)

    def body(x_ref, dh_ref, dout_ref, g_ref, gx_ref, gpre_ref):
        xv = x_ref[...]
        dhv = dh_ref[...]
        r = lax.rsqrt(jnp.mean(xv * xv, axis=-1, keepdims=True) + NORM_EPS)
        xn = xv * r
        gx_ref[...] = dout_ref[...] + _norm_bwd(dhv * g_ref[...], xn, r)

        @pl.when(pl.program_id(0) == 0)
        def _():
            gpre_ref[...] = jnp.zeros_like(gpre_ref)

        gpre_ref[...] += jnp.sum(dhv * xn, axis=0, keepdims=True)

    row = pl.BlockSpec((tm, D), lambda i: (i, 0))
    vec = pl.BlockSpec((1, D), lambda i: (0, 0))
    return pl.pallas_call(
        body, name="prenorm_bwd", grid=(T // tm,), in_specs=[row, row, row, vec], out_specs=[row, vec],
        out_shape=[jax.ShapeDtypeStruct((T, D), F32), jax.ShapeDtypeStruct((1, D), F32)],
        compiler_params=_params("arbitrary"))(x, dh, dout, gain)


def _attn_masks(n):
    row = lax.broadcasted_iota(jnp.int32, (2 * WINDOW, 2 * WINDOW), 0) % WINDOW
    col = lax.broadcasted_iota(jnp.int32, (2 * WINDOW, 2 * WINDOW), 1)
    valid = (col > row) & (col <= row + WINDOW) & ((n > 0) | (col >= WINDOW))
    low = lax.broadcasted_iota(jnp.int32, (1, LANES), 1) < ATTN_HEAD
    top = lax.broadcasted_iota(jnp.int32, (2 * WINDOW, 1), 0) < WINDOW
    return valid, low, top


def _dup_half(pair, keep):
    return jnp.where(keep, pair, pltpu.roll(pair, ATTN_HEAD, 1))


def _fold_half(v):
    return v + pltpu.roll(v, ATTN_HEAD, 1)


def _attn_scores(qpair, k2, low):
    q2 = jnp.concatenate([jnp.where(low, qpair, 0.0), jnp.where(low, 0.0, qpair)], axis=0).astype(BF16)
    return q2, _dot(q2, k2, NT_DIMS)


def _attn_softmax(raw, sink_lo, sink_hi, valid, top):
    s = jnp.where(valid, raw * (ATTN_HEAD ** -0.5), -jnp.inf)
    sink = jnp.where(top, sink_lo, sink_hi)
    m = jnp.maximum(jnp.max(s, axis=-1, keepdims=True), sink)
    p = jnp.exp(s - m)
    psink = jnp.exp(sink - m)
    inv = 1.0 / (jnp.sum(p, axis=-1, keepdims=True) + psink)
    return p * inv, psink * inv


def _attn_specs(D):
    kb = D // LANES
    vb = kb + D // (8 * LANES)
    gb = (D + D // 4) // 512
    wide = lambda off: [pl.BlockSpec((WINDOW, 512), functools.partial(lambda o, e, jp, n: (n, o + 2 * jp + e), off, e))
                        for e in (0, 1)]
    cur = lambda off: pl.BlockSpec((WINDOW, LANES), functools.partial(lambda o, jp, n: (n, o + jp), off))
    prev = lambda off: pl.BlockSpec((WINDOW, LANES),
                                    functools.partial(lambda o, jp, n: (jnp.maximum(n - 1, 0), o + jp), off))
    return wide(0) + [cur(kb), prev(kb), cur(vb), prev(vb)] + wide(gb)


def _attn_fwd(proj, sinks, D):
    T = proj.shape[0]
    nb, njp = T // WINDOW, D // 1024

    def body(sink_ref, qlo_ref, qhi_ref, kc_ref, kp_ref, vc_ref, vp_ref, glo_ref, ghi_ref, mix_ref, mixt_ref):
        jp, n = pl.program_id(0), pl.program_id(1)
        valid, low, top = _attn_masks(n)
        kk = jnp.concatenate([kp_ref[...], kc_ref[...]], axis=0)
        vv = jnp.concatenate([vp_ref[...], vc_ref[...]], axis=0)
        pairs = range(4)
        for hj, (q_ref, g_ref) in enumerate(((qlo_ref, glo_ref), (qhi_ref, ghi_ref))):
            keep = low if hj == 0 else jnp.logical_not(low)
            k2 = _dup_half(kk, keep).astype(BF16)
            v2 = _dup_half(vv, keep).astype(BF16)
            cols = [slice(LANES * p, LANES * (p + 1)) for p in pairs]
            scores = [_attn_scores(q_ref[:, cols[p]], k2, low)[1] for p in pairs]
            probs = []
            for p in pairs:
                head = (2 * jp + hj) * GQA + 2 * p
                probs.append(_attn_softmax(scores[p], sink_ref[0, head], sink_ref[0, head + 1], valid, top)[0])
            o2s = [_dot(probs[p].astype(BF16), v2, NN_DIMS) for p in pairs]
            for p in pairs:
                opair = jnp.where(low, o2s[p][:WINDOW], o2s[p][WINDOW:])
                g = g_ref[:, cols[p]]
                out = opair * (g * _sigmoid(g))
                oc = slice(512 * hj + LANES * p, 512 * hj + LANES * (p + 1))
                mix_ref[:, oc] = out.astype(BF16)
                mixt_ref[oc, :] = out.T.astype(BF16)

    return pl.pallas_call(
        body, name="attn_fwd", grid=(njp, nb),
        in_specs=[pl.BlockSpec(memory_space=pltpu.SMEM)] + _attn_specs(D),
        out_specs=[pl.BlockSpec((WINDOW, 1024), lambda jp, n: (n, jp)),
                   pl.BlockSpec((1024, WINDOW), lambda jp, n: (jp, n))],
        out_shape=[jax.ShapeDtypeStruct((T, 2 * D), BF16), jax.ShapeDtypeStruct((2 * D, T), BF16)],
        compiler_params=_params("parallel", "parallel"))(sinks, *([proj] * 8))


def _attn_bwd(proj, sinks, dmix, D, after=()):
    T = proj.shape[0]
    nb, njp = T // WINDOW, D // 1024
    n_after = len(after)

    def body(sink_ref, qlo_ref, qhi_ref, kc_ref, kp_ref, vc_ref, vp_ref, glo_ref, ghi_ref, dmix_ref, *rest):
        dq_ref, dk_ref, dv_ref, dg_ref, dsink_ref, kcarry_ref, vcarry_ref = rest[n_after:]
        jp, step = pl.program_id(0), pl.program_id(1)
        n = nb - 1 - step
        valid, low, top = _attn_masks(n)
        lane = lax.broadcasted_iota(jnp.int32, (1, LANES), 1)
        kk = jnp.concatenate([kp_ref[...], kc_ref[...]], axis=0)
        vv = jnp.concatenate([vp_ref[...], vc_ref[...]], axis=0)

        @pl.when(step == 0)
        def _():
            kcarry_ref[...] = jnp.zeros_like(kcarry_ref)
            vcarry_ref[...] = jnp.zeros_like(vcarry_ref)
            dsink_ref[...] = jnp.zeros_like(dsink_ref)

        dk_pair = jnp.zeros((2 * WINDOW, LANES), F32)
        dv_pair = jnp.zeros((2 * WINDOW, LANES), F32)
        dsink = jnp.zeros((1, LANES), F32)
        for hj, (q_ref, g_ref) in enumerate(((qlo_ref, glo_ref), (qhi_ref, ghi_ref))):
            keep = low if hj == 0 else jnp.logical_not(low)
            k2 = _dup_half(kk, keep).astype(BF16)
            v2 = _dup_half(vv, keep).astype(BF16)
            pairs = range(4)
            cols = [slice(LANES * p, LANES * (p + 1)) for p in pairs]
            ocs = [slice(512 * hj + LANES * p, 512 * hj + LANES * (p + 1)) for p in pairs]
            q2s, scores = zip(*[_attn_scores(q_ref[:, cols[p]], k2, low) for p in pairs])
            probs, psinks, do2s = [], [], []
            for p in pairs:
                head = (2 * jp + hj) * GQA + 2 * p
                pr, ps = _attn_softmax(scores[p], sink_ref[0, head], sink_ref[0, head + 1], valid, top)
                probs.append(pr)
                psinks.append(ps)
                g = g_ref[:, cols[p]]
                do = dmix_ref[:, ocs[p]] * (g * _sigmoid(g))
                do2s.append(jnp.concatenate([jnp.where(low, do, 0.0), jnp.where(low, 0.0, do)], axis=0).astype(BF16))
            pbs = [pr.astype(BF16) for pr in probs]
            o2s = [_dot(pbs[p], v2, NN_DIMS) for p in pairs]
            dps = [_dot(do2s[p], v2, NT_DIMS) for p in pairs]
            dss = []
            for p in pairs:
                opair = jnp.where(low, o2s[p][:WINDOW], o2s[p][WINDOW:])
                g = g_ref[:, cols[p]]
                sg = _sigmoid(g)
                dg_ref[:, ocs[p]] = (dmix_ref[:, ocs[p]] * opair * (sg * (1.0 + g * (1.0 - sg)))).astype(BF16)
                delta = jnp.sum(probs[p] * dps[p], axis=-1, keepdims=True)
                dss.append((probs[p] * (dps[p] - delta) * (ATTN_HEAD ** -0.5)).astype(BF16))
                ps = psinks[p] * delta
                local = hj * GQA + 2 * p
                dsink -= jnp.where(lane == local, jnp.sum(ps[:WINDOW], axis=0, keepdims=True), 0.0)
                dsink -= jnp.where(lane == local + 1, jnp.sum(ps[WINDOW:], axis=0, keepdims=True), 0.0)
            dq2s = [_dot(dss[p], k2, NN_DIMS) for p in pairs]
            dk_head = sum(_dot(dss[p], q2s[p], TN_DIMS) for p in pairs)
            dv_head = sum(_dot(pbs[p], do2s[p], TN_DIMS) for p in pairs)
            for p in pairs:
                dq_ref[:, ocs[p]] = jnp.where(low, dq2s[p][:WINDOW], dq2s[p][WINDOW:]).astype(BF16)
            dk_pair += jnp.where(keep, _fold_half(dk_head), 0.0)
            dv_pair += jnp.where(keep, _fold_half(dv_head), 0.0)
        dk_ref[...] = (dk_pair[WINDOW:] + kcarry_ref[...]).astype(BF16)
        dv_ref[...] = (dv_pair[WINDOW:] + vcarry_ref[...]).astype(BF16)
        kcarry_ref[...] = dk_pair[:WINDOW]
        vcarry_ref[...] = dv_pair[:WINDOW]
        dsink_ref[...] += dsink

    rev = lambda spec: pl.BlockSpec(spec.block_shape, functools.partial(
        lambda f, jp, s: f(jp, nb - 1 - s), spec.index_map))
    in_specs = [rev(s) for s in _attn_specs(D)]
    wide_out = pl.BlockSpec((WINDOW, 1024), lambda jp, s: (nb - 1 - s, jp))
    pair_out = pl.BlockSpec((WINDOW, LANES), lambda jp, s: (nb - 1 - s, jp))
    return pl.pallas_call(
        body, name="attn_bwd", grid=(njp, nb),
        in_specs=[pl.BlockSpec(memory_space=pltpu.SMEM)] + in_specs + [wide_out] + [ANY_SPEC] * n_after,
        out_specs=[wide_out, pair_out, pair_out, wide_out, pl.BlockSpec((None, 1, LANES), lambda jp, s: (jp, 0, 0))],
        out_shape=[jax.ShapeDtypeStruct((T, D), BF16), jax.ShapeDtypeStruct((T, D // 8), BF16),
                   jax.ShapeDtypeStruct((T, D // 8), BF16), jax.ShapeDtypeStruct((T, D), BF16),
                   jax.ShapeDtypeStruct((njp, 1, LANES), F32)],
        scratch_shapes=[pltpu.VMEM((WINDOW, LANES), F32), pltpu.VMEM((WINDOW, LANES), F32)],
        compiler_params=_params("parallel", "arbitrary"))(sinks, *([proj] * 8), dmix, *after)


RNN_TB = 512
RNN_HB = 8
RNN_WIDE = RNN_HB * RNN_HEAD


def _split3(v):
    a = v.astype(BF16)
    r = v - a.astype(F32)
    b = r.astype(BF16)
    c = (r - b.astype(F32)).astype(BF16)
    return a, b, c


def _tri_sum(tri, v):
    a, b, c = _split3(v)
    return _dot(tri, a, NN_DIMS) + _dot(tri, b, NN_DIMS) + _dot(tri, c, NN_DIMS)


def _lower_bound(lb_ref):
    l0, l1 = lb_ref[0:1, :], lb_ref[1:2, :]
    m = jnp.maximum(l0, l1)
    e0, e1 = jnp.exp(l0 - m), jnp.exp(l1 - m)
    return e0 / (e0 + e1)


def _rnn_gates(rq, rf, lb):
    sq = _sigmoid(rq)
    sf = _sigmoid(rf)
    f = lb + (1.0 - lb) * sf
    return sq, sf, f


def _rnn_decays(g, tri):
    return _rnn_factors(_tri_sum(tri, g))


def _rnn_factors(G):
    last = G[CHUNK - 1:CHUNK, :]
    mid = G[CHUNK // 2 - 1:CHUNK // 2, :]
    return G, jnp.exp(G), jnp.exp(G - mid), jnp.exp(mid - G), jnp.exp(last - G), jnp.exp(last)


def _chunk_masks():
    r = lax.broadcasted_iota(jnp.int32, (CHUNK, CHUNK), 0)
    c = lax.broadcasted_iota(jnp.int32, (CHUNK, CHUNK), 1)
    return r >= c, (r >= c).astype(BF16), (r <= c).astype(BF16)


def _rnn_specs(T, D, tb, rev):
    nt = T // tb
    base = (2 * D + D // 4) // LANES
    t_of = (lambda s: nt - 1 - s) if rev else (lambda s: s)
    assert base % RNN_HB == 0 and (D // LANES) % RNN_HB == 0
    cols = [pl.BlockSpec((tb, RNN_WIDE), functools.partial(lambda o, h, s: (t_of(s), o + h),
                                                            (base + i * (D // LANES)) // RNN_HB))
            for i in range(4)]
    return cols, t_of


def _rnn_fwd(proj, lb_logits, rnn_norm, mixed, mixed_t, D):
    T = proj.shape[0]
    tb = _pick(T, RNN_TB)
    nt, nh, cpb = T // tb, D // RNN_HEAD, tb // CHUNK
    cols, _ = _rnn_specs(T, D, tb, False)

    def body(rq_ref, rf_ref, ri_ref, rg_ref, lb_ref, gain_ref, mix_in, mixt_in,
             mix_ref, mixt_ref, o_ref, st_ref, state_ref):
        del mix_in, mixt_in
        causal, tri, _ = _chunk_masks()
        lb = _lower_bound(lb_ref)

        @pl.when(pl.program_id(1) == 0)
        def _():
            state_ref[...] = jnp.zeros_like(state_ref)

        def chunk(c, carry):
            rows = pl.ds(pl.multiple_of(c * CHUNK, CHUNK), CHUNK)
            heads = range(RNN_HB)
            lns = [slice(RNN_HEAD * hh, RNN_HEAD * (hh + 1)) for hh in heads]
            qs, ks, Gs = [], [], []
            for ln in lns:
                rq, rf = rq_ref[rows, ln], rf_ref[rows, ln]
                sq, _, f = _rnn_gates(rq, rf, lb[:, ln])
                qs.append(rq * sq)
                ks.append(1.0 - f)
                Gs.append(_tri_sum(tri, jnp.log(f)))
            atts, inters, vbs = [], [], []
            for hh, ln in enumerate(lns):
                _, eG, eq, ek, ekl, elast = _rnn_factors(Gs[hh])
                q, k = qs[hh], ks[hh]
                st = state_ref[hh]
                st_ref[hh, c] = st
                vb = ri_ref[rows, ln].astype(BF16)
                vbs.append(vb)
                atts.append(_dot((q * eq).astype(BF16), (k * ek).astype(BF16), NT_DIMS))
                inters.append(_dot((q * eG).astype(BF16), st.astype(BF16), NT_DIMS))
                state_ref[hh] = st * elast + _dot(vb, (k * ekl).astype(BF16), TN_DIMS)
            intras = [_dot(jnp.where(causal, atts[hh], 0.0).astype(BF16), vbs[hh], NN_DIMS) for hh in heads]
            for hh, ln in enumerate(lns):
                o = inters[hh] + intras[hh]
                o_ref[rows, ln] = o
                rg = rg_ref[rows, ln]
                r = lax.rsqrt(jnp.mean(o * o, axis=-1, keepdims=True) + NORM_EPS)
                out = (o * r * gain_ref[:, ln]) * (rg * _sigmoid(rg))
                mix_ref[rows, ln] = out.astype(BF16)
            return carry

        lax.fori_loop(0, cpb, chunk, 0)
        mixt_ref[...] = mix_ref[...].astype(F32).T.astype(BF16)

    nh //= RNN_HB
    vec2 = pl.BlockSpec((2, RNN_WIDE), lambda h, s: (0, h))
    vec1 = pl.BlockSpec((1, RNN_WIDE), lambda h, s: (0, h))
    return pl.pallas_call(
        body, name="rnn_fwd", grid=(nh, nt),
        in_specs=cols + [vec2, vec1, ANY_SPEC, ANY_SPEC],
        out_specs=[pl.BlockSpec((tb, RNN_WIDE), lambda h, s: (s, D // RNN_WIDE + h)),
                   pl.BlockSpec((RNN_WIDE, tb), lambda h, s: (D // RNN_WIDE + h, s)),
                   pl.BlockSpec((tb, RNN_WIDE), lambda h, s: (s, h)),
                   pl.BlockSpec((RNN_HB, cpb, RNN_HEAD, RNN_HEAD), lambda h, s: (h, s, 0, 0))],
        out_shape=[jax.ShapeDtypeStruct(mixed.shape, BF16), jax.ShapeDtypeStruct(mixed_t.shape, BF16),
                   jax.ShapeDtypeStruct((T, D), F32),
                   jax.ShapeDtypeStruct((nh * RNN_HB, T // CHUNK, RNN_HEAD, RNN_HEAD), F32)],
        scratch_shapes=[pltpu.VMEM((RNN_HB, RNN_HEAD, RNN_HEAD), F32)],
        input_output_aliases={6: 0, 7: 1},
        compiler_params=_params("parallel", "arbitrary"))(proj, proj, proj, proj, lb_logits, rnn_norm, mixed, mixed_t)


def _rnn_bwd(proj, lb_logits, rnn_norm, o_raw, states, dmix, D):
    T = proj.shape[0]
    tb = _pick(T, RNN_TB)
    nt, nh, cpb = T // tb, D // RNN_HEAD, tb // CHUNK
    cols, t_of = _rnn_specs(T, D, tb, True)

    def body(rq_ref, rf_ref, ri_ref, rg_ref, lb_ref, gain_ref, o_ref, st_ref, dmix_ref,
             drq_ref, drf_ref, dri_ref, drg_ref, dlb_ref, dgain_ref, dstate_ref):
        causal, tri, tri_t = _chunk_masks()
        lb = _lower_bound(lb_ref)
        gain = gain_ref[...]

        @pl.when(pl.program_id(1) == 0)
        def _():
            dstate_ref[...] = jnp.zeros_like(dstate_ref)
            dlb_ref[...] = jnp.zeros_like(dlb_ref)
            dgain_ref[...] = jnp.zeros_like(dgain_ref)

        def chunk(i, carry):
            c = cpb - 1 - i
            rows = pl.ds(pl.multiple_of(c * CHUNK, CHUNK), CHUNK)
            heads = range(RNN_HB)
            lns = [slice(RNN_HEAD * hh, RNN_HEAD * (hh + 1)) for hh in heads]
            last_row = lax.broadcasted_iota(jnp.int32, (CHUNK, 1), 0) == CHUNK - 1
            A = []
            for ln in lns:
                rq, rf, rg = rq_ref[rows, ln], rf_ref[rows, ln], rg_ref[rows, ln]
                o, dgated = o_ref[rows, ln], dmix_ref[rows, ln]
                gainh = gain[:, ln]
                sgt = _sigmoid(rg)
                r = lax.rsqrt(jnp.mean(o * o, axis=-1, keepdims=True) + NORM_EPS)
                on = o * r
                drg_ref[rows, ln] = (dgated * (on * gainh) * (sgt * (1.0 + rg * (1.0 - sgt)))).astype(BF16)
                d_on = dgated * (rg * sgt)
                dgain_ref[:, ln] += jnp.sum(d_on * on, axis=0, keepdims=True)
                dob = _norm_bwd(d_on * gainh, on, r).astype(BF16)
                sq, sf, f = _rnn_gates(rq, rf, lb[:, ln])
                A.append(dict(rq=rq, sq=sq, sf=sf, f=f, dob=dob, G=_tri_sum(tri, jnp.log(f))))
            for hh, ln in enumerate(lns):
                a = A[hh]
                _, eG, eq, ek, ekl, elast = _rnn_factors(a.pop("G"))
                q, k = a["rq"] * a["sq"], 1.0 - a["f"]
                st, dst = st_ref[hh, c], dstate_ref[hh]
                qg, kl = q * eG, k * ekl
                qmb, kmb = (q * eq).astype(BF16), (k * ek).astype(BF16)
                dob, vb, dstb = a["dob"], ri_ref[rows, ln].astype(BF16), dst.astype(BF16)
                a.update(eG=eG, eq=eq, ek=ek, ekl=ekl, qg=qg, kl=kl, qmb=qmb, kmb=kmb,
                         att=_dot(qmb, kmb, NT_DIMS), datt=_dot(dob, vb, NT_DIMS),
                         dqg=_dot(dob, st.astype(BF16), NN_DIMS), dkl=_dot(vb, dstb, NN_DIMS),
                         dri=_dot(kl.astype(BF16), dstb, NT_DIMS),
                         dlast=jnp.sum(dst * st, axis=0, keepdims=True) * elast)
                dstate_ref[hh] = dst * elast + _dot(dob, qg.astype(BF16), TN_DIMS)
            for hh, ln in enumerate(lns):
                a = A[hh]
                att = jnp.where(causal, a.pop("att"), 0.0).astype(BF16)
                datt = jnp.where(causal, a.pop("datt"), 0.0).astype(BF16)
                dqm = _dot(datt, a["kmb"], NN_DIMS)
                dkm = _dot(datt, a["qmb"], TN_DIMS)
                dri_ref[rows, ln] = (_dot(att, a["dob"], TN_DIMS) + a.pop("dri")).astype(BF16)
                dqg, dkl, kl = a.pop("dqg"), a.pop("dkl"), a.pop("kl")
                a["dq"] = dqg * a.pop("eG") + dqm * a.pop("eq")
                a["dk"] = dkm * a.pop("ek") + dkl * a.pop("ekl")
                dG = dqg * a.pop("qg") + dqm * a.pop("qmb").astype(F32) - dkm * a.pop("kmb").astype(F32) - dkl * kl
                dlast = jnp.sum(dkl * kl, axis=0, keepdims=True) + a.pop("dlast")
                a["dg"] = _tri_sum(tri_t, dG + jnp.where(last_row, dlast, 0.0))
            for hh, ln in enumerate(lns):
                a = A[hh]
                rq, sq, sf = a["rq"], a["sq"], a["sf"]
                df = a["dg"] / a["f"] - a["dk"]
                drq_ref[rows, ln] = (a["dq"] * (sq * (1.0 + rq * (1.0 - sq)))).astype(BF16)
                drf_ref[rows, ln] = (df * (1.0 - lb[:, ln]) * (sf * (1.0 - sf))).astype(BF16)
                dlb_ref[:, ln] += jnp.sum(df * (1.0 - sf), axis=0, keepdims=True)
            return carry

        lax.fori_loop(0, cpb, chunk, 0)

    nh //= RNN_HB
    vec2 = pl.BlockSpec((2, RNN_WIDE), lambda h, s: (0, h))
    vec1 = pl.BlockSpec((1, RNN_WIDE), lambda h, s: (0, h))
    blk = pl.BlockSpec((tb, RNN_WIDE), lambda h, s: (t_of(s), h))
    return pl.pallas_call(
        body, name="rnn_bwd", grid=(nh, nt),
        in_specs=cols + [vec2, vec1, blk,
                         pl.BlockSpec((RNN_HB, cpb, RNN_HEAD, RNN_HEAD), lambda h, s: (h, t_of(s), 0, 0)),
                         pl.BlockSpec((tb, RNN_WIDE), lambda h, s: (t_of(s), D // RNN_WIDE + h))],
        out_specs=[blk, blk, blk, blk, vec1, vec1],
        out_shape=[jax.ShapeDtypeStruct((T, D), BF16)] * 4 + [jax.ShapeDtypeStruct((1, D), F32)] * 2,
        scratch_shapes=[pltpu.VMEM((RNN_HB, RNN_HEAD, RNN_HEAD), F32)],
        compiler_params=_params("parallel", "arbitrary"))(proj, proj, proj, proj, lb_logits, rnn_norm, o_raw, states, dmix)


def _adamw(w, g, m, v):
    m = ADAM_B1 * m + (1.0 - ADAM_B1) * g
    v = ADAM_B2 * v + (1.0 - ADAM_B2) * (g * g)
    m_hat = m / (1.0 - ADAM_B1 ** ADAM_STEP)
    v_hat = v / (1.0 - ADAM_B2 ** ADAM_STEP)
    delta = -ADAM_LR * (m_hat / (jnp.sqrt(v_hat) + ADAM_EPS) + ADAM_WD * w)
    return delta, m, v


def _adamw_big(w, m, v, parts, tr, name):
    R, C = w.shape
    n_parts = len(parts)

    def body(w_ref, m_ref, v_ref, *rest):
        part_refs = rest[:n_parts]
        g_ref, d_ref, nm_ref, nv_ref = rest[n_parts:]
        g = part_refs[0][...].astype(F32)
        for p_ref in part_refs[1:]:
            g = g + p_ref[...].astype(F32)
        delta, nm, nv = _adamw(w_ref[...], g, m_ref[...], v_ref[...])
        g_ref[...] = g
        d_ref[...] = delta
        nm_ref[...] = nm
        nv_ref[...] = nv

    blk = pl.BlockSpec((tr, C), lambda i: (i, 0))
    part_specs = [pl.BlockSpec((None, tr, C), functools.partial(lambda s, i: (s, i, 0), slot)) for _, slot in parts]
    return pl.pallas_call(
        body, name=name, grid=(R // tr,), in_specs=[blk, blk, blk] + part_specs,
        out_specs=[blk] * 4, out_shape=[jax.ShapeDtypeStruct((R, C), F32)] * 4,
        compiler_params=_params("parallel"))(w, m, v, *[a for a, _ in parts])


def _pair_sum(pa, qa):
    n, R, C = qa.shape
    tr = _pick(R, 256)

    def body(p_ref, q_ref, r_ref):
        r_ref[...] = (p_ref[...].astype(F32) + q_ref[...].astype(F32)).astype(BF16)

    return pl.pallas_call(
        body, name="pair_sum", grid=(n, R // tr),
        in_specs=[pl.BlockSpec((None, tr, C), lambda j, i: (2 * j, i, 0)), pl.BlockSpec((None, tr, C), lambda j, i: (j, i, 0))],
        out_specs=pl.BlockSpec((None, tr, C), lambda j, i: (j, i, 0)),
        out_shape=jax.ShapeDtypeStruct((n, R, C), BF16), compiler_params=_params("parallel", "parallel"))(pa, qa)


def _adamw_small(total, sinks, lb_logits, rnn_norm, pre_norm, post_norm, moments, D):
    params = [sinks, lb_logits, rnn_norm, pre_norm, post_norm]

    def body(tot_ref, *refs):
        p_refs, m_refs, v_refs = refs[0:5], refs[5:10], refs[10:15]
        loss_ref, outs = refs[15], refs[16:]
        tot = tot_ref[...]
        l0, l1 = p_refs[1][0:1, :], p_refs[1][1:2, :]
        mx = jnp.maximum(l0, l1)
        e0, e1 = jnp.exp(l0 - mx), jnp.exp(l1 - mx)
        p0, p1 = e0 / (e0 + e1), e1 / (e0 + e1)
        dlb = tot[0:1, :]
        grads = [tot[5:6, 0:LANES], jnp.concatenate([dlb * p0 * (1.0 - p0), -dlb * p0 * p1], axis=0),
                 tot[1:2, :], tot[2:3, :], tot[3:4, :]]
        loss_ref[...] = 0.5 / D * jnp.sum(tot[4:5, :], axis=-1, keepdims=True)
        for i, g in enumerate(grads):
            delta, nm, nv = _adamw(p_refs[i][...], g, m_refs[i][...], v_refs[i][...])
            outs[4 * i][...] = g
            outs[4 * i + 1][...] = delta
            outs[4 * i + 2][...] = nm
            outs[4 * i + 3][...] = nv

    out_shape = [jax.ShapeDtypeStruct((1, 1), F32)]
    for p in params:
        out_shape += [jax.ShapeDtypeStruct(p.shape, F32)] * 4
    return pl.pallas_call(body, name="adamw_small", out_shape=out_shape)(total, *params, *moments)


SIBLING = 1


def _me():
    return lax.axis_index("x"), lax.axis_index("y"), lax.axis_index("c")


def _flat(px, py, pc):
    return 4 * px + 2 * py + pc


def _role_peer(role, x, y, c):
    if not isinstance(role, int):
        j = (role - 2) // 2
        px = jnp.where((role >= 2) & (j != 1), 1 - x, x)
        py = jnp.where((role >= 2) & (j != 0), 1 - y, y)
        return (px, py, jnp.where(role % 2 == 1, 1 - c, c))
    if role < 2:
        return (x, y, (1 - c) if role else c)
    j, other = (role - 2) // 2, (role - 2) % 2
    px = (1 - x) if j in (0, 2) else x
    py = (1 - y) if j in (1, 2) else y
    return (px, py, (1 - c) if other else c)


def _wave1(x, y, c):
    return jnp.where(c == 1, 2, 4)


def _wave2(x, y, c):
    return jnp.where(c == 1, 4, 2)


def _role_ids():
    x, y, c = _me()
    w1, w2 = _wave1(x, y, c), _wave2(x, y, c)
    order = [0, 1, w1, w2 + 1, w2, w1 + 1, 6, 7]
    return jnp.stack([_flat(*_role_peer(r, x, y, c)) for r in range(N_DEV)] + order).astype(jnp.int32)


def _comm_call(name, bufs, waits=(), starts=(), after=()):
    n_buf, n_wait, n_start, n_after = len(bufs), len(waits), len(starts), len(after)

    def body(*refs):
        buf_refs = refs[:n_buf]
        sem_refs = refs[n_buf:n_buf + 2 * n_wait]
        outs = refs[n_buf + 2 * n_wait + n_after:]
        new_sems, token = outs[:2 * n_start], outs[-1]
        x, y, c = _me()

        def val(v):
            return v(x, y, c) if callable(v) else v

        def block(ref, slot, rows):
            if rows is None:
                return ref.at[val(slot)]
            first = rows[0] if isinstance(rows[0], int) else pl.multiple_of(val(rows[0]), 16)
            return ref.at[val(slot), pl.ds(first, rows[1])]

        for w, (_, kind, like, *rows) in enumerate(waits):
            shape_ref = block(buf_refs[like], 0, (0, rows[0][1]) if rows else None)
            cp = pltpu.make_async_remote_copy(
                src_ref=shape_ref, dst_ref=shape_ref, send_sem=sem_refs[2 * w], recv_sem=sem_refs[2 * w + 1],
                device_id=(x, y, c), device_id_type=MESH)
            if kind == "send":
                cp.wait_send()
            else:
                cp.wait_recv()
        for s, (sb, ss, db, ds, role, *rows) in enumerate(starts):
            rows = rows[0] if rows else None
            pltpu.make_async_remote_copy(
                src_ref=block(buf_refs[sb], ss, rows), dst_ref=block(buf_refs[db], ds, rows), send_sem=new_sems[2 * s],
                recv_sem=new_sems[2 * s + 1], device_id=_role_peer(val(role), x, y, c), device_id_type=MESH).start()
        token[...] = jnp.zeros_like(token)

    sems = [s for flight, *_ in waits for s in flight]
    out = pl.pallas_call(
        body, name=name,
        out_shape=tuple([pltpu.SemaphoreType.DMA(())] * (2 * n_start) + [pltpu.HBM(b.shape, b.dtype) for b in bufs]
                        + [jax.ShapeDtypeStruct((8, LANES), F32)]),
        in_specs=tuple([HBM_SPEC] * n_buf + [SEM_SPEC] * (2 * n_wait) + [ANY_SPEC] * n_after),
        out_specs=tuple([SEM_SPEC] * (2 * n_start) + [HBM_SPEC] * n_buf + [pl.BlockSpec(memory_space=pltpu.VMEM)]),
        input_output_aliases={i: 2 * n_start + i for i in range(n_buf)},
        compiler_params=pltpu.CompilerParams(has_side_effects=EFFECT),
    )(*[pltpu.with_memory_space_constraint(b, pltpu.HBM) for b in bufs], *sems, *after)
    flights = [(out[2 * s], out[2 * s + 1]) for s in range(n_start)]
    return list(out[2 * n_start:2 * n_start + n_buf]), flights, out[-1]


def _landing(shape):
    return lax.empty(shape, BF16)


def _both(flights, like):
    return [(f, kind, like) for f in flights for kind in ("send", "recv")]


def _allreduce_small(rows, D):
    n_rows = len(rows)

    def body(*refs):
        in_refs, out_ref, all_ref = refs[:n_rows], refs[n_rows], refs[n_rows + 1]
        send_sems, recv_sems = refs[n_rows + 2], refs[n_rows + 3]
        x, y, c = _me()
        mine = all_ref.at[_flat(x, y, c)]
        mine[...] = jnp.zeros((8, D), F32)
        for i, r in enumerate(in_refs):
            mine[i:i + 1, :] = r[...]
        copies = []
        for k in range(N_DEV - 1):
            copies.append(pltpu.make_async_remote_copy(
                src_ref=mine, dst_ref=mine, send_sem=send_sems.at[k], recv_sem=recv_sems.at[k],
                device_id=_role_peer(k + 1, x, y, c), device_id_type=MESH))
        for cp in copies:
            cp.start()
        for cp in copies:
            cp.wait_recv()
        for cp in copies:
            cp.wait_send()
        tot = all_ref[0]
        for d in range(1, N_DEV):
            tot = tot + all_ref[d]
        out_ref[...] = tot

    vm = pl.BlockSpec(memory_space=pltpu.VMEM)
    return pl.pallas_call(
        body, name="allreduce_small", in_specs=[vm] * n_rows, out_specs=vm,
        out_shape=jax.ShapeDtypeStruct((8, D), F32),
        scratch_shapes=[pltpu.VMEM((N_DEV, 8, D), F32), pltpu.SemaphoreType.DMA((7,)), pltpu.SemaphoreType.DMA((7,))],
    )(*rows)


def _step(ids, x, target, g_in, g_out, sinks, lb_logits, rnn_norm, pre_norm, post_norm, dist):
    T, D = x.shape
    n_roles, _, wd = g_in.shape
    ro = g_out.shape[1]
    assert n_roles % 2 == 0 and (not dist or n_roles == N_DEV)
    n_chips = n_roles // 2 - 1

    half = D // 2
    up = lambda f: (lambda x, y, c: f(x, y, c) + 1)
    rows_of = lambda f: (lambda x, y, c: jnp.where(f(x, y, c) == 2, 0, half))
    if dist:
        (g_in,), fl_in, _ = _comm_call(
            "gather_in_start", [g_in], starts=[(0, 0, 0, SIBLING, SIBLING), (0, 0, 0, _wave1, _wave1)])
    h, h_t = _prenorm(x, pre_norm)
    if dist:
        (g_in,), _, _ = _comm_call("gather_in_sibling", [g_in], waits=[(fl_in[0], "recv", 0)], after=(h,))
    proj = _proj_in(ids, h, g_in, 0, 2, None, "proj_in_0")
    if not dist:
        for p in range(2, n_roles, 2):
            proj = _proj_in(ids, h, g_in, p, 2, proj, f"proj_in_{p}")
    else:
        (g_in,), fl_a, _ = _comm_call(
            "gather_in_wave_1", [g_in], waits=[(fl_in[1], "recv", 0)],
            starts=[(0, 0, 0, _wave2, _wave2), (0, _wave1, 0, up(_wave1), SIBLING),
                    (0, _wave1, 0, 6, _wave2, (rows_of(_wave1), half))], after=(proj,))
        proj = _proj_in(ids, h, g_in, 2, 1, proj, "proj_in_2")
        (g_in,), _, _ = _comm_call("gather_in_passed_1", [g_in], waits=[(fl_a[1], "recv", 0)], after=(proj,))
        proj = _proj_in(ids, h, g_in, 3, 1, proj, "proj_in_3")
        (g_in,), fl_b, _ = _comm_call(
            "gather_in_wave_2", [g_in], waits=[(fl_a[0], "recv", 0)],
            starts=[(0, _wave2, 0, up(_wave2), SIBLING), (0, _wave2, 0, 6, _wave1, (rows_of(_wave2), half))],
            after=(proj,))
        proj = _proj_in(ids, h, g_in, 4, 1, proj, "proj_in_4")
        (g_in,), _, _ = _comm_call("gather_in_passed_2", [g_in], waits=[(fl_b[0], "recv", 0)], after=(proj,))
        proj = _proj_in(ids, h, g_in, 5, 1, proj, "proj_in_5")
        (g_in,), fl_c, _ = _comm_call(
            "gather_in_diagonal", [g_in], waits=[(fl_a[2], "recv", 0, (0, half)), (fl_b[1], "recv", 0, (0, half))],
            starts=[(0, 6, 0, 7, SIBLING)], after=(proj,))
        proj = _proj_in(ids, h, g_in, 6, 1, proj, "proj_in_6")
        (g_in, g_out), fl_out, token = _comm_call(
            "gather_out_start", [g_in, g_out], waits=[(fl_c[0], "recv", 0)],
            starts=[(1, 0, 1, SIBLING, SIBLING)] + [(1, 0, 1, 2 + 2 * i, 2 + 2 * i) for i in range(3)], after=(proj,))
        proj = _proj_in(ids, h, g_in, 7, 1, proj, "proj_in_7", (token,))
        send_waits = [(f, "send", 0) for f in fl_in + fl_a[:2] + fl_b[:1] + fl_c] \
            + [(f, "send", 0, (0, half)) for f in (fl_a[2], fl_b[1])]
    mixed, mixed_t = _attn_fwd(proj, sinks, D)
    if dist:
        (g_in, g_out), fl_out_fwd, _ = _comm_call(
            "gather_out_pass", [g_in, g_out],
            waits=send_waits + [(fl_out[1 + i], "recv", 1) for i in range(3)],
            starts=[(1, 2 + 2 * i, 1, 3 + 2 * i, SIBLING) for i in range(3)], after=(mixed,))
    mixed, mixed_t, o_raw, states = _rnn_fwd(proj, lb_logits, rnn_norm, mixed, mixed_t, D)
    if dist:
        (g_out,), _, _ = _comm_call(
            "gather_out_done", [g_out],
            waits=[(fl_out[0], "recv", 0)] + [(f, "recv", 0) for f in fl_out_fwd]
            + [(f, "send", 0) for f in fl_out + fl_out_fwd], after=(states,))
    y = _proj_out(ids, mixed, g_out)
    dy, dout, g_post, sq_err = _loss_head(y, x, target, post_norm)

    dmix = _dmixed(ids, dy, g_out)
    p_out, after = _dw_out(ids, mixed_t, dy, n_roles), ()
    if dist:
        (p_out, l_out), fl_so, token = _comm_call(
            "scatter_out_start", [p_out, _landing((n_roles - 1, ro, D))],
            starts=[(0, r, 1, r - 1, r) for r in range(1, n_roles)])
        after = (token,)
    d_aq, d_ak, d_av, d_ag, d_sink = _attn_bwd(proj, sinks, dmix, D, after)
    d_rq, d_rf, d_ri, d_rg, d_lb, g_rnn = _rnn_bwd(proj, lb_logits, rnn_norm, o_raw, states, dmix, D)
    dproj = jnp.concatenate([d_aq, d_ak, d_av, d_ag, d_rq, d_rf, d_ri, d_rg], axis=1)

    p_far = _dw_in(ids, h_t, dproj, n_roles, 2, n_roles - 2, "dw_in_far")
    if dist:
        (p_far, q_far), fl_pair, token = _comm_call(
            "scatter_in_pair_start", [p_far, _landing((3, D, wd))],
            starts=[(0, 1 + 2 * j, 1, j, SIBLING) for j in range(3)])
        p_near = _dw_in(ids, h_t, dproj, n_roles, 0, 2, "dw_in_near", after=(token,))
        (p_far, q_far), _, _ = _comm_call("scatter_in_pair_wait", [p_far, q_far], waits=_both(fl_pair, 1), after=(p_near,))
        chip_sum = _pair_sum(p_far, q_far)
        (chip_sum, z_far), fl_chip, token = _comm_call(
            "scatter_in_chip_start", [chip_sum, _landing((3, D, wd))],
            starts=[(0, j, 1, j, 2 + 2 * j) for j in range(3)])
        (p_near, q_near), fl_sib, token = _comm_call(
            "scatter_in_sibling_start", [p_near, _landing((1, D, wd))], starts=[(0, 1, 1, 0, SIBLING)], after=(token,))
        after = (token,)
    else:
        p_near, after = _dw_in(ids, h_t, dproj, n_roles, 0, 2, "dw_in_near"), ()
    dh = _dh(ids, dproj, g_in, after)
    grad_x, g_pre = _prenorm_bwd(x, dh, dout, pre_norm)
    n_q = D // ATTN_HEAD
    sink_row = jnp.pad(d_sink[:, 0, :2 * GQA].reshape(1, n_q), ((0, 0), (0, D - n_q)))
    rows = [d_lb, g_rnn, g_pre, g_post, sq_err, sink_row]
    if not dist:
        return grad_x, [(p_near, r) for r in range(2)] + [(p_far, r) for r in range(n_roles - 2)], \
            [(p_out, r) for r in range(n_roles)], rows

    (p_out, l_out), _, _ = _comm_call("scatter_out_wait", [p_out, l_out], waits=_both(fl_so, 1), after=(grad_x,))
    (chip_sum, z_far, p_near, q_near), _, _ = _comm_call(
        "scatter_in_wait", [chip_sum, z_far, p_near, q_near], waits=_both(fl_chip, 1) + _both(fl_sib, 3), after=(p_out,))
    parts_in = [(p_near, 0), (q_near, 0)] + [(z_far, j) for j in range(3)]
    parts_out = [(p_out, 0)] + [(l_out, k) for k in range(n_roles - 1)]
    return grad_x, parts_in, parts_out, rows


def kernel(x, w_in, attn_sinks, lb_logits, rnn_norm, w_out, pre_norm, post_norm, loss_target, m_w_in, m_attn_sinks, m_lb_logits, m_rnn_norm, m_w_out, m_pre_norm, m_post_norm, v_w_in, v_attn_sinks, v_lb_logits, v_rnn_norm, v_w_out, v_pre_norm, v_post_norm):
    _, T, D = x.shape
    ro = w_out.shape[1]
    n_q = attn_sinks.shape[1]
    assert lb_logits.shape[0] == 2 and n_q == D // ATTN_HEAD and n_q <= LANES

    grad_x, parts_in, parts_out, small_rows = _step(
        _role_ids(), x[0], loss_target[0], _cast_slot0(w_in[0], N_DEV, "cast_w_in"),
        _cast_slot0(w_out[0], N_DEV, "cast_w_out"), attn_sinks, lb_logits, rnn_norm, pre_norm, post_norm, True)

    g_wo, d_wo, nm_wo, nv_wo = _adamw_big(w_out[0], m_w_out[0], v_w_out[0], parts_out, _pick(ro, 64), "adamw_w_out")
    g_wi, d_wi, nm_wi, nv_wi = _adamw_big(w_in[0], m_w_in[0], v_w_in[0], parts_in, _pick(D, 128), "adamw_w_in")

    total = _allreduce_small(small_rows, D)
    pad = lambda a: jnp.pad(a, ((0, 0), (0, LANES - n_q)))
    moments = [pad(m_attn_sinks), m_lb_logits, m_rnn_norm, m_pre_norm, m_post_norm,
               pad(v_attn_sinks), v_lb_logits, v_rnn_norm, v_pre_norm, v_post_norm]
    res = _adamw_small(total, pad(attn_sinks), lb_logits, rnn_norm, pre_norm, post_norm, moments, D)
    loss = res[0][0, 0]
    small = [[res[1 + 4 * i + j] for i in range(5)] for j in range(4)]
    for j in range(4):
        small[j][0] = small[j][0][:, :n_q]

    def assemble(j, wi, wo):
        s = small[j]
        return [wi[None], s[0], s[1], s[2], wo[None], s[3], s[4]]

    return (loss, grad_x[None], *assemble(0, g_wi, g_wo), *assemble(1, d_wi, d_wo),
            *assemble(2, nm_wi, nm_wo), *assemble(3, nv_wi, nv_wo))
```

```python
import functools

import jax
import jax.numpy as jnp
from jax import lax
from jax.experimental import pallas as pl
from jax.experimental.pallas import tpu as pltpu

F32 = jnp.float32
BF16 = jnp.bfloat16

ATTN_HEAD = 64
GQA = 8
WINDOW = 128
RNN_HEAD = 128
CHUNK = 64
NORM_EPS = 1e-6
LANES = 128
N_DEV = 8

ADAM_LR = 0.001
ADAM_B1 = 0.9
ADAM_B2 = 0.999
ADAM_EPS = 1e-08
ADAM_WD = 0.01
ADAM_STEP = 10

VMEM_LIMIT = 56 * 1024 * 1024
MESH = pl.DeviceIdType.MESH
ANY_SPEC = pl.BlockSpec(memory_space=pl.ANY)
HBM_SPEC = pl.BlockSpec(memory_space=pltpu.HBM)
SEM_SPEC = pl.BlockSpec(memory_space=pltpu.SEMAPHORE)
EFFECT = pltpu.SideEffectType.DATAFLOW_SIDE_EFFECTING

NT_DIMS = (((1,), (1,)), ((), ()))
NN_DIMS = (((1,), (0,)), ((), ()))
TN_DIMS = (((0,), (0,)), ((), ()))


def _params(*sem):
    return pltpu.CompilerParams(dimension_semantics=sem, vmem_limit_bytes=VMEM_LIMIT)


def _dot(a, b, dims):
    return lax.dot_general(a, b, dims, preferred_element_type=F32)


def _sigmoid(v):
    return 1.0 / (1.0 + jnp.exp(-v))


def _pick(n, pref):
    t = min(n, pref)
    assert n % t == 0, (n, pref)
    return t


MXU_COLS = 256
WIDE_TILE = 5 * MXU_COLS


def _col_tiles(wd):
    wide = wd // WIDE_TILE
    rest = wd - wide * WIDE_TILE
    assert rest % LANES == 0 and (rest == 0 or (wide * WIDE_TILE) % rest == 0), wd
    return [(WIDE_TILE, 0, wide)] * (wide > 0) + [(rest, wide * WIDE_TILE, 1)] * (rest > 0)


def _matmul(ids, a, b, *, grid, a_spec, b_spec, o_spec, out_shape, trans_b, name, after=(), prev=None):
    nk = grid[2]
    dims = NT_DIMS if trans_b else NN_DIMS
    n_skip = len(after) + (prev is not None)

    def body(ids_ref, a_ref, b_ref, *rest):
        del ids_ref
        o_ref, scratch = rest[n_skip], rest[n_skip + 1:]
        prod = _dot(a_ref[...], b_ref[...], dims)
        if nk == 1:
            o_ref[...] = prod.astype(o_ref.dtype)
        else:
            acc_ref, = scratch
            k = pl.program_id(2)

            @pl.when(k == 0)
            def _():
                acc_ref[...] = prod

            @pl.when(k > 0)
            def _():
                acc_ref[...] += prod

            @pl.when(k == nk - 1)
            def _():
                o_ref[...] = acc_ref[...].astype(o_ref.dtype)

    scratch = [] if nk == 1 else [pltpu.VMEM(tuple(d for d in o_spec.block_shape if d is not None), F32)]
    extra = list(after) + ([prev] if prev is not None else [])
    aliases = {3 + len(after): 0} if prev is not None else {}
    return pl.pallas_call(
        body, name=name, out_shape=out_shape, input_output_aliases=aliases,
        grid_spec=pltpu.PrefetchScalarGridSpec(
            num_scalar_prefetch=1, grid=grid, in_specs=[a_spec, b_spec] + [ANY_SPEC] * len(extra),
            out_specs=o_spec, scratch_shapes=scratch),
        compiler_params=_params("parallel", "parallel", "arbitrary"),
    )(ids, a, b, *extra)


def _proj_in(ids, h, w_roles, pos, n, prev, name, after=()):
    T, D = h.shape
    n_roles, _, wd = w_roles.shape
    tm = _pick(T, 1024)
    for t, (tn, first, count) in enumerate(_col_tiles(wd)):
        prev = _matmul(
            ids, h, w_roles, grid=(T // tm, n * count, 1),
            a_spec=pl.BlockSpec((tm, D), lambda i, j, k, ids: (i, 0)),
            b_spec=pl.BlockSpec((None, D, tn), functools.partial(
                lambda tn, first, count, i, j, k, ids: (ids[n_roles + pos + j // count], 0, first // tn + j % count),
                tn, first, count)),
            o_spec=pl.BlockSpec((pl.Element(tm), pl.Element(tn)), functools.partial(
                lambda tn, first, count, i, j, k, ids: (
                    i * tm, pl.multiple_of(ids[ids[n_roles + pos + j // count]] * wd + first + (j % count) * tn, LANES)),
                tn, first, count)),
            out_shape=jax.ShapeDtypeStruct((T, n_roles * wd), F32), trans_b=False, name=f"{name}_{t}",
            after=after, prev=prev)
    return prev


def _proj_out(ids, mixed, wo_roles):
    T, E = mixed.shape
    n_roles, R, D = wo_roles.shape
    tm, tn = _pick(T, 512), _pick(D, 512)

    def body(ids_ref, a_ref, b_ref, o_ref, a_roles):
        @pl.when(pl.program_id(1) == 0)
        def _():
            for k in range(n_roles):
                col = pl.multiple_of(ids_ref[k] * R, R)
                a_roles[:, k * R:(k + 1) * R] = a_ref[:, pl.ds(col, R)]

        o_ref[...] = _dot(a_roles[...], b_ref[...].reshape(E, tn), NN_DIMS)

    return pl.pallas_call(
        body, name="proj_out", out_shape=jax.ShapeDtypeStruct((T, D), F32),
        grid_spec=pltpu.PrefetchScalarGridSpec(
            num_scalar_prefetch=1, grid=(T // tm, D // tn),
            in_specs=[pl.BlockSpec((tm, E), lambda i, j, ids: (i, 0)),
                      pl.BlockSpec((n_roles, R, tn), lambda i, j, ids: (0, 0, j))],
            out_specs=pl.BlockSpec((tm, tn), lambda i, j, ids: (i, j)),
            scratch_shapes=[pltpu.VMEM((tm, E), BF16)]),
        compiler_params=_params("parallel", "arbitrary"))(ids, mixed, wo_roles)


def _dmixed(ids, dy, wo_roles):
    T, D = dy.shape
    n_roles, R, _ = wo_roles.shape
    tm = _pick(T, 1024)
    return _matmul(
        ids, dy, wo_roles, grid=(T // tm, n_roles, 1),
        a_spec=pl.BlockSpec((tm, D), lambda i, j, k, ids: (i, 0)),
        b_spec=pl.BlockSpec((None, R, D), lambda i, j, k, ids: (j, 0, 0)),
        o_spec=pl.BlockSpec((tm, R), lambda i, j, k, ids: (i, ids[j])),
        out_shape=jax.ShapeDtypeStruct((T, n_roles * R), F32), trans_b=True, name="dmixed")


def _dw_out(ids, mixed_t, dy, n_roles):
    E, T = mixed_t.shape
    D = dy.shape[1]
    R = E // n_roles
    tn = _pick(D, 512)
    return _matmul(
        ids, mixed_t, dy, grid=(n_roles, D // tn, 1),
        a_spec=pl.BlockSpec((R, T), lambda i, j, k, ids: (ids[i], 0)),
        b_spec=pl.BlockSpec((T, tn), lambda i, j, k, ids: (0, j)),
        o_spec=pl.BlockSpec((None, R, tn), lambda i, j, k, ids: (i, 0, j)),
        out_shape=jax.ShapeDtypeStruct((n_roles, R, D), BF16), trans_b=False, name="dw_out")


def _dw_in(ids, h_t, dproj, n_roles, r0, nr, name, after=()):
    D, T = h_t.shape
    wd = dproj.shape[1] // n_roles
    tm, out = _pick(D, 1024), None
    for t, (tn, first, count) in enumerate(_col_tiles(wd)):
        out = _matmul(
            ids, h_t, dproj, grid=(D // tm, nr * count, 1),
            a_spec=pl.BlockSpec((tm, T), lambda i, j, k, ids: (i, 0)),
            b_spec=pl.BlockSpec((pl.Element(T), pl.Element(tn)), functools.partial(
                lambda tn, first, count, i, j, k, ids: (
                    0, pl.multiple_of(ids[r0 + j // count] * wd + first + (j % count) * tn, LANES)),
                tn, first, count)),
            o_spec=pl.BlockSpec((None, tm, tn), functools.partial(
                lambda tn, first, count, i, j, k, ids: (j // count, i, first // tn + j % count), tn, first, count)),
            out_shape=jax.ShapeDtypeStruct((nr, D, wd), BF16), trans_b=False, name=f"{name}_{t}", after=after, prev=out)
    return out


def _dh(ids, dproj, w_roles, after=()):
    T = dproj.shape[0]
    n_roles, D, wd = w_roles.shape
    tm, tn = _pick(T, 1024), _pick(D, 512)
    nk, n_after = n_roles // 2, len(after)

    def body(ids_ref, a0_ref, a1_ref, b0_ref, b1_ref, *rest):
        del ids_ref
        o_ref, acc_ref = rest[n_after], rest[n_after + 1]
        prod = _dot(a0_ref[...], b0_ref[...], NT_DIMS) + _dot(a1_ref[...], b1_ref[...], NT_DIMS)
        k = pl.program_id(2)

        @pl.when(k == 0)
        def _():
            acc_ref[...] = prod

        @pl.when((k > 0) & (k < nk - 1))
        def _():
            acc_ref[...] += prod

        @pl.when(k == nk - 1)
        def _():
            o_ref[...] = acc_ref[...] + prod

    a_spec = lambda e: pl.BlockSpec((tm, wd), lambda i, j, k, ids: (i, ids[2 * k + e]))
    b_spec = lambda e: pl.BlockSpec((None, tn, wd), lambda i, j, k, ids: (2 * k + e, j, 0))
    return pl.pallas_call(
        body, name="dh", out_shape=jax.ShapeDtypeStruct((T, D), F32),
        grid_spec=pltpu.PrefetchScalarGridSpec(
            num_scalar_prefetch=1, grid=(T // tm, D // tn, nk),
            in_specs=[a_spec(0), a_spec(1), b_spec(0), b_spec(1)] + [ANY_SPEC] * n_after,
            out_specs=pl.BlockSpec((tm, tn), lambda i, j, k, ids: (i, j)),
            scratch_shapes=[pltpu.VMEM((tm, tn), F32)]),
        compiler_params=_params("parallel", "parallel", "arbitrary"),
    )(ids, dproj, dproj, w_roles, w_roles, *after)


def _cast_slot0(w, n_roles, name):
    R, C = w.shape
    tr = _pick(R, 256)

    def body(w_ref, o_ref):
        o_ref[...] = w_ref[...].astype(BF16)

    return pl.pallas_call(
        body, name=name, grid=(R // tr,), in_specs=[pl.BlockSpec((tr, C), lambda i: (i, 0))],
        out_specs=pl.BlockSpec((None, tr, C), lambda i: (0, i, 0)),
        out_shape=jax.ShapeDtypeStruct((n_roles, R, C), BF16), compiler_params=_params("parallel"))(w)


def _prenorm(x, gain):
    T, D = x.shape
    tm = _pick(T, 256)

    def body(x_ref, g_ref, h_ref, ht_ref):
        xv = x_ref[...]
        r = lax.rsqrt(jnp.mean(xv * xv, axis=-1, keepdims=True) + NORM_EPS)
        h = xv * r * g_ref[...]
        h_ref[...] = h.astype(BF16)
        ht_ref[...] = h.T.astype(BF16)

    return pl.pallas_call(
        body, name="prenorm", grid=(T // tm,),
        in_specs=[pl.BlockSpec((tm, D), lambda i: (i, 0)), pl.BlockSpec((1, D), lambda i: (0, 0))],
        out_specs=[pl.BlockSpec((tm, D), lambda i: (i, 0)), pl.BlockSpec((D, tm), lambda i: (0, i))],
        out_shape=[jax.ShapeDtypeStruct((T, D), BF16), jax.ShapeDtypeStruct((D, T), BF16)],
        compiler_params=_params("parallel"))(x, gain)


def _norm_bwd(u, yn, r):
    return r * (u - yn * jnp.mean(u * yn, axis=-1, keepdims=True))


def _loss_head(y, x, target, gain):
    T, D = y.shape
    tm = _pick(T, 256)

    def body(y_ref, x_ref, t_ref, g_ref, dy_ref, dout_ref, gpost_ref, sq_ref):
        yv = y_ref[...]
        g = g_ref[...]
        r = lax.rsqrt(jnp.mean(yv * yv, axis=-1, keepdims=True) + NORM_EPS)
        yn = yv * r
        err = x_ref[...] + yn * g - t_ref[...]
        dout = err * (1.0 / D)
        dy_ref[...] = _norm_bwd(dout * g, yn, r).astype(BF16)
        dout_ref[...] = dout

        @pl.when(pl.program_id(0) == 0)
        def _():
            gpost_ref[...] = jnp.zeros_like(gpost_ref)
            sq_ref[...] = jnp.zeros_like(sq_ref)

        gpost_ref[...] += jnp.sum(dout * yn, axis=0, keepdims=True)
        sq_ref[...] += jnp.sum(err * err, axis=0, keepdims=True)

    row = pl.BlockSpec((tm, D), lambda i: (i, 0))
    vec = pl.BlockSpec((1, D), lambda i: (0, 0))
    return pl.pallas_call(
        body, name="loss_head", grid=(T // tm,), in_specs=[row, row, row, vec], out_specs=[row, row, vec, vec],
        out_shape=[jax.ShapeDtypeStruct((T, D), BF16), jax.ShapeDtypeStruct((T, D), F32),
                   jax.ShapeDtypeStruct((1, D), F32), jax.ShapeDtypeStruct((1, D), F32)],
        compiler_params=_params("arbitrary"))(y, x, target, gain)


def _prenorm_bwd(x, dh, dout, gain):
    T, D = x.shape
    tm = _pick(T, 256)

    def body(x_ref, dh_ref, dout_ref, g_ref, gx_ref, gpre_ref):
        xv = x_ref[...]
        dhv = dh_ref[...]
        r = lax.rsqrt(jnp.mean(xv * xv, axis=-1, keepdims=True) + NORM_EPS)
        xn = xv * r
        gx_ref[...] = dout_ref[...] + _norm_bwd(dhv * g_ref[...], xn, r)

        @pl.when(pl.program_id(0) == 0)
        def _():
            gpre_ref[...] = jnp.zeros_like(gpre_ref)

        gpre_ref[...] += jnp.sum(dhv * xn, axis=0, keepdims=True)

    row = pl.BlockSpec((tm, D), lambda i: (i, 0))
    vec = pl.BlockSpec((1, D), lambda i: (0, 0))
    return pl.pallas_call(
        body, name="prenorm_bwd", grid=(T // tm,), in_specs=[row, row, row, vec], out_specs=[row, vec],
        out_shape=[jax.ShapeDtypeStruct((T, D), F32), jax.ShapeDtypeStruct((1, D), F32)],
        compiler_params=_params("arbitrary"))(x, dh, dout, gain)


def _attn_masks(n):
    row = lax.broadcasted_iota(jnp.int32, (2 * WINDOW, 2 * WINDOW), 0) % WINDOW
    col = lax.broadcasted_iota(jnp.int32, (2 * WINDOW, 2 * WINDOW), 1)
    valid = (col > row) & (col <= row + WINDOW) & ((n > 0) | (col >= WINDOW))
    low = lax.broadcasted_iota(jnp.int32, (1, LANES), 1) < ATTN_HEAD
    top = lax.broadcasted_iota(jnp.int32, (2 * WINDOW, 1), 0) < WINDOW
    return valid, low, top


def _dup_half(pair, keep):
    return jnp.where(keep, pair, pltpu.roll(pair, ATTN_HEAD, 1))


def _fold_half(v):
    return v + pltpu.roll(v, ATTN_HEAD, 1)


def _attn_scores(qpair, k2, low):
    q2 = jnp.concatenate([jnp.where(low, qpair, 0.0), jnp.where(low, 0.0, qpair)], axis=0).astype(BF16)
    return q2, _dot(q2, k2, NT_DIMS)


def _attn_softmax(raw, sink_lo, sink_hi, valid, top):
    s = jnp.where(valid, raw * (ATTN_HEAD ** -0.5), -jnp.inf)
    sink = jnp.where(top, sink_lo, sink_hi)
    m = jnp.maximum(jnp.max(s, axis=-1, keepdims=True), sink)
    p = jnp.exp(s - m)
    psink = jnp.exp(sink - m)
    inv = 1.0 / (jnp.sum(p, axis=-1, keepdims=True) + psink)
    return p * inv, psink * inv


def _attn_specs(D):
    kb = D // LANES
    vb = kb + D // (8 * LANES)
    gb = (D + D // 4) // 512
    wide = lambda off: [pl.BlockSpec((WINDOW, 512), functools.partial(lambda o, e, jp, n: (n, o + 2 * jp + e), off, e))
                        for e in (0, 1)]
    cur = lambda off: pl.BlockSpec((WINDOW, LANES), functools.partial(lambda o, jp, n: (n, o + jp), off))
    prev = lambda off: pl.BlockSpec((WINDOW, LANES),
                                    functools.partial(lambda o, jp, n: (jnp.maximum(n - 1, 0), o + jp), off))
    return wide(0) + [cur(kb), prev(kb), cur(vb), prev(vb)] + wide(gb)


def _attn_fwd(proj, sinks, D):
    T = proj.shape[0]
    nb, njp = T // WINDOW, D // 1024

    def body(sink_ref, qlo_ref, qhi_ref, kc_ref, kp_ref, vc_ref, vp_ref, glo_ref, ghi_ref, mix_ref, mixt_ref):
        jp, n = pl.program_id(0), pl.program_id(1)
        valid, low, top = _attn_masks(n)
        kk = jnp.concatenate([kp_ref[...], kc_ref[...]], axis=0)
        vv = jnp.concatenate([vp_ref[...], vc_ref[...]], axis=0)
        pairs = range(4)
        for hj, (q_ref, g_ref) in enumerate(((qlo_ref, glo_ref), (qhi_ref, ghi_ref))):
            keep = low if hj == 0 else jnp.logical_not(low)
            k2 = _dup_half(kk, keep).astype(BF16)
            v2 = _dup_half(vv, keep).astype(BF16)
            cols = [slice(LANES * p, LANES * (p + 1)) for p in pairs]
            scores = [_attn_scores(q_ref[:, cols[p]], k2, low)[1] for p in pairs]
            probs = []
            for p in pairs:
                head = (2 * jp + hj) * GQA + 2 * p
                probs.append(_attn_softmax(scores[p], sink_ref[0, head], sink_ref[0, head + 1], valid, top)[0])
            o2s = [_dot(probs[p].astype(BF16), v2, NN_DIMS) for p in pairs]
            for p in pairs:
                opair = jnp.where(low, o2s[p][:WINDOW], o2s[p][WINDOW:])
                g = g_ref[:, cols[p]]
                out = opair * (g * _sigmoid(g))
                oc = slice(512 * hj + LANES * p, 512 * hj + LANES * (p + 1))
                mix_ref[:, oc] = out.astype(BF16)
                mixt_ref[oc, :] = out.T.astype(BF16)

    return pl.pallas_call(
        body, name="attn_fwd", grid=(njp, nb),
        in_specs=[pl.BlockSpec(memory_space=pltpu.SMEM)] + _attn_specs(D),
        out_specs=[pl.BlockSpec((WINDOW, 1024), lambda jp, n: (n, jp)),
                   pl.BlockSpec((1024, WINDOW), lambda jp, n: (jp, n))],
        out_shape=[jax.ShapeDtypeStruct((T, 2 * D), BF16), jax.ShapeDtypeStruct((2 * D, T), BF16)],
        compiler_params=_params("parallel", "parallel"))(sinks, *([proj] * 8))


def _flush_windows(bufs, sems, hbm_ref, corners, slot, step, last):
    def copies(sl):
        return [pltpu.make_async_copy(
            b.at[sl], hbm_ref.at[pl.ds(r0, b.shape[1]), pl.ds(c0, b.shape[2])], sems.at[sl, i])
            for i, (b, (r0, c0)) in enumerate(zip(bufs, corners))]

    for cp in copies(slot):
        cp.start()

    @pl.when(step > 0)
    def _():
        for cp in copies(1 - slot):
            cp.wait()

    @pl.when(step == last)
    def _():
        for cp in copies(slot):
            cp.wait()


def _attn_bwd(proj, sinks, dmix, D, after=()):
    T = proj.shape[0]
    nb, njp = T // WINDOW, D // 1024
    n_after = len(after)

    def body(sink_ref, qlo_ref, qhi_ref, kc_ref, kp_ref, vc_ref, vp_ref, glo_ref, ghi_ref, dmix_ref, *rest):
        dproj_ref, dsink_ref, kcarry_ref, vcarry_ref, dq_buf, dk_buf, dv_buf, dg_buf, out_sems = rest[n_after:]
        jp, step = pl.program_id(0), pl.program_id(1)
        n = nb - 1 - step
        slot = step % 2
        dq_ref, dk_ref, dv_ref, dg_ref = dq_buf.at[slot], dk_buf.at[slot], dv_buf.at[slot], dg_buf.at[slot]
        valid, low, top = _attn_masks(n)
        lane = lax.broadcasted_iota(jnp.int32, (1, LANES), 1)
        kk = jnp.concatenate([kp_ref[...], kc_ref[...]], axis=0)
        vv = jnp.concatenate([vp_ref[...], vc_ref[...]], axis=0)

        @pl.when(step == 0)
        def _():
            kcarry_ref[...] = jnp.zeros_like(kcarry_ref)
            vcarry_ref[...] = jnp.zeros_like(vcarry_ref)
            dsink_ref[...] = jnp.zeros_like(dsink_ref)

        dk_pair = jnp.zeros((2 * WINDOW, LANES), F32)
        dv_pair = jnp.zeros((2 * WINDOW, LANES), F32)
        dsink = jnp.zeros((1, LANES), F32)
        for hj, (q_ref, g_ref) in enumerate(((qlo_ref, glo_ref), (qhi_ref, ghi_ref))):
            keep = low if hj == 0 else jnp.logical_not(low)
            k2 = _dup_half(kk, keep).astype(BF16)
            v2 = _dup_half(vv, keep).astype(BF16)
            pairs = range(4)
            cols = [slice(LANES * p, LANES * (p + 1)) for p in pairs]
            ocs = [slice(512 * hj + LANES * p, 512 * hj + LANES * (p + 1)) for p in pairs]
            q2s, scores = zip(*[_attn_scores(q_ref[:, cols[p]], k2, low) for p in pairs])
            probs, psinks, do2s = [], [], []
            for p in pairs:
                head = (2 * jp + hj) * GQA + 2 * p
                pr, ps = _attn_softmax(scores[p], sink_ref[0, head], sink_ref[0, head + 1], valid, top)
                probs.append(pr)
                psinks.append(ps)
                g = g_ref[:, cols[p]]
                do = dmix_ref[:, ocs[p]] * (g * _sigmoid(g))
                do2s.append(jnp.concatenate([jnp.where(low, do, 0.0), jnp.where(low, 0.0, do)], axis=0).astype(BF16))
            pbs = [pr.astype(BF16) for pr in probs]
            o2s = [_dot(pbs[p], v2, NN_DIMS) for p in pairs]
            dps = [_dot(do2s[p], v2, NT_DIMS) for p in pairs]
            dss = []
            for p in pairs:
                opair = jnp.where(low, o2s[p][:WINDOW], o2s[p][WINDOW:])
                g = g_ref[:, cols[p]]
                sg = _sigmoid(g)
                dg_ref[:, ocs[p]] = (dmix_ref[:, ocs[p]] * opair * (sg * (1.0 + g * (1.0 - sg)))).astype(BF16)
                delta = jnp.sum(probs[p] * dps[p], axis=-1, keepdims=True)
                dss.append((probs[p] * (dps[p] - delta) * (ATTN_HEAD ** -0.5)).astype(BF16))
                ps = psinks[p] * delta
                local = hj * GQA + 2 * p
                dsink -= jnp.where(lane == local, jnp.sum(ps[:WINDOW], axis=0, keepdims=True), 0.0)
                dsink -= jnp.where(lane == local + 1, jnp.sum(ps[WINDOW:], axis=0, keepdims=True), 0.0)
            dq2s = [_dot(dss[p], k2, NN_DIMS) for p in pairs]
            dk_head = sum(_dot(dss[p], q2s[p], TN_DIMS) for p in pairs)
            dv_head = sum(_dot(pbs[p], do2s[p], TN_DIMS) for p in pairs)
            for p in pairs:
                dq_ref[:, ocs[p]] = jnp.where(low, dq2s[p][:WINDOW], dq2s[p][WINDOW:]).astype(BF16)
            dk_pair += jnp.where(keep, _fold_half(dk_head), 0.0)
            dv_pair += jnp.where(keep, _fold_half(dv_head), 0.0)
        dk_ref[...] = (dk_pair[WINDOW:] + kcarry_ref[...]).astype(BF16)
        dv_ref[...] = (dv_pair[WINDOW:] + vcarry_ref[...]).astype(BF16)
        kcarry_ref[...] = dk_pair[:WINDOW]
        vcarry_ref[...] = dv_pair[:WINDOW]
        dsink_ref[...] += dsink
        row = pl.multiple_of(n * WINDOW, WINDOW)
        col = lambda base, width: pl.multiple_of(base + jp * width, LANES)
        corners = [(row, col(0, 1024)), (row, col(D, LANES)), (row, col(D + D // 8, LANES)), (row, col(D + D // 4, 1024))]
        _flush_windows([dq_buf, dk_buf, dv_buf, dg_buf], out_sems, dproj_ref, corners, slot, step, nb - 1)

    rev = lambda spec: pl.BlockSpec(spec.block_shape, functools.partial(
        lambda f, jp, s: f(jp, nb - 1 - s), spec.index_map))
    in_specs = [rev(s) for s in _attn_specs(D)]
    wide_in = pl.BlockSpec((WINDOW, 1024), lambda jp, s: (nb - 1 - s, jp))
    return pl.pallas_call(
        body, name="attn_bwd", grid=(njp, nb),
        in_specs=[pl.BlockSpec(memory_space=pltpu.SMEM)] + in_specs + [wide_in] + [ANY_SPEC] * n_after,
        out_specs=[ANY_SPEC, pl.BlockSpec((None, 1, LANES), lambda jp, s: (jp, 0, 0))],
        out_shape=[jax.ShapeDtypeStruct((T, 6 * D + D // 4), BF16), jax.ShapeDtypeStruct((njp, 1, LANES), F32)],
        scratch_shapes=[pltpu.VMEM((WINDOW, LANES), F32), pltpu.VMEM((WINDOW, LANES), F32),
                        pltpu.VMEM((2, WINDOW, 1024), BF16), pltpu.VMEM((2, WINDOW, LANES), BF16),
                        pltpu.VMEM((2, WINDOW, LANES), BF16), pltpu.VMEM((2, WINDOW, 1024), BF16),
                        pltpu.SemaphoreType.DMA((2, 4))],
        compiler_params=_params("parallel", "arbitrary"))(sinks, *([proj] * 8), dmix, *after)


RNN_TB = 512
RNN_HB = 8
RNN_WIDE = RNN_HB * RNN_HEAD


def _split3(v):
    a = v.astype(BF16)
    r = v - a.astype(F32)
    b = r.astype(BF16)
    c = (r - b.astype(F32)).astype(BF16)
    return a, b, c


def _tri_sum(tri, v):
    a, b, c = _split3(v)
    return _dot(tri, a, NN_DIMS) + _dot(tri, b, NN_DIMS) + _dot(tri, c, NN_DIMS)


def _lower_bound(lb_ref):
    l0, l1 = lb_ref[0:1, :], lb_ref[1:2, :]
    m = jnp.maximum(l0, l1)
    e0, e1 = jnp.exp(l0 - m), jnp.exp(l1 - m)
    return e0 / (e0 + e1)


def _rnn_gates(rq, rf, lb):
    sq = _sigmoid(rq)
    sf = _sigmoid(rf)
    f = lb + (1.0 - lb) * sf
    return sq, sf, f


def _rnn_decays(g, tri):
    return _rnn_factors(_tri_sum(tri, g))


def _rnn_factors(G):
    last = G[CHUNK - 1:CHUNK, :]
    mid = G[CHUNK // 2 - 1:CHUNK // 2, :]
    return G, jnp.exp(G), jnp.exp(G - mid), jnp.exp(mid - G), jnp.exp(last - G), jnp.exp(last)


def _chunk_masks():
    r = lax.broadcasted_iota(jnp.int32, (CHUNK, CHUNK), 0)
    c = lax.broadcasted_iota(jnp.int32, (CHUNK, CHUNK), 1)
    return r >= c, (r >= c).astype(BF16), (r <= c).astype(BF16)


def _rnn_specs(T, D, tb, rev):
    nt = T // tb
    base = (2 * D + D // 4) // LANES
    t_of = (lambda s: nt - 1 - s) if rev else (lambda s: s)
    assert base % RNN_HB == 0 and (D // LANES) % RNN_HB == 0
    cols = [pl.BlockSpec((tb, RNN_WIDE), functools.partial(lambda o, h, s: (t_of(s), o + h),
                                                            (base + i * (D // LANES)) // RNN_HB))
            for i in range(4)]
    return cols, t_of


def _rnn_fwd(proj, lb_logits, rnn_norm, mixed, mixed_t, D):
    T = proj.shape[0]
    tb = _pick(T, RNN_TB)
    nt, nh, cpb = T // tb, D // RNN_HEAD, tb // CHUNK
    cols, _ = _rnn_specs(T, D, tb, False)

    def body(rq_ref, rf_ref, ri_ref, rg_ref, lb_ref, gain_ref, mix_in, mixt_in,
             mix_ref, mixt_ref, o_ref, st_ref, state_ref):
        del mix_in, mixt_in
        causal, tri, _ = _chunk_masks()
        lb = _lower_bound(lb_ref)

        @pl.when(pl.program_id(1) == 0)
        def _():
            state_ref[...] = jnp.zeros_like(state_ref)

        def chunk(c, carry):
            rows = pl.ds(pl.multiple_of(c * CHUNK, CHUNK), CHUNK)
            heads = range(RNN_HB)
            lns = [slice(RNN_HEAD * hh, RNN_HEAD * (hh + 1)) for hh in heads]
            qs, ks, Gs = [], [], []
            for ln in lns:
                rq, rf = rq_ref[rows, ln], rf_ref[rows, ln]
                sq, _, f = _rnn_gates(rq, rf, lb[:, ln])
                qs.append(rq * sq)
                ks.append(1.0 - f)
                Gs.append(_tri_sum(tri, jnp.log(f)))
            atts, inters, vbs = [], [], []
            for hh, ln in enumerate(lns):
                _, eG, eq, ek, ekl, elast = _rnn_factors(Gs[hh])
                q, k = qs[hh], ks[hh]
                st = state_ref[hh]
                st_ref[hh, c] = st
                vb = ri_ref[rows, ln].astype(BF16)
                vbs.append(vb)
                atts.append(_dot((q * eq).astype(BF16), (k * ek).astype(BF16), NT_DIMS))
                inters.append(_dot((q * eG).astype(BF16), st.astype(BF16), NT_DIMS))
                state_ref[hh] = st * elast + _dot(vb, (k * ekl).astype(BF16), TN_DIMS)
            intras = [_dot(jnp.where(causal, atts[hh], 0.0).astype(BF16), vbs[hh], NN_DIMS) for hh in heads]
            for hh, ln in enumerate(lns):
                o = inters[hh] + intras[hh]
                o_ref[rows, ln] = o
                rg = rg_ref[rows, ln]
                r = lax.rsqrt(jnp.mean(o * o, axis=-1, keepdims=True) + NORM_EPS)
                out = (o * r * gain_ref[:, ln]) * (rg * _sigmoid(rg))
                mix_ref[rows, ln] = out.astype(BF16)
            return carry

        lax.fori_loop(0, cpb, chunk, 0)
        mixt_ref[...] = mix_ref[...].astype(F32).T.astype(BF16)

    nh //= RNN_HB
    vec2 = pl.BlockSpec((2, RNN_WIDE), lambda h, s: (0, h))
    vec1 = pl.BlockSpec((1, RNN_WIDE), lambda h, s: (0, h))
    return pl.pallas_call(
        body, name="rnn_fwd", grid=(nh, nt),
        in_specs=cols + [vec2, vec1, ANY_SPEC, ANY_SPEC],
        out_specs=[pl.BlockSpec((tb, RNN_WIDE), lambda h, s: (s, D // RNN_WIDE + h)),
                   pl.BlockSpec((RNN_WIDE, tb), lambda h, s: (D // RNN_WIDE + h, s)),
                   pl.BlockSpec((tb, RNN_WIDE), lambda h, s: (s, h)),
                   pl.BlockSpec((RNN_HB, cpb, RNN_HEAD, RNN_HEAD), lambda h, s: (h, s, 0, 0))],
        out_shape=[jax.ShapeDtypeStruct(mixed.shape, BF16), jax.ShapeDtypeStruct(mixed_t.shape, BF16),
                   jax.ShapeDtypeStruct((T, D), F32),
                   jax.ShapeDtypeStruct((nh * RNN_HB, T // CHUNK, RNN_HEAD, RNN_HEAD), F32)],
        scratch_shapes=[pltpu.VMEM((RNN_HB, RNN_HEAD, RNN_HEAD), F32)],
        input_output_aliases={6: 0, 7: 1},
        compiler_params=_params("parallel", "arbitrary"))(proj, proj, proj, proj, lb_logits, rnn_norm, mixed, mixed_t)


def _rnn_bwd(proj, lb_logits, rnn_norm, o_raw, states, dmix, dproj, D):
    T = proj.shape[0]
    tb = _pick(T, RNN_TB)
    nt, nh, cpb = T // tb, D // RNN_HEAD, tb // CHUNK
    cols, t_of = _rnn_specs(T, D, tb, True)

    def body(rq_ref, rf_ref, ri_ref, rg_ref, lb_ref, gain_ref, o_ref, st_ref, dmix_ref, dproj_in,
             dproj_ref, dlb_ref, dgain_ref, dstate_ref, out_buf, out_sems):
        del dproj_in
        step = pl.program_id(1)
        slot = step % 2
        drq_ref, drf_ref, dri_ref, drg_ref = [out_buf.at[i, slot] for i in range(4)]
        causal, tri, tri_t = _chunk_masks()
        lb = _lower_bound(lb_ref)
        gain = gain_ref[...]

        @pl.when(pl.program_id(1) == 0)
        def _():
            dstate_ref[...] = jnp.zeros_like(dstate_ref)
            dlb_ref[...] = jnp.zeros_like(dlb_ref)
            dgain_ref[...] = jnp.zeros_like(dgain_ref)

        def chunk(i, carry):
            c = cpb - 1 - i
            rows = pl.ds(pl.multiple_of(c * CHUNK, CHUNK), CHUNK)
            heads = range(RNN_HB)
            lns = [slice(RNN_HEAD * hh, RNN_HEAD * (hh + 1)) for hh in heads]
            last_row = lax.broadcasted_iota(jnp.int32, (CHUNK, 1), 0) == CHUNK - 1
            A = []
            for ln in lns:
                rq, rf, rg = rq_ref[rows, ln], rf_ref[rows, ln], rg_ref[rows, ln]
                o, dgated = o_ref[rows, ln], dmix_ref[rows, ln]
                gainh = gain[:, ln]
                sgt = _sigmoid(rg)
                r = lax.rsqrt(jnp.mean(o * o, axis=-1, keepdims=True) + NORM_EPS)
                on = o * r
                drg_ref[rows, ln] = (dgated * (on * gainh) * (sgt * (1.0 + rg * (1.0 - sgt)))).astype(BF16)
                d_on = dgated * (rg * sgt)
                dgain_ref[:, ln] += jnp.sum(d_on * on, axis=0, keepdims=True)
                dob = _norm_bwd(d_on * gainh, on, r).astype(BF16)
                sq, sf, f = _rnn_gates(rq, rf, lb[:, ln])
                A.append(dict(rq=rq, sq=sq, sf=sf, f=f, dob=dob, G=_tri_sum(tri, jnp.log(f))))
            for hh, ln in enumerate(lns):
                a = A[hh]
                _, eG, eq, ek, ekl, elast = _rnn_factors(a.pop("G"))
                q, k = a["rq"] * a["sq"], 1.0 - a["f"]
                st, dst = st_ref[hh, c], dstate_ref[hh]
                qg, kl = q * eG, k * ekl
                qmb, kmb = (q * eq).astype(BF16), (k * ek).astype(BF16)
                dob, vb, dstb = a["dob"], ri_ref[rows, ln].astype(BF16), dst.astype(BF16)
                a.update(eG=eG, eq=eq, ek=ek, ekl=ekl, qg=qg, kl=kl, qmb=qmb, kmb=kmb,
                         att=_dot(qmb, kmb, NT_DIMS), datt=_dot(dob, vb, NT_DIMS),
                         dqg=_dot(dob, st.astype(BF16), NN_DIMS), dkl=_dot(vb, dstb, NN_DIMS),
                         dri=_dot(kl.astype(BF16), dstb, NT_DIMS),
                         dlast=jnp.sum(dst * st, axis=0, keepdims=True) * elast)
                dstate_ref[hh] = dst * elast + _dot(dob, qg.astype(BF16), TN_DIMS)
            for hh, ln in enumerate(lns):
                a = A[hh]
                att = jnp.where(causal, a.pop("att"), 0.0).astype(BF16)
                datt = jnp.where(causal, a.pop("datt"), 0.0).astype(BF16)
                dqm = _dot(datt, a["kmb"], NN_DIMS)
                dkm = _dot(datt, a["qmb"], TN_DIMS)
                dri_ref[rows, ln] = (_dot(att, a["dob"], TN_DIMS) + a.pop("dri")).astype(BF16)
                dqg, dkl, kl = a.pop("dqg"), a.pop("dkl"), a.pop("kl")
                a["dq"] = dqg * a.pop("eG") + dqm * a.pop("eq")
                a["dk"] = dkm * a.pop("ek") + dkl * a.pop("ekl")
                dG = dqg * a.pop("qg") + dqm * a.pop("qmb").astype(F32) - dkm * a.pop("kmb").astype(F32) - dkl * kl
                dlast = jnp.sum(dkl * kl, axis=0, keepdims=True) + a.pop("dlast")
                a["dg"] = _tri_sum(tri_t, dG + jnp.where(last_row, dlast, 0.0))
            for hh, ln in enumerate(lns):
                a = A[hh]
                rq, sq, sf = a["rq"], a["sq"], a["sf"]
                df = a["dg"] / a["f"] - a["dk"]
                drq_ref[rows, ln] = (a["dq"] * (sq * (1.0 + rq * (1.0 - sq)))).astype(BF16)
                drf_ref[rows, ln] = (df * (1.0 - lb[:, ln]) * (sf * (1.0 - sf))).astype(BF16)
                dlb_ref[:, ln] += jnp.sum(df * (1.0 - sf), axis=0, keepdims=True)
            return carry

        lax.fori_loop(0, cpb, chunk, 0)
        row = pl.multiple_of(t_of(step) * tb, tb)
        col0 = 2 * D + D // 4 + pl.program_id(0) * RNN_WIDE
        corners = [(row, pl.multiple_of(col0 + i * D, LANES)) for i in range(4)]
        _flush_windows([out_buf.at[i] for i in range(4)], out_sems, dproj_ref, corners, slot, step, nt - 1)

    nh //= RNN_HB
    vec2 = pl.BlockSpec((2, RNN_WIDE), lambda h, s: (0, h))
    vec1 = pl.BlockSpec((1, RNN_WIDE), lambda h, s: (0, h))
    blk = pl.BlockSpec((tb, RNN_WIDE), lambda h, s: (t_of(s), h))
    return pl.pallas_call(
        body, name="rnn_bwd", grid=(nh, nt),
        in_specs=cols + [vec2, vec1, blk,
                         pl.BlockSpec((RNN_HB, cpb, RNN_HEAD, RNN_HEAD), lambda h, s: (h, t_of(s), 0, 0)),
                         pl.BlockSpec((tb, RNN_WIDE), lambda h, s: (t_of(s), D // RNN_WIDE + h)), ANY_SPEC],
        out_specs=[ANY_SPEC, vec1, vec1],
        out_shape=[jax.ShapeDtypeStruct(dproj.shape, BF16)] + [jax.ShapeDtypeStruct((1, D), F32)] * 2,
        scratch_shapes=[pltpu.VMEM((RNN_HB, RNN_HEAD, RNN_HEAD), F32), pltpu.VMEM((4, 2, tb, RNN_WIDE), BF16),
                        pltpu.SemaphoreType.DMA((2, 4))],
        input_output_aliases={9: 0},
        compiler_params=_params("parallel", "arbitrary"))(proj, proj, proj, proj, lb_logits, rnn_norm, o_raw, states, dmix, dproj)


def _adamw(w, g, m, v):
    m = ADAM_B1 * m + (1.0 - ADAM_B1) * g
    v = ADAM_B2 * v + (1.0 - ADAM_B2) * (g * g)
    m_hat = m / (1.0 - ADAM_B1 ** ADAM_STEP)
    v_hat = v / (1.0 - ADAM_B2 ** ADAM_STEP)
    delta = -ADAM_LR * (m_hat / (jnp.sqrt(v_hat) + ADAM_EPS) + ADAM_WD * w)
    return delta, m, v


def _adamw_big(w, m, v, parts, tr, name):
    R, C = w.shape
    n_parts = len(parts)

    def body(w_ref, m_ref, v_ref, *rest):
        part_refs = rest[:n_parts]
        g_ref, d_ref, nm_ref, nv_ref = rest[n_parts:]
        g = part_refs[0][...].astype(F32)
        for p_ref in part_refs[1:]:
            g = g + p_ref[...].astype(F32)
        delta, nm, nv = _adamw(w_ref[...], g, m_ref[...], v_ref[...])
        g_ref[...] = g
        d_ref[...] = delta
        nm_ref[...] = nm
        nv_ref[...] = nv

    blk = pl.BlockSpec((tr, C), lambda i: (i, 0))
    part_specs = [pl.BlockSpec((None, tr, C), functools.partial(lambda s, i: (s, i, 0), slot)) for _, slot in parts]
    return pl.pallas_call(
        body, name=name, grid=(R // tr,), in_specs=[blk, blk, blk] + part_specs,
        out_specs=[blk] * 4, out_shape=[jax.ShapeDtypeStruct((R, C), F32)] * 4,
        compiler_params=_params("parallel"))(w, m, v, *[a for a, _ in parts])


def _pair_sum(pa, qa):
    n, R, C = qa.shape
    tr = _pick(R, 256)

    def body(p_ref, q_ref, r_ref):
        r_ref[...] = (p_ref[...].astype(F32) + q_ref[...].astype(F32)).astype(BF16)

    return pl.pallas_call(
        body, name="pair_sum", grid=(n, R // tr),
        in_specs=[pl.BlockSpec((None, tr, C), lambda j, i: (2 * j, i, 0)), pl.BlockSpec((None, tr, C), lambda j, i: (j, i, 0))],
        out_specs=pl.BlockSpec((None, tr, C), lambda j, i: (j, i, 0)),
        out_shape=jax.ShapeDtypeStruct((n, R, C), BF16), compiler_params=_params("parallel", "parallel"))(pa, qa)


def _adamw_small(total, sinks, lb_logits, rnn_norm, pre_norm, post_norm, moments, D):
    params = [sinks, lb_logits, rnn_norm, pre_norm, post_norm]

    def body(tot_ref, *refs):
        p_refs, m_refs, v_refs = refs[0:5], refs[5:10], refs[10:15]
        loss_ref, outs = refs[15], refs[16:]
        tot = tot_ref[...]
        l0, l1 = p_refs[1][0:1, :], p_refs[1][1:2, :]
        mx = jnp.maximum(l0, l1)
        e0, e1 = jnp.exp(l0 - mx), jnp.exp(l1 - mx)
        p0, p1 = e0 / (e0 + e1), e1 / (e0 + e1)
        dlb = tot[0:1, :]
        grads = [tot[5:6, 0:LANES], jnp.concatenate([dlb * p0 * (1.0 - p0), -dlb * p0 * p1], axis=0),
                 tot[1:2, :], tot[2:3, :], tot[3:4, :]]
        loss_ref[...] = 0.5 / D * jnp.sum(tot[4:5, :], axis=-1, keepdims=True)
        for i, g in enumerate(grads):
            delta, nm, nv = _adamw(p_refs[i][...], g, m_refs[i][...], v_refs[i][...])
            outs[4 * i][...] = g
            outs[4 * i + 1][...] = delta
            outs[4 * i + 2][...] = nm
            outs[4 * i + 3][...] = nv

    out_shape = [jax.ShapeDtypeStruct((1, 1), F32)]
    for p in params:
        out_shape += [jax.ShapeDtypeStruct(p.shape, F32)] * 4
    return pl.pallas_call(body, name="adamw_small", out_shape=out_shape)(total, *params, *moments)


SIBLING = 1


def _me():
    return lax.axis_index("x"), lax.axis_index("y"), lax.axis_index("c")


def _flat(px, py, pc):
    return 4 * px + 2 * py + pc


def _role_peer(role, x, y, c):
    if not isinstance(role, int):
        j = (role - 2) // 2
        px = jnp.where((role >= 2) & (j != 1), 1 - x, x)
        py = jnp.where((role >= 2) & (j != 0), 1 - y, y)
        return (px, py, jnp.where(role % 2 == 1, 1 - c, c))
    if role < 2:
        return (x, y, (1 - c) if role else c)
    j, other = (role - 2) // 2, (role - 2) % 2
    px = (1 - x) if j in (0, 2) else x
    py = (1 - y) if j in (1, 2) else y
    return (px, py, (1 - c) if other else c)


def _wave1(x, y, c):
    return jnp.where(c == 1, 2, 4)


def _wave2(x, y, c):
    return jnp.where(c == 1, 4, 2)


def _role_ids():
    x, y, c = _me()
    w1, w2 = _wave1(x, y, c), _wave2(x, y, c)
    order = [0, 1, w1, w2 + 1, w2, w1 + 1, 6, 7]
    return jnp.stack([_flat(*_role_peer(r, x, y, c)) for r in range(N_DEV)] + order).astype(jnp.int32)


def _comm_call(name, bufs, waits=(), starts=(), after=()):
    n_buf, n_wait, n_start, n_after = len(bufs), len(waits), len(starts), len(after)

    def body(*refs):
        buf_refs = refs[:n_buf]
        sem_refs = refs[n_buf:n_buf + 2 * n_wait]
        outs = refs[n_buf + 2 * n_wait + n_after:]
        new_sems, token = outs[:2 * n_start], outs[-1]
        x, y, c = _me()

        def val(v):
            return v(x, y, c) if callable(v) else v

        def block(ref, slot, rows):
            if rows is None:
                return ref.at[val(slot)]
            first = rows[0] if isinstance(rows[0], int) else pl.multiple_of(val(rows[0]), 16)
            return ref.at[val(slot), pl.ds(first, rows[1])]

        for w, (_, kind, like, *rows) in enumerate(waits):
            shape_ref = block(buf_refs[like], 0, (0, rows[0][1]) if rows else None)
            cp = pltpu.make_async_remote_copy(
                src_ref=shape_ref, dst_ref=shape_ref, send_sem=sem_refs[2 * w], recv_sem=sem_refs[2 * w + 1],
                device_id=(x, y, c), device_id_type=MESH)
            if kind == "send":
                cp.wait_send()
            else:
                cp.wait_recv()
        for s, (sb, ss, db, ds, role, *rows) in enumerate(starts):
            rows = rows[0] if rows else None
            pltpu.make_async_remote_copy(
                src_ref=block(buf_refs[sb], ss, rows), dst_ref=block(buf_refs[db], ds, rows), send_sem=new_sems[2 * s],
                recv_sem=new_sems[2 * s + 1], device_id=_role_peer(val(role), x, y, c), device_id_type=MESH).start()
        token[...] = jnp.zeros_like(token)

    sems = [s for flight, *_ in waits for s in flight]
    out = pl.pallas_call(
        body, name=name,
        out_shape=tuple([pltpu.SemaphoreType.DMA(())] * (2 * n_start) + [pltpu.HBM(b.shape, b.dtype) for b in bufs]
                        + [jax.ShapeDtypeStruct((8, LANES), F32)]),
        in_specs=tuple([HBM_SPEC] * n_buf + [SEM_SPEC] * (2 * n_wait) + [ANY_SPEC] * n_after),
        out_specs=tuple([SEM_SPEC] * (2 * n_start) + [HBM_SPEC] * n_buf + [pl.BlockSpec(memory_space=pltpu.VMEM)]),
        input_output_aliases={i: 2 * n_start + i for i in range(n_buf)},
        compiler_params=pltpu.CompilerParams(has_side_effects=EFFECT),
    )(*[pltpu.with_memory_space_constraint(b, pltpu.HBM) for b in bufs], *sems, *after)
    flights = [(out[2 * s], out[2 * s + 1]) for s in range(n_start)]
    return list(out[2 * n_start:2 * n_start + n_buf]), flights, out[-1]


def _landing(shape):
    return lax.empty(shape, BF16)


def _both(flights, like):
    return [(f, kind, like) for f in flights for kind in ("send", "recv")]


def _allreduce_small(rows, D):
    n_rows = len(rows)

    def body(*refs):
        in_refs, out_ref, all_ref = refs[:n_rows], refs[n_rows], refs[n_rows + 1]
        send_sems, recv_sems = refs[n_rows + 2], refs[n_rows + 3]
        x, y, c = _me()
        mine = all_ref.at[_flat(x, y, c)]
        mine[...] = jnp.zeros((8, D), F32)
        for i, r in enumerate(in_refs):
            mine[i:i + 1, :] = r[...]
        copies = []
        for k in range(N_DEV - 1):
            copies.append(pltpu.make_async_remote_copy(
                src_ref=mine, dst_ref=mine, send_sem=send_sems.at[k], recv_sem=recv_sems.at[k],
                device_id=_role_peer(k + 1, x, y, c), device_id_type=MESH))
        for cp in copies:
            cp.start()
        for cp in copies:
            cp.wait_recv()
        for cp in copies:
            cp.wait_send()
        tot = all_ref[0]
        for d in range(1, N_DEV):
            tot = tot + all_ref[d]
        out_ref[...] = tot

    vm = pl.BlockSpec(memory_space=pltpu.VMEM)
    return pl.pallas_call(
        body, name="allreduce_small", in_specs=[vm] * n_rows, out_specs=vm,
        out_shape=jax.ShapeDtypeStruct((8, D), F32),
        scratch_shapes=[pltpu.VMEM((N_DEV, 8, D), F32), pltpu.SemaphoreType.DMA((7,)), pltpu.SemaphoreType.DMA((7,))],
    )(*rows)


def _step(ids, x, target, g_in, g_out, sinks, lb_logits, rnn_norm, pre_norm, post_norm, dist):
    T, D = x.shape
    n_roles, _, wd = g_in.shape
    ro = g_out.shape[1]
    assert n_roles % 2 == 0 and (not dist or n_roles == N_DEV)
    n_chips = n_roles // 2 - 1

    half = D // 2
    up = lambda f: (lambda x, y, c: f(x, y, c) + 1)
    rows_of = lambda f: (lambda x, y, c: jnp.where(f(x, y, c) == 2, 0, half))
    if dist:
        (g_in,), fl_in, _ = _comm_call(
            "gather_in_start", [g_in], starts=[(0, 0, 0, SIBLING, SIBLING), (0, 0, 0, _wave1, _wave1)])
    h, h_t = _prenorm(x, pre_norm)
    proj = _proj_in(ids, h, g_in, 0, 1, None, "proj_in_0")
    if dist:
        (g_in,), _, _ = _comm_call("gather_in_sibling", [g_in], waits=[(fl_in[0], "recv", 0)], after=(proj,))
    proj = _proj_in(ids, h, g_in, 1, 1, proj, "proj_in_1")
    if not dist:
        for p in range(2, n_roles, 2):
            proj = _proj_in(ids, h, g_in, p, 2, proj, f"proj_in_{p}")
    else:
        (g_in,), fl_a, _ = _comm_call(
            "gather_in_wave_1", [g_in], waits=[(fl_in[1], "recv", 0)],
            starts=[(0, 0, 0, _wave2, _wave2), (0, _wave1, 0, up(_wave1), SIBLING),
                    (0, _wave1, 0, 6, _wave2, (rows_of(_wave1), half))], after=(proj,))
        proj = _proj_in(ids, h, g_in, 2, 1, proj, "proj_in_2")
        (g_in,), _, _ = _comm_call("gather_in_passed_1", [g_in], waits=[(fl_a[1], "recv", 0)], after=(proj,))
        proj = _proj_in(ids, h, g_in, 3, 1, proj, "proj_in_3")
        (g_in,), fl_b, _ = _comm_call(
            "gather_in_wave_2", [g_in], waits=[(fl_a[0], "recv", 0)],
            starts=[(0, _wave2, 0, up(_wave2), SIBLING), (0, _wave2, 0, 6, _wave1, (rows_of(_wave2), half))],
            after=(proj,))
        proj = _proj_in(ids, h, g_in, 4, 1, proj, "proj_in_4")
        (g_in,), _, _ = _comm_call("gather_in_passed_2", [g_in], waits=[(fl_b[0], "recv", 0)], after=(proj,))
        proj = _proj_in(ids, h, g_in, 5, 1, proj, "proj_in_5")
        (g_in,), fl_c, _ = _comm_call(
            "gather_in_diagonal", [g_in], waits=[(fl_a[2], "recv", 0, (0, half)), (fl_b[1], "recv", 0, (0, half))],
            starts=[(0, 6, 0, 7, SIBLING)], after=(proj,))
        proj = _proj_in(ids, h, g_in, 6, 1, proj, "proj_in_6")
        (g_in, g_out), fl_out, token = _comm_call(
            "gather_out_start", [g_in, g_out], waits=[(fl_c[0], "recv", 0)],
            starts=[(1, 0, 1, SIBLING, SIBLING)] + [(1, 0, 1, 2 + 2 * i, 2 + 2 * i) for i in range(3)], after=(proj,))
        proj = _proj_in(ids, h, g_in, 7, 1, proj, "proj_in_7", (token,))
        send_waits = [(f, "send", 0) for f in fl_in + fl_a[:2] + fl_b[:1] + fl_c] \
            + [(f, "send", 0, (0, half)) for f in (fl_a[2], fl_b[1])]
    mixed, mixed_t = _attn_fwd(proj, sinks, D)
    if dist:
        (g_in, g_out), fl_out_fwd, _ = _comm_call(
            "gather_out_pass", [g_in, g_out],
            waits=send_waits + [(fl_out[1 + i], "recv", 1) for i in range(3)],
            starts=[(1, 2 + 2 * i, 1, 3 + 2 * i, SIBLING) for i in range(3)], after=(mixed,))
    mixed, mixed_t, o_raw, states = _rnn_fwd(proj, lb_logits, rnn_norm, mixed, mixed_t, D)
    if dist:
        (g_out,), _, _ = _comm_call(
            "gather_out_done", [g_out],
            waits=[(fl_out[0], "recv", 0)] + [(f, "recv", 0) for f in fl_out_fwd]
            + [(f, "send", 0) for f in fl_out + fl_out_fwd], after=(states,))
    y = _proj_out(ids, mixed, g_out)
    dy, dout, g_post, sq_err = _loss_head(y, x, target, post_norm)

    dmix = _dmixed(ids, dy, g_out)
    p_out, after = _dw_out(ids, mixed_t, dy, n_roles), ()
    if dist:
        (p_out, l_out), fl_so, token = _comm_call(
            "scatter_out_start", [p_out, _landing((n_roles - 1, ro, D))],
            starts=[(0, r, 1, r - 1, r) for r in range(1, n_roles)])
        after = (token,)
    dproj, d_sink = _attn_bwd(proj, sinks, dmix, D, after)
    dproj, d_lb, g_rnn = _rnn_bwd(proj, lb_logits, rnn_norm, o_raw, states, dmix, dproj, D)

    p_far = _dw_in(ids, h_t, dproj, n_roles, 2, n_roles - 2, "dw_in_far")
    if dist:
        (p_far, q_far), fl_pair, token = _comm_call(
            "scatter_in_pair_start", [p_far, _landing((3, D, wd))],
            starts=[(0, 1 + 2 * j, 1, j, SIBLING) for j in range(3)])
        p_near = _dw_in(ids, h_t, dproj, n_roles, 0, 2, "dw_in_near", after=(token,))
        (p_far, q_far), _, _ = _comm_call("scatter_in_pair_wait", [p_far, q_far], waits=_both(fl_pair, 1), after=(p_near,))
        chip_sum = _pair_sum(p_far, q_far)
        (chip_sum, z_far), fl_chip, token = _comm_call(
            "scatter_in_chip_start", [chip_sum, _landing((3, D, wd))],
            starts=[(0, j, 1, j, 2 + 2 * j) for j in range(3)])
        (p_near, q_near), fl_sib, token = _comm_call(
            "scatter_in_sibling_start", [p_near, _landing((1, D, wd))], starts=[(0, 1, 1, 0, SIBLING)], after=(token,))
        after = (token,)
    else:
        p_near, after = _dw_in(ids, h_t, dproj, n_roles, 0, 2, "dw_in_near"), ()
    dh = _dh(ids, dproj, g_in, after)
    grad_x, g_pre = _prenorm_bwd(x, dh, dout, pre_norm)
    n_q = D // ATTN_HEAD
    sink_row = jnp.pad(d_sink[:, 0, :2 * GQA].reshape(1, n_q), ((0, 0), (0, D - n_q)))
    rows = [d_lb, g_rnn, g_pre, g_post, sq_err, sink_row]
    if not dist:
        return grad_x, [(p_near, r) for r in range(2)] + [(p_far, r) for r in range(n_roles - 2)], \
            [(p_out, r) for r in range(n_roles)], rows

    (p_out, l_out), _, _ = _comm_call("scatter_out_wait", [p_out, l_out], waits=_both(fl_so, 1), after=(grad_x,))
    (chip_sum, z_far, p_near, q_near), _, _ = _comm_call(
        "scatter_in_wait", [chip_sum, z_far, p_near, q_near], waits=_both(fl_chip, 1) + _both(fl_sib, 3), after=(p_out,))
    parts_in = [(p_near, 0), (q_near, 0)] + [(z_far, j) for j in range(3)]
    parts_out = [(p_out, 0)] + [(l_out, k) for k in range(n_roles - 1)]
    return grad_x, parts_in, parts_out, rows


def kernel(x, w_in, attn_sinks, lb_logits, rnn_norm, w_out, pre_norm, post_norm, loss_target, m_w_in, m_attn_sinks, m_lb_logits, m_rnn_norm, m_w_out, m_pre_norm, m_post_norm, v_w_in, v_attn_sinks, v_lb_logits, v_rnn_norm, v_w_out, v_pre_norm, v_post_norm):
    _, T, D = x.shape
    ro = w_out.shape[1]
    n_q = attn_sinks.shape[1]
    assert lb_logits.shape[0] == 2 and n_q == D // ATTN_HEAD and n_q <= LANES

    grad_x, parts_in, parts_out, small_rows = _step(
        _role_ids(), x[0], loss_target[0], _cast_slot0(w_in[0], N_DEV, "cast_w_in"),
        _cast_slot0(w_out[0], N_DEV, "cast_w_out"), attn_sinks, lb_logits, rnn_norm, pre_norm, post_norm, True)

    g_wo, d_wo, nm_wo, nv_wo = _adamw_big(w_out[0], m_w_out[0], v_w_out[0], parts_out, _pick(ro, 64), "adamw_w_out")
    g_wi, d_wi, nm_wi, nv_wi = _adamw_big(w_in[0], m_w_in[0], v_w_in[0], parts_in, _pick(D, 128), "adamw_w_in")

    total = _allreduce_small(small_rows, D)
    pad = lambda a: jnp.pad(a, ((0, 0), (0, LANES - n_q)))
    moments = [pad(m_attn_sinks), m_lb_logits, m_rnn_norm, m_pre_norm, m_post_norm,
               pad(v_attn_sinks), v_lb_logits, v_rnn_norm, v_pre_norm, v_post_norm]
    res = _adamw_small(total, pad(attn_sinks), lb_logits, rnn_norm, pre_norm, post_norm, moments, D)
    loss = res[0][0, 0]
    small = [[res[1 + 4 * i + j] for i in range(5)] for j in range(4)]
    for j in range(4):
        small[j][0] = small[j][0][:, :n_q]

    def assemble(j, wi, wo):
        s = small[j]
        return [wi[None], s[0], s[1], s[2], wo[None], s[3], s[4]]

    return (loss, grad_x[None], *assemble(0, g_wi, g_wo), *assemble(1, d_wi, d_wo),
            *assemble(2, nm_wi, nm_wo), *assemble(3, nv_wi, nv_wo))
```

```python
import functools

import jax
import jax.numpy as jnp
from jax import lax
from jax.experimental import pallas as pl
from jax.experimental.pallas import tpu as pltpu

F32 = jnp.float32
BF16 = jnp.bfloat16

ATTN_HEAD = 64
GQA = 8
WINDOW = 128
RNN_HEAD = 128
CHUNK = 64
NORM_EPS = 1e-6
LANES = 128
N_DEV = 8

ADAM_LR = 0.001
ADAM_B1 = 0.9
ADAM_B2 = 0.999
ADAM_EPS = 1e-08
ADAM_WD = 0.01
ADAM_STEP = 10

VMEM_LIMIT = 56 * 1024 * 1024
MESH = pl.DeviceIdType.MESH
ANY_SPEC = pl.BlockSpec(memory_space=pl.ANY)
HBM_SPEC = pl.BlockSpec(memory_space=pltpu.HBM)
SEM_SPEC = pl.BlockSpec(memory_space=pltpu.SEMAPHORE)
EFFECT = pltpu.SideEffectType.DATAFLOW_SIDE_EFFECTING

NT_DIMS = (((1,), (1,)), ((), ()))
NN_DIMS = (((1,), (0,)), ((), ()))
TN_DIMS = (((0,), (0,)), ((), ()))


def _params(*sem):
    return pltpu.CompilerParams(dimension_semantics=sem, vmem_limit_bytes=VMEM_LIMIT)


def _dot(a, b, dims):
    return lax.dot_general(a, b, dims, preferred_element_type=F32)


def _sigmoid(v):
    return 1.0 / (1.0 + jnp.exp(-v))


def _pick(n, pref):
    t = min(n, pref)
    assert n % t == 0, (n, pref)
    return t


MXU_COLS = 256
WIDE_TILE = 5 * MXU_COLS


def _col_tiles(wd):
    wide = wd // WIDE_TILE
    rest = wd - wide * WIDE_TILE
    assert rest % LANES == 0 and (rest == 0 or (wide * WIDE_TILE) % rest == 0), wd
    return [(WIDE_TILE, 0, wide)] * (wide > 0) + [(rest, wide * WIDE_TILE, 1)] * (rest > 0)


def _matmul(ids, a, b, *, grid, a_spec, b_spec, o_spec, out_shape, trans_b, name, after=(), prev=None):
    nk = grid[2]
    dims = NT_DIMS if trans_b else NN_DIMS
    n_skip = len(after) + (prev is not None)

    def body(ids_ref, a_ref, b_ref, *rest):
        del ids_ref
        o_ref, scratch = rest[n_skip], rest[n_skip + 1:]
        prod = _dot(a_ref[...], b_ref[...], dims)
        if nk == 1:
            o_ref[...] = prod.astype(o_ref.dtype)
        else:
            acc_ref, = scratch
            k = pl.program_id(2)

            @pl.when(k == 0)
            def _():
                acc_ref[...] = prod

            @pl.when(k > 0)
            def _():
                acc_ref[...] += prod

            @pl.when(k == nk - 1)
            def _():
                o_ref[...] = acc_ref[...].astype(o_ref.dtype)

    scratch = [] if nk == 1 else [pltpu.VMEM(tuple(d for d in o_spec.block_shape if d is not None), F32)]
    extra = list(after) + ([prev] if prev is not None else [])
    aliases = {3 + len(after): 0} if prev is not None else {}
    return pl.pallas_call(
        body, name=name, out_shape=out_shape, input_output_aliases=aliases,
        grid_spec=pltpu.PrefetchScalarGridSpec(
            num_scalar_prefetch=1, grid=grid, in_specs=[a_spec, b_spec] + [ANY_SPEC] * len(extra),
            out_specs=o_spec, scratch_shapes=scratch),
        compiler_params=_params("parallel", "parallel", "arbitrary"),
    )(ids, a, b, *extra)


def _proj_in(ids, h, w_roles, pos, n, prev, name, after=()):
    T, D = h.shape
    n_roles, _, wd = w_roles.shape
    tm = _pick(T, 1024)
    for t, (tn, first, count) in enumerate(_col_tiles(wd)):
        prev = _matmul(
            ids, h, w_roles, grid=(T // tm, n * count, 1),
            a_spec=pl.BlockSpec((tm, D), lambda i, j, k, ids: (i, 0)),
            b_spec=pl.BlockSpec((None, D, tn), functools.partial(
                lambda tn, first, count, i, j, k, ids: (ids[n_roles + pos + j // count], 0, first // tn + j % count),
                tn, first, count)),
            o_spec=pl.BlockSpec((pl.Element(tm), pl.Element(tn)), functools.partial(
                lambda tn, first, count, i, j, k, ids: (
                    i * tm, pl.multiple_of(ids[ids[n_roles + pos + j // count]] * wd + first + (j % count) * tn, LANES)),
                tn, first, count)),
            out_shape=jax.ShapeDtypeStruct((T, n_roles * wd), F32), trans_b=False, name=f"{name}_{t}",
            after=after, prev=prev)
    return prev


def _proj_out(ids, mixed, wo_roles):
    T, E = mixed.shape
    n_roles, R, D = wo_roles.shape
    tm, tn = _pick(T, 512), _pick(D, 512)

    def body(ids_ref, a_ref, b_ref, o_ref, a_roles):
        @pl.when(pl.program_id(1) == 0)
        def _():
            for k in range(n_roles):
                col = pl.multiple_of(ids_ref[k] * R, R)
                a_roles[:, k * R:(k + 1) * R] = a_ref[:, pl.ds(col, R)]

        o_ref[...] = _dot(a_roles[...], b_ref[...].reshape(E, tn), NN_DIMS)

    return pl.pallas_call(
        body, name="proj_out", out_shape=jax.ShapeDtypeStruct((T, D), F32),
        grid_spec=pltpu.PrefetchScalarGridSpec(
            num_scalar_prefetch=1, grid=(T // tm, D // tn),
            in_specs=[pl.BlockSpec((tm, E), lambda i, j, ids: (i, 0)),
                      pl.BlockSpec((n_roles, R, tn), lambda i, j, ids: (0, 0, j))],
            out_specs=pl.BlockSpec((tm, tn), lambda i, j, ids: (i, j)),
            scratch_shapes=[pltpu.VMEM((tm, E), BF16)]),
        compiler_params=_params("parallel", "arbitrary"))(ids, mixed, wo_roles)


def _dmixed(ids, dy, wo_roles):
    T, D = dy.shape
    n_roles, R, _ = wo_roles.shape
    tm = _pick(T, 1024)
    return _matmul(
        ids, dy, wo_roles, grid=(T // tm, n_roles, 1),
        a_spec=pl.BlockSpec((tm, D), lambda i, j, k, ids: (i, 0)),
        b_spec=pl.BlockSpec((None, R, D), lambda i, j, k, ids: (j, 0, 0)),
        o_spec=pl.BlockSpec((tm, R), lambda i, j, k, ids: (i, ids[j])),
        out_shape=jax.ShapeDtypeStruct((T, n_roles * R), F32), trans_b=True, name="dmixed")


def _dw_out(ids, mixed_t, dy, n_roles):
    E, T = mixed_t.shape
    D = dy.shape[1]
    R = E // n_roles
    tn = _pick(D, 512)
    return _matmul(
        ids, mixed_t, dy, grid=(n_roles, D // tn, 1),
        a_spec=pl.BlockSpec((R, T), lambda i, j, k, ids: (ids[i], 0)),
        b_spec=pl.BlockSpec((T, tn), lambda i, j, k, ids: (0, j)),
        o_spec=pl.BlockSpec((None, R, tn), lambda i, j, k, ids: (i, 0, j)),
        out_shape=jax.ShapeDtypeStruct((n_roles, R, D), BF16), trans_b=False, name="dw_out")


def _dw_in(ids, h_t, dproj, n_roles, r0, nr, name, after=()):
    D, T = h_t.shape
    wd = dproj.shape[1] // n_roles
    tm, out = _pick(D, 1024), None
    for t, (tn, first, count) in enumerate(_col_tiles(wd)):
        out = _matmul(
            ids, h_t, dproj, grid=(D // tm, nr * count, 1),
            a_spec=pl.BlockSpec((tm, T), lambda i, j, k, ids: (i, 0)),
            b_spec=pl.BlockSpec((pl.Element(T), pl.Element(tn)), functools.partial(
                lambda tn, first, count, i, j, k, ids: (
                    0, pl.multiple_of(ids[r0 + j // count] * wd + first + (j % count) * tn, LANES)),
                tn, first, count)),
            o_spec=pl.BlockSpec((None, tm, tn), functools.partial(
                lambda tn, first, count, i, j, k, ids: (j // count, i, first // tn + j % count), tn, first, count)),
            out_shape=jax.ShapeDtypeStruct((nr, D, wd), BF16), trans_b=False, name=f"{name}_{t}", after=after, prev=out)
    return out


def _dh(ids, dproj, w_roles, after=()):
    T = dproj.shape[0]
    n_roles, D, wd = w_roles.shape
    tm, tn = _pick(T, 1024), _pick(D, 512)
    nk, n_after = n_roles // 2, len(after)

    def body(ids_ref, a0_ref, a1_ref, b0_ref, b1_ref, *rest):
        del ids_ref
        o_ref, acc_ref = rest[n_after], rest[n_after + 1]
        prod = _dot(a0_ref[...], b0_ref[...], NT_DIMS) + _dot(a1_ref[...], b1_ref[...], NT_DIMS)
        k = pl.program_id(2)

        @pl.when(k == 0)
        def _():
            acc_ref[...] = prod

        @pl.when((k > 0) & (k < nk - 1))
        def _():
            acc_ref[...] += prod

        @pl.when(k == nk - 1)
        def _():
            o_ref[...] = acc_ref[...] + prod

    a_spec = lambda e: pl.BlockSpec((tm, wd), lambda i, j, k, ids: (i, ids[2 * k + e]))
    b_spec = lambda e: pl.BlockSpec((None, tn, wd), lambda i, j, k, ids: (2 * k + e, j, 0))
    return pl.pallas_call(
        body, name="dh", out_shape=jax.ShapeDtypeStruct((T, D), F32),
        grid_spec=pltpu.PrefetchScalarGridSpec(
            num_scalar_prefetch=1, grid=(T // tm, D // tn, nk),
            in_specs=[a_spec(0), a_spec(1), b_spec(0), b_spec(1)] + [ANY_SPEC] * n_after,
            out_specs=pl.BlockSpec((tm, tn), lambda i, j, k, ids: (i, j)),
            scratch_shapes=[pltpu.VMEM((tm, tn), F32)]),
        compiler_params=_params("parallel", "parallel", "arbitrary"),
    )(ids, dproj, dproj, w_roles, w_roles, *after)


def _cast_slot0(w, n_roles, name):
    R, C = w.shape
    tr = _pick(R, 256)

    def body(w_ref, o_ref):
        o_ref[...] = w_ref[...].astype(BF16)

    return pl.pallas_call(
        body, name=name, grid=(R // tr,), in_specs=[pl.BlockSpec((tr, C), lambda i: (i, 0))],
        out_specs=pl.BlockSpec((None, tr, C), lambda i: (0, i, 0)),
        out_shape=jax.ShapeDtypeStruct((n_roles, R, C), BF16), compiler_params=_params("parallel"))(w)


def _prenorm(x, gain):
    T, D = x.shape
    tm = _pick(T, 256)

    def body(x_ref, g_ref, h_ref, ht_ref):
        xv = x_ref[...]
        r = lax.rsqrt(jnp.mean(xv * xv, axis=-1, keepdims=True) + NORM_EPS)
        h = xv * r * g_ref[...]
        h_ref[...] = h.astype(BF16)
        ht_ref[...] = h.T.astype(BF16)

    return pl.pallas_call(
        body, name="prenorm", grid=(T // tm,),
        in_specs=[pl.BlockSpec((tm, D), lambda i: (i, 0)), pl.BlockSpec((1, D), lambda i: (0, 0))],
        out_specs=[pl.BlockSpec((tm, D), lambda i: (i, 0)), pl.BlockSpec((D, tm), lambda i: (0, i))],
        out_shape=[jax.ShapeDtypeStruct((T, D), BF16), jax.ShapeDtypeStruct((D, T), BF16)],
        compiler_params=_params("parallel"))(x, gain)


def _norm_bwd(u, yn, r):
    return r * (u - yn * jnp.mean(u * yn, axis=-1, keepdims=True))


def _loss_head(y, x, target, gain):
    T, D = y.shape
    tm = _pick(T, 256)

    def body(y_ref, x_ref, t_ref, g_ref, dy_ref, dout_ref, gpost_ref, sq_ref):
        yv = y_ref[...]
        g = g_ref[...]
        r = lax.rsqrt(jnp.mean(yv * yv, axis=-1, keepdims=True) + NORM_EPS)
        yn = yv * r
        err = x_ref[...] + yn * g - t_ref[...]
        dout = err * (1.0 / D)
        dy_ref[...] = _norm_bwd(dout * g, yn, r).astype(BF16)
        dout_ref[...] = dout

        @pl.when(pl.program_id(0) == 0)
        def _():
            gpost_ref[...] = jnp.zeros_like(gpost_ref)
            sq_ref[...] = jnp.zeros_like(sq_ref)

        gpost_ref[...] += jnp.sum(dout * yn, axis=0, keepdims=True)
        sq_ref[...] += jnp.sum(err * err, axis=0, keepdims=True)

    row = pl.BlockSpec((tm, D), lambda i: (i, 0))
    vec = pl.BlockSpec((1, D), lambda i: (0, 0))
    return pl.pallas_call(
        body, name="loss_head", grid=(T // tm,), in_specs=[row, row, row, vec], out_specs=[row, row, vec, vec],
        out_shape=[jax.ShapeDtypeStruct((T, D), BF16), jax.ShapeDtypeStruct((T, D), F32),
                   jax.ShapeDtypeStruct((1, D), F32), jax.ShapeDtypeStruct((1, D), F32)],
        compiler_params=_params("arbitrary"))(y, x, target, gain)


def _prenorm_bwd(x, dh, dout, gain):
    T, D = x.shape
    tm = _pick(T, 256)

    def body(x_ref, dh_ref, dout_ref, g_ref, gx_ref, gpre_ref):
        xv = x_ref[...]
        dhv = dh_ref[...]
        r = lax.rsqrt(jnp.mean(xv * xv, axis=-1, keepdims=True) + NORM_EPS)
        xn = xv * r
        gx_ref[...] = dout_ref[...] + _norm_bwd(dhv * g_ref[...], xn, r)

        @pl.when(pl.program_id(0) == 0)
        def _():
            gpre_ref[...] = jnp.zeros_like(gpre_ref)

        gpre_ref[...] += jnp.sum(dhv * xn, axis=0, keepdims=True)

    row = pl.BlockSpec((tm, D), lambda i: (i, 0))
    vec = pl.BlockSpec((1, D), lambda i: (0, 0))
    return pl.pallas_call(
        body, name="prenorm_bwd", grid=(T // tm,), in_specs=[row, row, row, vec], out_specs=[row, vec],
        out_shape=[jax.ShapeDtypeStruct((T, D), F32), jax.ShapeDtypeStruct((1, D), F32)],
        compiler_params=_params("arbitrary"))(x, dh, dout, gain)


def _attn_masks(n):
    row = lax.broadcasted_iota(jnp.int32, (2 * WINDOW, 2 * WINDOW), 0) % WINDOW
    col = lax.broadcasted_iota(jnp.int32, (2 * WINDOW, 2 * WINDOW), 1)
    valid = (col > row) & (col <= row + WINDOW) & ((n > 0) | (col >= WINDOW))
    low = lax.broadcasted_iota(jnp.int32, (1, LANES), 1) < ATTN_HEAD
    top = lax.broadcasted_iota(jnp.int32, (2 * WINDOW, 1), 0) < WINDOW
    return jnp.where(valid, 0.0, -jnp.inf), low, top


def _dup_half(pair, keep):
    return jnp.where(keep, pair, pltpu.roll(pair, ATTN_HEAD, 1))


def _fold_half(v):
    return v + pltpu.roll(v, ATTN_HEAD, 1)


ATTN_GROUP_FWD = 8
ATTN_GROUP = 4
ATTN_SCALE = ATTN_HEAD ** -0.5


def _attn_scores(qpair, k2, low):
    qs = qpair * ATTN_SCALE
    q2 = jnp.concatenate([jnp.where(low, qs, 0.0), jnp.where(low, 0.0, qs)], axis=0).astype(BF16)
    return q2, _dot(q2, k2, NT_DIMS)


def _attn_softmax(scores, sink_lo, sink_hi, bias, top):
    s = scores + bias
    sink = jnp.where(top, sink_lo, sink_hi)
    m = jnp.maximum(jnp.max(s, axis=-1, keepdims=True), sink)
    p = jnp.exp(s - m)
    psink = jnp.exp(sink - m)
    inv = 1.0 / (jnp.sum(p, axis=-1, keepdims=True) + psink)
    return p * inv, psink * inv


def _attn_specs(D):
    kb = D // LANES
    vb = kb + D // (8 * LANES)
    gb = (D + D // 4) // 512
    wide = lambda off: [pl.BlockSpec((WINDOW, 512), functools.partial(lambda o, e, jp, n: (n, o + 2 * jp + e), off, e))
                        for e in (0, 1)]
    cur = lambda off: pl.BlockSpec((WINDOW, LANES), functools.partial(lambda o, jp, n: (n, o + jp), off))
    prev = lambda off: pl.BlockSpec((WINDOW, LANES),
                                    functools.partial(lambda o, jp, n: (jnp.maximum(n - 1, 0), o + jp), off))
    return wide(0) + [cur(kb), prev(kb), cur(vb), prev(vb)] + wide(gb)


def _attn_fwd(proj, sinks, D):
    T = proj.shape[0]
    nb, njp = T // WINDOW, D // 1024

    def body(sink_ref, qlo_ref, qhi_ref, kc_ref, kp_ref, vc_ref, vp_ref, glo_ref, ghi_ref, mix_ref, mixt_ref):
        jp, n = pl.program_id(0), pl.program_id(1)
        bias, low, top = _attn_masks(n)
        kk = jnp.concatenate([kp_ref[...], kc_ref[...]], axis=0)
        vv = jnp.concatenate([vp_ref[...], vc_ref[...]], axis=0)
        q_refs, g_refs = (qlo_ref, qhi_ref), (glo_ref, ghi_ref)
        keeps = (low, jnp.logical_not(low))
        k2s = [_dup_half(kk, keep).astype(BF16) for keep in keeps]
        v2s = [_dup_half(vv, keep).astype(BF16) for keep in keeps]
        all_units = [(hj, p) for hj in (0, 1) for p in range(4)]
        for first in range(0, len(all_units), ATTN_GROUP_FWD):
            units = all_units[first:first + ATTN_GROUP_FWD]
            cols = {u: slice(LANES * u[1], LANES * (u[1] + 1)) for u in units}
            scores = {u: _attn_scores(q_refs[u[0]][:, cols[u]], k2s[u[0]], low)[1] for u in units}
            probs = {}
            for u in units:
                head = (2 * jp + u[0]) * GQA + 2 * u[1]
                probs[u] = _attn_softmax(scores[u], sink_ref[0, head], sink_ref[0, head + 1], bias, top)[0]
            o2s = {u: _dot(probs[u].astype(BF16), v2s[u[0]], NN_DIMS) for u in units}
            for u in units:
                opair = jnp.where(low, o2s[u][:WINDOW], o2s[u][WINDOW:])
                g = g_refs[u[0]][:, cols[u]]
                out = opair * (g * _sigmoid(g))
                oc = slice(512 * u[0] + LANES * u[1], 512 * u[0] + LANES * (u[1] + 1))
                mix_ref[:, oc] = out.astype(BF16)
                mixt_ref[oc, :] = out.T.astype(BF16)

    return pl.pallas_call(
        body, name="attn_fwd", grid=(njp, nb),
        in_specs=[pl.BlockSpec(memory_space=pltpu.SMEM)] + _attn_specs(D),
        out_specs=[pl.BlockSpec((WINDOW, 1024), lambda jp, n: (n, jp)),
                   pl.BlockSpec((1024, WINDOW), lambda jp, n: (jp, n))],
        out_shape=[jax.ShapeDtypeStruct((T, 2 * D), BF16), jax.ShapeDtypeStruct((2 * D, T), BF16)],
        compiler_params=_params("parallel", "parallel"))(sinks, *([proj] * 8))


def _flush_windows(bufs, sems, hbm_ref, corners, slot, step, last):
    def copies(sl):
        return [pltpu.make_async_copy(
            b.at[sl], hbm_ref.at[pl.ds(r0, b.shape[1]), pl.ds(c0, b.shape[2])], sems.at[sl, i])
            for i, (b, (r0, c0)) in enumerate(zip(bufs, corners))]

    for cp in copies(slot):
        cp.start()

    @pl.when(step > 0)
    def _():
        for cp in copies(1 - slot):
            cp.wait()

    @pl.when(step == last)
    def _():
        for cp in copies(slot):
            cp.wait()


def _attn_bwd(proj, sinks, dmix, D, after=()):
    T = proj.shape[0]
    nb, njp = T // WINDOW, D // 1024
    n_after = len(after)

    def body(sink_ref, qlo_ref, qhi_ref, kc_ref, kp_ref, vc_ref, vp_ref, glo_ref, ghi_ref, dmix_ref, *rest):
        dproj_ref, dsink_ref, kcarry_ref, vcarry_ref, dq_buf, dk_buf, dv_buf, dg_buf, out_sems = rest[n_after:]
        jp, step = pl.program_id(0), pl.program_id(1)
        n = nb - 1 - step
        slot = step % 2
        dq_ref, dk_ref, dv_ref, dg_ref = dq_buf.at[slot], dk_buf.at[slot], dv_buf.at[slot], dg_buf.at[slot]
        bias, low, top = _attn_masks(n)
        lane = lax.broadcasted_iota(jnp.int32, (1, LANES), 1)
        kk = jnp.concatenate([kp_ref[...], kc_ref[...]], axis=0)
        vv = jnp.concatenate([vp_ref[...], vc_ref[...]], axis=0)

        @pl.when(step == 0)
        def _():
            kcarry_ref[...] = jnp.zeros_like(kcarry_ref)
            vcarry_ref[...] = jnp.zeros_like(vcarry_ref)
            dsink_ref[...] = jnp.zeros_like(dsink_ref)

        dk_pair = jnp.zeros((2 * WINDOW, LANES), F32)
        dv_pair = jnp.zeros((2 * WINDOW, LANES), F32)
        dsink = jnp.zeros((1, LANES), F32)
        q_refs, g_refs = (qlo_ref, qhi_ref), (glo_ref, ghi_ref)
        keeps = (low, jnp.logical_not(low))
        k2s = [_dup_half(kk, keep).astype(BF16) for keep in keeps]
        v2s = [_dup_half(vv, keep).astype(BF16) for keep in keeps]
        dk_heads = [jnp.zeros((2 * WINDOW, LANES), F32) for _ in keeps]
        dv_heads = [jnp.zeros((2 * WINDOW, LANES), F32) for _ in keeps]
        all_units = [(hj, p) for hj in (0, 1) for p in range(4)]
        for first in range(0, len(all_units), ATTN_GROUP):
            units = all_units[first:first + ATTN_GROUP]
            cols = {u: slice(LANES * u[1], LANES * (u[1] + 1)) for u in units}
            ocs = {u: slice(512 * u[0] + LANES * u[1], 512 * u[0] + LANES * (u[1] + 1)) for u in units}
            qsc = {u: _attn_scores(q_refs[u[0]][:, cols[u]], k2s[u[0]], low) for u in units}
            probs, psinks, do2s = {}, {}, {}
            for u in units:
                head = (2 * jp + u[0]) * GQA + 2 * u[1]
                probs[u], psinks[u] = _attn_softmax(qsc[u][1], sink_ref[0, head], sink_ref[0, head + 1], bias, top)
                g = g_refs[u[0]][:, cols[u]]
                do = dmix_ref[:, ocs[u]] * (g * _sigmoid(g))
                do2s[u] = jnp.concatenate([jnp.where(low, do, 0.0), jnp.where(low, 0.0, do)], axis=0).astype(BF16)
            pbs = {u: probs[u].astype(BF16) for u in units}
            o2s = {u: _dot(pbs[u], v2s[u[0]], NN_DIMS) for u in units}
            dps = {u: _dot(do2s[u], v2s[u[0]], NT_DIMS) for u in units}
            dss = {}
            for u in units:
                opair = jnp.where(low, o2s[u][:WINDOW], o2s[u][WINDOW:])
                g = g_refs[u[0]][:, cols[u]]
                sg = _sigmoid(g)
                dg_ref[:, ocs[u]] = (dmix_ref[:, ocs[u]] * opair * (sg * (1.0 + g * (1.0 - sg)))).astype(BF16)
                delta = jnp.sum(probs[u] * dps[u], axis=-1, keepdims=True)
                dss[u] = (probs[u] * (dps[u] - delta)).astype(BF16)
                ps = psinks[u] * delta
                local = u[0] * GQA + 2 * u[1]
                dsink -= jnp.where(lane == local, jnp.sum(ps[:WINDOW], axis=0, keepdims=True), 0.0)
                dsink -= jnp.where(lane == local + 1, jnp.sum(ps[WINDOW:], axis=0, keepdims=True), 0.0)
            dq2s = {u: _dot(dss[u], k2s[u[0]], NN_DIMS) for u in units}
            for u in units:
                dk_heads[u[0]] += _dot(dss[u], qsc[u][0], TN_DIMS)
                dv_heads[u[0]] += _dot(pbs[u], do2s[u], TN_DIMS)
            for u in units:
                dq_ref[:, ocs[u]] = (jnp.where(low, dq2s[u][:WINDOW], dq2s[u][WINDOW:]) * ATTN_SCALE).astype(BF16)
        for keep, dk_head, dv_head in zip(keeps, dk_heads, dv_heads):
            dk_pair += jnp.where(keep, _fold_half(dk_head), 0.0)
            dv_pair += jnp.where(keep, _fold_half(dv_head), 0.0)
        dk_ref[...] = (dk_pair[WINDOW:] + kcarry_ref[...]).astype(BF16)
        dv_ref[...] = (dv_pair[WINDOW:] + vcarry_ref[...]).astype(BF16)
        kcarry_ref[...] = dk_pair[:WINDOW]
        vcarry_ref[...] = dv_pair[:WINDOW]
        dsink_ref[...] += dsink
        row = pl.multiple_of(n * WINDOW, WINDOW)
        col = lambda base, width: pl.multiple_of(base + jp * width, LANES)
        corners = [(row, col(0, 1024)), (row, col(D, LANES)), (row, col(D + D // 8, LANES)), (row, col(D + D // 4, 1024))]
        _flush_windows([dq_buf, dk_buf, dv_buf, dg_buf], out_sems, dproj_ref, corners, slot, step, nb - 1)

    rev = lambda spec: pl.BlockSpec(spec.block_shape, functools.partial(
        lambda f, jp, s: f(jp, nb - 1 - s), spec.index_map))
    in_specs = [rev(s) for s in _attn_specs(D)]
    wide_in = pl.BlockSpec((WINDOW, 1024), lambda jp, s: (nb - 1 - s, jp))
    return pl.pallas_call(
        body, name="attn_bwd", grid=(njp, nb),
        in_specs=[pl.BlockSpec(memory_space=pltpu.SMEM)] + in_specs + [wide_in] + [ANY_SPEC] * n_after,
        out_specs=[ANY_SPEC, pl.BlockSpec((None, 1, LANES), lambda jp, s: (jp, 0, 0))],
        out_shape=[jax.ShapeDtypeStruct((T, 6 * D + D // 4), BF16), jax.ShapeDtypeStruct((njp, 1, LANES), F32)],
        scratch_shapes=[pltpu.VMEM((WINDOW, LANES), F32), pltpu.VMEM((WINDOW, LANES), F32),
                        pltpu.VMEM((2, WINDOW, 1024), BF16), pltpu.VMEM((2, WINDOW, LANES), BF16),
                        pltpu.VMEM((2, WINDOW, LANES), BF16), pltpu.VMEM((2, WINDOW, 1024), BF16),
                        pltpu.SemaphoreType.DMA((2, 4))],
        compiler_params=_params("parallel", "arbitrary"))(sinks, *([proj] * 8), dmix, *after)


RNN_TB = 512
RNN_HB = 8
RNN_WIDE = RNN_HB * RNN_HEAD


def _split3(v):
    a = v.astype(BF16)
    r = v - a.astype(F32)
    b = r.astype(BF16)
    c = (r - b.astype(F32)).astype(BF16)
    return a, b, c


def _tri_sum(tri, v):
    a, b, c = _split3(v)
    return _dot(tri, a, NN_DIMS) + _dot(tri, b, NN_DIMS) + _dot(tri, c, NN_DIMS)


def _lower_bound(lb_ref):
    l0, l1 = lb_ref[0:1, :], lb_ref[1:2, :]
    m = jnp.maximum(l0, l1)
    e0, e1 = jnp.exp(l0 - m), jnp.exp(l1 - m)
    return e0 / (e0 + e1)


def _rnn_gates(rq, rf, lb):
    sq = _sigmoid(rq)
    sf = _sigmoid(rf)
    f = lb + (1.0 - lb) * sf
    return sq, sf, f


def _rnn_factors(G):
    last = G[CHUNK - 1:CHUNK, :]
    mid = G[CHUNK // 2 - 1:CHUNK // 2, :]
    return G, jnp.exp(G), jnp.exp(G - mid), jnp.exp(mid - G), jnp.exp(last - G), jnp.exp(last)


def _chunk_masks():
    r = lax.broadcasted_iota(jnp.int32, (CHUNK, CHUNK), 0)
    c = lax.broadcasted_iota(jnp.int32, (CHUNK, CHUNK), 1)
    return r >= c, (r >= c).astype(BF16), (r <= c).astype(BF16)


def _rnn_specs(T, D, tb, rev):
    nt = T // tb
    base = (2 * D + D // 4) // LANES
    t_of = (lambda s: nt - 1 - s) if rev else (lambda s: s)
    assert base % RNN_HB == 0 and (D // LANES) % RNN_HB == 0
    cols = [pl.BlockSpec((tb, RNN_WIDE), functools.partial(lambda o, h, s: (t_of(s), o + h),
                                                            (base + i * (D // LANES)) // RNN_HB))
            for i in range(4)]
    return cols, t_of


def _rnn_fwd(proj, lb_logits, rnn_norm, mixed, mixed_t, D):
    T = proj.shape[0]
    tb = _pick(T, RNN_TB)
    nt, nh, cpb = T // tb, D // RNN_HEAD, tb // CHUNK
    cols, _ = _rnn_specs(T, D, tb, False)

    def body(rq_ref, rf_ref, ri_ref, rg_ref, lb_ref, gain_ref, mix_in, mixt_in,
             mix_ref, mixt_ref, o_ref, st_ref, state_ref):
        del mix_in, mixt_in
        causal, tri, _ = _chunk_masks()
        lb = _lower_bound(lb_ref)

        @pl.when(pl.program_id(1) == 0)
        def _():
            state_ref[...] = jnp.zeros_like(state_ref)

        def chunk(c, carry):
            rows = pl.ds(pl.multiple_of(c * CHUNK, CHUNK), CHUNK)
            heads = range(RNN_HB)
            lns = [slice(RNN_HEAD * hh, RNN_HEAD * (hh + 1)) for hh in heads]
            qs, ks, Gs = [], [], []
            for ln in lns:
                rq, rf = rq_ref[rows, ln], rf_ref[rows, ln]
                sq, _, f = _rnn_gates(rq, rf, lb[:, ln])
                qs.append(rq * sq)
                ks.append(1.0 - f)
                Gs.append(_tri_sum(tri, jnp.log(f)))
            atts, inters, vbs = [], [], []
            for hh, ln in enumerate(lns):
                _, eG, eq, ek, ekl, elast = _rnn_factors(Gs[hh])
                q, k = qs[hh], ks[hh]
                st = state_ref[hh]
                st_ref[hh, c] = st
                vb = ri_ref[rows, ln].astype(BF16)
                vbs.append(vb)
                atts.append(_dot((q * eq).astype(BF16), (k * ek).astype(BF16), NT_DIMS))
                inters.append(_dot((q * eG).astype(BF16), st.astype(BF16), NT_DIMS))
                state_ref[hh] = st * elast + _dot(vb, (k * ekl).astype(BF16), TN_DIMS)
            intras = [_dot(jnp.where(causal, atts[hh], 0.0).astype(BF16), vbs[hh], NN_DIMS) for hh in heads]
            for hh, ln in enumerate(lns):
                o = inters[hh] + intras[hh]
                o_ref[rows, ln] = o
                rg = rg_ref[rows, ln]
                r = lax.rsqrt(jnp.mean(o * o, axis=-1, keepdims=True) + NORM_EPS)
                out = (o * r * gain_ref[:, ln]) * (rg * _sigmoid(rg))
                mix_ref[rows, ln] = out.astype(BF16)
            return carry

        lax.fori_loop(0, cpb, chunk, 0)
        mixt_ref[...] = mix_ref[...].astype(F32).T.astype(BF16)

    nh //= RNN_HB
    vec2 = pl.BlockSpec((2, RNN_WIDE), lambda h, s: (0, h))
    vec1 = pl.BlockSpec((1, RNN_WIDE), lambda h, s: (0, h))
    return pl.pallas_call(
        body, name="rnn_fwd", grid=(nh, nt),
        in_specs=cols + [vec2, vec1, ANY_SPEC, ANY_SPEC],
        out_specs=[pl.BlockSpec((tb, RNN_WIDE), lambda h, s: (s, D // RNN_WIDE + h)),
                   pl.BlockSpec((RNN_WIDE, tb), lambda h, s: (D // RNN_WIDE + h, s)),
                   pl.BlockSpec((tb, RNN_WIDE), lambda h, s: (s, h)),
                   pl.BlockSpec((RNN_HB, cpb, RNN_HEAD, RNN_HEAD), lambda h, s: (h, s, 0, 0))],
        out_shape=[jax.ShapeDtypeStruct(mixed.shape, BF16), jax.ShapeDtypeStruct(mixed_t.shape, BF16),
                   jax.ShapeDtypeStruct((T, D), F32),
                   jax.ShapeDtypeStruct((nh * RNN_HB, T // CHUNK, RNN_HEAD, RNN_HEAD), F32)],
        scratch_shapes=[pltpu.VMEM((RNN_HB, RNN_HEAD, RNN_HEAD), F32)],
        input_output_aliases={6: 0, 7: 1},
        compiler_params=_params("parallel", "arbitrary"))(proj, proj, proj, proj, lb_logits, rnn_norm, mixed, mixed_t)


def _rnn_bwd(proj, lb_logits, rnn_norm, o_raw, states, dmix, dproj, D):
    T = proj.shape[0]
    tb = _pick(T, RNN_TB)
    nt, nh, cpb = T // tb, D // RNN_HEAD, tb // CHUNK
    cols, t_of = _rnn_specs(T, D, tb, True)

    def body(rq_ref, rf_ref, ri_ref, rg_ref, lb_ref, gain_ref, o_ref, st_ref, dmix_ref, dproj_in,
             dproj_ref, dlb_ref, dgain_ref, dstate_ref, out_buf, out_sems):
        del dproj_in
        step = pl.program_id(1)
        slot = step % 2
        drq_ref, drf_ref, dri_ref, drg_ref = [out_buf.at[i, slot] for i in range(4)]
        causal, tri, tri_t = _chunk_masks()
        lb = _lower_bound(lb_ref)
        gain = gain_ref[...]

        @pl.when(pl.program_id(1) == 0)
        def _():
            dstate_ref[...] = jnp.zeros_like(dstate_ref)
            dlb_ref[...] = jnp.zeros_like(dlb_ref)
            dgain_ref[...] = jnp.zeros_like(dgain_ref)

        def chunk(i, carry):
            c = cpb - 1 - i
            rows = pl.ds(pl.multiple_of(c * CHUNK, CHUNK), CHUNK)
            heads = range(RNN_HB)
            lns = [slice(RNN_HEAD * hh, RNN_HEAD * (hh + 1)) for hh in heads]
            last_row = lax.broadcasted_iota(jnp.int32, (CHUNK, 1), 0) == CHUNK - 1
            A = []
            for ln in lns:
                rq, rf, rg = rq_ref[rows, ln], rf_ref[rows, ln], rg_ref[rows, ln]
                o, dgated = o_ref[rows, ln], dmix_ref[rows, ln]
                gainh = gain[:, ln]
                sgt = _sigmoid(rg)
                r = lax.rsqrt(jnp.mean(o * o, axis=-1, keepdims=True) + NORM_EPS)
                on = o * r
                drg_ref[rows, ln] = (dgated * (on * gainh) * (sgt * (1.0 + rg * (1.0 - sgt)))).astype(BF16)
                d_on = dgated * (rg * sgt)
                dgain_ref[:, ln] += jnp.sum(d_on * on, axis=0, keepdims=True)
                dob = _norm_bwd(d_on * gainh, on, r).astype(BF16)
                sq, sf, f = _rnn_gates(rq, rf, lb[:, ln])
                A.append(dict(rq=rq, sq=sq, sf=sf, f=f, dob=dob, G=_tri_sum(tri, jnp.log(f))))
            for hh, ln in enumerate(lns):
                a = A[hh]
                _, eG, eq, ek, ekl, elast = _rnn_factors(a.pop("G"))
                q, k = a["rq"] * a["sq"], 1.0 - a["f"]
                st, dst = st_ref[hh, c], dstate_ref[hh]
                qg, kl = q * eG, k * ekl
                qmb, kmb = (q * eq).astype(BF16), (k * ek).astype(BF16)
                dob, vb, dstb = a["dob"], ri_ref[rows, ln].astype(BF16), dst.astype(BF16)
                a.update(eG=eG, eq=eq, ek=ek, ekl=ekl, qg=qg, kl=kl, qmb=qmb, kmb=kmb,
                         att=_dot(qmb, kmb, NT_DIMS), datt=_dot(dob, vb, NT_DIMS),
                         dqg=_dot(dob, st.astype(BF16), NN_DIMS), dkl=_dot(vb, dstb, NN_DIMS),
                         dri=_dot(kl.astype(BF16), dstb, NT_DIMS),
                         dlast=jnp.sum(dst * st, axis=0, keepdims=True) * elast)
                dstate_ref[hh] = dst * elast + _dot(dob, qg.astype(BF16), TN_DIMS)
            for hh, ln in enumerate(lns):
                a = A[hh]
                att = jnp.where(causal, a.pop("att"), 0.0).astype(BF16)
                datt = jnp.where(causal, a.pop("datt"), 0.0).astype(BF16)
                dqm = _dot(datt, a["kmb"], NN_DIMS)
                dkm = _dot(datt, a["qmb"], TN_DIMS)
                dri_ref[rows, ln] = (_dot(att, a["dob"], TN_DIMS) + a.pop("dri")).astype(BF16)
                dqg, dkl, kl = a.pop("dqg"), a.pop("dkl"), a.pop("kl")
                a["dq"] = dqg * a.pop("eG") + dqm * a.pop("eq")
                a["dk"] = dkm * a.pop("ek") + dkl * a.pop("ekl")
                dG = dqg * a.pop("qg") + dqm * a.pop("qmb").astype(F32) - dkm * a.pop("kmb").astype(F32) - dkl * kl
                dlast = jnp.sum(dkl * kl, axis=0, keepdims=True) + a.pop("dlast")
                a["dg"] = _tri_sum(tri_t, dG + jnp.where(last_row, dlast, 0.0))
            for hh, ln in enumerate(lns):
                a = A[hh]
                rq, sq, sf = a["rq"], a["sq"], a["sf"]
                df = a["dg"] / a["f"] - a["dk"]
                drq_ref[rows, ln] = (a["dq"] * (sq * (1.0 + rq * (1.0 - sq)))).astype(BF16)
                drf_ref[rows, ln] = (df * (1.0 - lb[:, ln]) * (sf * (1.0 - sf))).astype(BF16)
                dlb_ref[:, ln] += jnp.sum(df * (1.0 - sf), axis=0, keepdims=True)
            return carry

        lax.fori_loop(0, cpb, chunk, 0)
        row = pl.multiple_of(t_of(step) * tb, tb)
        col0 = 2 * D + D // 4 + pl.program_id(0) * RNN_WIDE
        corners = [(row, pl.multiple_of(col0 + i * D, LANES)) for i in range(4)]
        _flush_windows([out_buf.at[i] for i in range(4)], out_sems, dproj_ref, corners, slot, step, nt - 1)

    nh //= RNN_HB
    vec2 = pl.BlockSpec((2, RNN_WIDE), lambda h, s: (0, h))
    vec1 = pl.BlockSpec((1, RNN_WIDE), lambda h, s: (0, h))
    blk = pl.BlockSpec((tb, RNN_WIDE), lambda h, s: (t_of(s), h))
    return pl.pallas_call(
        body, name="rnn_bwd", grid=(nh, nt),
        in_specs=cols + [vec2, vec1, blk,
                         pl.BlockSpec((RNN_HB, cpb, RNN_HEAD, RNN_HEAD), lambda h, s: (h, t_of(s), 0, 0)),
                         pl.BlockSpec((tb, RNN_WIDE), lambda h, s: (t_of(s), D // RNN_WIDE + h)), ANY_SPEC],
        out_specs=[ANY_SPEC, vec1, vec1],
        out_shape=[jax.ShapeDtypeStruct(dproj.shape, BF16)] + [jax.ShapeDtypeStruct((1, D), F32)] * 2,
        scratch_shapes=[pltpu.VMEM((RNN_HB, RNN_HEAD, RNN_HEAD), F32), pltpu.VMEM((4, 2, tb, RNN_WIDE), BF16),
                        pltpu.SemaphoreType.DMA((2, 4))],
        input_output_aliases={9: 0},
        compiler_params=_params("parallel", "arbitrary"))(proj, proj, proj, proj, lb_logits, rnn_norm, o_raw, states, dmix, dproj)


def _adamw(w, g, m, v):
    m = ADAM_B1 * m + (1.0 - ADAM_B1) * g
    v = ADAM_B2 * v + (1.0 - ADAM_B2) * (g * g)
    m_hat = m / (1.0 - ADAM_B1 ** ADAM_STEP)
    v_hat = v / (1.0 - ADAM_B2 ** ADAM_STEP)
    delta = -ADAM_LR * (m_hat / (jnp.sqrt(v_hat) + ADAM_EPS) + ADAM_WD * w)
    return delta, m, v


def _adamw_big(w, m, v, parts, tr, name):
    R, C = w.shape
    n_parts = len(parts)

    def body(w_ref, m_ref, v_ref, *rest):
        part_refs = rest[:n_parts]
        g_ref, d_ref, nm_ref, nv_ref = rest[n_parts:]
        g = part_refs[0][...].astype(F32)
        for p_ref in part_refs[1:]:
            g = g + p_ref[...].astype(F32)
        delta, nm, nv = _adamw(w_ref[...], g, m_ref[...], v_ref[...])
        g_ref[...] = g
        d_ref[...] = delta
        nm_ref[...] = nm
        nv_ref[...] = nv

    blk = pl.BlockSpec((tr, C), lambda i: (i, 0))
    part_specs = [pl.BlockSpec((None, tr, C), functools.partial(lambda s, i: (s, i, 0), slot)) for _, slot in parts]
    return pl.pallas_call(
        body, name=name, grid=(R // tr,), in_specs=[blk, blk, blk] + part_specs,
        out_specs=[blk] * 4, out_shape=[jax.ShapeDtypeStruct((R, C), F32)] * 4,
        compiler_params=_params("parallel"))(w, m, v, *[a for a, _ in parts])


def _pair_sum(pa, qa):
    n, R, C = qa.shape
    tr = _pick(R, 256)

    def body(p_ref, q_ref, r_ref):
        r_ref[...] = (p_ref[...].astype(F32) + q_ref[...].astype(F32)).astype(BF16)

    return pl.pallas_call(
        body, name="pair_sum", grid=(n, R // tr),
        in_specs=[pl.BlockSpec((None, tr, C), lambda j, i: (2 * j, i, 0)), pl.BlockSpec((None, tr, C), lambda j, i: (j, i, 0))],
        out_specs=pl.BlockSpec((None, tr, C), lambda j, i: (j, i, 0)),
        out_shape=jax.ShapeDtypeStruct((n, R, C), BF16), compiler_params=_params("parallel", "parallel"))(pa, qa)


def _adamw_small(total, sinks, lb_logits, rnn_norm, pre_norm, post_norm, moments, D):
    params = [sinks, lb_logits, rnn_norm, pre_norm, post_norm]

    def body(tot_ref, *refs):
        p_refs, m_refs, v_refs = refs[0:5], refs[5:10], refs[10:15]
        loss_ref, outs = refs[15], refs[16:]
        tot = tot_ref[...]
        l0, l1 = p_refs[1][0:1, :], p_refs[1][1:2, :]
        mx = jnp.maximum(l0, l1)
        e0, e1 = jnp.exp(l0 - mx), jnp.exp(l1 - mx)
        p0, p1 = e0 / (e0 + e1), e1 / (e0 + e1)
        dlb = tot[0:1, :]
        grads = [tot[5:6, 0:LANES], jnp.concatenate([dlb * p0 * (1.0 - p0), -dlb * p0 * p1], axis=0),
                 tot[1:2, :], tot[2:3, :], tot[3:4, :]]
        loss_ref[...] = 0.5 / D * jnp.sum(tot[4:5, :], axis=-1, keepdims=True)
        for i, g in enumerate(grads):
            delta, nm, nv = _adamw(p_refs[i][...], g, m_refs[i][...], v_refs[i][...])
            outs[4 * i][...] = g
            outs[4 * i + 1][...] = delta
            outs[4 * i + 2][...] = nm
            outs[4 * i + 3][...] = nv

    out_shape = [jax.ShapeDtypeStruct((1, 1), F32)]
    for p in params:
        out_shape += [jax.ShapeDtypeStruct(p.shape, F32)] * 4
    return pl.pallas_call(body, name="adamw_small", out_shape=out_shape)(total, *params, *moments)


SIBLING = 1


def _me():
    return lax.axis_index("x"), lax.axis_index("y"), lax.axis_index("c")


def _flat(px, py, pc):
    return 4 * px + 2 * py + pc


def _role_peer(role, x, y, c):
    if not isinstance(role, int):
        j = (role - 2) // 2
        px = jnp.where((role >= 2) & (j != 1), 1 - x, x)
        py = jnp.where((role >= 2) & (j != 0), 1 - y, y)
        return (px, py, jnp.where(role % 2 == 1, 1 - c, c))
    if role < 2:
        return (x, y, (1 - c) if role else c)
    j, other = (role - 2) // 2, (role - 2) % 2
    px = (1 - x) if j in (0, 2) else x
    py = (1 - y) if j in (1, 2) else y
    return (px, py, (1 - c) if other else c)


def _wave1(x, y, c):
    return jnp.where(c == 1, 2, 4)


def _wave2(x, y, c):
    return jnp.where(c == 1, 4, 2)


def _role_ids():
    x, y, c = _me()
    w1, w2 = _wave1(x, y, c), _wave2(x, y, c)
    order = [0, 1, w1, w2 + 1, w2, w1 + 1, 6, 7]
    return jnp.stack([_flat(*_role_peer(r, x, y, c)) for r in range(N_DEV)] + order).astype(jnp.int32)


def _comm_call(name, bufs, waits=(), starts=(), after=()):
    n_buf, n_wait, n_start, n_after = len(bufs), len(waits), len(starts), len(after)

    def body(*refs):
        buf_refs = refs[:n_buf]
        sem_refs = refs[n_buf:n_buf + 2 * n_wait]
        outs = refs[n_buf + 2 * n_wait + n_after:]
        new_sems, token = outs[:2 * n_start], outs[-1]
        x, y, c = _me()

        def val(v):
            return v(x, y, c) if callable(v) else v

        def block(ref, slot, rows):
            if rows is None:
                return ref.at[val(slot)]
            first = rows[0] if isinstance(rows[0], int) else pl.multiple_of(val(rows[0]), 16)
            return ref.at[val(slot), pl.ds(first, rows[1])]

        for w, (_, kind, like, *rows) in enumerate(waits):
            shape_ref = block(buf_refs[like], 0, (0, rows[0][1]) if rows else None)
            cp = pltpu.make_async_remote_copy(
                src_ref=shape_ref, dst_ref=shape_ref, send_sem=sem_refs[2 * w], recv_sem=sem_refs[2 * w + 1],
                device_id=(x, y, c), device_id_type=MESH)
            if kind == "send":
                cp.wait_send()
            else:
                cp.wait_recv()
        for s, (sb, ss, db, ds, role, *rows) in enumerate(starts):
            rows = rows[0] if rows else None
            pltpu.make_async_remote_copy(
                src_ref=block(buf_refs[sb], ss, rows), dst_ref=block(buf_refs[db], ds, rows), send_sem=new_sems[2 * s],
                recv_sem=new_sems[2 * s + 1], device_id=_role_peer(val(role), x, y, c), device_id_type=MESH).start()
        token[...] = jnp.zeros_like(token)

    sems = [s for flight, *_ in waits for s in flight]
    out = pl.pallas_call(
        body, name=name,
        out_shape=tuple([pltpu.SemaphoreType.DMA(())] * (2 * n_start) + [pltpu.HBM(b.shape, b.dtype) for b in bufs]
                        + [jax.ShapeDtypeStruct((8, LANES), F32)]),
        in_specs=tuple([HBM_SPEC] * n_buf + [SEM_SPEC] * (2 * n_wait) + [ANY_SPEC] * n_after),
        out_specs=tuple([SEM_SPEC] * (2 * n_start) + [HBM_SPEC] * n_buf + [pl.BlockSpec(memory_space=pltpu.VMEM)]),
        input_output_aliases={i: 2 * n_start + i for i in range(n_buf)},
        compiler_params=pltpu.CompilerParams(has_side_effects=EFFECT),
    )(*[pltpu.with_memory_space_constraint(b, pltpu.HBM) for b in bufs], *sems, *after)
    flights = [(out[2 * s], out[2 * s + 1]) for s in range(n_start)]
    return list(out[2 * n_start:2 * n_start + n_buf]), flights, out[-1]


def _landing(shape):
    return lax.empty(shape, BF16)


def _both(flights, like):
    return [(f, kind, like) for f in flights for kind in ("send", "recv")]


def _allreduce_small(rows, D):
    n_rows = len(rows)

    def body(*refs):
        in_refs, out_ref, all_ref = refs[:n_rows], refs[n_rows], refs[n_rows + 1]
        send_sems, recv_sems = refs[n_rows + 2], refs[n_rows + 3]
        x, y, c = _me()
        mine = all_ref.at[_flat(x, y, c)]
        mine[...] = jnp.zeros((8, D), F32)
        for i, r in enumerate(in_refs):
            mine[i:i + 1, :] = r[...]
        copies = []
        for k in range(N_DEV - 1):
            copies.append(pltpu.make_async_remote_copy(
                src_ref=mine, dst_ref=mine, send_sem=send_sems.at[k], recv_sem=recv_sems.at[k],
                device_id=_role_peer(k + 1, x, y, c), device_id_type=MESH))
        for cp in copies:
            cp.start()
        for cp in copies:
            cp.wait_recv()
        for cp in copies:
            cp.wait_send()
        tot = all_ref[0]
        for d in range(1, N_DEV):
            tot = tot + all_ref[d]
        out_ref[...] = tot

    vm = pl.BlockSpec(memory_space=pltpu.VMEM)
    return pl.pallas_call(
        body, name="allreduce_small", in_specs=[vm] * n_rows, out_specs=vm,
        out_shape=jax.ShapeDtypeStruct((8, D), F32),
        scratch_shapes=[pltpu.VMEM((N_DEV, 8, D), F32), pltpu.SemaphoreType.DMA((7,)), pltpu.SemaphoreType.DMA((7,))],
    )(*rows)


def _step(ids, x, target, g_in, g_out, sinks, lb_logits, rnn_norm, pre_norm, post_norm, dist):
    T, D = x.shape
    n_roles, _, wd = g_in.shape
    ro = g_out.shape[1]
    assert n_roles % 2 == 0 and (not dist or n_roles == N_DEV)

    half = D // 2
    up = lambda f: (lambda x, y, c: f(x, y, c) + 1)
    rows_of = lambda f: (lambda x, y, c: jnp.where(f(x, y, c) == 2, 0, half))
    if dist:
        (g_in,), fl_in, _ = _comm_call(
            "gather_in_start", [g_in], starts=[(0, 0, 0, SIBLING, SIBLING), (0, 0, 0, _wave1, _wave1)])
    h, h_t = _prenorm(x, pre_norm)
    proj = _proj_in(ids, h, g_in, 0, 1, None, "proj_in_0")
    if dist:
        (g_in,), _, _ = _comm_call("gather_in_sibling", [g_in], waits=[(fl_in[0], "recv", 0)], after=(proj,))
    proj = _proj_in(ids, h, g_in, 1, 1, proj, "proj_in_1")
    if not dist:
        for p in range(2, n_roles, 2):
            proj = _proj_in(ids, h, g_in, p, 2, proj, f"proj_in_{p}")
    else:
        (g_in,), fl_a, _ = _comm_call(
            "gather_in_wave_1", [g_in], waits=[(fl_in[1], "recv", 0)],
            starts=[(0, 0, 0, _wave2, _wave2), (0, _wave1, 0, up(_wave1), SIBLING),
                    (0, _wave1, 0, 6, _wave2, (rows_of(_wave1), half))], after=(proj,))
        proj = _proj_in(ids, h, g_in, 2, 1, proj, "proj_in_2")
        (g_in,), _, _ = _comm_call("gather_in_passed_1", [g_in], waits=[(fl_a[1], "recv", 0)], after=(proj,))
        proj = _proj_in(ids, h, g_in, 3, 1, proj, "proj_in_3")
        (g_in,), fl_b, _ = _comm_call(
            "gather_in_wave_2", [g_in], waits=[(fl_a[0], "recv", 0)],
            starts=[(0, _wave2, 0, up(_wave2), SIBLING), (0, _wave2, 0, 6, _wave1, (rows_of(_wave2), half))],
            after=(proj,))
        proj = _proj_in(ids, h, g_in, 4, 1, proj, "proj_in_4")
        (g_in,), _, _ = _comm_call("gather_in_passed_2", [g_in], waits=[(fl_b[0], "recv", 0)], after=(proj,))
        proj = _proj_in(ids, h, g_in, 5, 1, proj, "proj_in_5")
        (g_in,), fl_c, _ = _comm_call(
            "gather_in_diagonal", [g_in], waits=[(fl_a[2], "recv", 0, (0, half)), (fl_b[1], "recv", 0, (0, half))],
            starts=[(0, 6, 0, 7, SIBLING)], after=(proj,))
        proj = _proj_in(ids, h, g_in, 6, 1, proj, "proj_in_6")
        (g_in, g_out), fl_out, token = _comm_call(
            "gather_out_start", [g_in, g_out], waits=[(fl_c[0], "recv", 0)],
            starts=[(1, 0, 1, SIBLING, SIBLING)] + [(1, 0, 1, 2 + 2 * i, 2 + 2 * i) for i in range(3)], after=(proj,))
        proj = _proj_in(ids, h, g_in, 7, 1, proj, "proj_in_7", (token,))
        send_waits = [(f, "send", 0) for f in fl_in + fl_a[:2] + fl_b[:1] + fl_c] \
            + [(f, "send", 0, (0, half)) for f in (fl_a[2], fl_b[1])]
    mixed, mixed_t = _attn_fwd(proj, sinks, D)
    if dist:
        (g_in, g_out), fl_out_fwd, _ = _comm_call(
            "gather_out_pass", [g_in, g_out],
            waits=send_waits + [(fl_out[1 + i], "recv", 1) for i in range(3)],
            starts=[(1, 2 + 2 * i, 1, 3 + 2 * i, SIBLING) for i in range(3)], after=(mixed,))
    mixed, mixed_t, o_raw, states = _rnn_fwd(proj, lb_logits, rnn_norm, mixed, mixed_t, D)
    if dist:
        (g_out,), _, _ = _comm_call(
            "gather_out_done", [g_out],
            waits=[(fl_out[0], "recv", 0)] + [(f, "recv", 0) for f in fl_out_fwd]
            + [(f, "send", 0) for f in fl_out + fl_out_fwd], after=(states,))
    y = _proj_out(ids, mixed, g_out)
    dy, dout, g_post, sq_err = _loss_head(y, x, target, post_norm)

    dmix = _dmixed(ids, dy, g_out)
    p_out, after = _dw_out(ids, mixed_t, dy, n_roles), ()
    if dist:
        (p_out, l_out), fl_so, token = _comm_call(
            "scatter_out_start", [p_out, _landing((n_roles - 1, ro, D))],
            starts=[(0, r, 1, r - 1, r) for r in range(1, n_roles)])
        after = (token,)
    dproj, d_sink = _attn_bwd(proj, sinks, dmix, D, after)
    dproj, d_lb, g_rnn = _rnn_bwd(proj, lb_logits, rnn_norm, o_raw, states, dmix, dproj, D)

    p_far = _dw_in(ids, h_t, dproj, n_roles, 2, n_roles - 2, "dw_in_far")
    if dist:
        (p_far, q_far), fl_pair, token = _comm_call(
            "scatter_in_pair_start", [p_far, _landing((3, D, wd))],
            starts=[(0, 1 + 2 * j, 1, j, SIBLING) for j in range(3)])
        p_near = _dw_in(ids, h_t, dproj, n_roles, 0, 2, "dw_in_near", after=(token,))
        (p_far, q_far), _, _ = _comm_call("scatter_in_pair_wait", [p_far, q_far], waits=_both(fl_pair, 1), after=(p_near,))
        chip_sum = _pair_sum(p_far, q_far)
        (chip_sum, z_far), fl_chip, token = _comm_call(
            "scatter_in_chip_start", [chip_sum, _landing((3, D, wd))],
            starts=[(0, j, 1, j, 2 + 2 * j) for j in range(3)])
        (p_near, q_near), fl_sib, token = _comm_call(
            "scatter_in_sibling_start", [p_near, _landing((1, D, wd))], starts=[(0, 1, 1, 0, SIBLING)], after=(token,))
        after = (token,)
    else:
        p_near, after = _dw_in(ids, h_t, dproj, n_roles, 0, 2, "dw_in_near"), ()
    dh = _dh(ids, dproj, g_in, after)
    grad_x, g_pre = _prenorm_bwd(x, dh, dout, pre_norm)
    n_q = D // ATTN_HEAD
    sink_row = jnp.pad(d_sink[:, 0, :2 * GQA].reshape(1, n_q), ((0, 0), (0, D - n_q)))
    rows = [d_lb, g_rnn, g_pre, g_post, sq_err, sink_row]
    if not dist:
        return grad_x, [(p_near, r) for r in range(2)] + [(p_far, r) for r in range(n_roles - 2)], \
            [(p_out, r) for r in range(n_roles)], rows

    (p_out, l_out), _, _ = _comm_call("scatter_out_wait", [p_out, l_out], waits=_both(fl_so, 1), after=(grad_x,))
    (chip_sum, z_far, p_near, q_near), _, _ = _comm_call(
        "scatter_in_wait", [chip_sum, z_far, p_near, q_near], waits=_both(fl_chip, 1) + _both(fl_sib, 3), after=(p_out,))
    parts_in = [(p_near, 0), (q_near, 0)] + [(z_far, j) for j in range(3)]
    parts_out = [(p_out, 0)] + [(l_out, k) for k in range(n_roles - 1)]
    return grad_x, parts_in, parts_out, rows


def kernel(x, w_in, attn_sinks, lb_logits, rnn_norm, w_out, pre_norm, post_norm, loss_target, m_w_in, m_attn_sinks, m_lb_logits, m_rnn_norm, m_w_out, m_pre_norm, m_post_norm, v_w_in, v_attn_sinks, v_lb_logits, v_rnn_norm, v_w_out, v_pre_norm, v_post_norm):
    _, T, D = x.shape
    ro = w_out.shape[1]
    n_q = attn_sinks.shape[1]
    assert lb_logits.shape[0] == 2 and n_q == D // ATTN_HEAD and n_q <= LANES

    grad_x, parts_in, parts_out, small_rows = _step(
        _role_ids(), x[0], loss_target[0], _cast_slot0(w_in[0], N_DEV, "cast_w_in"),
        _cast_slot0(w_out[0], N_DEV, "cast_w_out"), attn_sinks, lb_logits, rnn_norm, pre_norm, post_norm, True)

    g_wo, d_wo, nm_wo, nv_wo = _adamw_big(w_out[0], m_w_out[0], v_w_out[0], parts_out, _pick(ro, 64), "adamw_w_out")
    g_wi, d_wi, nm_wi, nv_wi = _adamw_big(w_in[0], m_w_in[0], v_w_in[0], parts_in, _pick(D, 128), "adamw_w_in")

    total = _allreduce_small(small_rows, D)
    pad = lambda a: jnp.pad(a, ((0, 0), (0, LANES - n_q)))
    moments = [pad(m_attn_sinks), m_lb_logits, m_rnn_norm, m_pre_norm, m_post_norm,
               pad(v_attn_sinks), v_lb_logits, v_rnn_norm, v_pre_norm, v_post_norm]
    res = _adamw_small(total, pad(attn_sinks), lb_logits, rnn_norm, pre_norm, post_norm, moments, D)
    loss = res[0][0, 0]
    small = [[res[1 + 4 * i + j] for i in range(5)] for j in range(4)]
    for j in range(4):
        small[j][0] = small[j][0][:, :n_q]

    def assemble(j, wi, wo):
        s = small[j]
        return [wi[None], s[0], s[1], s[2], wo[None], s[3], s[4]]

    return (loss, grad_x[None], *assemble(0, g_wi, g_wo), *assemble(1, d_wi, d_wo),
            *assemble(2, nm_wi, nm_wo), *assemble(3, nv_wi, nv_wo))
```

```python
import functools

import jax
import jax.numpy as jnp
from jax import lax
from jax.experimental import pallas as pl
from jax.experimental.pallas import tpu as pltpu

F32 = jnp.float32
BF16 = jnp.bfloat16

ATTN_HEAD = 64
GQA = 8
WINDOW = 128
RNN_HEAD = 128
CHUNK = 64
NORM_EPS = 1e-6
LANES = 128
N_DEV = 8

ADAM_LR = 0.001
ADAM_B1 = 0.9
ADAM_B2 = 0.999
ADAM_EPS = 1e-08
ADAM_WD = 0.01
ADAM_STEP = 10

VMEM_LIMIT = 56 * 1024 * 1024
MESH = pl.DeviceIdType.MESH
ANY_SPEC = pl.BlockSpec(memory_space=pl.ANY)
HBM_SPEC = pl.BlockSpec(memory_space=pltpu.HBM)
SEM_SPEC = pl.BlockSpec(memory_space=pltpu.SEMAPHORE)
EFFECT = pltpu.SideEffectType.DATAFLOW_SIDE_EFFECTING

NT_DIMS = (((1,), (1,)), ((), ()))
NN_DIMS = (((1,), (0,)), ((), ()))
TN_DIMS = (((0,), (0,)), ((), ()))


def _params(*sem):
    return pltpu.CompilerParams(dimension_semantics=sem, vmem_limit_bytes=VMEM_LIMIT)


def _dot(a, b, dims):
    return lax.dot_general(a, b, dims, preferred_element_type=F32)


def _sigmoid(v):
    return 1.0 / (1.0 + jnp.exp(-v))


def _pick(n, pref):
    t = min(n, pref)
    assert n % t == 0, (n, pref)
    return t


MXU_COLS = 256
WIDE_TILE = 5 * MXU_COLS


def _col_tiles(wd):
    wide = wd // WIDE_TILE
    rest = wd - wide * WIDE_TILE
    assert rest % LANES == 0 and (rest == 0 or (wide * WIDE_TILE) % rest == 0), wd
    return [(WIDE_TILE, 0, wide)] * (wide > 0) + [(rest, wide * WIDE_TILE, 1)] * (rest > 0)


def _matmul(ids, a, b, *, grid, a_spec, b_spec, o_spec, out_shape, trans_b, name, after=(), prev=None):
    nk = grid[2]
    dims = NT_DIMS if trans_b else NN_DIMS
    n_skip = len(after) + (prev is not None)

    def body(ids_ref, a_ref, b_ref, *rest):
        del ids_ref
        o_ref, scratch = rest[n_skip], rest[n_skip + 1:]
        prod = _dot(a_ref[...], b_ref[...], dims)
        if nk == 1:
            o_ref[...] = prod.astype(o_ref.dtype)
        else:
            acc_ref, = scratch
            k = pl.program_id(2)

            @pl.when(k == 0)
            def _():
                acc_ref[...] = prod

            @pl.when(k > 0)
            def _():
                acc_ref[...] += prod

            @pl.when(k == nk - 1)
            def _():
                o_ref[...] = acc_ref[...].astype(o_ref.dtype)

    scratch = [] if nk == 1 else [pltpu.VMEM(tuple(d for d in o_spec.block_shape if d is not None), F32)]
    extra = list(after) + ([prev] if prev is not None else [])
    aliases = {3 + len(after): 0} if prev is not None else {}
    return pl.pallas_call(
        body, name=name, out_shape=out_shape, input_output_aliases=aliases,
        grid_spec=pltpu.PrefetchScalarGridSpec(
            num_scalar_prefetch=1, grid=grid, in_specs=[a_spec, b_spec] + [ANY_SPEC] * len(extra),
            out_specs=o_spec, scratch_shapes=scratch),
        compiler_params=_params("parallel", "parallel", "arbitrary"),
    )(ids, a, b, *extra)


def _proj_in(ids, h, w_roles, pos, n, prev, name, after=()):
    T, D = h.shape
    n_roles, _, wd = w_roles.shape
    tm = _pick(T, 1024)
    for t, (tn, first, count) in enumerate(_col_tiles(wd)):
        prev = _matmul(
            ids, h, w_roles, grid=(T // tm, n * count, 1),
            a_spec=pl.BlockSpec((tm, D), lambda i, j, k, ids: (i, 0)),
            b_spec=pl.BlockSpec((None, D, tn), functools.partial(
                lambda tn, first, count, i, j, k, ids: (ids[n_roles + pos + j // count], 0, first // tn + j % count),
                tn, first, count)),
            o_spec=pl.BlockSpec((pl.Element(tm), pl.Element(tn)), functools.partial(
                lambda tn, first, count, i, j, k, ids: (
                    i * tm, pl.multiple_of(ids[ids[n_roles + pos + j // count]] * wd + first + (j % count) * tn, LANES)),
                tn, first, count)),
            out_shape=jax.ShapeDtypeStruct((T, n_roles * wd), F32), trans_b=False, name=f"{name}_{t}",
            after=after, prev=prev)
    return prev


def _proj_out(ids, mixed, wo_roles):
    T, E = mixed.shape
    n_roles, R, D = wo_roles.shape
    tm, tn = _pick(T, 512), _pick(D, 512)

    def body(ids_ref, a_ref, b_ref, o_ref, a_roles):
        @pl.when(pl.program_id(1) == 0)
        def _():
            for k in range(n_roles):
                col = pl.multiple_of(ids_ref[k] * R, R)
                a_roles[:, k * R:(k + 1) * R] = a_ref[:, pl.ds(col, R)]

        o_ref[...] = _dot(a_roles[...], b_ref[...].reshape(E, tn), NN_DIMS)

    return pl.pallas_call(
        body, name="proj_out", out_shape=jax.ShapeDtypeStruct((T, D), F32),
        grid_spec=pltpu.PrefetchScalarGridSpec(
            num_scalar_prefetch=1, grid=(T // tm, D // tn),
            in_specs=[pl.BlockSpec((tm, E), lambda i, j, ids: (i, 0)),
                      pl.BlockSpec((n_roles, R, tn), lambda i, j, ids: (0, 0, j))],
            out_specs=pl.BlockSpec((tm, tn), lambda i, j, ids: (i, j)),
            scratch_shapes=[pltpu.VMEM((tm, E), BF16)]),
        compiler_params=_params("parallel", "arbitrary"))(ids, mixed, wo_roles)


def _dmixed(ids, dy, wo_roles):
    T, D = dy.shape
    n_roles, R, _ = wo_roles.shape
    tm = _pick(T, 1024)
    return _matmul(
        ids, dy, wo_roles, grid=(T // tm, n_roles, 1),
        a_spec=pl.BlockSpec((tm, D), lambda i, j, k, ids: (i, 0)),
        b_spec=pl.BlockSpec((None, R, D), lambda i, j, k, ids: (j, 0, 0)),
        o_spec=pl.BlockSpec((tm, R), lambda i, j, k, ids: (i, ids[j])),
        out_shape=jax.ShapeDtypeStruct((T, n_roles * R), F32), trans_b=True, name="dmixed")


def _dw_out(ids, mixed_t, dy, n_roles):
    E, T = mixed_t.shape
    D = dy.shape[1]
    R = E // n_roles
    tn = _pick(D, 512)
    return _matmul(
        ids, mixed_t, dy, grid=(n_roles, D // tn, 1),
        a_spec=pl.BlockSpec((R, T), lambda i, j, k, ids: (ids[i], 0)),
        b_spec=pl.BlockSpec((T, tn), lambda i, j, k, ids: (0, j)),
        o_spec=pl.BlockSpec((None, R, tn), lambda i, j, k, ids: (i, 0, j)),
        out_shape=jax.ShapeDtypeStruct((n_roles, R, D), BF16), trans_b=False, name="dw_out")


def _dw_in(ids, h_t, dproj, n_roles, r0, nr, name, after=()):
    D, T = h_t.shape
    wd = dproj.shape[1] // n_roles
    tm, out = _pick(D, 1024), None
    for t, (tn, first, count) in enumerate(_col_tiles(wd)):
        out = _matmul(
            ids, h_t, dproj, grid=(D // tm, nr * count, 1),
            a_spec=pl.BlockSpec((tm, T), lambda i, j, k, ids: (i, 0)),
            b_spec=pl.BlockSpec((pl.Element(T), pl.Element(tn)), functools.partial(
                lambda tn, first, count, i, j, k, ids: (
                    0, pl.multiple_of(ids[r0 + j // count] * wd + first + (j % count) * tn, LANES)),
                tn, first, count)),
            o_spec=pl.BlockSpec((None, tm, tn), functools.partial(
                lambda tn, first, count, i, j, k, ids: (j // count, i, first // tn + j % count), tn, first, count)),
            out_shape=jax.ShapeDtypeStruct((nr, D, wd), BF16), trans_b=False, name=f"{name}_{t}", after=after, prev=out)
    return out


def _dh(ids, dproj, w_roles, after=()):
    T = dproj.shape[0]
    n_roles, D, wd = w_roles.shape
    tm, tn = _pick(T, 1024), _pick(D, 512)
    nk, n_after = n_roles // 2, len(after)

    def body(ids_ref, a0_ref, a1_ref, b0_ref, b1_ref, *rest):
        del ids_ref
        o_ref, acc_ref = rest[n_after], rest[n_after + 1]
        prod = _dot(a0_ref[...], b0_ref[...], NT_DIMS) + _dot(a1_ref[...], b1_ref[...], NT_DIMS)
        k = pl.program_id(2)

        @pl.when(k == 0)
        def _():
            acc_ref[...] = prod

        @pl.when((k > 0) & (k < nk - 1))
        def _():
            acc_ref[...] += prod

        @pl.when(k == nk - 1)
        def _():
            o_ref[...] = acc_ref[...] + prod

    a_spec = lambda e: pl.BlockSpec((tm, wd), lambda i, j, k, ids: (i, ids[2 * k + e]))
    b_spec = lambda e: pl.BlockSpec((None, tn, wd), lambda i, j, k, ids: (2 * k + e, j, 0))
    return pl.pallas_call(
        body, name="dh", out_shape=jax.ShapeDtypeStruct((T, D), F32),
        grid_spec=pltpu.PrefetchScalarGridSpec(
            num_scalar_prefetch=1, grid=(T // tm, D // tn, nk),
            in_specs=[a_spec(0), a_spec(1), b_spec(0), b_spec(1)] + [ANY_SPEC] * n_after,
            out_specs=pl.BlockSpec((tm, tn), lambda i, j, k, ids: (i, j)),
            scratch_shapes=[pltpu.VMEM((tm, tn), F32)]),
        compiler_params=_params("parallel", "parallel", "arbitrary"),
    )(ids, dproj, dproj, w_roles, w_roles, *after)


def _cast_slot0(w, n_roles, name):
    R, C = w.shape
    tr = _pick(R, 256)

    def body(w_ref, o_ref):
        o_ref[...] = w_ref[...].astype(BF16)

    return pl.pallas_call(
        body, name=name, grid=(R // tr,), in_specs=[pl.BlockSpec((tr, C), lambda i: (i, 0))],
        out_specs=pl.BlockSpec((None, tr, C), lambda i: (0, i, 0)),
        out_shape=jax.ShapeDtypeStruct((n_roles, R, C), BF16), compiler_params=_params("parallel"))(w)


def _prenorm(x, gain):
    T, D = x.shape
    tm = _pick(T, 256)

    def body(x_ref, g_ref, h_ref, ht_ref):
        xv = x_ref[...]
        r = lax.rsqrt(jnp.mean(xv * xv, axis=-1, keepdims=True) + NORM_EPS)
        h = xv * r * g_ref[...]
        h_ref[...] = h.astype(BF16)
        ht_ref[...] = h.T.astype(BF16)

    return pl.pallas_call(
        body, name="prenorm", grid=(T // tm,),
        in_specs=[pl.BlockSpec((tm, D), lambda i: (i, 0)), pl.BlockSpec((1, D), lambda i: (0, 0))],
        out_specs=[pl.BlockSpec((tm, D), lambda i: (i, 0)), pl.BlockSpec((D, tm), lambda i: (0, i))],
        out_shape=[jax.ShapeDtypeStruct((T, D), BF16), jax.ShapeDtypeStruct((D, T), BF16)],
        compiler_params=_params("parallel"))(x, gain)


def _norm_bwd(u, yn, r):
    return r * (u - yn * jnp.mean(u * yn, axis=-1, keepdims=True))


def _loss_head(y, x, target, gain):
    T, D = y.shape
    tm = _pick(T, 256)

    def body(y_ref, x_ref, t_ref, g_ref, dy_ref, dout_ref, gpost_ref, sq_ref):
        yv = y_ref[...]
        g = g_ref[...]
        r = lax.rsqrt(jnp.mean(yv * yv, axis=-1, keepdims=True) + NORM_EPS)
        yn = yv * r
        err = x_ref[...] + yn * g - t_ref[...]
        dout = err * (1.0 / D)
        dy_ref[...] = _norm_bwd(dout * g, yn, r).astype(BF16)
        dout_ref[...] = dout

        @pl.when(pl.program_id(0) == 0)
        def _():
            gpost_ref[...] = jnp.zeros_like(gpost_ref)
            sq_ref[...] = jnp.zeros_like(sq_ref)

        gpost_ref[...] += jnp.sum(dout * yn, axis=0, keepdims=True)
        sq_ref[...] += jnp.sum(err * err, axis=0, keepdims=True)

    row = pl.BlockSpec((tm, D), lambda i: (i, 0))
    vec = pl.BlockSpec((1, D), lambda i: (0, 0))
    return pl.pallas_call(
        body, name="loss_head", grid=(T // tm,), in_specs=[row, row, row, vec], out_specs=[row, row, vec, vec],
        out_shape=[jax.ShapeDtypeStruct((T, D), BF16), jax.ShapeDtypeStruct((T, D), F32),
                   jax.ShapeDtypeStruct((1, D), F32), jax.ShapeDtypeStruct((1, D), F32)],
        compiler_params=_params("arbitrary"))(y, x, target, gain)


def _prenorm_bwd(x, dh, dout, gain):
    T, D = x.shape
    tm = _pick(T, 256)

    def body(x_ref, dh_ref, dout_ref, g_ref, gx_ref, gpre_ref):
        xv = x_ref[...]
        dhv = dh_ref[...]
        r = lax.rsqrt(jnp.mean(xv * xv, axis=-1, keepdims=True) + NORM_EPS)
        xn = xv * r
        gx_ref[...] = dout_ref[...] + _norm_bwd(dhv * g_ref[...], xn, r)

        @pl.when(pl.program_id(0) == 0)
        def _():
            gpre_ref[...] = jnp.zeros_like(gpre_ref)

        gpre_ref[...] += jnp.sum(dhv * xn, axis=0, keepdims=True)

    row = pl.BlockSpec((tm, D), lambda i: (i, 0))
    vec = pl.BlockSpec((1, D), lambda i: (0, 0))
    return pl.pallas_call(
        body, name="prenorm_bwd", grid=(T // tm,), in_specs=[row, row, row, vec], out_specs=[row, vec],
        out_shape=[jax.ShapeDtypeStruct((T, D), F32), jax.ShapeDtypeStruct((1, D), F32)],
        compiler_params=_params("arbitrary"))(x, dh, dout, gain)


def _attn_masks(n):
    row = lax.broadcasted_iota(jnp.int32, (2 * WINDOW, 2 * WINDOW), 0) % WINDOW
    col = lax.broadcasted_iota(jnp.int32, (2 * WINDOW, 2 * WINDOW), 1)
    valid = (col > row) & (col <= row + WINDOW) & ((n > 0) | (col >= WINDOW))
    low = lax.broadcasted_iota(jnp.int32, (1, LANES), 1) < ATTN_HEAD
    top = lax.broadcasted_iota(jnp.int32, (2 * WINDOW, 1), 0) < WINDOW
    return jnp.where(valid, 0.0, -jnp.inf), low, top


def _dup_half(pair, keep):
    return jnp.where(keep, pair, pltpu.roll(pair, ATTN_HEAD, 1))


def _fold_half(v):
    return v + pltpu.roll(v, ATTN_HEAD, 1)


ATTN_GROUP_FWD = 8
ATTN_GROUP = 4
ATTN_SCALE = ATTN_HEAD ** -0.5


def _attn_scores(qpair, k2, low):
    qs = qpair * ATTN_SCALE
    q2 = jnp.concatenate([jnp.where(low, qs, 0.0), jnp.where(low, 0.0, qs)], axis=0).astype(BF16)
    return q2, _dot(q2, k2, NT_DIMS)


def _attn_softmax(scores, sink_lo, sink_hi, bias, top):
    s = scores + bias
    sink = jnp.where(top, sink_lo, sink_hi)
    m = jnp.maximum(jnp.max(s, axis=-1, keepdims=True), sink)
    p = jnp.exp(s - m)
    psink = jnp.exp(sink - m)
    inv = 1.0 / (jnp.sum(p, axis=-1, keepdims=True) + psink)
    return p * inv, psink * inv


def _attn_specs(D):
    kb = D // LANES
    vb = kb + D // (8 * LANES)
    gb = (D + D // 4) // 512
    wide = lambda off: [pl.BlockSpec((WINDOW, 512), functools.partial(lambda o, e, jp, n: (n, o + 2 * jp + e), off, e))
                        for e in (0, 1)]
    cur = lambda off: pl.BlockSpec((WINDOW, LANES), functools.partial(lambda o, jp, n: (n, o + jp), off))
    prev = lambda off: pl.BlockSpec((WINDOW, LANES),
                                    functools.partial(lambda o, jp, n: (jnp.maximum(n - 1, 0), o + jp), off))
    return wide(0) + [cur(kb), prev(kb), cur(vb), prev(vb)] + wide(gb)


def _attn_fwd(proj, sinks, D):
    T = proj.shape[0]
    nb, njp = T // WINDOW, D // 1024

    def body(sink_ref, qlo_ref, qhi_ref, kc_ref, kp_ref, vc_ref, vp_ref, glo_ref, ghi_ref, mix_ref, mixt_ref):
        jp, n = pl.program_id(0), pl.program_id(1)
        bias, low, top = _attn_masks(n)
        kk = jnp.concatenate([kp_ref[...], kc_ref[...]], axis=0)
        vv = jnp.concatenate([vp_ref[...], vc_ref[...]], axis=0)
        q_refs, g_refs = (qlo_ref, qhi_ref), (glo_ref, ghi_ref)
        keeps = (low, jnp.logical_not(low))
        k2s = [_dup_half(kk, keep).astype(BF16) for keep in keeps]
        v2s = [_dup_half(vv, keep).astype(BF16) for keep in keeps]
        all_units = [(hj, p) for hj in (0, 1) for p in range(4)]
        for first in range(0, len(all_units), ATTN_GROUP_FWD):
            units = all_units[first:first + ATTN_GROUP_FWD]
            cols = {u: slice(LANES * u[1], LANES * (u[1] + 1)) for u in units}
            scores = {u: _attn_scores(q_refs[u[0]][:, cols[u]], k2s[u[0]], low)[1] for u in units}
            probs = {}
            for u in units:
                head = (2 * jp + u[0]) * GQA + 2 * u[1]
                probs[u] = _attn_softmax(scores[u], sink_ref[0, head], sink_ref[0, head + 1], bias, top)[0]
            o2s = {u: _dot(probs[u].astype(BF16), v2s[u[0]], NN_DIMS) for u in units}
            for u in units:
                opair = jnp.where(low, o2s[u][:WINDOW], o2s[u][WINDOW:])
                g = g_refs[u[0]][:, cols[u]]
                out = opair * (g * _sigmoid(g))
                oc = slice(512 * u[0] + LANES * u[1], 512 * u[0] + LANES * (u[1] + 1))
                mix_ref[:, oc] = out.astype(BF16)
                mixt_ref[oc, :] = out.T.astype(BF16)

    return pl.pallas_call(
        body, name="attn_fwd", grid=(njp, nb),
        in_specs=[pl.BlockSpec(memory_space=pltpu.SMEM)] + _attn_specs(D),
        out_specs=[pl.BlockSpec((WINDOW, 1024), lambda jp, n: (n, jp)),
                   pl.BlockSpec((1024, WINDOW), lambda jp, n: (jp, n))],
        out_shape=[jax.ShapeDtypeStruct((T, 2 * D), BF16), jax.ShapeDtypeStruct((2 * D, T), BF16)],
        compiler_params=_params("parallel", "parallel"))(sinks, *([proj] * 8))


def _flush_windows(bufs, sems, hbm_ref, corners, slot, step, last):
    def copies(sl):
        return [pltpu.make_async_copy(
            b.at[sl], hbm_ref.at[pl.ds(r0, b.shape[1]), pl.ds(c0, b.shape[2])], sems.at[sl, i])
            for i, (b, (r0, c0)) in enumerate(zip(bufs, corners))]

    for cp in copies(slot):
        cp.start()

    @pl.when(step > 0)
    def _():
        for cp in copies(1 - slot):
            cp.wait()

    @pl.when(step == last)
    def _():
        for cp in copies(slot):
            cp.wait()


def _attn_bwd(proj, sinks, dmix, D, after=()):
    T = proj.shape[0]
    nb, njp = T // WINDOW, D // 1024
    n_after = len(after)

    def body(sink_ref, qlo_ref, qhi_ref, kc_ref, kp_ref, vc_ref, vp_ref, glo_ref, ghi_ref, dmix_ref, *rest):
        dproj_ref, dsink_ref, kcarry_ref, vcarry_ref, dq_buf, dk_buf, dv_buf, dg_buf, out_sems = rest[n_after:]
        jp, step = pl.program_id(0), pl.program_id(1)
        n = nb - 1 - step
        slot = step % 2
        dq_ref, dk_ref, dv_ref, dg_ref = dq_buf.at[slot], dk_buf.at[slot], dv_buf.at[slot], dg_buf.at[slot]
        bias, low, top = _attn_masks(n)
        lane = lax.broadcasted_iota(jnp.int32, (1, LANES), 1)
        kk = jnp.concatenate([kp_ref[...], kc_ref[...]], axis=0)
        vv = jnp.concatenate([vp_ref[...], vc_ref[...]], axis=0)

        @pl.when(step == 0)
        def _():
            kcarry_ref[...] = jnp.zeros_like(kcarry_ref)
            vcarry_ref[...] = jnp.zeros_like(vcarry_ref)
            dsink_ref[...] = jnp.zeros_like(dsink_ref)

        dk_pair = jnp.zeros((2 * WINDOW, LANES), F32)
        dv_pair = jnp.zeros((2 * WINDOW, LANES), F32)
        dsink = jnp.zeros((1, LANES), F32)
        q_refs, g_refs = (qlo_ref, qhi_ref), (glo_ref, ghi_ref)
        keeps = (low, jnp.logical_not(low))
        k2s = [_dup_half(kk, keep).astype(BF16) for keep in keeps]
        v2s = [_dup_half(vv, keep).astype(BF16) for keep in keeps]
        dk_heads = [jnp.zeros((2 * WINDOW, LANES), F32) for _ in keeps]
        dv_heads = [jnp.zeros((2 * WINDOW, LANES), F32) for _ in keeps]
        all_units = [(hj, p) for hj in (0, 1) for p in range(4)]
        for first in range(0, len(all_units), ATTN_GROUP):
            units = all_units[first:first + ATTN_GROUP]
            cols = {u: slice(LANES * u[1], LANES * (u[1] + 1)) for u in units}
            ocs = {u: slice(512 * u[0] + LANES * u[1], 512 * u[0] + LANES * (u[1] + 1)) for u in units}
            qsc = {u: _attn_scores(q_refs[u[0]][:, cols[u]], k2s[u[0]], low) for u in units}
            probs, psinks, do2s = {}, {}, {}
            for u in units:
                head = (2 * jp + u[0]) * GQA + 2 * u[1]
                probs[u], psinks[u] = _attn_softmax(qsc[u][1], sink_ref[0, head], sink_ref[0, head + 1], bias, top)
                g = g_refs[u[0]][:, cols[u]]
                do = dmix_ref[:, ocs[u]] * (g * _sigmoid(g))
                do2s[u] = jnp.concatenate([jnp.where(low, do, 0.0), jnp.where(low, 0.0, do)], axis=0).astype(BF16)
            pbs = {u: probs[u].astype(BF16) for u in units}
            o2s = {u: _dot(pbs[u], v2s[u[0]], NN_DIMS) for u in units}
            dps = {u: _dot(do2s[u], v2s[u[0]], NT_DIMS) for u in units}
            dss = {}
            for u in units:
                opair = jnp.where(low, o2s[u][:WINDOW], o2s[u][WINDOW:])
                g = g_refs[u[0]][:, cols[u]]
                sg = _sigmoid(g)
                dg_ref[:, ocs[u]] = (dmix_ref[:, ocs[u]] * opair * (sg * (1.0 + g * (1.0 - sg)))).astype(BF16)
                delta = jnp.sum(probs[u] * dps[u], axis=-1, keepdims=True)
                dss[u] = (probs[u] * (dps[u] - delta)).astype(BF16)
                ps = psinks[u] * delta
                local = u[0] * GQA + 2 * u[1]
                dsink -= jnp.where(lane == local, jnp.sum(ps[:WINDOW], axis=0, keepdims=True), 0.0)
                dsink -= jnp.where(lane == local + 1, jnp.sum(ps[WINDOW:], axis=0, keepdims=True), 0.0)
            dq2s = {u: _dot(dss[u], k2s[u[0]], NN_DIMS) for u in units}
            for u in units:
                dk_heads[u[0]] += _dot(dss[u], qsc[u][0], TN_DIMS)
                dv_heads[u[0]] += _dot(pbs[u], do2s[u], TN_DIMS)
            for u in units:
                dq_ref[:, ocs[u]] = (jnp.where(low, dq2s[u][:WINDOW], dq2s[u][WINDOW:]) * ATTN_SCALE).astype(BF16)
        for keep, dk_head, dv_head in zip(keeps, dk_heads, dv_heads):
            dk_pair += jnp.where(keep, _fold_half(dk_head), 0.0)
            dv_pair += jnp.where(keep, _fold_half(dv_head), 0.0)
        dk_ref[...] = (dk_pair[WINDOW:] + kcarry_ref[...]).astype(BF16)
        dv_ref[...] = (dv_pair[WINDOW:] + vcarry_ref[...]).astype(BF16)
        kcarry_ref[...] = dk_pair[:WINDOW]
        vcarry_ref[...] = dv_pair[:WINDOW]
        dsink_ref[...] += dsink
        row = pl.multiple_of(n * WINDOW, WINDOW)
        col = lambda base, width: pl.multiple_of(base + jp * width, LANES)
        corners = [(row, col(0, 1024)), (row, col(D, LANES)), (row, col(D + D // 8, LANES)), (row, col(D + D // 4, 1024))]
        _flush_windows([dq_buf, dk_buf, dv_buf, dg_buf], out_sems, dproj_ref, corners, slot, step, nb - 1)

    rev = lambda spec: pl.BlockSpec(spec.block_shape, functools.partial(
        lambda f, jp, s: f(jp, nb - 1 - s), spec.index_map))
    in_specs = [rev(s) for s in _attn_specs(D)]
    wide_in = pl.BlockSpec((WINDOW, 1024), lambda jp, s: (nb - 1 - s, jp))
    return pl.pallas_call(
        body, name="attn_bwd", grid=(njp, nb),
        in_specs=[pl.BlockSpec(memory_space=pltpu.SMEM)] + in_specs + [wide_in] + [ANY_SPEC] * n_after,
        out_specs=[ANY_SPEC, pl.BlockSpec((None, 1, LANES), lambda jp, s: (jp, 0, 0))],
        out_shape=[jax.ShapeDtypeStruct((T, 6 * D + D // 4), BF16), jax.ShapeDtypeStruct((njp, 1, LANES), F32)],
        scratch_shapes=[pltpu.VMEM((WINDOW, LANES), F32), pltpu.VMEM((WINDOW, LANES), F32),
                        pltpu.VMEM((2, WINDOW, 1024), BF16), pltpu.VMEM((2, WINDOW, LANES), BF16),
                        pltpu.VMEM((2, WINDOW, LANES), BF16), pltpu.VMEM((2, WINDOW, 1024), BF16),
                        pltpu.SemaphoreType.DMA((2, 4))],
        compiler_params=_params("parallel", "arbitrary"))(sinks, *([proj] * 8), dmix, *after)


RNN_TB = 512
RNN_HB = 8
RNN_WIDE = RNN_HB * RNN_HEAD


def _split3(v):
    a = v.astype(BF16)
    r = v - a.astype(F32)
    b = r.astype(BF16)
    c = (r - b.astype(F32)).astype(BF16)
    return a, b, c


def _tri_sum(tri, v):
    a, b, c = _split3(v)
    return _dot(tri, a, NN_DIMS) + _dot(tri, b, NN_DIMS) + _dot(tri, c, NN_DIMS)


def _lower_bound(lb_ref):
    l0, l1 = lb_ref[0:1, :], lb_ref[1:2, :]
    m = jnp.maximum(l0, l1)
    e0, e1 = jnp.exp(l0 - m), jnp.exp(l1 - m)
    return e0 / (e0 + e1)


def _rnn_gates(rq, rf, lb):
    sq = _sigmoid(rq)
    sf = _sigmoid(rf)
    f = lb + (1.0 - lb) * sf
    return sq, sf, f


def _rnn_factors(G):
    last = G[CHUNK - 1:CHUNK, :]
    mid = G[CHUNK // 2 - 1:CHUNK // 2, :]
    return G, jnp.exp(G), jnp.exp(G - mid), jnp.exp(mid - G), jnp.exp(last - G), jnp.exp(last)


def _chunk_masks():
    r = lax.broadcasted_iota(jnp.int32, (CHUNK, CHUNK), 0)
    c = lax.broadcasted_iota(jnp.int32, (CHUNK, CHUNK), 1)
    return r >= c, (r >= c).astype(BF16), (r <= c).astype(BF16)


def _rnn_specs(T, D, tb, rev):
    nt = T // tb
    base = (2 * D + D // 4) // LANES
    t_of = (lambda s: nt - 1 - s) if rev else (lambda s: s)
    assert base % RNN_HB == 0 and (D // LANES) % RNN_HB == 0
    cols = [pl.BlockSpec((tb, RNN_WIDE), functools.partial(lambda o, h, s: (t_of(s), o + h),
                                                            (base + i * (D // LANES)) // RNN_HB))
            for i in range(4)]
    return cols, t_of


def _rnn_fwd(proj, lb_logits, rnn_norm, mixed, mixed_t, D):
    T = proj.shape[0]
    tb = _pick(T, RNN_TB)
    nt, nh, cpb = T // tb, D // RNN_HEAD, tb // CHUNK
    cols, _ = _rnn_specs(T, D, tb, False)

    def body(rq_ref, rf_ref, ri_ref, rg_ref, lb_ref, gain_ref, mix_in, mixt_in,
             mix_ref, mixt_ref, o_ref, st_ref, state_ref):
        del mix_in, mixt_in
        causal, tri, _ = _chunk_masks()
        lb = _lower_bound(lb_ref)

        @pl.when(pl.program_id(1) == 0)
        def _():
            state_ref[...] = jnp.zeros_like(state_ref)

        def chunk(c, carry):
            rows = pl.ds(pl.multiple_of(c * CHUNK, CHUNK), CHUNK)
            heads = range(RNN_HB)
            lns = [slice(RNN_HEAD * hh, RNN_HEAD * (hh + 1)) for hh in heads]
            qs, ks, Gs = [], [], []
            for ln in lns:
                rq, rf = rq_ref[rows, ln], rf_ref[rows, ln]
                sq, _, f = _rnn_gates(rq, rf, lb[:, ln])
                qs.append(rq * sq)
                ks.append(1.0 - f)
                Gs.append(_tri_sum(tri, jnp.log(f)))
            atts, inters, vbs = [], [], []
            for hh, ln in enumerate(lns):
                _, eG, eq, ek, ekl, elast = _rnn_factors(Gs[hh])
                q, k = qs[hh], ks[hh]
                st = state_ref[hh]
                st_ref[hh, c] = st
                vb = ri_ref[rows, ln].astype(BF16)
                vbs.append(vb)
                atts.append(_dot((q * eq).astype(BF16), (k * ek).astype(BF16), NT_DIMS))
                inters.append(_dot((q * eG).astype(BF16), st.astype(BF16), NT_DIMS))
                state_ref[hh] = st * elast + _dot(vb, (k * ekl).astype(BF16), TN_DIMS)
            intras = [_dot(jnp.where(causal, atts[hh], 0.0).astype(BF16), vbs[hh], NN_DIMS) for hh in heads]
            for hh, ln in enumerate(lns):
                o = inters[hh] + intras[hh]
                o_ref[rows, ln] = o
                rg = rg_ref[rows, ln]
                r = lax.rsqrt(jnp.mean(o * o, axis=-1, keepdims=True) + NORM_EPS)
                out = (o * r * gain_ref[:, ln]) * (rg * _sigmoid(rg))
                mix_ref[rows, ln] = out.astype(BF16)
            return carry

        lax.fori_loop(0, cpb, chunk, 0, unroll=True)
        mixt_ref[...] = mix_ref[...].astype(F32).T.astype(BF16)

    nh //= RNN_HB
    vec2 = pl.BlockSpec((2, RNN_WIDE), lambda h, s: (0, h))
    vec1 = pl.BlockSpec((1, RNN_WIDE), lambda h, s: (0, h))
    return pl.pallas_call(
        body, name="rnn_fwd", grid=(nh, nt),
        in_specs=cols + [vec2, vec1, ANY_SPEC, ANY_SPEC],
        out_specs=[pl.BlockSpec((tb, RNN_WIDE), lambda h, s: (s, D // RNN_WIDE + h)),
                   pl.BlockSpec((RNN_WIDE, tb), lambda h, s: (D // RNN_WIDE + h, s)),
                   pl.BlockSpec((tb, RNN_WIDE), lambda h, s: (s, h)),
                   pl.BlockSpec((RNN_HB, cpb, RNN_HEAD, RNN_HEAD), lambda h, s: (h, s, 0, 0))],
        out_shape=[jax.ShapeDtypeStruct(mixed.shape, BF16), jax.ShapeDtypeStruct(mixed_t.shape, BF16),
                   jax.ShapeDtypeStruct((T, D), F32),
                   jax.ShapeDtypeStruct((nh * RNN_HB, T // CHUNK, RNN_HEAD, RNN_HEAD), F32)],
        scratch_shapes=[pltpu.VMEM((RNN_HB, RNN_HEAD, RNN_HEAD), F32)],
        input_output_aliases={6: 0, 7: 1},
        compiler_params=_params("parallel", "arbitrary"))(proj, proj, proj, proj, lb_logits, rnn_norm, mixed, mixed_t)


def _rnn_bwd(proj, lb_logits, rnn_norm, o_raw, states, dmix, dproj, D):
    T = proj.shape[0]
    tb = _pick(T, RNN_TB)
    nt, nh, cpb = T // tb, D // RNN_HEAD, tb // CHUNK
    cols, t_of = _rnn_specs(T, D, tb, True)

    def body(rq_ref, rf_ref, ri_ref, rg_ref, lb_ref, gain_ref, o_ref, st_ref, dmix_ref, dproj_in,
             dproj_ref, dlb_ref, dgain_ref, dstate_ref, out_buf, out_sems):
        del dproj_in
        step = pl.program_id(1)
        slot = step % 2
        drq_ref, drf_ref, dri_ref, drg_ref = [out_buf.at[i, slot] for i in range(4)]
        causal, tri, tri_t = _chunk_masks()
        lb = _lower_bound(lb_ref)
        gain = gain_ref[...]

        @pl.when(pl.program_id(1) == 0)
        def _():
            dstate_ref[...] = jnp.zeros_like(dstate_ref)
            dlb_ref[...] = jnp.zeros_like(dlb_ref)
            dgain_ref[...] = jnp.zeros_like(dgain_ref)

        def chunk(i, carry):
            c = cpb - 1 - i
            rows = pl.ds(pl.multiple_of(c * CHUNK, CHUNK), CHUNK)
            heads = range(RNN_HB)
            lns = [slice(RNN_HEAD * hh, RNN_HEAD * (hh + 1)) for hh in heads]
            last_row = lax.broadcasted_iota(jnp.int32, (CHUNK, 1), 0) == CHUNK - 1
            A = []
            for ln in lns:
                rq, rf, rg = rq_ref[rows, ln], rf_ref[rows, ln], rg_ref[rows, ln]
                o, dgated = o_ref[rows, ln], dmix_ref[rows, ln]
                gainh = gain[:, ln]
                sgt = _sigmoid(rg)
                r = lax.rsqrt(jnp.mean(o * o, axis=-1, keepdims=True) + NORM_EPS)
                on = o * r
                drg_ref[rows, ln] = (dgated * (on * gainh) * (sgt * (1.0 + rg * (1.0 - sgt)))).astype(BF16)
                d_on = dgated * (rg * sgt)
                dgain_ref[:, ln] += jnp.sum(d_on * on, axis=0, keepdims=True)
                dob = _norm_bwd(d_on * gainh, on, r).astype(BF16)
                sq, sf, f = _rnn_gates(rq, rf, lb[:, ln])
                A.append(dict(rq=rq, sq=sq, sf=sf, f=f, dob=dob, G=_tri_sum(tri, jnp.log(f))))
            for hh, ln in enumerate(lns):
                a = A[hh]
                _, eG, eq, ek, ekl, elast = _rnn_factors(a.pop("G"))
                q, k = a["rq"] * a["sq"], 1.0 - a["f"]
                st, dst = st_ref[hh, c], dstate_ref[hh]
                qg, kl = q * eG, k * ekl
                qmb, kmb = (q * eq).astype(BF16), (k * ek).astype(BF16)
                dob, vb, dstb = a["dob"], ri_ref[rows, ln].astype(BF16), dst.astype(BF16)
                a.update(eG=eG, eq=eq, ek=ek, ekl=ekl, qg=qg, kl=kl, qmb=qmb, kmb=kmb,
                         att=_dot(qmb, kmb, NT_DIMS), datt=_dot(dob, vb, NT_DIMS),
                         dqg=_dot(dob, st.astype(BF16), NN_DIMS), dkl=_dot(vb, dstb, NN_DIMS),
                         dri=_dot(kl.astype(BF16), dstb, NT_DIMS),
                         dlast=jnp.sum(dst * st, axis=0, keepdims=True) * elast)
                dstate_ref[hh] = dst * elast + _dot(dob, qg.astype(BF16), TN_DIMS)
            for hh, ln in enumerate(lns):
                a = A[hh]
                att = jnp.where(causal, a.pop("att"), 0.0).astype(BF16)
                datt = jnp.where(causal, a.pop("datt"), 0.0).astype(BF16)
                dqm = _dot(datt, a["kmb"], NN_DIMS)
                dkm = _dot(datt, a["qmb"], TN_DIMS)
                dri_ref[rows, ln] = (_dot(att, a["dob"], TN_DIMS) + a.pop("dri")).astype(BF16)
                dqg, dkl, kl = a.pop("dqg"), a.pop("dkl"), a.pop("kl")
                a["dq"] = dqg * a.pop("eG") + dqm * a.pop("eq")
                a["dk"] = dkm * a.pop("ek") + dkl * a.pop("ekl")
                dG = dqg * a.pop("qg") + dqm * a.pop("qmb").astype(F32) - dkm * a.pop("kmb").astype(F32) - dkl * kl
                dlast = jnp.sum(dkl * kl, axis=0, keepdims=True) + a.pop("dlast")
                a["dg"] = _tri_sum(tri_t, dG + jnp.where(last_row, dlast, 0.0))
            for hh, ln in enumerate(lns):
                a = A[hh]
                rq, sq, sf = a["rq"], a["sq"], a["sf"]
                df = a["dg"] / a["f"] - a["dk"]
                drq_ref[rows, ln] = (a["dq"] * (sq * (1.0 + rq * (1.0 - sq)))).astype(BF16)
                drf_ref[rows, ln] = (df * (1.0 - lb[:, ln]) * (sf * (1.0 - sf))).astype(BF16)
                dlb_ref[:, ln] += jnp.sum(df * (1.0 - sf), axis=0, keepdims=True)
            return carry

        lax.fori_loop(0, cpb, chunk, 0, unroll=True)
        row = pl.multiple_of(t_of(step) * tb, tb)
        col0 = 2 * D + D // 4 + pl.program_id(0) * RNN_WIDE
        corners = [(row, pl.multiple_of(col0 + i * D, LANES)) for i in range(4)]
        _flush_windows([out_buf.at[i] for i in range(4)], out_sems, dproj_ref, corners, slot, step, nt - 1)

    nh //= RNN_HB
    vec2 = pl.BlockSpec((2, RNN_WIDE), lambda h, s: (0, h))
    vec1 = pl.BlockSpec((1, RNN_WIDE), lambda h, s: (0, h))
    blk = pl.BlockSpec((tb, RNN_WIDE), lambda h, s: (t_of(s), h))
    return pl.pallas_call(
        body, name="rnn_bwd", grid=(nh, nt),
        in_specs=cols + [vec2, vec1, blk,
                         pl.BlockSpec((RNN_HB, cpb, RNN_HEAD, RNN_HEAD), lambda h, s: (h, t_of(s), 0, 0)),
                         pl.BlockSpec((tb, RNN_WIDE), lambda h, s: (t_of(s), D // RNN_WIDE + h)), ANY_SPEC],
        out_specs=[ANY_SPEC, vec1, vec1],
        out_shape=[jax.ShapeDtypeStruct(dproj.shape, BF16)] + [jax.ShapeDtypeStruct((1, D), F32)] * 2,
        scratch_shapes=[pltpu.VMEM((RNN_HB, RNN_HEAD, RNN_HEAD), F32), pltpu.VMEM((4, 2, tb, RNN_WIDE), BF16),
                        pltpu.SemaphoreType.DMA((2, 4))],
        input_output_aliases={9: 0},
        compiler_params=_params("parallel", "arbitrary"))(proj, proj, proj, proj, lb_logits, rnn_norm, o_raw, states, dmix, dproj)


def _adamw(w, g, m, v):
    m = ADAM_B1 * m + (1.0 - ADAM_B1) * g
    v = ADAM_B2 * v + (1.0 - ADAM_B2) * (g * g)
    m_hat = m / (1.0 - ADAM_B1 ** ADAM_STEP)
    v_hat = v / (1.0 - ADAM_B2 ** ADAM_STEP)
    delta = -ADAM_LR * (m_hat / (jnp.sqrt(v_hat) + ADAM_EPS) + ADAM_WD * w)
    return delta, m, v


def _adamw_big(w, m, v, parts, tr, name):
    R, C = w.shape
    n_parts = len(parts)

    def body(w_ref, m_ref, v_ref, *rest):
        part_refs = rest[:n_parts]
        g_ref, d_ref, nm_ref, nv_ref = rest[n_parts:]
        g = part_refs[0][...].astype(F32)
        for p_ref in part_refs[1:]:
            g = g + p_ref[...].astype(F32)
        delta, nm, nv = _adamw(w_ref[...], g, m_ref[...], v_ref[...])
        g_ref[...] = g
        d_ref[...] = delta
        nm_ref[...] = nm
        nv_ref[...] = nv

    blk = pl.BlockSpec((tr, C), lambda i: (i, 0))
    part_specs = [pl.BlockSpec((None, tr, C), functools.partial(lambda s, i: (s, i, 0), slot)) for _, slot in parts]
    return pl.pallas_call(
        body, name=name, grid=(R // tr,), in_specs=[blk, blk, blk] + part_specs,
        out_specs=[blk] * 4, out_shape=[jax.ShapeDtypeStruct((R, C), F32)] * 4,
        compiler_params=_params("parallel"))(w, m, v, *[a for a, _ in parts])


def _pair_sum(pa, qa):
    n, R, C = qa.shape
    tr = _pick(R, 256)

    def body(p_ref, q_ref, r_ref):
        r_ref[...] = (p_ref[...].astype(F32) + q_ref[...].astype(F32)).astype(BF16)

    return pl.pallas_call(
        body, name="pair_sum", grid=(n, R // tr),
        in_specs=[pl.BlockSpec((None, tr, C), lambda j, i: (2 * j, i, 0)), pl.BlockSpec((None, tr, C), lambda j, i: (j, i, 0))],
        out_specs=pl.BlockSpec((None, tr, C), lambda j, i: (j, i, 0)),
        out_shape=jax.ShapeDtypeStruct((n, R, C), BF16), compiler_params=_params("parallel", "parallel"))(pa, qa)


def _adamw_small(total, sinks, lb_logits, rnn_norm, pre_norm, post_norm, moments, D):
    params = [sinks, lb_logits, rnn_norm, pre_norm, post_norm]

    def body(tot_ref, *refs):
        p_refs, m_refs, v_refs = refs[0:5], refs[5:10], refs[10:15]
        loss_ref, outs = refs[15], refs[16:]
        tot = tot_ref[...]
        l0, l1 = p_refs[1][0:1, :], p_refs[1][1:2, :]
        mx = jnp.maximum(l0, l1)
        e0, e1 = jnp.exp(l0 - mx), jnp.exp(l1 - mx)
        p0, p1 = e0 / (e0 + e1), e1 / (e0 + e1)
        dlb = tot[0:1, :]
        grads = [tot[5:6, 0:LANES], jnp.concatenate([dlb * p0 * (1.0 - p0), -dlb * p0 * p1], axis=0),
                 tot[1:2, :], tot[2:3, :], tot[3:4, :]]
        loss_ref[...] = 0.5 / D * jnp.sum(tot[4:5, :], axis=-1, keepdims=True)
        for i, g in enumerate(grads):
            delta, nm, nv = _adamw(p_refs[i][...], g, m_refs[i][...], v_refs[i][...])
            outs[4 * i][...] = g
            outs[4 * i + 1][...] = delta
            outs[4 * i + 2][...] = nm
            outs[4 * i + 3][...] = nv

    out_shape = [jax.ShapeDtypeStruct((1, 1), F32)]
    for p in params:
        out_shape += [jax.ShapeDtypeStruct(p.shape, F32)] * 4
    return pl.pallas_call(body, name="adamw_small", out_shape=out_shape)(total, *params, *moments)


SIBLING = 1


def _me():
    return lax.axis_index("x"), lax.axis_index("y"), lax.axis_index("c")


def _flat(px, py, pc):
    return 4 * px + 2 * py + pc


def _role_peer(role, x, y, c):
    if not isinstance(role, int):
        j = (role - 2) // 2
        px = jnp.where((role >= 2) & (j != 1), 1 - x, x)
        py = jnp.where((role >= 2) & (j != 0), 1 - y, y)
        return (px, py, jnp.where(role % 2 == 1, 1 - c, c))
    if role < 2:
        return (x, y, (1 - c) if role else c)
    j, other = (role - 2) // 2, (role - 2) % 2
    px = (1 - x) if j in (0, 2) else x
    py = (1 - y) if j in (1, 2) else y
    return (px, py, (1 - c) if other else c)


def _wave1(x, y, c):
    return jnp.where(c == 1, 2, 4)


def _wave2(x, y, c):
    return jnp.where(c == 1, 4, 2)


def _role_ids():
    x, y, c = _me()
    w1, w2 = _wave1(x, y, c), _wave2(x, y, c)
    order = [0, 1, w1, w2 + 1, w2, w1 + 1, 6, 7]
    return jnp.stack([_flat(*_role_peer(r, x, y, c)) for r in range(N_DEV)] + order).astype(jnp.int32)


def _comm_call(name, bufs, waits=(), starts=(), after=()):
    n_buf, n_wait, n_start, n_after = len(bufs), len(waits), len(starts), len(after)

    def body(*refs):
        buf_refs = refs[:n_buf]
        sem_refs = refs[n_buf:n_buf + 2 * n_wait]
        outs = refs[n_buf + 2 * n_wait + n_after:]
        new_sems, token = outs[:2 * n_start], outs[-1]
        x, y, c = _me()

        def val(v):
            return v(x, y, c) if callable(v) else v

        def block(ref, slot, rows):
            if rows is None:
                return ref.at[val(slot)]
            first = rows[0] if isinstance(rows[0], int) else pl.multiple_of(val(rows[0]), 16)
            return ref.at[val(slot), pl.ds(first, rows[1])]

        for w, (_, kind, like, *rows) in enumerate(waits):
            shape_ref = block(buf_refs[like], 0, (0, rows[0][1]) if rows else None)
            cp = pltpu.make_async_remote_copy(
                src_ref=shape_ref, dst_ref=shape_ref, send_sem=sem_refs[2 * w], recv_sem=sem_refs[2 * w + 1],
                device_id=(x, y, c), device_id_type=MESH)
            if kind == "send":
                cp.wait_send()
            else:
                cp.wait_recv()
        for s, (sb, ss, db, ds, role, *rows) in enumerate(starts):
            rows = rows[0] if rows else None
            pltpu.make_async_remote_copy(
                src_ref=block(buf_refs[sb], ss, rows), dst_ref=block(buf_refs[db], ds, rows), send_sem=new_sems[2 * s],
                recv_sem=new_sems[2 * s + 1], device_id=_role_peer(val(role), x, y, c), device_id_type=MESH).start()
        token[...] = jnp.zeros_like(token)

    sems = [s for flight, *_ in waits for s in flight]
    out = pl.pallas_call(
        body, name=name,
        out_shape=tuple([pltpu.SemaphoreType.DMA(())] * (2 * n_start) + [pltpu.HBM(b.shape, b.dtype) for b in bufs]
                        + [jax.ShapeDtypeStruct((8, LANES), F32)]),
        in_specs=tuple([HBM_SPEC] * n_buf + [SEM_SPEC] * (2 * n_wait) + [ANY_SPEC] * n_after),
        out_specs=tuple([SEM_SPEC] * (2 * n_start) + [HBM_SPEC] * n_buf + [pl.BlockSpec(memory_space=pltpu.VMEM)]),
        input_output_aliases={i: 2 * n_start + i for i in range(n_buf)},
        compiler_params=pltpu.CompilerParams(has_side_effects=EFFECT),
    )(*[pltpu.with_memory_space_constraint(b, pltpu.HBM) for b in bufs], *sems, *after)
    flights = [(out[2 * s], out[2 * s + 1]) for s in range(n_start)]
    return list(out[2 * n_start:2 * n_start + n_buf]), flights, out[-1]


def _landing(shape):
    return lax.empty(shape, BF16)


def _both(flights, like):
    return [(f, kind, like) for f in flights for kind in ("send", "recv")]


def _allreduce_small(rows, D):
    n_rows = len(rows)

    def body(*refs):
        in_refs, out_ref, all_ref = refs[:n_rows], refs[n_rows], refs[n_rows + 1]
        send_sems, recv_sems = refs[n_rows + 2], refs[n_rows + 3]
        x, y, c = _me()
        mine = all_ref.at[_flat(x, y, c)]
        mine[...] = jnp.zeros((8, D), F32)
        for i, r in enumerate(in_refs):
            mine[i:i + 1, :] = r[...]
        copies = []
        for k in range(N_DEV - 1):
            copies.append(pltpu.make_async_remote_copy(
                src_ref=mine, dst_ref=mine, send_sem=send_sems.at[k], recv_sem=recv_sems.at[k],
                device_id=_role_peer(k + 1, x, y, c), device_id_type=MESH))
        for cp in copies:
            cp.start()
        for cp in copies:
            cp.wait_recv()
        for cp in copies:
            cp.wait_send()
        tot = all_ref[0]
        for d in range(1, N_DEV):
            tot = tot + all_ref[d]
        out_ref[...] = tot

    vm = pl.BlockSpec(memory_space=pltpu.VMEM)
    return pl.pallas_call(
        body, name="allreduce_small", in_specs=[vm] * n_rows, out_specs=vm,
        out_shape=jax.ShapeDtypeStruct((8, D), F32),
        scratch_shapes=[pltpu.VMEM((N_DEV, 8, D), F32), pltpu.SemaphoreType.DMA((7,)), pltpu.SemaphoreType.DMA((7,))],
    )(*rows)


def _step(ids, x, target, g_in, g_out, sinks, lb_logits, rnn_norm, pre_norm, post_norm, dist):
    T, D = x.shape
    n_roles, _, wd = g_in.shape
    ro = g_out.shape[1]
    assert n_roles % 2 == 0 and (not dist or n_roles == N_DEV)

    half = D // 2
    up = lambda f: (lambda x, y, c: f(x, y, c) + 1)
    rows_of = lambda f: (lambda x, y, c: jnp.where(f(x, y, c) == 2, 0, half))
    if dist:
        (g_in,), fl_in, _ = _comm_call(
            "gather_in_start", [g_in], starts=[(0, 0, 0, SIBLING, SIBLING), (0, 0, 0, _wave1, _wave1)])
    h, h_t = _prenorm(x, pre_norm)
    proj = _proj_in(ids, h, g_in, 0, 1, None, "proj_in_0")
    if dist:
        (g_in,), _, _ = _comm_call("gather_in_sibling", [g_in], waits=[(fl_in[0], "recv", 0)], after=(proj,))
    proj = _proj_in(ids, h, g_in, 1, 1, proj, "proj_in_1")
    if not dist:
        for p in range(2, n_roles, 2):
            proj = _proj_in(ids, h, g_in, p, 2, proj, f"proj_in_{p}")
    else:
        (g_in,), fl_a, _ = _comm_call(
            "gather_in_wave_1", [g_in], waits=[(fl_in[1], "recv", 0)],
            starts=[(0, 0, 0, _wave2, _wave2), (0, _wave1, 0, up(_wave1), SIBLING),
                    (0, _wave1, 0, 6, _wave2, (rows_of(_wave1), half))], after=(proj,))
        proj = _proj_in(ids, h, g_in, 2, 1, proj, "proj_in_2")
        (g_in,), _, _ = _comm_call("gather_in_passed_1", [g_in], waits=[(fl_a[1], "recv", 0)], after=(proj,))
        proj = _proj_in(ids, h, g_in, 3, 1, proj, "proj_in_3")
        (g_in,), fl_b, _ = _comm_call(
            "gather_in_wave_2", [g_in], waits=[(fl_a[0], "recv", 0)],
            starts=[(0, _wave2, 0, up(_wave2), SIBLING), (0, _wave2, 0, 6, _wave1, (rows_of(_wave2), half))],
            after=(proj,))
        proj = _proj_in(ids, h, g_in, 4, 1, proj, "proj_in_4")
        (g_in,), _, _ = _comm_call("gather_in_passed_2", [g_in], waits=[(fl_b[0], "recv", 0)], after=(proj,))
        proj = _proj_in(ids, h, g_in, 5, 1, proj, "proj_in_5")
        (g_in,), fl_c, _ = _comm_call(
            "gather_in_diagonal", [g_in], waits=[(fl_a[2], "recv", 0, (0, half)), (fl_b[1], "recv", 0, (0, half))],
            starts=[(0, 6, 0, 7, SIBLING)], after=(proj,))
        proj = _proj_in(ids, h, g_in, 6, 1, proj, "proj_in_6")
        (g_in, g_out), fl_out, token = _comm_call(
            "gather_out_start", [g_in, g_out], waits=[(fl_c[0], "recv", 0)],
            starts=[(1, 0, 1, SIBLING, SIBLING)] + [(1, 0, 1, 2 + 2 * i, 2 + 2 * i) for i in range(3)], after=(proj,))
        proj = _proj_in(ids, h, g_in, 7, 1, proj, "proj_in_7", (token,))
        send_waits = [(f, "send", 0) for f in fl_in + fl_a[:2] + fl_b[:1] + fl_c] \
            + [(f, "send", 0, (0, half)) for f in (fl_a[2], fl_b[1])]
    mixed, mixed_t = _attn_fwd(proj, sinks, D)
    if dist:
        (g_in, g_out), fl_out_fwd, _ = _comm_call(
            "gather_out_pass", [g_in, g_out],
            waits=send_waits + [(fl_out[1 + i], "recv", 1) for i in range(3)],
            starts=[(1, 2 + 2 * i, 1, 3 + 2 * i, SIBLING) for i in range(3)], after=(mixed,))
    mixed, mixed_t, o_raw, states = _rnn_fwd(proj, lb_logits, rnn_norm, mixed, mixed_t, D)
    if dist:
        (g_out,), _, _ = _comm_call(
            "gather_out_done", [g_out],
            waits=[(fl_out[0], "recv", 0)] + [(f, "recv", 0) for f in fl_out_fwd]
            + [(f, "send", 0) for f in fl_out + fl_out_fwd], after=(states,))
    y = _proj_out(ids, mixed, g_out)
    dy, dout, g_post, sq_err = _loss_head(y, x, target, post_norm)

    dmix = _dmixed(ids, dy, g_out)
    p_out, after = _dw_out(ids, mixed_t, dy, n_roles), ()
    if dist:
        (p_out, l_out), fl_so, token = _comm_call(
            "scatter_out_start", [p_out, _landing((n_roles - 1, ro, D))],
            starts=[(0, r, 1, r - 1, r) for r in range(1, n_roles)])
        after = (token,)
    dproj, d_sink = _attn_bwd(proj, sinks, dmix, D, after)
    dproj, d_lb, g_rnn = _rnn_bwd(proj, lb_logits, rnn_norm, o_raw, states, dmix, dproj, D)

    p_far = _dw_in(ids, h_t, dproj, n_roles, 2, n_roles - 2, "dw_in_far")
    if dist:
        (p_far, q_far), fl_pair, token = _comm_call(
            "scatter_in_pair_start", [p_far, _landing((3, D, wd))],
            starts=[(0, 1 + 2 * j, 1, j, SIBLING) for j in range(3)])
        p_near = _dw_in(ids, h_t, dproj, n_roles, 0, 2, "dw_in_near", after=(token,))
        (p_far, q_far), _, _ = _comm_call("scatter_in_pair_wait", [p_far, q_far], waits=_both(fl_pair, 1), after=(p_near,))
        chip_sum = _pair_sum(p_far, q_far)
        (chip_sum, z_far), fl_chip, token = _comm_call(
            "scatter_in_chip_start", [chip_sum, _landing((3, D, wd))],
            starts=[(0, j, 1, j, 2 + 2 * j) for j in range(3)])
        (p_near, q_near), fl_sib, token = _comm_call(
            "scatter_in_sibling_start", [p_near, _landing((1, D, wd))], starts=[(0, 1, 1, 0, SIBLING)], after=(token,))
        after = (token,)
    else:
        p_near, after = _dw_in(ids, h_t, dproj, n_roles, 0, 2, "dw_in_near"), ()
    dh = _dh(ids, dproj, g_in, after)
    grad_x, g_pre = _prenorm_bwd(x, dh, dout, pre_norm)
    n_q = D // ATTN_HEAD
    sink_row = jnp.pad(d_sink[:, 0, :2 * GQA].reshape(1, n_q), ((0, 0), (0, D - n_q)))
    rows = [d_lb, g_rnn, g_pre, g_post, sq_err, sink_row]
    if not dist:
        return grad_x, [(p_near, r) for r in range(2)] + [(p_far, r) for r in range(n_roles - 2)], \
            [(p_out, r) for r in range(n_roles)], rows

    (p_out, l_out), _, _ = _comm_call("scatter_out_wait", [p_out, l_out], waits=_both(fl_so, 1), after=(grad_x,))
    (chip_sum, z_far, p_near, q_near), _, _ = _comm_call(
        "scatter_in_wait", [chip_sum, z_far, p_near, q_near], waits=_both(fl_chip, 1) + _both(fl_sib, 3), after=(p_out,))
    parts_in = [(p_near, 0), (q_near, 0)] + [(z_far, j) for j in range(3)]
    parts_out = [(p_out, 0)] + [(l_out, k) for k in range(n_roles - 1)]
    return grad_x, parts_in, parts_out, rows


def kernel(x, w_in, attn_sinks, lb_logits, rnn_norm, w_out, pre_norm, post_norm, loss_target, m_w_in, m_attn_sinks, m_lb_logits, m_rnn_norm, m_w_out, m_pre_norm, m_post_norm, v_w_in, v_attn_sinks, v_lb_logits, v_rnn_norm, v_w_out, v_pre_norm, v_post_norm):
    _, T, D = x.shape
    ro = w_out.shape[1]
    n_q = attn_sinks.shape[1]
    assert lb_logits.shape[0] == 2 and n_q == D // ATTN_HEAD and n_q <= LANES

    grad_x, parts_in, parts_out, small_rows = _step(
        _role_ids(), x[0], loss_target[0], _cast_slot0(w_in[0], N_DEV, "cast_w_in"),
        _cast_slot0(w_out[0], N_DEV, "cast_w_out"), attn_sinks, lb_logits, rnn_norm, pre_norm, post_norm, True)

    g_wo, d_wo, nm_wo, nv_wo = _adamw_big(w_out[0], m_w_out[0], v_w_out[0], parts_out, _pick(ro, 64), "adamw_w_out")
    g_wi, d_wi, nm_wi, nv_wi = _adamw_big(w_in[0], m_w_in[0], v_w_in[0], parts_in, _pick(D, 128), "adamw_w_in")

    total = _allreduce_small(small_rows, D)
    pad = lambda a: jnp.pad(a, ((0, 0), (0, LANES - n_q)))
    moments = [pad(m_attn_sinks), m_lb_logits, m_rnn_norm, m_pre_norm, m_post_norm,
               pad(v_attn_sinks), v_lb_logits, v_rnn_norm, v_pre_norm, v_post_norm]
    res = _adamw_small(total, pad(attn_sinks), lb_logits, rnn_norm, pre_norm, post_norm, moments, D)
    loss = res[0][0, 0]
    small = [[res[1 + 4 * i + j] for i in range(5)] for j in range(4)]
    for j in range(4):
        small[j][0] = small[j][0][:, :n_q]

    def assemble(j, wi, wo):
        s = small[j]
        return [wi[None], s[0], s[1], s[2], wo[None], s[3], s[4]]

    return (loss, grad_x[None], *assemble(0, g_wi, g_wo), *assemble(1, d_wi, d_wo),
            *assemble(2, nm_wi, nm_wo), *assemble(3, nv_wi, nv_wo))
```

```python
import functools

import jax
import jax.numpy as jnp
from jax import lax
from jax.experimental import pallas as pl
from jax.experimental.pallas import tpu as pltpu

F32 = jnp.float32
BF16 = jnp.bfloat16

ATTN_HEAD = 64
GQA = 8
WINDOW = 128
RNN_HEAD = 128
CHUNK = 64
NORM_EPS = 1e-6
LANES = 128
N_DEV = 8

ADAM_LR = 0.001
ADAM_B1 = 0.9
ADAM_B2 = 0.999
ADAM_EPS = 1e-08
ADAM_WD = 0.01
ADAM_STEP = 10

VMEM_LIMIT = 56 * 1024 * 1024
MESH = pl.DeviceIdType.MESH
ANY_SPEC = pl.BlockSpec(memory_space=pl.ANY)
HBM_SPEC = pl.BlockSpec(memory_space=pltpu.HBM)
SEM_SPEC = pl.BlockSpec(memory_space=pltpu.SEMAPHORE)
EFFECT = pltpu.SideEffectType.DATAFLOW_SIDE_EFFECTING

NT_DIMS = (((1,), (1,)), ((), ()))
NN_DIMS = (((1,), (0,)), ((), ()))
TN_DIMS = (((0,), (0,)), ((), ()))


def _params(*sem):
    return pltpu.CompilerParams(dimension_semantics=sem, vmem_limit_bytes=VMEM_LIMIT)


def _dot(a, b, dims):
    return lax.dot_general(a, b, dims, preferred_element_type=F32)


def _sigmoid(v):
    return 1.0 / (1.0 + jnp.exp(-v))


def _pick(n, pref):
    t = min(n, pref)
    assert n % t == 0, (n, pref)
    return t


MXU_COLS = 256
WIDE_TILE = 5 * MXU_COLS


def _col_tiles(wd):
    wide = wd // WIDE_TILE
    rest = wd - wide * WIDE_TILE
    assert rest % LANES == 0 and (rest == 0 or (wide * WIDE_TILE) % rest == 0), wd
    return [(WIDE_TILE, 0, wide)] * (wide > 0) + [(rest, wide * WIDE_TILE, 1)] * (rest > 0)


def _matmul(ids, a, b, *, grid, a_spec, b_spec, o_spec, out_shape, trans_b, name, after=(), prev=None):
    nk = grid[2]
    dims = NT_DIMS if trans_b else NN_DIMS
    n_skip = len(after) + (prev is not None)

    def body(ids_ref, a_ref, b_ref, *rest):
        del ids_ref
        o_ref, scratch = rest[n_skip], rest[n_skip + 1:]
        prod = _dot(a_ref[...], b_ref[...], dims)
        if nk == 1:
            o_ref[...] = prod.astype(o_ref.dtype)
        else:
            acc_ref, = scratch
            k = pl.program_id(2)

            @pl.when(k == 0)
            def _():
                acc_ref[...] = prod

            @pl.when(k > 0)
            def _():
                acc_ref[...] += prod

            @pl.when(k == nk - 1)
            def _():
                o_ref[...] = acc_ref[...].astype(o_ref.dtype)

    scratch = [] if nk == 1 else [pltpu.VMEM(tuple(d for d in o_spec.block_shape if d is not None), F32)]
    extra = list(after) + ([prev] if prev is not None else [])
    aliases = {3 + len(after): 0} if prev is not None else {}
    return pl.pallas_call(
        body, name=name, out_shape=out_shape, input_output_aliases=aliases,
        grid_spec=pltpu.PrefetchScalarGridSpec(
            num_scalar_prefetch=1, grid=grid, in_specs=[a_spec, b_spec] + [ANY_SPEC] * len(extra),
            out_specs=o_spec, scratch_shapes=scratch),
        compiler_params=_params("parallel", "parallel", "arbitrary"),
    )(ids, a, b, *extra)


def _proj_in(ids, h, w_roles, pos, n, prev, name, after=()):
    T, D = h.shape
    n_roles, _, wd = w_roles.shape
    tm = _pick(T, 1024)
    for t, (tn, first, count) in enumerate(_col_tiles(wd)):
        prev = _matmul(
            ids, h, w_roles, grid=(T // tm, n * count, 1),
            a_spec=pl.BlockSpec((tm, D), lambda i, j, k, ids: (i, 0)),
            b_spec=pl.BlockSpec((None, D, tn), functools.partial(
                lambda tn, first, count, i, j, k, ids: (ids[n_roles + pos + j // count], 0, first // tn + j % count),
                tn, first, count)),
            o_spec=pl.BlockSpec((pl.Element(tm), pl.Element(tn)), functools.partial(
                lambda tn, first, count, i, j, k, ids: (
                    i * tm, pl.multiple_of(ids[ids[n_roles + pos + j // count]] * wd + first + (j % count) * tn, LANES)),
                tn, first, count)),
            out_shape=jax.ShapeDtypeStruct((T, n_roles * wd), F32), trans_b=False, name=f"{name}_{t}",
            after=after, prev=prev)
    return prev


def _proj_out(ids, mixed, wo_roles):
    T, E = mixed.shape
    n_roles, R, D = wo_roles.shape
    tm, tn = _pick(T, 512), _pick(D, 512)

    def body(ids_ref, a_ref, b_ref, o_ref, a_roles):
        @pl.when(pl.program_id(1) == 0)
        def _():
            for k in range(n_roles):
                col = pl.multiple_of(ids_ref[k] * R, R)
                a_roles[:, k * R:(k + 1) * R] = a_ref[:, pl.ds(col, R)]

        o_ref[...] = _dot(a_roles[...], b_ref[...].reshape(E, tn), NN_DIMS)

    return pl.pallas_call(
        body, name="proj_out", out_shape=jax.ShapeDtypeStruct((T, D), F32),
        grid_spec=pltpu.PrefetchScalarGridSpec(
            num_scalar_prefetch=1, grid=(T // tm, D // tn),
            in_specs=[pl.BlockSpec((tm, E), lambda i, j, ids: (i, 0)),
                      pl.BlockSpec((n_roles, R, tn), lambda i, j, ids: (0, 0, j))],
            out_specs=pl.BlockSpec((tm, tn), lambda i, j, ids: (i, j)),
            scratch_shapes=[pltpu.VMEM((tm, E), BF16)]),
        compiler_params=_params("parallel", "arbitrary"))(ids, mixed, wo_roles)


def _dmixed(ids, dy, wo_roles):
    T, D = dy.shape
    n_roles, R, _ = wo_roles.shape
    tm = _pick(T, 1024)
    return _matmul(
        ids, dy, wo_roles, grid=(T // tm, n_roles, 1),
        a_spec=pl.BlockSpec((tm, D), lambda i, j, k, ids: (i, 0)),
        b_spec=pl.BlockSpec((None, R, D), lambda i, j, k, ids: (j, 0, 0)),
        o_spec=pl.BlockSpec((tm, R), lambda i, j, k, ids: (i, ids[j])),
        out_shape=jax.ShapeDtypeStruct((T, n_roles * R), F32), trans_b=True, name="dmixed")


def _dw_out(ids, mixed_t, dy, n_roles):
    E, T = mixed_t.shape
    D = dy.shape[1]
    R = E // n_roles
    tn = _pick(D, 512)
    return _matmul(
        ids, mixed_t, dy, grid=(n_roles, D // tn, 1),
        a_spec=pl.BlockSpec((R, T), lambda i, j, k, ids: (ids[i], 0)),
        b_spec=pl.BlockSpec((T, tn), lambda i, j, k, ids: (0, j)),
        o_spec=pl.BlockSpec((None, R, tn), lambda i, j, k, ids: (i, 0, j)),
        out_shape=jax.ShapeDtypeStruct((n_roles, R, D), BF16), trans_b=False, name="dw_out")


def _dw_in(ids, h_t, dproj, n_roles, r0, nr, name, after=()):
    D, T = h_t.shape
    wd = dproj.shape[1] // n_roles
    tm, out = _pick(D, 1024), None
    for t, (tn, first, count) in enumerate(_col_tiles(wd)):
        out = _matmul(
            ids, h_t, dproj, grid=(D // tm, nr * count, 1),
            a_spec=pl.BlockSpec((tm, T), lambda i, j, k, ids: (i, 0)),
            b_spec=pl.BlockSpec((pl.Element(T), pl.Element(tn)), functools.partial(
                lambda tn, first, count, i, j, k, ids: (
                    0, pl.multiple_of(ids[r0 + j // count] * wd + first + (j % count) * tn, LANES)),
                tn, first, count)),
            o_spec=pl.BlockSpec((None, tm, tn), functools.partial(
                lambda tn, first, count, i, j, k, ids: (j // count, i, first // tn + j % count), tn, first, count)),
            out_shape=jax.ShapeDtypeStruct((nr, D, wd), BF16), trans_b=False, name=f"{name}_{t}", after=after, prev=out)
    return out


def _dh(ids, dproj, w_roles, after=()):
    T = dproj.shape[0]
    n_roles, D, wd = w_roles.shape
    tm, tn = _pick(T, 1024), _pick(D, 512)
    nk, n_after = n_roles // 2, len(after)

    def body(ids_ref, a0_ref, a1_ref, b0_ref, b1_ref, *rest):
        del ids_ref
        o_ref, acc_ref = rest[n_after], rest[n_after + 1]
        prod = _dot(a0_ref[...], b0_ref[...], NT_DIMS) + _dot(a1_ref[...], b1_ref[...], NT_DIMS)
        k = pl.program_id(2)

        @pl.when(k == 0)
        def _():
            acc_ref[...] = prod

        @pl.when((k > 0) & (k < nk - 1))
        def _():
            acc_ref[...] += prod

        @pl.when(k == nk - 1)
        def _():
            o_ref[...] = acc_ref[...] + prod

    a_spec = lambda e: pl.BlockSpec((tm, wd), lambda i, j, k, ids: (i, ids[2 * k + e]))
    b_spec = lambda e: pl.BlockSpec((None, tn, wd), lambda i, j, k, ids: (2 * k + e, j, 0))
    return pl.pallas_call(
        body, name="dh", out_shape=jax.ShapeDtypeStruct((T, D), F32),
        grid_spec=pltpu.PrefetchScalarGridSpec(
            num_scalar_prefetch=1, grid=(T // tm, D // tn, nk),
            in_specs=[a_spec(0), a_spec(1), b_spec(0), b_spec(1)] + [ANY_SPEC] * n_after,
            out_specs=pl.BlockSpec((tm, tn), lambda i, j, k, ids: (i, j)),
            scratch_shapes=[pltpu.VMEM((tm, tn), F32)]),
        compiler_params=_params("parallel", "parallel", "arbitrary"),
    )(ids, dproj, dproj, w_roles, w_roles, *after)


def _cast_slot0(w, n_roles, name):
    R, C = w.shape
    tr = _pick(R, 256)

    def body(w_ref, o_ref):
        o_ref[...] = w_ref[...].astype(BF16)

    return pl.pallas_call(
        body, name=name, grid=(R // tr,), in_specs=[pl.BlockSpec((tr, C), lambda i: (i, 0))],
        out_specs=pl.BlockSpec((None, tr, C), lambda i: (0, i, 0)),
        out_shape=jax.ShapeDtypeStruct((n_roles, R, C), BF16), compiler_params=_params("parallel"))(w)


def _prenorm(x, gain):
    T, D = x.shape
    tm = _pick(T, 256)

    def body(x_ref, g_ref, h_ref, ht_ref):
        xv = x_ref[...]
        r = lax.rsqrt(jnp.mean(xv * xv, axis=-1, keepdims=True) + NORM_EPS)
        h = xv * r * g_ref[...]
        h_ref[...] = h.astype(BF16)
        ht_ref[...] = h.T.astype(BF16)

    return pl.pallas_call(
        body, name="prenorm", grid=(T // tm,),
        in_specs=[pl.BlockSpec((tm, D), lambda i: (i, 0)), pl.BlockSpec((1, D), lambda i: (0, 0))],
        out_specs=[pl.BlockSpec((tm, D), lambda i: (i, 0)), pl.BlockSpec((D, tm), lambda i: (0, i))],
        out_shape=[jax.ShapeDtypeStruct((T, D), BF16), jax.ShapeDtypeStruct((D, T), BF16)],
        compiler_params=_params("parallel"))(x, gain)


def _norm_bwd(u, yn, r):
    return r * (u - yn * jnp.mean(u * yn, axis=-1, keepdims=True))


def _loss_head(y, x, target, gain):
    T, D = y.shape
    tm = _pick(T, 256)

    def body(y_ref, x_ref, t_ref, g_ref, dy_ref, dout_ref, gpost_ref, sq_ref):
        yv = y_ref[...]
        g = g_ref[...]
        r = lax.rsqrt(jnp.mean(yv * yv, axis=-1, keepdims=True) + NORM_EPS)
        yn = yv * r
        err = x_ref[...] + yn * g - t_ref[...]
        dout = err * (1.0 / D)
        dy_ref[...] = _norm_bwd(dout * g, yn, r).astype(BF16)
        dout_ref[...] = dout

        @pl.when(pl.program_id(0) == 0)
        def _():
            gpost_ref[...] = jnp.zeros_like(gpost_ref)
            sq_ref[...] = jnp.zeros_like(sq_ref)

        gpost_ref[...] += jnp.sum(dout * yn, axis=0, keepdims=True)
        sq_ref[...] += jnp.sum(err * err, axis=0, keepdims=True)

    row = pl.BlockSpec((tm, D), lambda i: (i, 0))
    vec = pl.BlockSpec((1, D), lambda i: (0, 0))
    return pl.pallas_call(
        body, name="loss_head", grid=(T // tm,), in_specs=[row, row, row, vec], out_specs=[row, row, vec, vec],
        out_shape=[jax.ShapeDtypeStruct((T, D), BF16), jax.ShapeDtypeStruct((T, D), F32),
                   jax.ShapeDtypeStruct((1, D), F32), jax.ShapeDtypeStruct((1, D), F32)],
        compiler_params=_params("arbitrary"))(y, x, target, gain)


def _prenorm_bwd(x, dh, dout, gain):
    T, D = x.shape
    tm = _pick(T, 256)

    def body(x_ref, dh_ref, dout_ref, g_ref, gx_ref, gpre_ref):
        xv = x_ref[...]
        dhv = dh_ref[...]
        r = lax.rsqrt(jnp.mean(xv * xv, axis=-1, keepdims=True) + NORM_EPS)
        xn = xv * r
        gx_ref[...] = dout_ref[...] + _norm_bwd(dhv * g_ref[...], xn, r)

        @pl.when(pl.program_id(0) == 0)
        def _():
            gpre_ref[...] = jnp.zeros_like(gpre_ref)

        gpre_ref[...] += jnp.sum(dhv * xn, axis=0, keepdims=True)

    row = pl.BlockSpec((tm, D), lambda i: (i, 0))
    vec = pl.BlockSpec((1, D), lambda i: (0, 0))
    return pl.pallas_call(
        body, name="prenorm_bwd", grid=(T // tm,), in_specs=[row, row, row, vec], out_specs=[row, vec],
        out_shape=[jax.ShapeDtypeStruct((T, D), F32), jax.ShapeDtypeStruct((1, D), F32)],
        compiler_params=_params("arbitrary"))(x, dh, dout, gain)


def _attn_masks(n):
    row = lax.broadcasted_iota(jnp.int32, (2 * WINDOW, 2 * WINDOW), 0) % WINDOW
    col = lax.broadcasted_iota(jnp.int32, (2 * WINDOW, 2 * WINDOW), 1)
    valid = (col > row) & (col <= row + WINDOW) & ((n > 0) | (col >= WINDOW))
    low = lax.broadcasted_iota(jnp.int32, (1, LANES), 1) < ATTN_HEAD
    top = lax.broadcasted_iota(jnp.int32, (2 * WINDOW, 1), 0) < WINDOW
    return jnp.where(valid, 0.0, -jnp.inf), low, top


def _dup_half(pair, keep):
    return jnp.where(keep, pair, pltpu.roll(pair, ATTN_HEAD, 1))


def _fold_half(v):
    return v + pltpu.roll(v, ATTN_HEAD, 1)


ATTN_GROUP_FWD = 8
ATTN_GROUP = 4
ATTN_SCALE = ATTN_HEAD ** -0.5


def _attn_scores(qpair, k2, low):
    qs = qpair * ATTN_SCALE
    q2 = jnp.concatenate([jnp.where(low, qs, 0.0), jnp.where(low, 0.0, qs)], axis=0).astype(BF16)
    return q2, _dot(q2, k2, NT_DIMS)


def _attn_softmax(scores, sink_lo, sink_hi, bias, top):
    s = scores + bias
    sink = jnp.where(top, sink_lo, sink_hi)
    m = jnp.maximum(jnp.max(s, axis=-1, keepdims=True), sink)
    p = jnp.exp(s - m)
    psink = jnp.exp(sink - m)
    inv = 1.0 / (jnp.sum(p, axis=-1, keepdims=True) + psink)
    return p * inv, psink * inv


def _attn_fwd(proj, sinks, D):
    T = proj.shape[0]
    nb, njp = T // WINDOW, D // 1024
    assert nb % 2 == 0
    two = 2 * WINDOW

    def body(sink_ref, qlo_ref, qhi_ref, kc_ref, kp_ref, vc_ref, vp_ref, glo_ref, ghi_ref, mix_ref, mixt_ref):
        jp, m = pl.program_id(0), pl.program_id(1)
        q_refs, g_refs = (qlo_ref, qhi_ref), (glo_ref, ghi_ref)
        biases, k2s, v2s = [], {}, {}
        for blk in (0, 1):
            bias, low, top = _attn_masks(2 * m + blk)
            biases.append(bias)
            keys = kc_ref[0:two, :] if blk else jnp.concatenate([kp_ref[...], kc_ref[0:WINDOW, :]], axis=0)
            vals = vc_ref[0:two, :] if blk else jnp.concatenate([vp_ref[...], vc_ref[0:WINDOW, :]], axis=0)
            for hj, keep in enumerate((low, jnp.logical_not(low))):
                k2s[blk, hj] = _dup_half(keys, keep).astype(BF16)
                v2s[blk, hj] = _dup_half(vals, keep).astype(BF16)
        all_units = [(blk, hj, p) for blk in (0, 1) for hj in (0, 1) for p in range(4)]
        for first in range(0, len(all_units), ATTN_GROUP_FWD):
            units = all_units[first:first + ATTN_GROUP_FWD]
            rows = {u: slice(WINDOW * u[0], WINDOW * (u[0] + 1)) for u in units}
            cols = {u: slice(LANES * u[2], LANES * (u[2] + 1)) for u in units}
            scores = {u: _attn_scores(q_refs[u[1]][rows[u], cols[u]], k2s[u[0], u[1]], low)[1] for u in units}
            probs = {}
            for u in units:
                head = (2 * jp + u[1]) * GQA + 2 * u[2]
                probs[u] = _attn_softmax(scores[u], sink_ref[0, head], sink_ref[0, head + 1], biases[u[0]], top)[0]
            o2s = {u: _dot(probs[u].astype(BF16), v2s[u[0], u[1]], NN_DIMS) for u in units}
            for u in units:
                opair = jnp.where(low, o2s[u][:WINDOW], o2s[u][WINDOW:])
                g = g_refs[u[1]][rows[u], cols[u]]
                out = opair * (g * _sigmoid(g))
                oc = slice(512 * u[1] + LANES * u[2], 512 * u[1] + LANES * (u[2] + 1))
                mix_ref[rows[u], oc] = out.astype(BF16)
                mixt_ref[oc, rows[u]] = out.T.astype(BF16)

    kb = D // LANES
    vb = kb + D // (8 * LANES)
    gb = (D + D // 4) // 512
    wide = lambda off: [pl.BlockSpec((two, 512), functools.partial(lambda o, e, jp, m: (m, o + 2 * jp + e), off, e))
                        for e in (0, 1)]
    cur = lambda off: pl.BlockSpec((two, LANES), functools.partial(lambda o, jp, m: (m, o + jp), off))
    prev = lambda off: pl.BlockSpec((WINDOW, LANES),
                                    functools.partial(lambda o, jp, m: (jnp.maximum(2 * m - 1, 0), o + jp), off))
    return pl.pallas_call(
        body, name="attn_fwd", grid=(njp, nb // 2),
        in_specs=[pl.BlockSpec(memory_space=pltpu.SMEM)] + wide(0) + [cur(kb), prev(kb), cur(vb), prev(vb)] + wide(gb),
        out_specs=[pl.BlockSpec((two, 1024), lambda jp, m: (m, jp)),
                   pl.BlockSpec((1024, two), lambda jp, m: (jp, m))],
        out_shape=[jax.ShapeDtypeStruct((T, 2 * D), BF16), jax.ShapeDtypeStruct((2 * D, T), BF16)],
        compiler_params=_params("parallel", "parallel"))(sinks, *([proj] * 8))


def _flush_windows(bufs, sems, hbm_ref, corners, slot, step, last):
    def copies(sl):
        return [pltpu.make_async_copy(
            b.at[sl], hbm_ref.at[pl.ds(r0, b.shape[1]), pl.ds(c0, b.shape[2])], sems.at[sl, i])
            for i, (b, (r0, c0)) in enumerate(zip(bufs, corners))]

    for cp in copies(slot):
        cp.start()

    @pl.when(step > 0)
    def _():
        for cp in copies(1 - slot):
            cp.wait()

    @pl.when(step == last)
    def _():
        for cp in copies(slot):
            cp.wait()


def _attn_bwd(proj, sinks, dmix, D, after=()):
    T = proj.shape[0]
    nb, njp = T // WINDOW, D // 1024
    assert nb % 2 == 0
    two, ns = 2 * WINDOW, nb // 2
    n_after = len(after)

    def body(sink_ref, qlo_ref, qhi_ref, kc_ref, kp_ref, vc_ref, vp_ref, glo_ref, ghi_ref, dmix_ref, *rest):
        dproj_ref, dsink_ref, kcarry_ref, vcarry_ref, dq_buf, dk_buf, dv_buf, dg_buf, out_sems = rest[n_after:]
        jp, step = pl.program_id(0), pl.program_id(1)
        m = ns - 1 - step
        slot = step % 2
        dq_ref, dk_ref, dv_ref, dg_ref = dq_buf.at[slot], dk_buf.at[slot], dv_buf.at[slot], dg_buf.at[slot]
        lane = lax.broadcasted_iota(jnp.int32, (1, LANES), 1)

        @pl.when(step == 0)
        def _():
            kcarry_ref[...] = jnp.zeros_like(kcarry_ref)
            vcarry_ref[...] = jnp.zeros_like(vcarry_ref)
            dsink_ref[...] = jnp.zeros_like(dsink_ref)

        q_refs, g_refs = (qlo_ref, qhi_ref), (glo_ref, ghi_ref)
        biases, k2s, v2s, dk_heads, dv_heads = [], {}, {}, {}, {}
        for blk in (0, 1):
            bias, low, top = _attn_masks(2 * m + blk)
            biases.append(bias)
            keys = kc_ref[0:two, :] if blk else jnp.concatenate([kp_ref[...], kc_ref[0:WINDOW, :]], axis=0)
            vals = vc_ref[0:two, :] if blk else jnp.concatenate([vp_ref[...], vc_ref[0:WINDOW, :]], axis=0)
            for hj, keep in enumerate((low, jnp.logical_not(low))):
                k2s[blk, hj] = _dup_half(keys, keep).astype(BF16)
                v2s[blk, hj] = _dup_half(vals, keep).astype(BF16)
                dk_heads[blk, hj] = jnp.zeros((two, LANES), F32)
                dv_heads[blk, hj] = jnp.zeros((two, LANES), F32)
        keeps = (low, jnp.logical_not(low))
        dsink = jnp.zeros((1, LANES), F32)
        all_units = [(blk, hj, p) for blk in (1, 0) for hj in (0, 1) for p in range(4)]
        for first in range(0, len(all_units), ATTN_GROUP):
            units = all_units[first:first + ATTN_GROUP]
            rows = {u: slice(WINDOW * u[0], WINDOW * (u[0] + 1)) for u in units}
            cols = {u: slice(LANES * u[2], LANES * (u[2] + 1)) for u in units}
            ocs = {u: slice(512 * u[1] + LANES * u[2], 512 * u[1] + LANES * (u[2] + 1)) for u in units}
            qsc = {u: _attn_scores(q_refs[u[1]][rows[u], cols[u]], k2s[u[0], u[1]], low) for u in units}
            probs, psinks, do2s = {}, {}, {}
            for u in units:
                head = (2 * jp + u[1]) * GQA + 2 * u[2]
                probs[u], psinks[u] = _attn_softmax(qsc[u][1], sink_ref[0, head], sink_ref[0, head + 1], biases[u[0]], top)
                g = g_refs[u[1]][rows[u], cols[u]]
                do = dmix_ref[rows[u], ocs[u]] * (g * _sigmoid(g))
                do2s[u] = jnp.concatenate([jnp.where(low, do, 0.0), jnp.where(low, 0.0, do)], axis=0).astype(BF16)
            pbs = {u: probs[u].astype(BF16) for u in units}
            o2s = {u: _dot(pbs[u], v2s[u[0], u[1]], NN_DIMS) for u in units}
            dps = {u: _dot(do2s[u], v2s[u[0], u[1]], NT_DIMS) for u in units}
            dss = {}
            for u in units:
                opair = jnp.where(low, o2s[u][:WINDOW], o2s[u][WINDOW:])
                g = g_refs[u[1]][rows[u], cols[u]]
                sg = _sigmoid(g)
                dg_ref[rows[u], ocs[u]] = (dmix_ref[rows[u], ocs[u]] * opair * (sg * (1.0 + g * (1.0 - sg)))).astype(BF16)
                delta = jnp.sum(probs[u] * dps[u], axis=-1, keepdims=True)
                dss[u] = (probs[u] * (dps[u] - delta)).astype(BF16)
                ps = psinks[u] * delta
                local = u[1] * GQA + 2 * u[2]
                dsink -= jnp.where(lane == local, jnp.sum(ps[:WINDOW], axis=0, keepdims=True), 0.0)
                dsink -= jnp.where(lane == local + 1, jnp.sum(ps[WINDOW:], axis=0, keepdims=True), 0.0)
            dq2s = {u: _dot(dss[u], k2s[u[0], u[1]], NN_DIMS) for u in units}
            for u in units:
                dk_heads[u[0], u[1]] += _dot(dss[u], qsc[u][0], TN_DIMS)
                dv_heads[u[0], u[1]] += _dot(pbs[u], do2s[u], TN_DIMS)
            for u in units:
                dq_ref[rows[u], ocs[u]] = (jnp.where(low, dq2s[u][:WINDOW], dq2s[u][WINDOW:]) * ATTN_SCALE).astype(BF16)
        dk_pair = [sum(jnp.where(keep, _fold_half(dk_heads[blk, hj]), 0.0) for hj, keep in enumerate(keeps)) for blk in (0, 1)]
        dv_pair = [sum(jnp.where(keep, _fold_half(dv_heads[blk, hj]), 0.0) for hj, keep in enumerate(keeps)) for blk in (0, 1)]
        dk_ref[WINDOW:, :] = (dk_pair[1][WINDOW:] + kcarry_ref[...]).astype(BF16)
        dv_ref[WINDOW:, :] = (dv_pair[1][WINDOW:] + vcarry_ref[...]).astype(BF16)
        dk_ref[:WINDOW, :] = (dk_pair[0][WINDOW:] + dk_pair[1][:WINDOW]).astype(BF16)
        dv_ref[:WINDOW, :] = (dv_pair[0][WINDOW:] + dv_pair[1][:WINDOW]).astype(BF16)
        kcarry_ref[...] = dk_pair[0][:WINDOW]
        vcarry_ref[...] = dv_pair[0][:WINDOW]
        dsink_ref[...] += dsink
        row = pl.multiple_of(m * two, two)
        col = lambda base, width: pl.multiple_of(base + jp * width, LANES)
        corners = [(row, col(0, 1024)), (row, col(D, LANES)), (row, col(D + D // 8, LANES)), (row, col(D + D // 4, 1024))]
        _flush_windows([dq_buf, dk_buf, dv_buf, dg_buf], out_sems, dproj_ref, corners, slot, step, ns - 1)

    kb = D // LANES
    vb = kb + D // (8 * LANES)
    gb = (D + D // 4) // 512
    wide = lambda off: [pl.BlockSpec((two, 512), functools.partial(lambda o, e, jp, s: (ns - 1 - s, o + 2 * jp + e), off, e))
                        for e in (0, 1)]
    cur = lambda off: pl.BlockSpec((two, LANES), functools.partial(lambda o, jp, s: (ns - 1 - s, o + jp), off))
    prev = lambda off: pl.BlockSpec((WINDOW, LANES), functools.partial(
        lambda o, jp, s: (jnp.maximum(2 * (ns - 1 - s) - 1, 0), o + jp), off))
    wide_in = pl.BlockSpec((two, 1024), lambda jp, s: (ns - 1 - s, jp))
    return pl.pallas_call(
        body, name="attn_bwd", grid=(njp, ns),
        in_specs=[pl.BlockSpec(memory_space=pltpu.SMEM)] + wide(0) + [cur(kb), prev(kb), cur(vb), prev(vb)] + wide(gb)
        + [wide_in] + [ANY_SPEC] * n_after,
        out_specs=[ANY_SPEC, pl.BlockSpec((None, 1, LANES), lambda jp, s: (jp, 0, 0))],
        out_shape=[jax.ShapeDtypeStruct((T, 6 * D + D // 4), BF16), jax.ShapeDtypeStruct((njp, 1, LANES), F32)],
        scratch_shapes=[pltpu.VMEM((WINDOW, LANES), F32), pltpu.VMEM((WINDOW, LANES), F32),
                        pltpu.VMEM((2, two, 1024), BF16), pltpu.VMEM((2, two, LANES), BF16),
                        pltpu.VMEM((2, two, LANES), BF16), pltpu.VMEM((2, two, 1024), BF16),
                        pltpu.SemaphoreType.DMA((2, 4))],
        compiler_params=_params("parallel", "arbitrary"))(sinks, *([proj] * 8), dmix, *after)


RNN_TB = 512
RNN_HB = 8
RNN_WIDE = RNN_HB * RNN_HEAD


def _split3(v):
    a = v.astype(BF16)
    r = v - a.astype(F32)
    b = r.astype(BF16)
    c = (r - b.astype(F32)).astype(BF16)
    return a, b, c


def _tri_sum(tri, v):
    a, b, c = _split3(v)
    return _dot(tri, a, NN_DIMS) + _dot(tri, b, NN_DIMS) + _dot(tri, c, NN_DIMS)


def _lower_bound(lb_ref):
    l0, l1 = lb_ref[0:1, :], lb_ref[1:2, :]
    m = jnp.maximum(l0, l1)
    e0, e1 = jnp.exp(l0 - m), jnp.exp(l1 - m)
    return e0 / (e0 + e1)


def _rnn_gates(rq, rf, lb):
    sq = _sigmoid(rq)
    sf = _sigmoid(rf)
    f = lb + (1.0 - lb) * sf
    return sq, sf, f


def _rnn_factors(G):
    last = G[CHUNK - 1:CHUNK, :]
    mid = G[CHUNK // 2 - 1:CHUNK // 2, :]
    return G, jnp.exp(G), jnp.exp(G - mid), jnp.exp(mid - G), jnp.exp(last - G), jnp.exp(last)


def _chunk_masks():
    r = lax.broadcasted_iota(jnp.int32, (CHUNK, CHUNK), 0)
    c = lax.broadcasted_iota(jnp.int32, (CHUNK, CHUNK), 1)
    return r >= c, (r >= c).astype(BF16), (r <= c).astype(BF16)


def _rnn_specs(T, D, tb, rev):
    nt = T // tb
    base = (2 * D + D // 4) // LANES
    t_of = (lambda s: nt - 1 - s) if rev else (lambda s: s)
    assert base % RNN_HB == 0 and (D // LANES) % RNN_HB == 0
    cols = [pl.BlockSpec((tb, RNN_WIDE), functools.partial(lambda o, h, s: (t_of(s), o + h),
                                                            (base + i * (D // LANES)) // RNN_HB))
            for i in range(4)]
    return cols, t_of


def _rnn_fwd(proj, lb_logits, rnn_norm, mixed, mixed_t, D):
    T = proj.shape[0]
    tb = _pick(T, RNN_TB)
    nt, nh, cpb = T // tb, D // RNN_HEAD, tb // CHUNK
    cols, _ = _rnn_specs(T, D, tb, False)

    def body(rq_ref, rf_ref, ri_ref, rg_ref, lb_ref, gain_ref, mix_in, mixt_in,
             mix_ref, mixt_ref, o_ref, st_ref, state_ref):
        del mix_in, mixt_in
        causal, tri, _ = _chunk_masks()
        lb = _lower_bound(lb_ref)

        @pl.when(pl.program_id(1) == 0)
        def _():
            state_ref[...] = jnp.zeros_like(state_ref)

        def chunk(c, carry):
            rows = pl.ds(pl.multiple_of(c * CHUNK, CHUNK), CHUNK)
            heads = range(RNN_HB)
            lns = [slice(RNN_HEAD * hh, RNN_HEAD * (hh + 1)) for hh in heads]
            qs, ks, Gs = [], [], []
            for ln in lns:
                rq, rf = rq_ref[rows, ln], rf_ref[rows, ln]
                sq, _, f = _rnn_gates(rq, rf, lb[:, ln])
                qs.append(rq * sq)
                ks.append(1.0 - f)
                Gs.append(_tri_sum(tri, jnp.log(f)))
            atts, inters, vbs = [], [], []
            for hh, ln in enumerate(lns):
                _, eG, eq, ek, ekl, elast = _rnn_factors(Gs[hh])
                q, k = qs[hh], ks[hh]
                st = state_ref[hh]
                st_ref[hh, c] = st
                vb = ri_ref[rows, ln].astype(BF16)
                vbs.append(vb)
                atts.append(_dot((q * eq).astype(BF16), (k * ek).astype(BF16), NT_DIMS))
                inters.append(_dot((q * eG).astype(BF16), st.astype(BF16), NT_DIMS))
                state_ref[hh] = st * elast + _dot(vb, (k * ekl).astype(BF16), TN_DIMS)
            intras = [_dot(jnp.where(causal, atts[hh], 0.0).astype(BF16), vbs[hh], NN_DIMS) for hh in heads]
            for hh, ln in enumerate(lns):
                o = inters[hh] + intras[hh]
                o_ref[rows, ln] = o
                rg = rg_ref[rows, ln]
                r = lax.rsqrt(jnp.mean(o * o, axis=-1, keepdims=True) + NORM_EPS)
                out = (o * r * gain_ref[:, ln]) * (rg * _sigmoid(rg))
                mix_ref[rows, ln] = out.astype(BF16)
            return carry

        lax.fori_loop(0, cpb, chunk, 0, unroll=True)
        mixt_ref[...] = mix_ref[...].astype(F32).T.astype(BF16)

    nh //= RNN_HB
    vec2 = pl.BlockSpec((2, RNN_WIDE), lambda h, s: (0, h))
    vec1 = pl.BlockSpec((1, RNN_WIDE), lambda h, s: (0, h))
    return pl.pallas_call(
        body, name="rnn_fwd", grid=(nh, nt),
        in_specs=cols + [vec2, vec1, ANY_SPEC, ANY_SPEC],
        out_specs=[pl.BlockSpec((tb, RNN_WIDE), lambda h, s: (s, D // RNN_WIDE + h)),
                   pl.BlockSpec((RNN_WIDE, tb), lambda h, s: (D // RNN_WIDE + h, s)),
                   pl.BlockSpec((tb, RNN_WIDE), lambda h, s: (s, h)),
                   pl.BlockSpec((RNN_HB, cpb, RNN_HEAD, RNN_HEAD), lambda h, s: (h, s, 0, 0))],
        out_shape=[jax.ShapeDtypeStruct(mixed.shape, BF16), jax.ShapeDtypeStruct(mixed_t.shape, BF16),
                   jax.ShapeDtypeStruct((T, D), F32),
                   jax.ShapeDtypeStruct((nh * RNN_HB, T // CHUNK, RNN_HEAD, RNN_HEAD), F32)],
        scratch_shapes=[pltpu.VMEM((RNN_HB, RNN_HEAD, RNN_HEAD), F32)],
        input_output_aliases={6: 0, 7: 1},
        compiler_params=_params("parallel", "arbitrary"))(proj, proj, proj, proj, lb_logits, rnn_norm, mixed, mixed_t)


def _rnn_bwd(proj, lb_logits, rnn_norm, o_raw, states, dmix, dproj, D):
    T = proj.shape[0]
    tb = _pick(T, RNN_TB)
    nt, nh, cpb = T // tb, D // RNN_HEAD, tb // CHUNK
    cols, t_of = _rnn_specs(T, D, tb, True)

    def body(rq_ref, rf_ref, ri_ref, rg_ref, lb_ref, gain_ref, o_ref, st_ref, dmix_ref, dproj_in,
             dproj_ref, dlb_ref, dgain_ref, dstate_ref, out_buf, out_sems):
        del dproj_in
        step = pl.program_id(1)
        slot = step % 2
        drq_ref, drf_ref, dri_ref, drg_ref = [out_buf.at[i, slot] for i in range(4)]
        causal, tri, tri_t = _chunk_masks()
        lb = _lower_bound(lb_ref)
        gain = gain_ref[...]

        @pl.when(pl.program_id(1) == 0)
        def _():
            dstate_ref[...] = jnp.zeros_like(dstate_ref)
            dlb_ref[...] = jnp.zeros_like(dlb_ref)
            dgain_ref[...] = jnp.zeros_like(dgain_ref)

        def chunk(i, carry):
            c = cpb - 1 - i
            rows = pl.ds(pl.multiple_of(c * CHUNK, CHUNK), CHUNK)
            heads = range(RNN_HB)
            lns = [slice(RNN_HEAD * hh, RNN_HEAD * (hh + 1)) for hh in heads]
            last_row = lax.broadcasted_iota(jnp.int32, (CHUNK, 1), 0) == CHUNK - 1
            A = []
            for ln in lns:
                rq, rf, rg = rq_ref[rows, ln], rf_ref[rows, ln], rg_ref[rows, ln]
                o, dgated = o_ref[rows, ln], dmix_ref[rows, ln]
                gainh = gain[:, ln]
                sgt = _sigmoid(rg)
                r = lax.rsqrt(jnp.mean(o * o, axis=-1, keepdims=True) + NORM_EPS)
                on = o * r
                drg_ref[rows, ln] = (dgated * (on * gainh) * (sgt * (1.0 + rg * (1.0 - sgt)))).astype(BF16)
                d_on = dgated * (rg * sgt)
                dgain_ref[:, ln] += jnp.sum(d_on * on, axis=0, keepdims=True)
                dob = _norm_bwd(d_on * gainh, on, r).astype(BF16)
                sq, sf, f = _rnn_gates(rq, rf, lb[:, ln])
                A.append(dict(rq=rq, sq=sq, sf=sf, f=f, dob=dob, G=_tri_sum(tri, jnp.log(f))))
            for hh, ln in enumerate(lns):
                a = A[hh]
                _, eG, eq, ek, ekl, elast = _rnn_factors(a.pop("G"))
                q, k = a["rq"] * a["sq"], 1.0 - a["f"]
                st, dst = st_ref[hh, c], dstate_ref[hh]
                qg, kl = q * eG, k * ekl
                qmb, kmb = (q * eq).astype(BF16), (k * ek).astype(BF16)
                dob, vb, dstb = a["dob"], ri_ref[rows, ln].astype(BF16), dst.astype(BF16)
                a.update(eG=eG, eq=eq, ek=ek, ekl=ekl, qg=qg, kl=kl, qmb=qmb, kmb=kmb,
                         att=_dot(qmb, kmb, NT_DIMS), datt=_dot(dob, vb, NT_DIMS),
                         dqg=_dot(dob, st.astype(BF16), NN_DIMS), dkl=_dot(vb, dstb, NN_DIMS),
                         dri=_dot(kl.astype(BF16), dstb, NT_DIMS),
                         dlast=jnp.sum(dst * st, axis=0, keepdims=True) * elast)
                dstate_ref[hh] = dst * elast + _dot(dob, qg.astype(BF16), TN_DIMS)
            for hh, ln in enumerate(lns):
                a = A[hh]
                att = jnp.where(causal, a.pop("att"), 0.0).astype(BF16)
                datt = jnp.where(causal, a.pop("datt"), 0.0).astype(BF16)
                dqm = _dot(datt, a["kmb"], NN_DIMS)
                dkm = _dot(datt, a["qmb"], TN_DIMS)
                dri_ref[rows, ln] = (_dot(att, a["dob"], TN_DIMS) + a.pop("dri")).astype(BF16)
                dqg, dkl, kl = a.pop("dqg"), a.pop("dkl"), a.pop("kl")
                a["dq"] = dqg * a.pop("eG") + dqm * a.pop("eq")
                a["dk"] = dkm * a.pop("ek") + dkl * a.pop("ekl")
                dG = dqg * a.pop("qg") + dqm * a.pop("qmb").astype(F32) - dkm * a.pop("kmb").astype(F32) - dkl * kl
                dlast = jnp.sum(dkl * kl, axis=0, keepdims=True) + a.pop("dlast")
                a["dg"] = _tri_sum(tri_t, dG + jnp.where(last_row, dlast, 0.0))
            for hh, ln in enumerate(lns):
                a = A[hh]
                rq, sq, sf = a["rq"], a["sq"], a["sf"]
                df = a["dg"] / a["f"] - a["dk"]
                drq_ref[rows, ln] = (a["dq"] * (sq * (1.0 + rq * (1.0 - sq)))).astype(BF16)
                drf_ref[rows, ln] = (df * (1.0 - lb[:, ln]) * (sf * (1.0 - sf))).astype(BF16)
                dlb_ref[:, ln] += jnp.sum(df * (1.0 - sf), axis=0, keepdims=True)
            return carry

        lax.fori_loop(0, cpb, chunk, 0, unroll=True)
        row = pl.multiple_of(t_of(step) * tb, tb)
        col0 = 2 * D + D // 4 + pl.program_id(0) * RNN_WIDE
        corners = [(row, pl.multiple_of(col0 + i * D, LANES)) for i in range(4)]
        _flush_windows([out_buf.at[i] for i in range(4)], out_sems, dproj_ref, corners, slot, step, nt - 1)

    nh //= RNN_HB
    vec2 = pl.BlockSpec((2, RNN_WIDE), lambda h, s: (0, h))
    vec1 = pl.BlockSpec((1, RNN_WIDE), lambda h, s: (0, h))
    blk = pl.BlockSpec((tb, RNN_WIDE), lambda h, s: (t_of(s), h))
    return pl.pallas_call(
        body, name="rnn_bwd", grid=(nh, nt),
        in_specs=cols + [vec2, vec1, blk,
                         pl.BlockSpec((RNN_HB, cpb, RNN_HEAD, RNN_HEAD), lambda h, s: (h, t_of(s), 0, 0)),
                         pl.BlockSpec((tb, RNN_WIDE), lambda h, s: (t_of(s), D // RNN_WIDE + h)), ANY_SPEC],
        out_specs=[ANY_SPEC, vec1, vec1],
        out_shape=[jax.ShapeDtypeStruct(dproj.shape, BF16)] + [jax.ShapeDtypeStruct((1, D), F32)] * 2,
        scratch_shapes=[pltpu.VMEM((RNN_HB, RNN_HEAD, RNN_HEAD), F32), pltpu.VMEM((4, 2, tb, RNN_WIDE), BF16),
                        pltpu.SemaphoreType.DMA((2, 4))],
        input_output_aliases={9: 0},
        compiler_params=_params("parallel", "arbitrary"))(proj, proj, proj, proj, lb_logits, rnn_norm, o_raw, states, dmix, dproj)


def _adamw(w, g, m, v):
    m = ADAM_B1 * m + (1.0 - ADAM_B1) * g
    v = ADAM_B2 * v + (1.0 - ADAM_B2) * (g * g)
    m_hat = m / (1.0 - ADAM_B1 ** ADAM_STEP)
    v_hat = v / (1.0 - ADAM_B2 ** ADAM_STEP)
    delta = -ADAM_LR * (m_hat / (jnp.sqrt(v_hat) + ADAM_EPS) + ADAM_WD * w)
    return delta, m, v


def _adamw_big(w, m, v, parts, tr, name):
    R, C = w.shape
    n_parts = len(parts)

    def body(w_ref, m_ref, v_ref, *rest):
        part_refs = rest[:n_parts]
        g_ref, d_ref, nm_ref, nv_ref = rest[n_parts:]
        g = part_refs[0][...].astype(F32)
        for p_ref in part_refs[1:]:
            g = g + p_ref[...].astype(F32)
        delta, nm, nv = _adamw(w_ref[...], g, m_ref[...], v_ref[...])
        g_ref[...] = g
        d_ref[...] = delta
        nm_ref[...] = nm
        nv_ref[...] = nv

    blk = pl.BlockSpec((tr, C), lambda i: (i, 0))
    part_specs = [pl.BlockSpec((None, tr, C), functools.partial(lambda s, i: (s, i, 0), slot)) for _, slot in parts]
    return pl.pallas_call(
        body, name=name, grid=(R // tr,), in_specs=[blk, blk, blk] + part_specs,
        out_specs=[blk] * 4, out_shape=[jax.ShapeDtypeStruct((R, C), F32)] * 4,
        compiler_params=_params("parallel"))(w, m, v, *[a for a, _ in parts])


def _pair_sum(pa, qa):
    n, R, C = qa.shape
    tr = _pick(R, 256)

    def body(p_ref, q_ref, r_ref):
        r_ref[...] = (p_ref[...].astype(F32) + q_ref[...].astype(F32)).astype(BF16)

    return pl.pallas_call(
        body, name="pair_sum", grid=(n, R // tr),
        in_specs=[pl.BlockSpec((None, tr, C), lambda j, i: (2 * j, i, 0)), pl.BlockSpec((None, tr, C), lambda j, i: (j, i, 0))],
        out_specs=pl.BlockSpec((None, tr, C), lambda j, i: (j, i, 0)),
        out_shape=jax.ShapeDtypeStruct((n, R, C), BF16), compiler_params=_params("parallel", "parallel"))(pa, qa)


def _adamw_small(total, sinks, lb_logits, rnn_norm, pre_norm, post_norm, moments, D):
    params = [sinks, lb_logits, rnn_norm, pre_norm, post_norm]

    def body(tot_ref, *refs):
        p_refs, m_refs, v_refs = refs[0:5], refs[5:10], refs[10:15]
        loss_ref, outs = refs[15], refs[16:]
        tot = tot_ref[...]
        l0, l1 = p_refs[1][0:1, :], p_refs[1][1:2, :]
        mx = jnp.maximum(l0, l1)
        e0, e1 = jnp.exp(l0 - mx), jnp.exp(l1 - mx)
        p0, p1 = e0 / (e0 + e1), e1 / (e0 + e1)
        dlb = tot[0:1, :]
        grads = [tot[5:6, 0:LANES], jnp.concatenate([dlb * p0 * (1.0 - p0), -dlb * p0 * p1], axis=0),
                 tot[1:2, :], tot[2:3, :], tot[3:4, :]]
        loss_ref[...] = 0.5 / D * jnp.sum(tot[4:5, :], axis=-1, keepdims=True)
        for i, g in enumerate(grads):
            delta, nm, nv = _adamw(p_refs[i][...], g, m_refs[i][...], v_refs[i][...])
            outs[4 * i][...] = g
            outs[4 * i + 1][...] = delta
            outs[4 * i + 2][...] = nm
            outs[4 * i + 3][...] = nv

    out_shape = [jax.ShapeDtypeStruct((1, 1), F32)]
    for p in params:
        out_shape += [jax.ShapeDtypeStruct(p.shape, F32)] * 4
    return pl.pallas_call(body, name="adamw_small", out_shape=out_shape)(total, *params, *moments)


SIBLING = 1


def _me():
    return lax.axis_index("x"), lax.axis_index("y"), lax.axis_index("c")


def _flat(px, py, pc):
    return 4 * px + 2 * py + pc


def _role_peer(role, x, y, c):
    if not isinstance(role, int):
        j = (role - 2) // 2
        px = jnp.where((role >= 2) & (j != 1), 1 - x, x)
        py = jnp.where((role >= 2) & (j != 0), 1 - y, y)
        return (px, py, jnp.where(role % 2 == 1, 1 - c, c))
    if role < 2:
        return (x, y, (1 - c) if role else c)
    j, other = (role - 2) // 2, (role - 2) % 2
    px = (1 - x) if j in (0, 2) else x
    py = (1 - y) if j in (1, 2) else y
    return (px, py, (1 - c) if other else c)


def _wave1(x, y, c):
    return jnp.where(c == 1, 2, 4)


def _wave2(x, y, c):
    return jnp.where(c == 1, 4, 2)


def _role_ids():
    x, y, c = _me()
    w1, w2 = _wave1(x, y, c), _wave2(x, y, c)
    order = [0, 1, w1, w2 + 1, w2, w1 + 1, 6, 7]
    return jnp.stack([_flat(*_role_peer(r, x, y, c)) for r in range(N_DEV)] + order).astype(jnp.int32)


def _comm_call(name, bufs, waits=(), starts=(), after=()):
    n_buf, n_wait, n_start, n_after = len(bufs), len(waits), len(starts), len(after)

    def body(*refs):
        buf_refs = refs[:n_buf]
        sem_refs = refs[n_buf:n_buf + 2 * n_wait]
        outs = refs[n_buf + 2 * n_wait + n_after:]
        new_sems, token = outs[:2 * n_start], outs[-1]
        x, y, c = _me()

        def val(v):
            return v(x, y, c) if callable(v) else v

        def block(ref, slot, rows):
            if rows is None:
                return ref.at[val(slot)]
            first = rows[0] if isinstance(rows[0], int) else pl.multiple_of(val(rows[0]), 16)
            return ref.at[val(slot), pl.ds(first, rows[1])]

        for w, (_, kind, like, *rows) in enumerate(waits):
            shape_ref = block(buf_refs[like], 0, (0, rows[0][1]) if rows else None)
            cp = pltpu.make_async_remote_copy(
                src_ref=shape_ref, dst_ref=shape_ref, send_sem=sem_refs[2 * w], recv_sem=sem_refs[2 * w + 1],
                device_id=(x, y, c), device_id_type=MESH)
            if kind == "send":
                cp.wait_send()
            else:
                cp.wait_recv()
        for s, (sb, ss, db, ds, role, *rows) in enumerate(starts):
            rows = rows[0] if rows else None
            pltpu.make_async_remote_copy(
                src_ref=block(buf_refs[sb], ss, rows), dst_ref=block(buf_refs[db], ds, rows), send_sem=new_sems[2 * s],
                recv_sem=new_sems[2 * s + 1], device_id=_role_peer(val(role), x, y, c), device_id_type=MESH).start()
        token[...] = jnp.zeros_like(token)

    sems = [s for flight, *_ in waits for s in flight]
    out = pl.pallas_call(
        body, name=name,
        out_shape=tuple([pltpu.SemaphoreType.DMA(())] * (2 * n_start) + [pltpu.HBM(b.shape, b.dtype) for b in bufs]
                        + [jax.ShapeDtypeStruct((8, LANES), F32)]),
        in_specs=tuple([HBM_SPEC] * n_buf + [SEM_SPEC] * (2 * n_wait) + [ANY_SPEC] * n_after),
        out_specs=tuple([SEM_SPEC] * (2 * n_start) + [HBM_SPEC] * n_buf + [pl.BlockSpec(memory_space=pltpu.VMEM)]),
        input_output_aliases={i: 2 * n_start + i for i in range(n_buf)},
        compiler_params=pltpu.CompilerParams(has_side_effects=EFFECT),
    )(*[pltpu.with_memory_space_constraint(b, pltpu.HBM) for b in bufs], *sems, *after)
    flights = [(out[2 * s], out[2 * s + 1]) for s in range(n_start)]
    return list(out[2 * n_start:2 * n_start + n_buf]), flights, out[-1]


def _landing(shape):
    return lax.empty(shape, BF16)


def _both(flights, like):
    return [(f, kind, like) for f in flights for kind in ("send", "recv")]


def _allreduce_small(rows, D):
    n_rows = len(rows)

    def body(*refs):
        in_refs, out_ref, all_ref = refs[:n_rows], refs[n_rows], refs[n_rows + 1]
        send_sems, recv_sems = refs[n_rows + 2], refs[n_rows + 3]
        x, y, c = _me()
        mine = all_ref.at[_flat(x, y, c)]
        mine[...] = jnp.zeros((8, D), F32)
        for i, r in enumerate(in_refs):
            mine[i:i + 1, :] = r[...]
        copies = []
        for k in range(N_DEV - 1):
            copies.append(pltpu.make_async_remote_copy(
                src_ref=mine, dst_ref=mine, send_sem=send_sems.at[k], recv_sem=recv_sems.at[k],
                device_id=_role_peer(k + 1, x, y, c), device_id_type=MESH))
        for cp in copies:
            cp.start()
        for cp in copies:
            cp.wait_recv()
        for cp in copies:
            cp.wait_send()
        tot = all_ref[0]
        for d in range(1, N_DEV):
            tot = tot + all_ref[d]
        out_ref[...] = tot

    vm = pl.BlockSpec(memory_space=pltpu.VMEM)
    return pl.pallas_call(
        body, name="allreduce_small", in_specs=[vm] * n_rows, out_specs=vm,
        out_shape=jax.ShapeDtypeStruct((8, D), F32),
        scratch_shapes=[pltpu.VMEM((N_DEV, 8, D), F32), pltpu.SemaphoreType.DMA((7,)), pltpu.SemaphoreType.DMA((7,))],
    )(*rows)


def _step(ids, x, target, g_in, g_out, sinks, lb_logits, rnn_norm, pre_norm, post_norm, dist):
    T, D = x.shape
    n_roles, _, wd = g_in.shape
    ro = g_out.shape[1]
    assert n_roles % 2 == 0 and (not dist or n_roles == N_DEV)

    half = D // 2
    up = lambda f: (lambda x, y, c: f(x, y, c) + 1)
    rows_of = lambda f: (lambda x, y, c: jnp.where(f(x, y, c) == 2, 0, half))
    if dist:
        (g_in,), fl_in, _ = _comm_call(
            "gather_in_start", [g_in], starts=[(0, 0, 0, SIBLING, SIBLING), (0, 0, 0, _wave1, _wave1)])
    h, h_t = _prenorm(x, pre_norm)
    proj = _proj_in(ids, h, g_in, 0, 1, None, "proj_in_0")
    if dist:
        (g_in,), _, _ = _comm_call("gather_in_sibling", [g_in], waits=[(fl_in[0], "recv", 0)], after=(proj,))
    proj = _proj_in(ids, h, g_in, 1, 1, proj, "proj_in_1")
    if not dist:
        for p in range(2, n_roles, 2):
            proj = _proj_in(ids, h, g_in, p, 2, proj, f"proj_in_{p}")
    else:
        (g_in,), fl_a, _ = _comm_call(
            "gather_in_wave_1", [g_in], waits=[(fl_in[1], "recv", 0)],
            starts=[(0, 0, 0, _wave2, _wave2), (0, _wave1, 0, up(_wave1), SIBLING),
                    (0, _wave1, 0, 6, _wave2, (rows_of(_wave1), half))], after=(proj,))
        proj = _proj_in(ids, h, g_in, 2, 1, proj, "proj_in_2")
        (g_in,), _, _ = _comm_call("gather_in_passed_1", [g_in], waits=[(fl_a[1], "recv", 0)], after=(proj,))
        proj = _proj_in(ids, h, g_in, 3, 1, proj, "proj_in_3")
        (g_in,), fl_b, _ = _comm_call(
            "gather_in_wave_2", [g_in], waits=[(fl_a[0], "recv", 0)],
            starts=[(0, _wave2, 0, up(_wave2), SIBLING), (0, _wave2, 0, 6, _wave1, (rows_of(_wave2), half))],
            after=(proj,))
        proj = _proj_in(ids, h, g_in, 4, 1, proj, "proj_in_4")
        (g_in,), _, _ = _comm_call("gather_in_passed_2", [g_in], waits=[(fl_b[0], "recv", 0)], after=(proj,))
        proj = _proj_in(ids, h, g_in, 5, 1, proj, "proj_in_5")
        (g_in,), fl_c, _ = _comm_call(
            "gather_in_diagonal", [g_in], waits=[(fl_a[2], "recv", 0, (0, half)), (fl_b[1], "recv", 0, (0, half))],
            starts=[(0, 6, 0, 7, SIBLING)], after=(proj,))
        proj = _proj_in(ids, h, g_in, 6, 1, proj, "proj_in_6")
        (g_in, g_out), fl_out, token = _comm_call(
            "gather_out_start", [g_in, g_out], waits=[(fl_c[0], "recv", 0)],
            starts=[(1, 0, 1, SIBLING, SIBLING)] + [(1, 0, 1, 2 + 2 * i, 2 + 2 * i) for i in range(3)], after=(proj,))
        proj = _proj_in(ids, h, g_in, 7, 1, proj, "proj_in_7", (token,))
        send_waits = [(f, "send", 0) for f in fl_in + fl_a[:2] + fl_b[:1] + fl_c] \
            + [(f, "send", 0, (0, half)) for f in (fl_a[2], fl_b[1])]
    mixed, mixed_t = _attn_fwd(proj, sinks, D)
    if dist:
        (g_in, g_out), fl_out_fwd, _ = _comm_call(
            "gather_out_pass", [g_in, g_out],
            waits=send_waits + [(fl_out[1 + i], "recv", 1) for i in range(3)],
            starts=[(1, 2 + 2 * i, 1, 3 + 2 * i, SIBLING) for i in range(3)], after=(mixed,))
    mixed, mixed_t, o_raw, states = _rnn_fwd(proj, lb_logits, rnn_norm, mixed, mixed_t, D)
    if dist:
        (g_out,), _, _ = _comm_call(
            "gather_out_done", [g_out],
            waits=[(fl_out[0], "recv", 0)] + [(f, "recv", 0) for f in fl_out_fwd]
            + [(f, "send", 0) for f in fl_out + fl_out_fwd], after=(states,))
    y = _proj_out(ids, mixed, g_out)
    dy, dout, g_post, sq_err = _loss_head(y, x, target, post_norm)

    dmix = _dmixed(ids, dy, g_out)
    p_out, after = _dw_out(ids, mixed_t, dy, n_roles), ()
    if dist:
        (p_out, l_out), fl_so, token = _comm_call(
            "scatter_out_start", [p_out, _landing((n_roles - 1, ro, D))],
            starts=[(0, r, 1, r - 1, r) for r in range(1, n_roles)])
        after = (token,)
    dproj, d_sink = _attn_bwd(proj, sinks, dmix, D, after)
    dproj, d_lb, g_rnn = _rnn_bwd(proj, lb_logits, rnn_norm, o_raw, states, dmix, dproj, D)

    p_far = _dw_in(ids, h_t, dproj, n_roles, 2, n_roles - 2, "dw_in_far")
    if dist:
        (p_far, q_far), fl_pair, token = _comm_call(
            "scatter_in_pair_start", [p_far, _landing((3, D, wd))],
            starts=[(0, 1 + 2 * j, 1, j, SIBLING) for j in range(3)])
        p_near = _dw_in(ids, h_t, dproj, n_roles, 0, 2, "dw_in_near", after=(token,))
        (p_far, q_far), _, _ = _comm_call("scatter_in_pair_wait", [p_far, q_far], waits=_both(fl_pair, 1), after=(p_near,))
        chip_sum = _pair_sum(p_far, q_far)
        (chip_sum, z_far), fl_chip, token = _comm_call(
            "scatter_in_chip_start", [chip_sum, _landing((3, D, wd))],
            starts=[(0, j, 1, j, 2 + 2 * j) for j in range(3)])
        (p_near, q_near), fl_sib, token = _comm_call(
            "scatter_in_sibling_start", [p_near, _landing((1, D, wd))], starts=[(0, 1, 1, 0, SIBLING)], after=(token,))
        after = (token,)
    else:
        p_near, after = _dw_in(ids, h_t, dproj, n_roles, 0, 2, "dw_in_near"), ()
    dh = _dh(ids, dproj, g_in, after)
    grad_x, g_pre = _prenorm_bwd(x, dh, dout, pre_norm)
    n_q = D // ATTN_HEAD
    sink_row = jnp.pad(d_sink[:, 0, :2 * GQA].reshape(1, n_q), ((0, 0), (0, D - n_q)))
    rows = [d_lb, g_rnn, g_pre, g_post, sq_err, sink_row]
    if not dist:
        return grad_x, [(p_near, r) for r in range(2)] + [(p_far, r) for r in range(n_roles - 2)], \
            [(p_out, r) for r in range(n_roles)], rows

    (p_out, l_out), _, _ = _comm_call("scatter_out_wait", [p_out, l_out], waits=_both(fl_so, 1), after=(grad_x,))
    (chip_sum, z_far, p_near, q_near), _, _ = _comm_call(
        "scatter_in_wait", [chip_sum, z_far, p_near, q_near], waits=_both(fl_chip, 1) + _both(fl_sib, 3), after=(p_out,))
    parts_in = [(p_near, 0), (q_near, 0)] + [(z_far, j) for j in range(3)]
    parts_out = [(p_out, 0)] + [(l_out, k) for k in range(n_roles - 1)]
    return grad_x, parts_in, parts_out, rows


def kernel(x, w_in, attn_sinks, lb_logits, rnn_norm, w_out, pre_norm, post_norm, loss_target, m_w_in, m_attn_sinks, m_lb_logits, m_rnn_norm, m_w_out, m_pre_norm, m_post_norm, v_w_in, v_attn_sinks, v_lb_logits, v_rnn_norm, v_w_out, v_pre_norm, v_post_norm):
    _, T, D = x.shape
    ro = w_out.shape[1]
    n_q = attn_sinks.shape[1]
    assert lb_logits.shape[0] == 2 and n_q == D // ATTN_HEAD and n_q <= LANES

    grad_x, parts_in, parts_out, small_rows = _step(
        _role_ids(), x[0], loss_target[0], _cast_slot0(w_in[0], N_DEV, "cast_w_in"),
        _cast_slot0(w_out[0], N_DEV, "cast_w_out"), attn_sinks, lb_logits, rnn_norm, pre_norm, post_norm, True)

    g_wo, d_wo, nm_wo, nv_wo = _adamw_big(w_out[0], m_w_out[0], v_w_out[0], parts_out, _pick(ro, 64), "adamw_w_out")
    g_wi, d_wi, nm_wi, nv_wi = _adamw_big(w_in[0], m_w_in[0], v_w_in[0], parts_in, _pick(D, 128), "adamw_w_in")

    total = _allreduce_small(small_rows, D)
    pad = lambda a: jnp.pad(a, ((0, 0), (0, LANES - n_q)))
    moments = [pad(m_attn_sinks), m_lb_logits, m_rnn_norm, m_pre_norm, m_post_norm,
               pad(v_attn_sinks), v_lb_logits, v_rnn_norm, v_pre_norm, v_post_norm]
    res = _adamw_small(total, pad(attn_sinks), lb_logits, rnn_norm, pre_norm, post_norm, moments, D)
    loss = res[0][0, 0]
    small = [[res[1 + 4 * i + j] for i in range(5)] for j in range(4)]
    for j in range(4):
        small[j][0] = small[j][0][:, :n_q]

    def assemble(j, wi, wo):
        s = small[j]
        return [wi[None], s[0], s[1], s[2], wo[None], s[3], s[4]]

    return (loss, grad_x[None], *assemble(0, g_wi, g_wo), *assemble(1, d_wi, d_wo),
            *assemble(2, nm_wi, nm_wo), *assemble(3, nv_wi, nv_wo))
```

```python
import functools

import jax
import jax.numpy as jnp
from jax import lax
from jax.experimental import pallas as pl
from jax.experimental.pallas import tpu as pltpu

F32 = jnp.float32
BF16 = jnp.bfloat16

ATTN_HEAD = 64
GQA = 8
WINDOW = 128
RNN_HEAD = 128
CHUNK = 64
NORM_EPS = 1e-6
LANES = 128
N_DEV = 8

ADAM_LR = 0.001
ADAM_B1 = 0.9
ADAM_B2 = 0.999
ADAM_EPS = 1e-08
ADAM_WD = 0.01
ADAM_STEP = 10

VMEM_LIMIT = 56 * 1024 * 1024
MESH = pl.DeviceIdType.MESH
ANY_SPEC = pl.BlockSpec(memory_space=pl.ANY)
HBM_SPEC = pl.BlockSpec(memory_space=pltpu.HBM)
SEM_SPEC = pl.BlockSpec(memory_space=pltpu.SEMAPHORE)
EFFECT = pltpu.SideEffectType.DATAFLOW_SIDE_EFFECTING

NT_DIMS = (((1,), (1,)), ((), ()))
NN_DIMS = (((1,), (0,)), ((), ()))
TN_DIMS = (((0,), (0,)), ((), ()))


def _params(*sem):
    return pltpu.CompilerParams(dimension_semantics=sem, vmem_limit_bytes=VMEM_LIMIT)


def _dot(a, b, dims):
    return lax.dot_general(a, b, dims, preferred_element_type=F32)


def _sigmoid(v):
    return 1.0 / (1.0 + jnp.exp(-v))


def _pick(n, pref):
    t = min(n, pref)
    assert n % t == 0, (n, pref)
    return t


MXU_COLS = 256
WIDE_TILE = 5 * MXU_COLS


def _col_tiles(wd):
    wide = wd // WIDE_TILE
    rest = wd - wide * WIDE_TILE
    assert rest % LANES == 0 and (rest == 0 or (wide * WIDE_TILE) % rest == 0), wd
    return [(WIDE_TILE, 0, wide)] * (wide > 0) + [(rest, wide * WIDE_TILE, 1)] * (rest > 0)


def _matmul(ids, a, b, *, grid, a_spec, b_spec, o_spec, out_shape, trans_b, name, after=(), prev=None):
    nk = grid[2]
    dims = NT_DIMS if trans_b else NN_DIMS
    n_skip = len(after) + (prev is not None)

    def body(ids_ref, a_ref, b_ref, *rest):
        del ids_ref
        o_ref, scratch = rest[n_skip], rest[n_skip + 1:]
        prod = _dot(a_ref[...], b_ref[...], dims)
        if nk == 1:
            o_ref[...] = prod.astype(o_ref.dtype)
        else:
            acc_ref, = scratch
            k = pl.program_id(2)

            @pl.when(k == 0)
            def _():
                acc_ref[...] = prod

            @pl.when(k > 0)
            def _():
                acc_ref[...] += prod

            @pl.when(k == nk - 1)
            def _():
                o_ref[...] = acc_ref[...].astype(o_ref.dtype)

    scratch = [] if nk == 1 else [pltpu.VMEM(tuple(d for d in o_spec.block_shape if d is not None), F32)]
    extra = list(after) + ([prev] if prev is not None else [])
    aliases = {3 + len(after): 0} if prev is not None else {}
    return pl.pallas_call(
        body, name=name, out_shape=out_shape, input_output_aliases=aliases,
        grid_spec=pltpu.PrefetchScalarGridSpec(
            num_scalar_prefetch=1, grid=grid, in_specs=[a_spec, b_spec] + [ANY_SPEC] * len(extra),
            out_specs=o_spec, scratch_shapes=scratch),
        compiler_params=_params("parallel", "parallel", "arbitrary"),
    )(ids, a, b, *extra)


def _proj_in(ids, h, w_roles, pos, n, prev, name, after=()):
    T, D = h.shape
    n_roles, _, wd = w_roles.shape
    tm = _pick(T, 1024)
    for t, (tn, first, count) in enumerate(_col_tiles(wd)):
        prev = _matmul(
            ids, h, w_roles, grid=(T // tm, n * count, 1),
            a_spec=pl.BlockSpec((tm, D), lambda i, j, k, ids: (i, 0)),
            b_spec=pl.BlockSpec((None, D, tn), functools.partial(
                lambda tn, first, count, i, j, k, ids: (ids[n_roles + pos + j // count], 0, first // tn + j % count),
                tn, first, count)),
            o_spec=pl.BlockSpec((pl.Element(tm), pl.Element(tn)), functools.partial(
                lambda tn, first, count, i, j, k, ids: (
                    i * tm, pl.multiple_of(ids[ids[n_roles + pos + j // count]] * wd + first + (j % count) * tn, LANES)),
                tn, first, count)),
            out_shape=jax.ShapeDtypeStruct((T, n_roles * wd), F32), trans_b=False, name=f"{name}_{t}",
            after=after, prev=prev)
    return prev


def _proj_out(ids, mixed, wo_roles):
    T, E = mixed.shape
    n_roles, R, D = wo_roles.shape
    tm, tn = _pick(T, 512), _pick(D, 512)

    def body(ids_ref, a_ref, b_ref, o_ref, a_roles):
        @pl.when(pl.program_id(1) == 0)
        def _():
            for k in range(n_roles):
                col = pl.multiple_of(ids_ref[k] * R, R)
                a_roles[:, k * R:(k + 1) * R] = a_ref[:, pl.ds(col, R)]

        o_ref[...] = _dot(a_roles[...], b_ref[...].reshape(E, tn), NN_DIMS)

    return pl.pallas_call(
        body, name="proj_out", out_shape=jax.ShapeDtypeStruct((T, D), F32),
        grid_spec=pltpu.PrefetchScalarGridSpec(
            num_scalar_prefetch=1, grid=(T // tm, D // tn),
            in_specs=[pl.BlockSpec((tm, E), lambda i, j, ids: (i, 0)),
                      pl.BlockSpec((n_roles, R, tn), lambda i, j, ids: (0, 0, j))],
            out_specs=pl.BlockSpec((tm, tn), lambda i, j, ids: (i, j)),
            scratch_shapes=[pltpu.VMEM((tm, E), BF16)]),
        compiler_params=_params("parallel", "arbitrary"))(ids, mixed, wo_roles)


def _dmixed(ids, dy, wo_roles):
    T, D = dy.shape
    n_roles, R, _ = wo_roles.shape
    tm = _pick(T, 1024)
    return _matmul(
        ids, dy, wo_roles, grid=(T // tm, n_roles, 1),
        a_spec=pl.BlockSpec((tm, D), lambda i, j, k, ids: (i, 0)),
        b_spec=pl.BlockSpec((None, R, D), lambda i, j, k, ids: (j, 0, 0)),
        o_spec=pl.BlockSpec((tm, R), lambda i, j, k, ids: (i, ids[j])),
        out_shape=jax.ShapeDtypeStruct((T, n_roles * R), F32), trans_b=True, name="dmixed")


def _dw_out(ids, mixed_t, dy, n_roles):
    E, T = mixed_t.shape
    D = dy.shape[1]
    R = E // n_roles
    tn = _pick(D, 512)
    return _matmul(
        ids, mixed_t, dy, grid=(n_roles, D // tn, 1),
        a_spec=pl.BlockSpec((R, T), lambda i, j, k, ids: (ids[i], 0)),
        b_spec=pl.BlockSpec((T, tn), lambda i, j, k, ids: (0, j)),
        o_spec=pl.BlockSpec((None, R, tn), lambda i, j, k, ids: (i, 0, j)),
        out_shape=jax.ShapeDtypeStruct((n_roles, R, D), BF16), trans_b=False, name="dw_out")


def _dw_in(ids, h_t, dproj, n_roles, r0, nr, name, after=()):
    D, T = h_t.shape
    wd = dproj.shape[1] // n_roles
    tm, out = _pick(D, 1024), None
    for t, (tn, first, count) in enumerate(_col_tiles(wd)):
        out = _matmul(
            ids, h_t, dproj, grid=(D // tm, nr * count, 1),
            a_spec=pl.BlockSpec((tm, T), lambda i, j, k, ids: (i, 0)),
            b_spec=pl.BlockSpec((pl.Element(T), pl.Element(tn)), functools.partial(
                lambda tn, first, count, i, j, k, ids: (
                    0, pl.multiple_of(ids[r0 + j // count] * wd + first + (j % count) * tn, LANES)),
                tn, first, count)),
            o_spec=pl.BlockSpec((None, tm, tn), functools.partial(
                lambda tn, first, count, i, j, k, ids: (j // count, i, first // tn + j % count), tn, first, count)),
            out_shape=jax.ShapeDtypeStruct((nr, D, wd), BF16), trans_b=False, name=f"{name}_{t}", after=after, prev=out)
    return out


def _dh(ids, dproj, w_roles, after=()):
    T = dproj.shape[0]
    n_roles, D, wd = w_roles.shape
    tm, tn = _pick(T, 1024), _pick(D, 512)
    ni, nj, n_after = T // tm, D // tn, len(after)
    assert n_roles % 2 == 0

    def body(ids_ref, a_hbm, b_hbm, *rest):
        o_ref, a_buf, b_buf, sems = rest[n_after:]
        i, j = pl.program_id(0), pl.program_id(1)

        def copies(ii, jj, k):
            slot = k % 2
            rows = pl.ds(pl.multiple_of(ii * tm, tm), tm)
            cols = pl.ds(pl.multiple_of(ids_ref[k] * wd, LANES), wd)
            return (pltpu.make_async_copy(a_hbm.at[rows, cols], a_buf.at[slot], sems.at[0, slot]),
                    pltpu.make_async_copy(b_hbm.at[k, pl.ds(pl.multiple_of(jj * tn, tn), tn), :], b_buf.at[slot],
                                          sems.at[1, slot]))

        @pl.when((i == 0) & (j == 0))
        def _():
            for cp in copies(i, j, 0):
                cp.start()

        acc = None
        for k in range(n_roles):
            for cp in copies(i, j, k):
                cp.wait()
            if k + 1 < n_roles:
                for cp in copies(i, j, k + 1):
                    cp.start()
            else:
                last = (i == ni - 1) & (j == nj - 1)
                nxt_i = jnp.where(j == nj - 1, i + 1, i)
                nxt_j = jnp.where(j == nj - 1, 0, j + 1)

                @pl.when(jnp.logical_not(last))
                def _():
                    for cp in copies(nxt_i, nxt_j, 0):
                        cp.start()
            prod = _dot(a_buf[k % 2], b_buf[k % 2], NT_DIMS)
            acc = prod if acc is None else acc + prod
        o_ref[...] = acc

    return pl.pallas_call(
        body, name="dh", out_shape=jax.ShapeDtypeStruct((T, D), F32),
        grid_spec=pltpu.PrefetchScalarGridSpec(
            num_scalar_prefetch=1, grid=(ni, nj),
            in_specs=[ANY_SPEC, ANY_SPEC] + [ANY_SPEC] * n_after,
            out_specs=pl.BlockSpec((tm, tn), lambda i, j, ids: (i, j)),
            scratch_shapes=[pltpu.VMEM((2, tm, wd), BF16), pltpu.VMEM((2, tn, wd), BF16),
                            pltpu.SemaphoreType.DMA((2, 2))]),
        compiler_params=_params("arbitrary", "arbitrary"),
    )(ids, dproj, w_roles, *after)


def _cast_slot0(w, n_roles, name):
    R, C = w.shape
    tr = _pick(R, 256)

    def body(w_ref, o_ref):
        o_ref[...] = w_ref[...].astype(BF16)

    return pl.pallas_call(
        body, name=name, grid=(R // tr,), in_specs=[pl.BlockSpec((tr, C), lambda i: (i, 0))],
        out_specs=pl.BlockSpec((None, tr, C), lambda i: (0, i, 0)),
        out_shape=jax.ShapeDtypeStruct((n_roles, R, C), BF16), compiler_params=_params("parallel"))(w)


def _prenorm(x, gain):
    T, D = x.shape
    tm = _pick(T, 256)

    def body(x_ref, g_ref, h_ref, ht_ref):
        xv = x_ref[...]
        r = lax.rsqrt(jnp.mean(xv * xv, axis=-1, keepdims=True) + NORM_EPS)
        h = xv * r * g_ref[...]
        h_ref[...] = h.astype(BF16)
        ht_ref[...] = h.T.astype(BF16)

    return pl.pallas_call(
        body, name="prenorm", grid=(T // tm,),
        in_specs=[pl.BlockSpec((tm, D), lambda i: (i, 0)), pl.BlockSpec((1, D), lambda i: (0, 0))],
        out_specs=[pl.BlockSpec((tm, D), lambda i: (i, 0)), pl.BlockSpec((D, tm), lambda i: (0, i))],
        out_shape=[jax.ShapeDtypeStruct((T, D), BF16), jax.ShapeDtypeStruct((D, T), BF16)],
        compiler_params=_params("parallel"))(x, gain)


def _norm_bwd(u, yn, r):
    return r * (u - yn * jnp.mean(u * yn, axis=-1, keepdims=True))


def _loss_head(y, x, target, gain):
    T, D = y.shape
    tm = _pick(T, 256)

    def body(y_ref, x_ref, t_ref, g_ref, dy_ref, dout_ref, gpost_ref, sq_ref):
        yv = y_ref[...]
        g = g_ref[...]
        r = lax.rsqrt(jnp.mean(yv * yv, axis=-1, keepdims=True) + NORM_EPS)
        yn = yv * r
        err = x_ref[...] + yn * g - t_ref[...]
        dout = err * (1.0 / D)
        dy_ref[...] = _norm_bwd(dout * g, yn, r).astype(BF16)
        dout_ref[...] = dout

        @pl.when(pl.program_id(0) == 0)
        def _():
            gpost_ref[...] = jnp.zeros_like(gpost_ref)
            sq_ref[...] = jnp.zeros_like(sq_ref)

        gpost_ref[...] += jnp.sum(dout * yn, axis=0, keepdims=True)
        sq_ref[...] += jnp.sum(err * err, axis=0, keepdims=True)

    row = pl.BlockSpec((tm, D), lambda i: (i, 0))
    vec = pl.BlockSpec((1, D), lambda i: (0, 0))
    return pl.pallas_call(
        body, name="loss_head", grid=(T // tm,), in_specs=[row, row, row, vec], out_specs=[row, row, vec, vec],
        out_shape=[jax.ShapeDtypeStruct((T, D), BF16), jax.ShapeDtypeStruct((T, D), F32),
                   jax.ShapeDtypeStruct((1, D), F32), jax.ShapeDtypeStruct((1, D), F32)],
        compiler_params=_params("arbitrary"))(y, x, target, gain)


def _prenorm_bwd(x, dh, dout, gain):
    T, D = x.shape
    tm = _pick(T, 256)

    def body(x_ref, dh_ref, dout_ref, g_ref, gx_ref, gpre_ref):
        xv = x_ref[...]
        dhv = dh_ref[...]
        r = lax.rsqrt(jnp.mean(xv * xv, axis=-1, keepdims=True) + NORM_EPS)
        xn = xv * r
        gx_ref[...] = dout_ref[...] + _norm_bwd(dhv * g_ref[...], xn, r)

        @pl.when(pl.program_id(0) == 0)
        def _():
            gpre_ref[...] = jnp.zeros_like(gpre_ref)

        gpre_ref[...] += jnp.sum(dhv * xn, axis=0, keepdims=True)

    row = pl.BlockSpec((tm, D), lambda i: (i, 0))
    vec = pl.BlockSpec((1, D), lambda i: (0, 0))
    return pl.pallas_call(
        body, name="prenorm_bwd", grid=(T // tm,), in_specs=[row, row, row, vec], out_specs=[row, vec],
        out_shape=[jax.ShapeDtypeStruct((T, D), F32), jax.ShapeDtypeStruct((1, D), F32)],
        compiler_params=_params("arbitrary"))(x, dh, dout, gain)


def _attn_masks(n):
    row = lax.broadcasted_iota(jnp.int32, (2 * WINDOW, 2 * WINDOW), 0) % WINDOW
    col = lax.broadcasted_iota(jnp.int32, (2 * WINDOW, 2 * WINDOW), 1)
    valid = (col > row) & (col <= row + WINDOW) & ((n > 0) | (col >= WINDOW))
    low = lax.broadcasted_iota(jnp.int32, (1, LANES), 1) < ATTN_HEAD
    top = lax.broadcasted_iota(jnp.int32, (2 * WINDOW, 1), 0) < WINDOW
    return jnp.where(valid, 0.0, -jnp.inf), low, top


def _dup_half(pair, keep):
    return jnp.where(keep, pair, pltpu.roll(pair, ATTN_HEAD, 1))


def _fold_half(v):
    return v + pltpu.roll(v, ATTN_HEAD, 1)


ATTN_GROUP_FWD = 8
ATTN_GROUP = 4
ATTN_SCALE = ATTN_HEAD ** -0.5


def _attn_scores(qpair, k2, low):
    qs = qpair * ATTN_SCALE
    q2 = jnp.concatenate([jnp.where(low, qs, 0.0), jnp.where(low, 0.0, qs)], axis=0).astype(BF16)
    return q2, _dot(q2, k2, NT_DIMS)


def _attn_softmax(scores, sink_lo, sink_hi, bias, top):
    s = scores + bias
    sink = jnp.where(top, sink_lo, sink_hi)
    m = jnp.maximum(jnp.max(s, axis=-1, keepdims=True), sink)
    p = jnp.exp(s - m)
    psink = jnp.exp(sink - m)
    inv = 1.0 / (jnp.sum(p, axis=-1, keepdims=True) + psink)
    return p * inv, psink * inv


def _attn_fwd(proj, sinks, D):
    T = proj.shape[0]
    nb, njp = T // WINDOW, D // 1024
    assert nb % 2 == 0
    two = 2 * WINDOW

    def body(sink_ref, qlo_ref, qhi_ref, kc_ref, kp_ref, vc_ref, vp_ref, glo_ref, ghi_ref, mix_ref, mixt_ref):
        jp, m = pl.program_id(0), pl.program_id(1)
        q_refs, g_refs = (qlo_ref, qhi_ref), (glo_ref, ghi_ref)
        biases, k2s, v2s = [], {}, {}
        for blk in (0, 1):
            bias, low, top = _attn_masks(2 * m + blk)
            biases.append(bias)
            keys = kc_ref[0:two, :] if blk else jnp.concatenate([kp_ref[...], kc_ref[0:WINDOW, :]], axis=0)
            vals = vc_ref[0:two, :] if blk else jnp.concatenate([vp_ref[...], vc_ref[0:WINDOW, :]], axis=0)
            for hj, keep in enumerate((low, jnp.logical_not(low))):
                k2s[blk, hj] = _dup_half(keys, keep).astype(BF16)
                v2s[blk, hj] = _dup_half(vals, keep).astype(BF16)
        all_units = [(blk, hj, p) for blk in (0, 1) for hj in (0, 1) for p in range(4)]
        for first in range(0, len(all_units), ATTN_GROUP_FWD):
            units = all_units[first:first + ATTN_GROUP_FWD]
            rows = {u: slice(WINDOW * u[0], WINDOW * (u[0] + 1)) for u in units}
            cols = {u: slice(LANES * u[2], LANES * (u[2] + 1)) for u in units}
            scores = {u: _attn_scores(q_refs[u[1]][rows[u], cols[u]], k2s[u[0], u[1]], low)[1] for u in units}
            probs = {}
            for u in units:
                head = (2 * jp + u[1]) * GQA + 2 * u[2]
                probs[u] = _attn_softmax(scores[u], sink_ref[0, head], sink_ref[0, head + 1], biases[u[0]], top)[0]
            o2s = {u: _dot(probs[u].astype(BF16), v2s[u[0], u[1]], NN_DIMS) for u in units}
            for u in units:
                opair = jnp.where(low, o2s[u][:WINDOW], o2s[u][WINDOW:])
                g = g_refs[u[1]][rows[u], cols[u]]
                out = opair * (g * _sigmoid(g))
                oc = slice(512 * u[1] + LANES * u[2], 512 * u[1] + LANES * (u[2] + 1))
                mix_ref[rows[u], oc] = out.astype(BF16)
                mixt_ref[oc, rows[u]] = out.T.astype(BF16)

    kb = D // LANES
    vb = kb + D // (8 * LANES)
    gb = (D + D // 4) // 512
    wide = lambda off: [pl.BlockSpec((two, 512), functools.partial(lambda o, e, jp, m: (m, o + 2 * jp + e), off, e))
                        for e in (0, 1)]
    cur = lambda off: pl.BlockSpec((two, LANES), functools.partial(lambda o, jp, m: (m, o + jp), off))
    prev = lambda off: pl.BlockSpec((WINDOW, LANES),
                                    functools.partial(lambda o, jp, m: (jnp.maximum(2 * m - 1, 0), o + jp), off))
    return pl.pallas_call(
        body, name="attn_fwd", grid=(njp, nb // 2),
        in_specs=[pl.BlockSpec(memory_space=pltpu.SMEM)] + wide(0) + [cur(kb), prev(kb), cur(vb), prev(vb)] + wide(gb),
        out_specs=[pl.BlockSpec((two, 1024), lambda jp, m: (m, jp)),
                   pl.BlockSpec((1024, two), lambda jp, m: (jp, m))],
        out_shape=[jax.ShapeDtypeStruct((T, 2 * D), BF16), jax.ShapeDtypeStruct((2 * D, T), BF16)],
        compiler_params=_params("parallel", "parallel"))(sinks, *([proj] * 8))


def _flush_windows(bufs, sems, hbm_ref, corners, slot, step, last):
    def copies(sl):
        return [pltpu.make_async_copy(
            b.at[sl], hbm_ref.at[pl.ds(r0, b.shape[1]), pl.ds(c0, b.shape[2])], sems.at[sl, i])
            for i, (b, (r0, c0)) in enumerate(zip(bufs, corners))]

    for cp in copies(slot):
        cp.start()

    @pl.when(step > 0)
    def _():
        for cp in copies(1 - slot):
            cp.wait()

    @pl.when(step == last)
    def _():
        for cp in copies(slot):
            cp.wait()


def _attn_bwd(proj, sinks, dmix, D, after=()):
    T = proj.shape[0]
    nb, njp = T // WINDOW, D // 1024
    assert nb % 2 == 0
    two, ns = 2 * WINDOW, nb // 2
    n_after = len(after)

    def body(sink_ref, qlo_ref, qhi_ref, kc_ref, kp_ref, vc_ref, vp_ref, glo_ref, ghi_ref, dmix_ref, *rest):
        dproj_ref, dsink_ref, kcarry_ref, vcarry_ref, dq_buf, dk_buf, dv_buf, dg_buf, out_sems = rest[n_after:]
        jp, step = pl.program_id(0), pl.program_id(1)
        m = ns - 1 - step
        slot = step % 2
        dq_ref, dk_ref, dv_ref, dg_ref = dq_buf.at[slot], dk_buf.at[slot], dv_buf.at[slot], dg_buf.at[slot]
        lane = lax.broadcasted_iota(jnp.int32, (1, LANES), 1)

        @pl.when(step == 0)
        def _():
            kcarry_ref[...] = jnp.zeros_like(kcarry_ref)
            vcarry_ref[...] = jnp.zeros_like(vcarry_ref)
            dsink_ref[...] = jnp.zeros_like(dsink_ref)

        q_refs, g_refs = (qlo_ref, qhi_ref), (glo_ref, ghi_ref)
        biases, k2s, v2s, dk_heads, dv_heads = [], {}, {}, {}, {}
        for blk in (0, 1):
            bias, low, top = _attn_masks(2 * m + blk)
            biases.append(bias)
            keys = kc_ref[0:two, :] if blk else jnp.concatenate([kp_ref[...], kc_ref[0:WINDOW, :]], axis=0)
            vals = vc_ref[0:two, :] if blk else jnp.concatenate([vp_ref[...], vc_ref[0:WINDOW, :]], axis=0)
            for hj, keep in enumerate((low, jnp.logical_not(low))):
                k2s[blk, hj] = _dup_half(keys, keep).astype(BF16)
                v2s[blk, hj] = _dup_half(vals, keep).astype(BF16)
                dk_heads[blk, hj] = jnp.zeros((two, LANES), F32)
                dv_heads[blk, hj] = jnp.zeros((two, LANES), F32)
        keeps = (low, jnp.logical_not(low))
        dsink = jnp.zeros((1, LANES), F32)
        all_units = [(blk, hj, p) for blk in (1, 0) for hj in (0, 1) for p in range(4)]
        for first in range(0, len(all_units), ATTN_GROUP):
            units = all_units[first:first + ATTN_GROUP]
            rows = {u: slice(WINDOW * u[0], WINDOW * (u[0] + 1)) for u in units}
            cols = {u: slice(LANES * u[2], LANES * (u[2] + 1)) for u in units}
            ocs = {u: slice(512 * u[1] + LANES * u[2], 512 * u[1] + LANES * (u[2] + 1)) for u in units}
            qsc = {u: _attn_scores(q_refs[u[1]][rows[u], cols[u]], k2s[u[0], u[1]], low) for u in units}
            probs, psinks, do2s = {}, {}, {}
            for u in units:
                head = (2 * jp + u[1]) * GQA + 2 * u[2]
                probs[u], psinks[u] = _attn_softmax(qsc[u][1], sink_ref[0, head], sink_ref[0, head + 1], biases[u[0]], top)
                g = g_refs[u[1]][rows[u], cols[u]]
                do = dmix_ref[rows[u], ocs[u]] * (g * _sigmoid(g))
                do2s[u] = jnp.concatenate([jnp.where(low, do, 0.0), jnp.where(low, 0.0, do)], axis=0).astype(BF16)
            pbs = {u: probs[u].astype(BF16) for u in units}
            o2s = {u: _dot(pbs[u], v2s[u[0], u[1]], NN_DIMS) for u in units}
            dps = {u: _dot(do2s[u], v2s[u[0], u[1]], NT_DIMS) for u in units}
            dss = {}
            for u in units:
                opair = jnp.where(low, o2s[u][:WINDOW], o2s[u][WINDOW:])
                g = g_refs[u[1]][rows[u], cols[u]]
                sg = _sigmoid(g)
                dg_ref[rows[u], ocs[u]] = (dmix_ref[rows[u], ocs[u]] * opair * (sg * (1.0 + g * (1.0 - sg)))).astype(BF16)
                delta = jnp.sum(probs[u] * dps[u], axis=-1, keepdims=True)
                dss[u] = (probs[u] * (dps[u] - delta)).astype(BF16)
                ps = psinks[u] * delta
                local = u[1] * GQA + 2 * u[2]
                dsink -= jnp.where(lane == local, jnp.sum(ps[:WINDOW], axis=0, keepdims=True), 0.0)
                dsink -= jnp.where(lane == local + 1, jnp.sum(ps[WINDOW:], axis=0, keepdims=True), 0.0)
            dq2s = {u: _dot(dss[u], k2s[u[0], u[1]], NN_DIMS) for u in units}
            for u in units:
                dk_heads[u[0], u[1]] += _dot(dss[u], qsc[u][0], TN_DIMS)
                dv_heads[u[0], u[1]] += _dot(pbs[u], do2s[u], TN_DIMS)
            for u in units:
                dq_ref[rows[u], ocs[u]] = (jnp.where(low, dq2s[u][:WINDOW], dq2s[u][WINDOW:]) * ATTN_SCALE).astype(BF16)
        dk_pair = [sum(jnp.where(keep, _fold_half(dk_heads[blk, hj]), 0.0) for hj, keep in enumerate(keeps)) for blk in (0, 1)]
        dv_pair = [sum(jnp.where(keep, _fold_half(dv_heads[blk, hj]), 0.0) for hj, keep in enumerate(keeps)) for blk in (0, 1)]
        dk_ref[WINDOW:, :] = (dk_pair[1][WINDOW:] + kcarry_ref[...]).astype(BF16)
        dv_ref[WINDOW:, :] = (dv_pair[1][WINDOW:] + vcarry_ref[...]).astype(BF16)
        dk_ref[:WINDOW, :] = (dk_pair[0][WINDOW:] + dk_pair[1][:WINDOW]).astype(BF16)
        dv_ref[:WINDOW, :] = (dv_pair[0][WINDOW:] + dv_pair[1][:WINDOW]).astype(BF16)
        kcarry_ref[...] = dk_pair[0][:WINDOW]
        vcarry_ref[...] = dv_pair[0][:WINDOW]
        dsink_ref[...] += dsink
        row = pl.multiple_of(m * two, two)
        col = lambda base, width: pl.multiple_of(base + jp * width, LANES)
        corners = [(row, col(0, 1024)), (row, col(D, LANES)), (row, col(D + D // 8, LANES)), (row, col(D + D // 4, 1024))]
        _flush_windows([dq_buf, dk_buf, dv_buf, dg_buf], out_sems, dproj_ref, corners, slot, step, ns - 1)

    kb = D // LANES
    vb = kb + D // (8 * LANES)
    gb = (D + D // 4) // 512
    wide = lambda off: [pl.BlockSpec((two, 512), functools.partial(lambda o, e, jp, s: (ns - 1 - s, o + 2 * jp + e), off, e))
                        for e in (0, 1)]
    cur = lambda off: pl.BlockSpec((two, LANES), functools.partial(lambda o, jp, s: (ns - 1 - s, o + jp), off))
    prev = lambda off: pl.BlockSpec((WINDOW, LANES), functools.partial(
        lambda o, jp, s: (jnp.maximum(2 * (ns - 1 - s) - 1, 0), o + jp), off))
    wide_in = pl.BlockSpec((two, 1024), lambda jp, s: (ns - 1 - s, jp))
    return pl.pallas_call(
        body, name="attn_bwd", grid=(njp, ns),
        in_specs=[pl.BlockSpec(memory_space=pltpu.SMEM)] + wide(0) + [cur(kb), prev(kb), cur(vb), prev(vb)] + wide(gb)
        + [wide_in] + [ANY_SPEC] * n_after,
        out_specs=[ANY_SPEC, pl.BlockSpec((None, 1, LANES), lambda jp, s: (jp, 0, 0))],
        out_shape=[jax.ShapeDtypeStruct((T, 6 * D + D // 4), BF16), jax.ShapeDtypeStruct((njp, 1, LANES), F32)],
        scratch_shapes=[pltpu.VMEM((WINDOW, LANES), F32), pltpu.VMEM((WINDOW, LANES), F32),
                        pltpu.VMEM((2, two, 1024), BF16), pltpu.VMEM((2, two, LANES), BF16),
                        pltpu.VMEM((2, two, LANES), BF16), pltpu.VMEM((2, two, 1024), BF16),
                        pltpu.SemaphoreType.DMA((2, 4))],
        compiler_params=_params("parallel", "arbitrary"))(sinks, *([proj] * 8), dmix, *after)


RNN_TB = 512
RNN_HB = 8
RNN_WIDE = RNN_HB * RNN_HEAD


def _split3(v):
    a = v.astype(BF16)
    r = v - a.astype(F32)
    b = r.astype(BF16)
    c = (r - b.astype(F32)).astype(BF16)
    return a, b, c


def _tri_sum(tri, v):
    a, b, c = _split3(v)
    return _dot(tri, a, NN_DIMS) + _dot(tri, b, NN_DIMS) + _dot(tri, c, NN_DIMS)


def _lower_bound(lb_ref):
    l0, l1 = lb_ref[0:1, :], lb_ref[1:2, :]
    m = jnp.maximum(l0, l1)
    e0, e1 = jnp.exp(l0 - m), jnp.exp(l1 - m)
    return e0 / (e0 + e1)


def _rnn_gates(rq, rf, lb):
    sq = _sigmoid(rq)
    sf = _sigmoid(rf)
    f = lb + (1.0 - lb) * sf
    return sq, sf, f


def _rnn_factors(G):
    last = G[CHUNK - 1:CHUNK, :]
    mid = G[CHUNK // 2 - 1:CHUNK // 2, :]
    return G, jnp.exp(G), jnp.exp(G - mid), jnp.exp(mid - G), jnp.exp(last - G), jnp.exp(last)


def _chunk_masks():
    r = lax.broadcasted_iota(jnp.int32, (CHUNK, CHUNK), 0)
    c = lax.broadcasted_iota(jnp.int32, (CHUNK, CHUNK), 1)
    return r >= c, (r >= c).astype(BF16), (r <= c).astype(BF16)


def _rnn_specs(T, D, tb, rev):
    nt = T // tb
    base = (2 * D + D // 4) // LANES
    t_of = (lambda s: nt - 1 - s) if rev else (lambda s: s)
    assert base % RNN_HB == 0 and (D // LANES) % RNN_HB == 0
    cols = [pl.BlockSpec((tb, RNN_WIDE), functools.partial(lambda o, h, s: (t_of(s), o + h),
                                                            (base + i * (D // LANES)) // RNN_HB))
            for i in range(4)]
    return cols, t_of


def _rnn_fwd(proj, lb_logits, rnn_norm, mixed, mixed_t, D):
    T = proj.shape[0]
    tb = _pick(T, RNN_TB)
    nt, nh, cpb = T // tb, D // RNN_HEAD, tb // CHUNK
    cols, _ = _rnn_specs(T, D, tb, False)

    def body(rq_ref, rf_ref, ri_ref, rg_ref, lb_ref, gain_ref, mix_in, mixt_in,
             mix_ref, mixt_ref, o_ref, st_ref, state_ref):
        del mix_in, mixt_in
        causal, tri, _ = _chunk_masks()
        lb = _lower_bound(lb_ref)

        @pl.when(pl.program_id(1) == 0)
        def _():
            state_ref[...] = jnp.zeros_like(state_ref)

        def chunk(c, carry):
            rows = pl.ds(pl.multiple_of(c * CHUNK, CHUNK), CHUNK)
            heads = range(RNN_HB)
            lns = [slice(RNN_HEAD * hh, RNN_HEAD * (hh + 1)) for hh in heads]
            qs, ks, Gs = [], [], []
            for ln in lns:
                rq, rf = rq_ref[rows, ln], rf_ref[rows, ln]
                sq, _, f = _rnn_gates(rq, rf, lb[:, ln])
                qs.append(rq * sq)
                ks.append(1.0 - f)
                Gs.append(_tri_sum(tri, jnp.log(f)))
            atts, inters, vbs = [], [], []
            for hh, ln in enumerate(lns):
                _, eG, eq, ek, ekl, elast = _rnn_factors(Gs[hh])
                q, k = qs[hh], ks[hh]
                st = state_ref[hh]
                st_ref[hh, c] = st
                vb = ri_ref[rows, ln].astype(BF16)
                vbs.append(vb)
                atts.append(_dot((q * eq).astype(BF16), (k * ek).astype(BF16), NT_DIMS))
                inters.append(_dot((q * eG).astype(BF16), st.astype(BF16), NT_DIMS))
                state_ref[hh] = st * elast + _dot(vb, (k * ekl).astype(BF16), TN_DIMS)
            intras = [_dot(jnp.where(causal, atts[hh], 0.0).astype(BF16), vbs[hh], NN_DIMS) for hh in heads]
            for hh, ln in enumerate(lns):
                o = inters[hh] + intras[hh]
                o_ref[rows, ln] = o
                rg = rg_ref[rows, ln]
                r = lax.rsqrt(jnp.mean(o * o, axis=-1, keepdims=True) + NORM_EPS)
                out = (o * r * gain_ref[:, ln]) * (rg * _sigmoid(rg))
                mix_ref[rows, ln] = out.astype(BF16)
            return carry

        lax.fori_loop(0, cpb, chunk, 0, unroll=True)
        mixt_ref[...] = mix_ref[...].astype(F32).T.astype(BF16)

    nh //= RNN_HB
    vec2 = pl.BlockSpec((2, RNN_WIDE), lambda h, s: (0, h))
    vec1 = pl.BlockSpec((1, RNN_WIDE), lambda h, s: (0, h))
    return pl.pallas_call(
        body, name="rnn_fwd", grid=(nh, nt),
        in_specs=cols + [vec2, vec1, ANY_SPEC, ANY_SPEC],
        out_specs=[pl.BlockSpec((tb, RNN_WIDE), lambda h, s: (s, D // RNN_WIDE + h)),
                   pl.BlockSpec((RNN_WIDE, tb), lambda h, s: (D // RNN_WIDE + h, s)),
                   pl.BlockSpec((tb, RNN_WIDE), lambda h, s: (s, h)),
                   pl.BlockSpec((RNN_HB, cpb, RNN_HEAD, RNN_HEAD), lambda h, s: (h, s, 0, 0))],
        out_shape=[jax.ShapeDtypeStruct(mixed.shape, BF16), jax.ShapeDtypeStruct(mixed_t.shape, BF16),
                   jax.ShapeDtypeStruct((T, D), F32),
                   jax.ShapeDtypeStruct((nh * RNN_HB, T // CHUNK, RNN_HEAD, RNN_HEAD), F32)],
        scratch_shapes=[pltpu.VMEM((RNN_HB, RNN_HEAD, RNN_HEAD), F32)],
        input_output_aliases={6: 0, 7: 1},
        compiler_params=_params("parallel", "arbitrary"))(proj, proj, proj, proj, lb_logits, rnn_norm, mixed, mixed_t)


def _rnn_bwd(proj, lb_logits, rnn_norm, o_raw, states, dmix, dproj, D):
    T = proj.shape[0]
    tb = _pick(T, RNN_TB)
    nt, nh, cpb = T // tb, D // RNN_HEAD, tb // CHUNK
    cols, t_of = _rnn_specs(T, D, tb, True)

    def body(rq_ref, rf_ref, ri_ref, rg_ref, lb_ref, gain_ref, o_ref, st_ref, dmix_ref, dproj_in,
             dproj_ref, dlb_ref, dgain_ref, dstate_ref, out_buf, out_sems):
        del dproj_in
        step = pl.program_id(1)
        slot = step % 2
        drq_ref, drf_ref, dri_ref, drg_ref = [out_buf.at[i, slot] for i in range(4)]
        causal, tri, tri_t = _chunk_masks()
        lb = _lower_bound(lb_ref)
        gain = gain_ref[...]

        @pl.when(pl.program_id(1) == 0)
        def _():
            dstate_ref[...] = jnp.zeros_like(dstate_ref)
            dlb_ref[...] = jnp.zeros_like(dlb_ref)
            dgain_ref[...] = jnp.zeros_like(dgain_ref)

        def chunk(i, carry):
            c = cpb - 1 - i
            rows = pl.ds(pl.multiple_of(c * CHUNK, CHUNK), CHUNK)
            heads = range(RNN_HB)
            lns = [slice(RNN_HEAD * hh, RNN_HEAD * (hh + 1)) for hh in heads]
            last_row = lax.broadcasted_iota(jnp.int32, (CHUNK, 1), 0) == CHUNK - 1
            A = []
            for ln in lns:
                rq, rf, rg = rq_ref[rows, ln], rf_ref[rows, ln], rg_ref[rows, ln]
                o, dgated = o_ref[rows, ln], dmix_ref[rows, ln]
                gainh = gain[:, ln]
                sgt = _sigmoid(rg)
                r = lax.rsqrt(jnp.mean(o * o, axis=-1, keepdims=True) + NORM_EPS)
                on = o * r
                drg_ref[rows, ln] = (dgated * (on * gainh) * (sgt * (1.0 + rg * (1.0 - sgt)))).astype(BF16)
                d_on = dgated * (rg * sgt)
                dgain_ref[:, ln] += jnp.sum(d_on * on, axis=0, keepdims=True)
                dob = _norm_bwd(d_on * gainh, on, r).astype(BF16)
                sq, sf, f = _rnn_gates(rq, rf, lb[:, ln])
                A.append(dict(rq=rq, sq=sq, sf=sf, f=f, dob=dob, G=_tri_sum(tri, jnp.log(f))))
            for hh, ln in enumerate(lns):
                a = A[hh]
                _, eG, eq, ek, ekl, elast = _rnn_factors(a.pop("G"))
                q, k = a["rq"] * a["sq"], 1.0 - a["f"]
                st, dst = st_ref[hh, c], dstate_ref[hh]
                qg, kl = q * eG, k * ekl
                qmb, kmb = (q * eq).astype(BF16), (k * ek).astype(BF16)
                dob, vb, dstb = a["dob"], ri_ref[rows, ln].astype(BF16), dst.astype(BF16)
                a.update(eG=eG, eq=eq, ek=ek, ekl=ekl, qg=qg, kl=kl, qmb=qmb, kmb=kmb,
                         att=_dot(qmb, kmb, NT_DIMS), datt=_dot(dob, vb, NT_DIMS),
                         dqg=_dot(dob, st.astype(BF16), NN_DIMS), dkl=_dot(vb, dstb, NN_DIMS),
                         dri=_dot(kl.astype(BF16), dstb, NT_DIMS),
                         dlast=jnp.sum(dst * st, axis=0, keepdims=True) * elast)
                dstate_ref[hh] = dst * elast + _dot(dob, qg.astype(BF16), TN_DIMS)
            for hh, ln in enumerate(lns):
                a = A[hh]
                att = jnp.where(causal, a.pop("att"), 0.0).astype(BF16)
                datt = jnp.where(causal, a.pop("datt"), 0.0).astype(BF16)
                dqm = _dot(datt, a["kmb"], NN_DIMS)
                dkm = _dot(datt, a["qmb"], TN_DIMS)
                dri_ref[rows, ln] = (_dot(att, a["dob"], TN_DIMS) + a.pop("dri")).astype(BF16)
                dqg, dkl, kl = a.pop("dqg"), a.pop("dkl"), a.pop("kl")
                a["dq"] = dqg * a.pop("eG") + dqm * a.pop("eq")
                a["dk"] = dkm * a.pop("ek") + dkl * a.pop("ekl")
                dG = dqg * a.pop("qg") + dqm * a.pop("qmb").astype(F32) - dkm * a.pop("kmb").astype(F32) - dkl * kl
                dlast = jnp.sum(dkl * kl, axis=0, keepdims=True) + a.pop("dlast")
                a["dg"] = _tri_sum(tri_t, dG + jnp.where(last_row, dlast, 0.0))
            for hh, ln in enumerate(lns):
                a = A[hh]
                rq, sq, sf = a["rq"], a["sq"], a["sf"]
                df = a["dg"] / a["f"] - a["dk"]
                drq_ref[rows, ln] = (a["dq"] * (sq * (1.0 + rq * (1.0 - sq)))).astype(BF16)
                drf_ref[rows, ln] = (df * (1.0 - lb[:, ln]) * (sf * (1.0 - sf))).astype(BF16)
                dlb_ref[:, ln] += jnp.sum(df * (1.0 - sf), axis=0, keepdims=True)
            return carry

        lax.fori_loop(0, cpb, chunk, 0, unroll=True)
        row = pl.multiple_of(t_of(step) * tb, tb)
        col0 = 2 * D + D // 4 + pl.program_id(0) * RNN_WIDE
        corners = [(row, pl.multiple_of(col0 + i * D, LANES)) for i in range(4)]
        _flush_windows([out_buf.at[i] for i in range(4)], out_sems, dproj_ref, corners, slot, step, nt - 1)

    nh //= RNN_HB
    vec2 = pl.BlockSpec((2, RNN_WIDE), lambda h, s: (0, h))
    vec1 = pl.BlockSpec((1, RNN_WIDE), lambda h, s: (0, h))
    blk = pl.BlockSpec((tb, RNN_WIDE), lambda h, s: (t_of(s), h))
    return pl.pallas_call(
        body, name="rnn_bwd", grid=(nh, nt),
        in_specs=cols + [vec2, vec1, blk,
                         pl.BlockSpec((RNN_HB, cpb, RNN_HEAD, RNN_HEAD), lambda h, s: (h, t_of(s), 0, 0)),
                         pl.BlockSpec((tb, RNN_WIDE), lambda h, s: (t_of(s), D // RNN_WIDE + h)), ANY_SPEC],
        out_specs=[ANY_SPEC, vec1, vec1],
        out_shape=[jax.ShapeDtypeStruct(dproj.shape, BF16)] + [jax.ShapeDtypeStruct((1, D), F32)] * 2,
        scratch_shapes=[pltpu.VMEM((RNN_HB, RNN_HEAD, RNN_HEAD), F32), pltpu.VMEM((4, 2, tb, RNN_WIDE), BF16),
                        pltpu.SemaphoreType.DMA((2, 4))],
        input_output_aliases={9: 0},
        compiler_params=_params("parallel", "arbitrary"))(proj, proj, proj, proj, lb_logits, rnn_norm, o_raw, states, dmix, dproj)


def _adamw(w, g, m, v):
    m = ADAM_B1 * m + (1.0 - ADAM_B1) * g
    v = ADAM_B2 * v + (1.0 - ADAM_B2) * (g * g)
    m_hat = m / (1.0 - ADAM_B1 ** ADAM_STEP)
    v_hat = v / (1.0 - ADAM_B2 ** ADAM_STEP)
    delta = -ADAM_LR * (m_hat / (jnp.sqrt(v_hat) + ADAM_EPS) + ADAM_WD * w)
    return delta, m, v


def _adamw_big(w, m, v, parts, tr, name):
    R, C = w.shape
    n_parts = len(parts)

    def body(w_ref, m_ref, v_ref, *rest):
        part_refs = rest[:n_parts]
        g_ref, d_ref, nm_ref, nv_ref = rest[n_parts:]
        g = part_refs[0][...].astype(F32)
        for p_ref in part_refs[1:]:
            g = g + p_ref[...].astype(F32)
        delta, nm, nv = _adamw(w_ref[...], g, m_ref[...], v_ref[...])
        g_ref[...] = g
        d_ref[...] = delta
        nm_ref[...] = nm
        nv_ref[...] = nv

    blk = pl.BlockSpec((tr, C), lambda i: (i, 0))
    part_specs = [pl.BlockSpec((None, tr, C), functools.partial(lambda s, i: (s, i, 0), slot)) for _, slot in parts]
    return pl.pallas_call(
        body, name=name, grid=(R // tr,), in_specs=[blk, blk, blk] + part_specs,
        out_specs=[blk] * 4, out_shape=[jax.ShapeDtypeStruct((R, C), F32)] * 4,
        compiler_params=_params("parallel"))(w, m, v, *[a for a, _ in parts])


def _pair_sum(pa, qa):
    n, R, C = qa.shape
    tr = _pick(R, 256)

    def body(p_ref, q_ref, r_ref):
        r_ref[...] = (p_ref[...].astype(F32) + q_ref[...].astype(F32)).astype(BF16)

    return pl.pallas_call(
        body, name="pair_sum", grid=(n, R // tr),
        in_specs=[pl.BlockSpec((None, tr, C), lambda j, i: (2 * j, i, 0)), pl.BlockSpec((None, tr, C), lambda j, i: (j, i, 0))],
        out_specs=pl.BlockSpec((None, tr, C), lambda j, i: (j, i, 0)),
        out_shape=jax.ShapeDtypeStruct((n, R, C), BF16), compiler_params=_params("parallel", "parallel"))(pa, qa)


def _adamw_small(total, sinks, lb_logits, rnn_norm, pre_norm, post_norm, moments, D):
    params = [sinks, lb_logits, rnn_norm, pre_norm, post_norm]

    def body(tot_ref, *refs):
        p_refs, m_refs, v_refs = refs[0:5], refs[5:10], refs[10:15]
        loss_ref, outs = refs[15], refs[16:]
        tot = tot_ref[...]
        l0, l1 = p_refs[1][0:1, :], p_refs[1][1:2, :]
        mx = jnp.maximum(l0, l1)
        e0, e1 = jnp.exp(l0 - mx), jnp.exp(l1 - mx)
        p0, p1 = e0 / (e0 + e1), e1 / (e0 + e1)
        dlb = tot[0:1, :]
        grads = [tot[5:6, 0:LANES], jnp.concatenate([dlb * p0 * (1.0 - p0), -dlb * p0 * p1], axis=0),
                 tot[1:2, :], tot[2:3, :], tot[3:4, :]]
        loss_ref[...] = 0.5 / D * jnp.sum(tot[4:5, :], axis=-1, keepdims=True)
        for i, g in enumerate(grads):
            delta, nm, nv = _adamw(p_refs[i][...], g, m_refs[i][...], v_refs[i][...])
            outs[4 * i][...] = g
            outs[4 * i + 1][...] = delta
            outs[4 * i + 2][...] = nm
            outs[4 * i + 3][...] = nv

    out_shape = [jax.ShapeDtypeStruct((1, 1), F32)]
    for p in params:
        out_shape += [jax.ShapeDtypeStruct(p.shape, F32)] * 4
    return pl.pallas_call(body, name="adamw_small", out_shape=out_shape)(total, *params, *moments)


SIBLING = 1


def _me():
    return lax.axis_index("x"), lax.axis_index("y"), lax.axis_index("c")


def _flat(px, py, pc):
    return 4 * px + 2 * py + pc


def _role_peer(role, x, y, c):
    if not isinstance(role, int):
        j = (role - 2) // 2
        px = jnp.where((role >= 2) & (j != 1), 1 - x, x)
        py = jnp.where((role >= 2) & (j != 0), 1 - y, y)
        return (px, py, jnp.where(role % 2 == 1, 1 - c, c))
    if role < 2:
        return (x, y, (1 - c) if role else c)
    j, other = (role - 2) // 2, (role - 2) % 2
    px = (1 - x) if j in (0, 2) else x
    py = (1 - y) if j in (1, 2) else y
    return (px, py, (1 - c) if other else c)


def _wave1(x, y, c):
    return jnp.where(c == 1, 2, 4)


def _wave2(x, y, c):
    return jnp.where(c == 1, 4, 2)


def _role_ids():
    x, y, c = _me()
    w1, w2 = _wave1(x, y, c), _wave2(x, y, c)
    order = [0, 1, w1, w2 + 1, w2, w1 + 1, 6, 7]
    return jnp.stack([_flat(*_role_peer(r, x, y, c)) for r in range(N_DEV)] + order).astype(jnp.int32)


def _comm_call(name, bufs, waits=(), starts=(), after=()):
    n_buf, n_wait, n_start, n_after = len(bufs), len(waits), len(starts), len(after)

    def body(*refs):
        buf_refs = refs[:n_buf]
        sem_refs = refs[n_buf:n_buf + 2 * n_wait]
        outs = refs[n_buf + 2 * n_wait + n_after:]
        new_sems, token = outs[:2 * n_start], outs[-1]
        x, y, c = _me()

        def val(v):
            return v(x, y, c) if callable(v) else v

        def block(ref, slot, rows):
            if rows is None:
                return ref.at[val(slot)]
            first = rows[0] if isinstance(rows[0], int) else pl.multiple_of(val(rows[0]), 16)
            return ref.at[val(slot), pl.ds(first, rows[1])]

        for w, (_, kind, like, *rows) in enumerate(waits):
            shape_ref = block(buf_refs[like], 0, (0, rows[0][1]) if rows else None)
            cp = pltpu.make_async_remote_copy(
                src_ref=shape_ref, dst_ref=shape_ref, send_sem=sem_refs[2 * w], recv_sem=sem_refs[2 * w + 1],
                device_id=(x, y, c), device_id_type=MESH)
            if kind == "send":
                cp.wait_send()
            else:
                cp.wait_recv()
        for s, (sb, ss, db, ds, role, *rows) in enumerate(starts):
            rows = rows[0] if rows else None
            pltpu.make_async_remote_copy(
                src_ref=block(buf_refs[sb], ss, rows), dst_ref=block(buf_refs[db], ds, rows), send_sem=new_sems[2 * s],
                recv_sem=new_sems[2 * s + 1], device_id=_role_peer(val(role), x, y, c), device_id_type=MESH).start()
        token[...] = jnp.zeros_like(token)

    sems = [s for flight, *_ in waits for s in flight]
    out = pl.pallas_call(
        body, name=name,
        out_shape=tuple([pltpu.SemaphoreType.DMA(())] * (2 * n_start) + [pltpu.HBM(b.shape, b.dtype) for b in bufs]
                        + [jax.ShapeDtypeStruct((8, LANES), F32)]),
        in_specs=tuple([HBM_SPEC] * n_buf + [SEM_SPEC] * (2 * n_wait) + [ANY_SPEC] * n_after),
        out_specs=tuple([SEM_SPEC] * (2 * n_start) + [HBM_SPEC] * n_buf + [pl.BlockSpec(memory_space=pltpu.VMEM)]),
        input_output_aliases={i: 2 * n_start + i for i in range(n_buf)},
        compiler_params=pltpu.CompilerParams(has_side_effects=EFFECT),
    )(*[pltpu.with_memory_space_constraint(b, pltpu.HBM) for b in bufs], *sems, *after)
    flights = [(out[2 * s], out[2 * s + 1]) for s in range(n_start)]
    return list(out[2 * n_start:2 * n_start + n_buf]), flights, out[-1]


def _landing(shape):
    return lax.empty(shape, BF16)


def _both(flights, like):
    return [(f, kind, like) for f in flights for kind in ("send", "recv")]


def _allreduce_small(rows, D):
    n_rows = len(rows)

    def body(*refs):
        in_refs, out_ref, all_ref = refs[:n_rows], refs[n_rows], refs[n_rows + 1]
        send_sems, recv_sems = refs[n_rows + 2], refs[n_rows + 3]
        x, y, c = _me()
        mine = all_ref.at[_flat(x, y, c)]
        mine[...] = jnp.zeros((8, D), F32)
        for i, r in enumerate(in_refs):
            mine[i:i + 1, :] = r[...]
        copies = []
        for k in range(N_DEV - 1):
            copies.append(pltpu.make_async_remote_copy(
                src_ref=mine, dst_ref=mine, send_sem=send_sems.at[k], recv_sem=recv_sems.at[k],
                device_id=_role_peer(k + 1, x, y, c), device_id_type=MESH))
        for cp in copies:
            cp.start()
        for cp in copies:
            cp.wait_recv()
        for cp in copies:
            cp.wait_send()
        tot = all_ref[0]
        for d in range(1, N_DEV):
            tot = tot + all_ref[d]
        out_ref[...] = tot

    vm = pl.BlockSpec(memory_space=pltpu.VMEM)
    return pl.pallas_call(
        body, name="allreduce_small", in_specs=[vm] * n_rows, out_specs=vm,
        out_shape=jax.ShapeDtypeStruct((8, D), F32),
        scratch_shapes=[pltpu.VMEM((N_DEV, 8, D), F32), pltpu.SemaphoreType.DMA((7,)), pltpu.SemaphoreType.DMA((7,))],
    )(*rows)


def _step(ids, x, target, g_in, g_out, sinks, lb_logits, rnn_norm, pre_norm, post_norm, dist):
    T, D = x.shape
    n_roles, _, wd = g_in.shape
    ro = g_out.shape[1]
    assert n_roles % 2 == 0 and (not dist or n_roles == N_DEV)

    half = D // 2
    up = lambda f: (lambda x, y, c: f(x, y, c) + 1)
    rows_of = lambda f: (lambda x, y, c: jnp.where(f(x, y, c) == 2, 0, half))
    if dist:
        (g_in,), fl_in, _ = _comm_call(
            "gather_in_start", [g_in], starts=[(0, 0, 0, SIBLING, SIBLING), (0, 0, 0, _wave1, _wave1)])
    h, h_t = _prenorm(x, pre_norm)
    proj = _proj_in(ids, h, g_in, 0, 1, None, "proj_in_0")
    if dist:
        (g_in,), _, _ = _comm_call("gather_in_sibling", [g_in], waits=[(fl_in[0], "recv", 0)], after=(proj,))
    proj = _proj_in(ids, h, g_in, 1, 1, proj, "proj_in_1")
    if not dist:
        for p in range(2, n_roles, 2):
            proj = _proj_in(ids, h, g_in, p, 2, proj, f"proj_in_{p}")
    else:
        (g_in,), fl_a, _ = _comm_call(
            "gather_in_wave_1", [g_in], waits=[(fl_in[1], "recv", 0)],
            starts=[(0, 0, 0, _wave2, _wave2), (0, _wave1, 0, up(_wave1), SIBLING),
                    (0, _wave1, 0, 6, _wave2, (rows_of(_wave1), half))], after=(proj,))
        proj = _proj_in(ids, h, g_in, 2, 1, proj, "proj_in_2")
        (g_in,), _, _ = _comm_call("gather_in_passed_1", [g_in], waits=[(fl_a[1], "recv", 0)], after=(proj,))
        proj = _proj_in(ids, h, g_in, 3, 1, proj, "proj_in_3")
        (g_in,), fl_b, _ = _comm_call(
            "gather_in_wave_2", [g_in], waits=[(fl_a[0], "recv", 0)],
            starts=[(0, _wave2, 0, up(_wave2), SIBLING), (0, _wave2, 0, 6, _wave1, (rows_of(_wave2), half))],
            after=(proj,))
        proj = _proj_in(ids, h, g_in, 4, 1, proj, "proj_in_4")
        (g_in,), _, _ = _comm_call("gather_in_passed_2", [g_in], waits=[(fl_b[0], "recv", 0)], after=(proj,))
        proj = _proj_in(ids, h, g_in, 5, 1, proj, "proj_in_5")
        (g_in,), fl_c, _ = _comm_call(
            "gather_in_diagonal", [g_in], waits=[(fl_a[2], "recv", 0, (0, half)), (fl_b[1], "recv", 0, (0, half))],
            starts=[(0, 6, 0, 7, SIBLING)], after=(proj,))
        proj = _proj_in(ids, h, g_in, 6, 1, proj, "proj_in_6")
        (g_in, g_out), fl_out, token = _comm_call(
            "gather_out_start", [g_in, g_out], waits=[(fl_c[0], "recv", 0)],
            starts=[(1, 0, 1, SIBLING, SIBLING)] + [(1, 0, 1, 2 + 2 * i, 2 + 2 * i) for i in range(3)], after=(proj,))
        proj = _proj_in(ids, h, g_in, 7, 1, proj, "proj_in_7", (token,))
        send_waits = [(f, "send", 0) for f in fl_in + fl_a[:2] + fl_b[:1] + fl_c] \
            + [(f, "send", 0, (0, half)) for f in (fl_a[2], fl_b[1])]
    mixed, mixed_t = _attn_fwd(proj, sinks, D)
    if dist:
        (g_in, g_out), fl_out_fwd, _ = _comm_call(
            "gather_out_pass", [g_in, g_out],
            waits=send_waits + [(fl_out[1 + i], "recv", 1) for i in range(3)],
            starts=[(1, 2 + 2 * i, 1, 3 + 2 * i, SIBLING) for i in range(3)], after=(mixed,))
    mixed, mixed_t, o_raw, states = _rnn_fwd(proj, lb_logits, rnn_norm, mixed, mixed_t, D)
    if dist:
        (g_out,), _, _ = _comm_call(
            "gather_out_done", [g_out],
            waits=[(fl_out[0], "recv", 0)] + [(f, "recv", 0) for f in fl_out_fwd]
            + [(f, "send", 0) for f in fl_out + fl_out_fwd], after=(states,))
    y = _proj_out(ids, mixed, g_out)
    dy, dout, g_post, sq_err = _loss_head(y, x, target, post_norm)

    dmix = _dmixed(ids, dy, g_out)
    p_out, after = _dw_out(ids, mixed_t, dy, n_roles), ()
    if dist:
        (p_out, l_out), fl_so, token = _comm_call(
            "scatter_out_start", [p_out, _landing((n_roles - 1, ro, D))],
            starts=[(0, r, 1, r - 1, r) for r in range(1, n_roles)])
        after = (token,)
    dproj, d_sink = _attn_bwd(proj, sinks, dmix, D, after)
    dproj, d_lb, g_rnn = _rnn_bwd(proj, lb_logits, rnn_norm, o_raw, states, dmix, dproj, D)

    p_far = _dw_in(ids, h_t, dproj, n_roles, 2, n_roles - 2, "dw_in_far")
    if dist:
        (p_far, q_far), fl_pair, token = _comm_call(
            "scatter_in_pair_start", [p_far, _landing((3, D, wd))],
            starts=[(0, 1 + 2 * j, 1, j, SIBLING) for j in range(3)])
        p_near = _dw_in(ids, h_t, dproj, n_roles, 0, 2, "dw_in_near", after=(token,))
        (p_far, q_far), _, _ = _comm_call("scatter_in_pair_wait", [p_far, q_far], waits=_both(fl_pair, 1), after=(p_near,))
        chip_sum = _pair_sum(p_far, q_far)
        (chip_sum, z_far), fl_chip, token = _comm_call(
            "scatter_in_chip_start", [chip_sum, _landing((3, D, wd))],
            starts=[(0, j, 1, j, 2 + 2 * j) for j in range(3)])
        (p_near, q_near), fl_sib, token = _comm_call(
            "scatter_in_sibling_start", [p_near, _landing((1, D, wd))], starts=[(0, 1, 1, 0, SIBLING)], after=(token,))
        after = (token,)
    else:
        p_near, after = _dw_in(ids, h_t, dproj, n_roles, 0, 2, "dw_in_near"), ()
    dh = _dh(ids, dproj, g_in, after)
    grad_x, g_pre = _prenorm_bwd(x, dh, dout, pre_norm)
    n_q = D // ATTN_HEAD
    sink_row = jnp.pad(d_sink[:, 0, :2 * GQA].reshape(1, n_q), ((0, 0), (0, D - n_q)))
    rows = [d_lb, g_rnn, g_pre, g_post, sq_err, sink_row]
    if not dist:
        return grad_x, [(p_near, r) for r in range(2)] + [(p_far, r) for r in range(n_roles - 2)], \
            [(p_out, r) for r in range(n_roles)], rows

    (p_out, l_out), _, _ = _comm_call("scatter_out_wait", [p_out, l_out], waits=_both(fl_so, 1), after=(grad_x,))
    (chip_sum, z_far, p_near, q_near), _, _ = _comm_call(
        "scatter_in_wait", [chip_sum, z_far, p_near, q_near], waits=_both(fl_chip, 1) + _both(fl_sib, 3), after=(p_out,))
    parts_in = [(p_near, 0), (q_near, 0)] + [(z_far, j) for j in range(3)]
    parts_out = [(p_out, 0)] + [(l_out, k) for k in range(n_roles - 1)]
    return grad_x, parts_in, parts_out, rows


def kernel(x, w_in, attn_sinks, lb_logits, rnn_norm, w_out, pre_norm, post_norm, loss_target, m_w_in, m_attn_sinks, m_lb_logits, m_rnn_norm, m_w_out, m_pre_norm, m_post_norm, v_w_in, v_attn_sinks, v_lb_logits, v_rnn_norm, v_w_out, v_pre_norm, v_post_norm):
    _, T, D = x.shape
    ro = w_out.shape[1]
    n_q = attn_sinks.shape[1]
    assert lb_logits.shape[0] == 2 and n_q == D // ATTN_HEAD and n_q <= LANES

    grad_x, parts_in, parts_out, small_rows = _step(
        _role_ids(), x[0], loss_target[0], _cast_slot0(w_in[0], N_DEV, "cast_w_in"),
        _cast_slot0(w_out[0], N_DEV, "cast_w_out"), attn_sinks, lb_logits, rnn_norm, pre_norm, post_norm, True)

    g_wo, d_wo, nm_wo, nv_wo = _adamw_big(w_out[0], m_w_out[0], v_w_out[0], parts_out, _pick(ro, 64), "adamw_w_out")
    g_wi, d_wi, nm_wi, nv_wi = _adamw_big(w_in[0], m_w_in[0], v_w_in[0], parts_in, _pick(D, 128), "adamw_w_in")

    total = _allreduce_small(small_rows, D)
    pad = lambda a: jnp.pad(a, ((0, 0), (0, LANES - n_q)))
    moments = [pad(m_attn_sinks), m_lb_logits, m_rnn_norm, m_pre_norm, m_post_norm,
               pad(v_attn_sinks), v_lb_logits, v_rnn_norm, v_pre_norm, v_post_norm]
    res = _adamw_small(total, pad(attn_sinks), lb_logits, rnn_norm, pre_norm, post_norm, moments, D)
    loss = res[0][0, 0]
    small = [[res[1 + 4 * i + j] for i in range(5)] for j in range(4)]
    for j in range(4):
        small[j][0] = small[j][0][:, :n_q]

    def assemble(j, wi, wo):
        s = small[j]
        return [wi[None], s[0], s[1], s[2], wo[None], s[3], s[4]]

    return (loss, grad_x[None], *assemble(0, g_wi, g_wo), *assemble(1, d_wi, d_wo),
            *assemble(2, nm_wi, nm_wo), *assemble(3, nv_wi, nv_wo))
```

```python
import functools

import jax
import jax.numpy as jnp
from jax import lax
from jax.experimental import pallas as pl
from jax.experimental.pallas import tpu as pltpu

F32 = jnp.float32
BF16 = jnp.bfloat16

ATTN_HEAD = 64
GQA = 8
WINDOW = 128
RNN_HEAD = 128
CHUNK = 64
NORM_EPS = 1e-6
LANES = 128
N_DEV = 8

ADAM_LR = 0.001
ADAM_B1 = 0.9
ADAM_B2 = 0.999
ADAM_EPS = 1e-08
ADAM_WD = 0.01
ADAM_STEP = 10

VMEM_LIMIT = 56 * 1024 * 1024
MESH = pl.DeviceIdType.MESH
ANY_SPEC = pl.BlockSpec(memory_space=pl.ANY)
HBM_SPEC = pl.BlockSpec(memory_space=pltpu.HBM)
SEM_SPEC = pl.BlockSpec(memory_space=pltpu.SEMAPHORE)
EFFECT = pltpu.SideEffectType.DATAFLOW_SIDE_EFFECTING

NT_DIMS = (((1,), (1,)), ((), ()))
NN_DIMS = (((1,), (0,)), ((), ()))
TN_DIMS = (((0,), (0,)), ((), ()))


def _params(*sem):
    return pltpu.CompilerParams(dimension_semantics=sem, vmem_limit_bytes=VMEM_LIMIT)


def _dot(a, b, dims):
    return lax.dot_general(a, b, dims, preferred_element_type=F32)


def _sigmoid(v):
    return 0.5 * jnp.tanh(0.5 * v) + 0.5


def _pick(n, pref):
    t = min(n, pref)
    assert n % t == 0, (n, pref)
    return t


MXU_COLS = 256
WIDE_TILE = 5 * MXU_COLS


def _col_tiles(wd):
    wide = wd // WIDE_TILE
    rest = wd - wide * WIDE_TILE
    assert rest % LANES == 0 and (rest == 0 or (wide * WIDE_TILE) % rest == 0), wd
    return [(WIDE_TILE, 0, wide)] * (wide > 0) + [(rest, wide * WIDE_TILE, 1)] * (rest > 0)


def _matmul(ids, a, b, *, grid, a_spec, b_spec, o_spec, out_shape, trans_b, name, after=(), prev=None):
    nk = grid[2]
    dims = NT_DIMS if trans_b else NN_DIMS
    n_skip = len(after) + (prev is not None)

    def body(ids_ref, a_ref, b_ref, *rest):
        del ids_ref
        o_ref, scratch = rest[n_skip], rest[n_skip + 1:]
        prod = _dot(a_ref[...], b_ref[...], dims)
        if nk == 1:
            o_ref[...] = prod.astype(o_ref.dtype)
        else:
            acc_ref, = scratch
            k = pl.program_id(2)

            @pl.when(k == 0)
            def _():
                acc_ref[...] = prod

            @pl.when(k > 0)
            def _():
                acc_ref[...] += prod

            @pl.when(k == nk - 1)
            def _():
                o_ref[...] = acc_ref[...].astype(o_ref.dtype)

    scratch = [] if nk == 1 else [pltpu.VMEM(tuple(d for d in o_spec.block_shape if d is not None), F32)]
    extra = list(after) + ([prev] if prev is not None else [])
    aliases = {3 + len(after): 0} if prev is not None else {}
    return pl.pallas_call(
        body, name=name, out_shape=out_shape, input_output_aliases=aliases,
        grid_spec=pltpu.PrefetchScalarGridSpec(
            num_scalar_prefetch=1, grid=grid, in_specs=[a_spec, b_spec] + [ANY_SPEC] * len(extra),
            out_specs=o_spec, scratch_shapes=scratch),
        compiler_params=_params("parallel", "parallel", "arbitrary"),
    )(ids, a, b, *extra)


def _proj_in(ids, h, w_roles, pos, n, prev, name, after=()):
    T, D = h.shape
    n_roles, _, wd = w_roles.shape
    tm = _pick(T, 1024)
    for t, (tn, first, count) in enumerate(_col_tiles(wd)):
        prev = _matmul(
            ids, h, w_roles, grid=(T // tm, n * count, 1),
            a_spec=pl.BlockSpec((tm, D), lambda i, j, k, ids: (i, 0)),
            b_spec=pl.BlockSpec((None, D, tn), functools.partial(
                lambda tn, first, count, i, j, k, ids: (ids[n_roles + pos + j // count], 0, first // tn + j % count),
                tn, first, count)),
            o_spec=pl.BlockSpec((pl.Element(tm), pl.Element(tn)), functools.partial(
                lambda tn, first, count, i, j, k, ids: (
                    i * tm, pl.multiple_of(ids[ids[n_roles + pos + j // count]] * wd + first + (j % count) * tn, LANES)),
                tn, first, count)),
            out_shape=jax.ShapeDtypeStruct((T, n_roles * wd), F32), trans_b=False, name=f"{name}_{t}",
            after=after, prev=prev)
    return prev


def _proj_out(ids, mixed, wo_roles):
    T, E = mixed.shape
    n_roles, R, D = wo_roles.shape
    tm, tn = _pick(T, 512), _pick(D, 512)

    def body(ids_ref, a_ref, b_ref, o_ref, a_roles):
        @pl.when(pl.program_id(1) == 0)
        def _():
            for k in range(n_roles):
                col = pl.multiple_of(ids_ref[k] * R, R)
                a_roles[:, k * R:(k + 1) * R] = a_ref[:, pl.ds(col, R)]

        o_ref[...] = _dot(a_roles[...], b_ref[...].reshape(E, tn), NN_DIMS)

    return pl.pallas_call(
        body, name="proj_out", out_shape=jax.ShapeDtypeStruct((T, D), F32),
        grid_spec=pltpu.PrefetchScalarGridSpec(
            num_scalar_prefetch=1, grid=(T // tm, D // tn),
            in_specs=[pl.BlockSpec((tm, E), lambda i, j, ids: (i, 0)),
                      pl.BlockSpec((n_roles, R, tn), lambda i, j, ids: (0, 0, j))],
            out_specs=pl.BlockSpec((tm, tn), lambda i, j, ids: (i, j)),
            scratch_shapes=[pltpu.VMEM((tm, E), BF16)]),
        compiler_params=_params("parallel", "arbitrary"))(ids, mixed, wo_roles)


def _dmixed(ids, dy, wo_roles):
    T, D = dy.shape
    n_roles, R, _ = wo_roles.shape
    tm = _pick(T, 1024)
    return _matmul(
        ids, dy, wo_roles, grid=(T // tm, n_roles, 1),
        a_spec=pl.BlockSpec((tm, D), lambda i, j, k, ids: (i, 0)),
        b_spec=pl.BlockSpec((None, R, D), lambda i, j, k, ids: (j, 0, 0)),
        o_spec=pl.BlockSpec((tm, R), lambda i, j, k, ids: (i, ids[j])),
        out_shape=jax.ShapeDtypeStruct((T, n_roles * R), F32), trans_b=True, name="dmixed")


def _dw_out(ids, mixed_t, dy, n_roles):
    E, T = mixed_t.shape
    D = dy.shape[1]
    R = E // n_roles
    tn = _pick(D, 512)
    return _matmul(
        ids, mixed_t, dy, grid=(n_roles, D // tn, 1),
        a_spec=pl.BlockSpec((R, T), lambda i, j, k, ids: (ids[i], 0)),
        b_spec=pl.BlockSpec((T, tn), lambda i, j, k, ids: (0, j)),
        o_spec=pl.BlockSpec((None, R, tn), lambda i, j, k, ids: (i, 0, j)),
        out_shape=jax.ShapeDtypeStruct((n_roles, R, D), BF16), trans_b=False, name="dw_out")


def _dw_in(ids, h_t, dproj, n_roles, r0, nr, name, after=()):
    D, T = h_t.shape
    wd = dproj.shape[1] // n_roles
    tm, out = _pick(D, 1024), None
    for t, (tn, first, count) in enumerate(_col_tiles(wd)):
        out = _matmul(
            ids, h_t, dproj, grid=(D // tm, nr * count, 1),
            a_spec=pl.BlockSpec((tm, T), lambda i, j, k, ids: (i, 0)),
            b_spec=pl.BlockSpec((pl.Element(T), pl.Element(tn)), functools.partial(
                lambda tn, first, count, i, j, k, ids: (
                    0, pl.multiple_of(ids[r0 + j // count] * wd + first + (j % count) * tn, LANES)),
                tn, first, count)),
            o_spec=pl.BlockSpec((None, tm, tn), functools.partial(
                lambda tn, first, count, i, j, k, ids: (j // count, i, first // tn + j % count), tn, first, count)),
            out_shape=jax.ShapeDtypeStruct((nr, D, wd), BF16), trans_b=False, name=f"{name}_{t}", after=after, prev=out)
    return out


def _dh(ids, dproj, w_roles, after=()):
    T = dproj.shape[0]
    n_roles, D, wd = w_roles.shape
    tm, tn = _pick(T, 1024), _pick(D, 512)
    nk, n_after = n_roles // 2, len(after)

    def body(ids_ref, a0_ref, a1_ref, b0_ref, b1_ref, *rest):
        del ids_ref
        o_ref, acc_ref = rest[n_after], rest[n_after + 1]
        prod = _dot(a0_ref[...], b0_ref[...], NT_DIMS) + _dot(a1_ref[...], b1_ref[...], NT_DIMS)
        k = pl.program_id(2)

        @pl.when(k == 0)
        def _():
            acc_ref[...] = prod

        @pl.when((k > 0) & (k < nk - 1))
        def _():
            acc_ref[...] += prod

        @pl.when(k == nk - 1)
        def _():
            o_ref[...] = acc_ref[...] + prod

    a_spec = lambda e: pl.BlockSpec((tm, wd), lambda i, j, k, ids: (i, ids[2 * k + e]))
    b_spec = lambda e: pl.BlockSpec((None, tn, wd), lambda i, j, k, ids: (2 * k + e, j, 0))
    return pl.pallas_call(
        body, name="dh", out_shape=jax.ShapeDtypeStruct((T, D), F32),
        grid_spec=pltpu.PrefetchScalarGridSpec(
            num_scalar_prefetch=1, grid=(T // tm, D // tn, nk),
            in_specs=[a_spec(0), a_spec(1), b_spec(0), b_spec(1)] + [ANY_SPEC] * n_after,
            out_specs=pl.BlockSpec((tm, tn), lambda i, j, k, ids: (i, j)),
            scratch_shapes=[pltpu.VMEM((tm, tn), F32)]),
        compiler_params=_params("parallel", "parallel", "arbitrary"),
    )(ids, dproj, dproj, w_roles, w_roles, *after)


def _cast_slot0(w, n_roles, name):
    R, C = w.shape
    tr = _pick(R, 256)

    def body(w_ref, o_ref):
        o_ref[...] = w_ref[...].astype(BF16)

    return pl.pallas_call(
        body, name=name, grid=(R // tr,), in_specs=[pl.BlockSpec((tr, C), lambda i: (i, 0))],
        out_specs=pl.BlockSpec((None, tr, C), lambda i: (0, i, 0)),
        out_shape=jax.ShapeDtypeStruct((n_roles, R, C), BF16), compiler_params=_params("parallel"))(w)


def _prenorm(x, gain):
    T, D = x.shape
    tm = _pick(T, 256)

    def body(x_ref, g_ref, h_ref, ht_ref):
        xv = x_ref[...]
        r = lax.rsqrt(jnp.mean(xv * xv, axis=-1, keepdims=True) + NORM_EPS)
        h = xv * r * g_ref[...]
        h_ref[...] = h.astype(BF16)
        ht_ref[...] = h.T.astype(BF16)

    return pl.pallas_call(
        body, name="prenorm", grid=(T // tm,),
        in_specs=[pl.BlockSpec((tm, D), lambda i: (i, 0)), pl.BlockSpec((1, D), lambda i: (0, 0))],
        out_specs=[pl.BlockSpec((tm, D), lambda i: (i, 0)), pl.BlockSpec((D, tm), lambda i: (0, i))],
        out_shape=[jax.ShapeDtypeStruct((T, D), BF16), jax.ShapeDtypeStruct((D, T), BF16)],
        compiler_params=_params("parallel"))(x, gain)


def _norm_bwd(u, yn, r):
    return r * (u - yn * jnp.mean(u * yn, axis=-1, keepdims=True))


def _loss_head(y, x, target, gain):
    T, D = y.shape
    tm = _pick(T, 256)

    def body(y_ref, x_ref, t_ref, g_ref, dy_ref, dout_ref, gpost_ref, sq_ref):
        yv = y_ref[...]
        g = g_ref[...]
        r = lax.rsqrt(jnp.mean(yv * yv, axis=-1, keepdims=True) + NORM_EPS)
        yn = yv * r
        err = x_ref[...] + yn * g - t_ref[...]
        dout = err * (1.0 / D)
        dy_ref[...] = _norm_bwd(dout * g, yn, r).astype(BF16)
        dout_ref[...] = dout

        @pl.when(pl.program_id(0) == 0)
        def _():
            gpost_ref[...] = jnp.zeros_like(gpost_ref)
            sq_ref[...] = jnp.zeros_like(sq_ref)

        gpost_ref[...] += jnp.sum(dout * yn, axis=0, keepdims=True)
        sq_ref[...] += jnp.sum(err * err, axis=0, keepdims=True)

    row = pl.BlockSpec((tm, D), lambda i: (i, 0))
    vec = pl.BlockSpec((1, D), lambda i: (0, 0))
    return pl.pallas_call(
        body, name="loss_head", grid=(T // tm,), in_specs=[row, row, row, vec], out_specs=[row, row, vec, vec],
        out_shape=[jax.ShapeDtypeStruct((T, D), BF16), jax.ShapeDtypeStruct((T, D), F32),
                   jax.ShapeDtypeStruct((1, D), F32), jax.ShapeDtypeStruct((1, D), F32)],
        compiler_params=_params("arbitrary"))(y, x, target, gain)


def _prenorm_bwd(x, dh, dout, gain):
    T, D = x.shape
    tm = _pick(T, 256)

    def body(x_ref, dh_ref, dout_ref, g_ref, gx_ref, gpre_ref):
        xv = x_ref[...]
        dhv = dh_ref[...]
        r = lax.rsqrt(jnp.mean(xv * xv, axis=-1, keepdims=True) + NORM_EPS)
        xn = xv * r
        gx_ref[...] = dout_ref[...] + _norm_bwd(dhv * g_ref[...], xn, r)

        @pl.when(pl.program_id(0) == 0)
        def _():
            gpre_ref[...] = jnp.zeros_like(gpre_ref)

        gpre_ref[...] += jnp.sum(dhv * xn, axis=0, keepdims=True)

    row = pl.BlockSpec((tm, D), lambda i: (i, 0))
    vec = pl.BlockSpec((1, D), lambda i: (0, 0))
    return pl.pallas_call(
        body, name="prenorm_bwd", grid=(T // tm,), in_specs=[row, row, row, vec], out_specs=[row, vec],
        out_shape=[jax.ShapeDtypeStruct((T, D), F32), jax.ShapeDtypeStruct((1, D), F32)],
        compiler_params=_params("arbitrary"))(x, dh, dout, gain)


def _attn_masks(n):
    row = lax.broadcasted_iota(jnp.int32, (2 * WINDOW, 2 * WINDOW), 0) % WINDOW
    col = lax.broadcasted_iota(jnp.int32, (2 * WINDOW, 2 * WINDOW), 1)
    valid = (col > row) & (col <= row + WINDOW) & ((n > 0) | (col >= WINDOW))
    low = lax.broadcasted_iota(jnp.int32, (1, LANES), 1) < ATTN_HEAD
    top = lax.broadcasted_iota(jnp.int32, (2 * WINDOW, 1), 0) < WINDOW
    return jnp.where(valid, 0.0, -jnp.inf), low, top


def _dup_half(pair, keep):
    return jnp.where(keep, pair, pltpu.roll(pair, ATTN_HEAD, 1))


def _fold_half(v):
    return v + pltpu.roll(v, ATTN_HEAD, 1)


ATTN_GROUP_FWD = 8
ATTN_GROUP = 4
ATTN_SCALE = ATTN_HEAD ** -0.5


def _attn_scores(qpair, k2, low):
    qs = qpair * ATTN_SCALE
    q2 = jnp.concatenate([jnp.where(low, qs, 0.0), jnp.where(low, 0.0, qs)], axis=0).astype(BF16)
    return q2, _dot(q2, k2, NT_DIMS)


def _attn_softmax(scores, sink_lo, sink_hi, bias, top):
    s = scores + bias
    sink = jnp.where(top, sink_lo, sink_hi)
    m = jnp.maximum(jnp.max(s, axis=-1, keepdims=True), sink)
    p = jnp.exp(s - m)
    psink = jnp.exp(sink - m)
    inv = 1.0 / (jnp.sum(p, axis=-1, keepdims=True) + psink)
    return p * inv, psink * inv


def _attn_fwd(proj, sinks, D):
    T = proj.shape[0]
    nb, njp = T // WINDOW, D // 1024
    assert nb % 2 == 0
    two = 2 * WINDOW

    def body(sink_ref, qlo_ref, qhi_ref, kc_ref, kp_ref, vc_ref, vp_ref, glo_ref, ghi_ref, mix_ref, mixt_ref):
        jp, m = pl.program_id(0), pl.program_id(1)
        q_refs, g_refs = (qlo_ref, qhi_ref), (glo_ref, ghi_ref)
        biases, k2s, v2s = [], {}, {}
        for blk in (0, 1):
            bias, low, top = _attn_masks(2 * m + blk)
            biases.append(bias)
            keys = kc_ref[0:two, :] if blk else jnp.concatenate([kp_ref[...], kc_ref[0:WINDOW, :]], axis=0)
            vals = vc_ref[0:two, :] if blk else jnp.concatenate([vp_ref[...], vc_ref[0:WINDOW, :]], axis=0)
            for hj, keep in enumerate((low, jnp.logical_not(low))):
                k2s[blk, hj] = _dup_half(keys, keep).astype(BF16)
                v2s[blk, hj] = _dup_half(vals, keep).astype(BF16)
        all_units = [(blk, hj, p) for blk in (0, 1) for hj in (0, 1) for p in range(4)]
        for first in range(0, len(all_units), ATTN_GROUP_FWD):
            units = all_units[first:first + ATTN_GROUP_FWD]
            rows = {u: slice(WINDOW * u[0], WINDOW * (u[0] + 1)) for u in units}
            cols = {u: slice(LANES * u[2], LANES * (u[2] + 1)) for u in units}
            scores = {u: _attn_scores(q_refs[u[1]][rows[u], cols[u]], k2s[u[0], u[1]], low)[1] for u in units}
            probs = {}
            for u in units:
                head = (2 * jp + u[1]) * GQA + 2 * u[2]
                probs[u] = _attn_softmax(scores[u], sink_ref[0, head], sink_ref[0, head + 1], biases[u[0]], top)[0]
            o2s = {u: _dot(probs[u].astype(BF16), v2s[u[0], u[1]], NN_DIMS) for u in units}
            for u in units:
                opair = jnp.where(low, o2s[u][:WINDOW], o2s[u][WINDOW:])
                g = g_refs[u[1]][rows[u], cols[u]]
                out = opair * (g * _sigmoid(g))
                oc = slice(512 * u[1] + LANES * u[2], 512 * u[1] + LANES * (u[2] + 1))
                mix_ref[rows[u], oc] = out.astype(BF16)
                mixt_ref[oc, rows[u]] = out.T.astype(BF16)

    kb = D // LANES
    vb = kb + D // (8 * LANES)
    gb = (D + D // 4) // 512
    wide = lambda off: [pl.BlockSpec((two, 512), functools.partial(lambda o, e, jp, m: (m, o + 2 * jp + e), off, e))
                        for e in (0, 1)]
    cur = lambda off: pl.BlockSpec((two, LANES), functools.partial(lambda o, jp, m: (m, o + jp), off))
    prev = lambda off: pl.BlockSpec((WINDOW, LANES),
                                    functools.partial(lambda o, jp, m: (jnp.maximum(2 * m - 1, 0), o + jp), off))
    return pl.pallas_call(
        body, name="attn_fwd", grid=(njp, nb // 2),
        in_specs=[pl.BlockSpec(memory_space=pltpu.SMEM)] + wide(0) + [cur(kb), prev(kb), cur(vb), prev(vb)] + wide(gb),
        out_specs=[pl.BlockSpec((two, 1024), lambda jp, m: (m, jp)),
                   pl.BlockSpec((1024, two), lambda jp, m: (jp, m))],
        out_shape=[jax.ShapeDtypeStruct((T, 2 * D), BF16), jax.ShapeDtypeStruct((2 * D, T), BF16)],
        compiler_params=_params("parallel", "parallel"))(sinks, *([proj] * 8))


def _flush_windows(bufs, sems, hbm_ref, corners, slot, step, last):
    def copies(sl):
        return [pltpu.make_async_copy(
            b.at[sl], hbm_ref.at[pl.ds(r0, b.shape[1]), pl.ds(c0, b.shape[2])], sems.at[sl, i])
            for i, (b, (r0, c0)) in enumerate(zip(bufs, corners))]

    for cp in copies(slot):
        cp.start()

    @pl.when(step > 0)
    def _():
        for cp in copies(1 - slot):
            cp.wait()

    @pl.when(step == last)
    def _():
        for cp in copies(slot):
            cp.wait()


def _attn_bwd(proj, sinks, dmix, D, after=()):
    T = proj.shape[0]
    nb, njp = T // WINDOW, D // 1024
    assert nb % 2 == 0
    two, ns = 2 * WINDOW, nb // 2
    n_after = len(after)

    def body(sink_ref, qlo_ref, qhi_ref, kc_ref, kp_ref, vc_ref, vp_ref, glo_ref, ghi_ref, dmix_ref, *rest):
        dproj_ref, dsink_ref, kcarry_ref, vcarry_ref, dq_buf, dk_buf, dv_buf, dg_buf, out_sems = rest[n_after:]
        jp, step = pl.program_id(0), pl.program_id(1)
        m = ns - 1 - step
        slot = step % 2
        dq_ref, dk_ref, dv_ref, dg_ref = dq_buf.at[slot], dk_buf.at[slot], dv_buf.at[slot], dg_buf.at[slot]
        lane = lax.broadcasted_iota(jnp.int32, (1, LANES), 1)

        @pl.when(step == 0)
        def _():
            kcarry_ref[...] = jnp.zeros_like(kcarry_ref)
            vcarry_ref[...] = jnp.zeros_like(vcarry_ref)
            dsink_ref[...] = jnp.zeros_like(dsink_ref)

        q_refs, g_refs = (qlo_ref, qhi_ref), (glo_ref, ghi_ref)
        biases, k2s, v2s, dk_heads, dv_heads = [], {}, {}, {}, {}
        for blk in (0, 1):
            bias, low, top = _attn_masks(2 * m + blk)
            biases.append(bias)
            keys = kc_ref[0:two, :] if blk else jnp.concatenate([kp_ref[...], kc_ref[0:WINDOW, :]], axis=0)
            vals = vc_ref[0:two, :] if blk else jnp.concatenate([vp_ref[...], vc_ref[0:WINDOW, :]], axis=0)
            for hj, keep in enumerate((low, jnp.logical_not(low))):
                k2s[blk, hj] = _dup_half(keys, keep).astype(BF16)
                v2s[blk, hj] = _dup_half(vals, keep).astype(BF16)
                dk_heads[blk, hj] = jnp.zeros((two, LANES), F32)
                dv_heads[blk, hj] = jnp.zeros((two, LANES), F32)
        keeps = (low, jnp.logical_not(low))
        dsink = jnp.zeros((1, LANES), F32)
        all_units = [(blk, hj, p) for blk in (1, 0) for hj in (0, 1) for p in range(4)]
        for first in range(0, len(all_units), ATTN_GROUP):
            units = all_units[first:first + ATTN_GROUP]
            rows = {u: slice(WINDOW * u[0], WINDOW * (u[0] + 1)) for u in units}
            cols = {u: slice(LANES * u[2], LANES * (u[2] + 1)) for u in units}
            ocs = {u: slice(512 * u[1] + LANES * u[2], 512 * u[1] + LANES * (u[2] + 1)) for u in units}
            qsc = {u: _attn_scores(q_refs[u[1]][rows[u], cols[u]], k2s[u[0], u[1]], low) for u in units}
            probs, psinks, do2s = {}, {}, {}
            for u in units:
                head = (2 * jp + u[1]) * GQA + 2 * u[2]
                probs[u], psinks[u] = _attn_softmax(qsc[u][1], sink_ref[0, head], sink_ref[0, head + 1], biases[u[0]], top)
                g = g_refs[u[1]][rows[u], cols[u]]
                do = dmix_ref[rows[u], ocs[u]] * (g * _sigmoid(g))
                do2s[u] = jnp.concatenate([jnp.where(low, do, 0.0), jnp.where(low, 0.0, do)], axis=0).astype(BF16)
            pbs = {u: probs[u].astype(BF16) for u in units}
            o2s = {u: _dot(pbs[u], v2s[u[0], u[1]], NN_DIMS) for u in units}
            dps = {u: _dot(do2s[u], v2s[u[0], u[1]], NT_DIMS) for u in units}
            dss = {}
            for u in units:
                opair = jnp.where(low, o2s[u][:WINDOW], o2s[u][WINDOW:])
                g = g_refs[u[1]][rows[u], cols[u]]
                sg = _sigmoid(g)
                dg_ref[rows[u], ocs[u]] = (dmix_ref[rows[u], ocs[u]] * opair * (sg * (1.0 + g * (1.0 - sg)))).astype(BF16)
                delta = jnp.sum(probs[u] * dps[u], axis=-1, keepdims=True)
                dss[u] = (probs[u] * (dps[u] - delta)).astype(BF16)
                ps = psinks[u] * delta
                local = u[1] * GQA + 2 * u[2]
                dsink -= jnp.where(lane == local, jnp.sum(ps[:WINDOW], axis=0, keepdims=True), 0.0)
                dsink -= jnp.where(lane == local + 1, jnp.sum(ps[WINDOW:], axis=0, keepdims=True), 0.0)
            dq2s = {u: _dot(dss[u], k2s[u[0], u[1]], NN_DIMS) for u in units}
            for u in units:
                dk_heads[u[0], u[1]] += _dot(dss[u], qsc[u][0], TN_DIMS)
                dv_heads[u[0], u[1]] += _dot(pbs[u], do2s[u], TN_DIMS)
            for u in units:
                dq_ref[rows[u], ocs[u]] = (jnp.where(low, dq2s[u][:WINDOW], dq2s[u][WINDOW:]) * ATTN_SCALE).astype(BF16)
        dk_pair = [sum(jnp.where(keep, _fold_half(dk_heads[blk, hj]), 0.0) for hj, keep in enumerate(keeps)) for blk in (0, 1)]
        dv_pair = [sum(jnp.where(keep, _fold_half(dv_heads[blk, hj]), 0.0) for hj, keep in enumerate(keeps)) for blk in (0, 1)]
        dk_ref[WINDOW:, :] = (dk_pair[1][WINDOW:] + kcarry_ref[...]).astype(BF16)
        dv_ref[WINDOW:, :] = (dv_pair[1][WINDOW:] + vcarry_ref[...]).astype(BF16)
        dk_ref[:WINDOW, :] = (dk_pair[0][WINDOW:] + dk_pair[1][:WINDOW]).astype(BF16)
        dv_ref[:WINDOW, :] = (dv_pair[0][WINDOW:] + dv_pair[1][:WINDOW]).astype(BF16)
        kcarry_ref[...] = dk_pair[0][:WINDOW]
        vcarry_ref[...] = dv_pair[0][:WINDOW]
        dsink_ref[...] += dsink
        row = pl.multiple_of(m * two, two)
        col = lambda base, width: pl.multiple_of(base + jp * width, LANES)
        corners = [(row, col(0, 1024)), (row, col(D, LANES)), (row, col(D + D // 8, LANES)), (row, col(D + D // 4, 1024))]
        _flush_windows([dq_buf, dk_buf, dv_buf, dg_buf], out_sems, dproj_ref, corners, slot, step, ns - 1)

    kb = D // LANES
    vb = kb + D // (8 * LANES)
    gb = (D + D // 4) // 512
    wide = lambda off: [pl.BlockSpec((two, 512), functools.partial(lambda o, e, jp, s: (ns - 1 - s, o + 2 * jp + e), off, e))
                        for e in (0, 1)]
    cur = lambda off: pl.BlockSpec((two, LANES), functools.partial(lambda o, jp, s: (ns - 1 - s, o + jp), off))
    prev = lambda off: pl.BlockSpec((WINDOW, LANES), functools.partial(
        lambda o, jp, s: (jnp.maximum(2 * (ns - 1 - s) - 1, 0), o + jp), off))
    wide_in = pl.BlockSpec((two, 1024), lambda jp, s: (ns - 1 - s, jp))
    return pl.pallas_call(
        body, name="attn_bwd", grid=(njp, ns),
        in_specs=[pl.BlockSpec(memory_space=pltpu.SMEM)] + wide(0) + [cur(kb), prev(kb), cur(vb), prev(vb)] + wide(gb)
        + [wide_in] + [ANY_SPEC] * n_after,
        out_specs=[ANY_SPEC, pl.BlockSpec((None, 1, LANES), lambda jp, s: (jp, 0, 0))],
        out_shape=[jax.ShapeDtypeStruct((T, 6 * D + D // 4), BF16), jax.ShapeDtypeStruct((njp, 1, LANES), F32)],
        scratch_shapes=[pltpu.VMEM((WINDOW, LANES), F32), pltpu.VMEM((WINDOW, LANES), F32),
                        pltpu.VMEM((2, two, 1024), BF16), pltpu.VMEM((2, two, LANES), BF16),
                        pltpu.VMEM((2, two, LANES), BF16), pltpu.VMEM((2, two, 1024), BF16),
                        pltpu.SemaphoreType.DMA((2, 4))],
        compiler_params=_params("parallel", "arbitrary"))(sinks, *([proj] * 8), dmix, *after)


RNN_TB = 512
RNN_HB = 8
RNN_WIDE = RNN_HB * RNN_HEAD


def _split3(v):
    a = v.astype(BF16)
    r = v - a.astype(F32)
    b = r.astype(BF16)
    c = (r - b.astype(F32)).astype(BF16)
    return a, b, c


def _tri_sum(tri, v):
    a, b, c = _split3(v)
    return _dot(tri, a, NN_DIMS) + _dot(tri, b, NN_DIMS) + _dot(tri, c, NN_DIMS)


def _lower_bound(lb_ref):
    l0, l1 = lb_ref[0:1, :], lb_ref[1:2, :]
    m = jnp.maximum(l0, l1)
    e0, e1 = jnp.exp(l0 - m), jnp.exp(l1 - m)
    return e0 / (e0 + e1)


def _rnn_gates(rq, rf, lb):
    sq = _sigmoid(rq)
    sf = _sigmoid(rf)
    f = lb + (1.0 - lb) * sf
    return sq, sf, f


def _rnn_factors(G):
    last = G[CHUNK - 1:CHUNK, :]
    mid = G[CHUNK // 2 - 1:CHUNK // 2, :]
    return G, jnp.exp(G), jnp.exp(G - mid), jnp.exp(mid - G), jnp.exp(last - G), jnp.exp(last)


def _chunk_masks():
    r = lax.broadcasted_iota(jnp.int32, (CHUNK, CHUNK), 0)
    c = lax.broadcasted_iota(jnp.int32, (CHUNK, CHUNK), 1)
    return r >= c, (r >= c).astype(BF16), (r <= c).astype(BF16)


def _rnn_specs(T, D, tb, rev):
    nt = T // tb
    base = (2 * D + D // 4) // LANES
    t_of = (lambda s: nt - 1 - s) if rev else (lambda s: s)
    assert base % RNN_HB == 0 and (D // LANES) % RNN_HB == 0
    cols = [pl.BlockSpec((tb, RNN_WIDE), functools.partial(lambda o, h, s: (t_of(s), o + h),
                                                            (base + i * (D // LANES)) // RNN_HB))
            for i in range(4)]
    return cols, t_of


def _rnn_fwd(proj, lb_logits, rnn_norm, mixed, mixed_t, D):
    T = proj.shape[0]
    tb = _pick(T, RNN_TB)
    nt, nh, cpb = T // tb, D // RNN_HEAD, tb // CHUNK
    cols, _ = _rnn_specs(T, D, tb, False)

    def body(rq_ref, rf_ref, ri_ref, rg_ref, lb_ref, gain_ref, mix_in, mixt_in,
             mix_ref, mixt_ref, o_ref, st_ref, state_ref):
        del mix_in, mixt_in
        causal, tri, _ = _chunk_masks()
        lb = _lower_bound(lb_ref)

        @pl.when(pl.program_id(1) == 0)
        def _():
            state_ref[...] = jnp.zeros_like(state_ref)

        def chunk(c, carry):
            rows = pl.ds(pl.multiple_of(c * CHUNK, CHUNK), CHUNK)
            heads = range(RNN_HB)
            lns = [slice(RNN_HEAD * hh, RNN_HEAD * (hh + 1)) for hh in heads]
            qs, ks, Gs = [], [], []
            for ln in lns:
                rq, rf = rq_ref[rows, ln], rf_ref[rows, ln]
                sq, _, f = _rnn_gates(rq, rf, lb[:, ln])
                qs.append(rq * sq)
                ks.append(1.0 - f)
                Gs.append(_tri_sum(tri, jnp.log(f)))
            atts, inters, vbs = [], [], []
            for hh, ln in enumerate(lns):
                _, eG, eq, ek, ekl, elast = _rnn_factors(Gs[hh])
                q, k = qs[hh], ks[hh]
                st = state_ref[hh]
                st_ref[hh, c] = st
                vb = ri_ref[rows, ln].astype(BF16)
                vbs.append(vb)
                atts.append(_dot((q * eq).astype(BF16), (k * ek).astype(BF16), NT_DIMS))
                inters.append(_dot((q * eG).astype(BF16), st.astype(BF16), NT_DIMS))
                state_ref[hh] = st * elast + _dot(vb, (k * ekl).astype(BF16), TN_DIMS)
            intras = [_dot(jnp.where(causal, atts[hh], 0.0).astype(BF16), vbs[hh], NN_DIMS) for hh in heads]
            for hh, ln in enumerate(lns):
                o = inters[hh] + intras[hh]
                o_ref[rows, ln] = o
                rg = rg_ref[rows, ln]
                r = lax.rsqrt(jnp.mean(o * o, axis=-1, keepdims=True) + NORM_EPS)
                out = (o * r * gain_ref[:, ln]) * (rg * _sigmoid(rg))
                mix_ref[rows, ln] = out.astype(BF16)
            return carry

        lax.fori_loop(0, cpb, chunk, 0, unroll=True)
        mixt_ref[...] = mix_ref[...].astype(F32).T.astype(BF16)

    nh //= RNN_HB
    vec2 = pl.BlockSpec((2, RNN_WIDE), lambda h, s: (0, h))
    vec1 = pl.BlockSpec((1, RNN_WIDE), lambda h, s: (0, h))
    return pl.pallas_call(
        body, name="rnn_fwd", grid=(nh, nt),
        in_specs=cols + [vec2, vec1, ANY_SPEC, ANY_SPEC],
        out_specs=[pl.BlockSpec((tb, RNN_WIDE), lambda h, s: (s, D // RNN_WIDE + h)),
                   pl.BlockSpec((RNN_WIDE, tb), lambda h, s: (D // RNN_WIDE + h, s)),
                   pl.BlockSpec((tb, RNN_WIDE), lambda h, s: (s, h)),
                   pl.BlockSpec((RNN_HB, cpb, RNN_HEAD, RNN_HEAD), lambda h, s: (h, s, 0, 0))],
        out_shape=[jax.ShapeDtypeStruct(mixed.shape, BF16), jax.ShapeDtypeStruct(mixed_t.shape, BF16),
                   jax.ShapeDtypeStruct((T, D), F32),
                   jax.ShapeDtypeStruct((nh * RNN_HB, T // CHUNK, RNN_HEAD, RNN_HEAD), F32)],
        scratch_shapes=[pltpu.VMEM((RNN_HB, RNN_HEAD, RNN_HEAD), F32)],
        input_output_aliases={6: 0, 7: 1},
        compiler_params=_params("parallel", "arbitrary"))(proj, proj, proj, proj, lb_logits, rnn_norm, mixed, mixed_t)


def _rnn_bwd(proj, lb_logits, rnn_norm, o_raw, states, dmix, dproj, D):
    T = proj.shape[0]
    tb = _pick(T, RNN_TB)
    nt, nh, cpb = T // tb, D // RNN_HEAD, tb // CHUNK
    cols, t_of = _rnn_specs(T, D, tb, True)

    def body(rq_ref, rf_ref, ri_ref, rg_ref, lb_ref, gain_ref, o_ref, st_ref, dmix_ref, dproj_in,
             dproj_ref, dlb_ref, dgain_ref, dstate_ref, out_buf, out_sems):
        del dproj_in
        step = pl.program_id(1)
        slot = step % 2
        drq_ref, drf_ref, dri_ref, drg_ref = [out_buf.at[i, slot] for i in range(4)]
        causal, tri, tri_t = _chunk_masks()
        lb = _lower_bound(lb_ref)
        gain = gain_ref[...]

        @pl.when(pl.program_id(1) == 0)
        def _():
            dstate_ref[...] = jnp.zeros_like(dstate_ref)
            dlb_ref[...] = jnp.zeros_like(dlb_ref)
            dgain_ref[...] = jnp.zeros_like(dgain_ref)

        def chunk(i, carry):
            c = cpb - 1 - i
            rows = pl.ds(pl.multiple_of(c * CHUNK, CHUNK), CHUNK)
            heads = range(RNN_HB)
            lns = [slice(RNN_HEAD * hh, RNN_HEAD * (hh + 1)) for hh in heads]
            last_row = lax.broadcasted_iota(jnp.int32, (CHUNK, 1), 0) == CHUNK - 1
            A = []
            for ln in lns:
                rq, rf, rg = rq_ref[rows, ln], rf_ref[rows, ln], rg_ref[rows, ln]
                o, dgated = o_ref[rows, ln], dmix_ref[rows, ln]
                gainh = gain[:, ln]
                sgt = _sigmoid(rg)
                r = lax.rsqrt(jnp.mean(o * o, axis=-1, keepdims=True) + NORM_EPS)
                on = o * r
                drg_ref[rows, ln] = (dgated * (on * gainh) * (sgt * (1.0 + rg * (1.0 - sgt)))).astype(BF16)
                d_on = dgated * (rg * sgt)
                dgain_ref[:, ln] += jnp.sum(d_on * on, axis=0, keepdims=True)
                dob = _norm_bwd(d_on * gainh, on, r).astype(BF16)
                sq, sf, f = _rnn_gates(rq, rf, lb[:, ln])
                A.append(dict(rq=rq, sq=sq, sf=sf, f=f, dob=dob, G=_tri_sum(tri, jnp.log(f))))
            for hh, ln in enumerate(lns):
                a = A[hh]
                _, eG, eq, ek, ekl, elast = _rnn_factors(a.pop("G"))
                q, k = a["rq"] * a["sq"], 1.0 - a["f"]
                st, dst = st_ref[hh, c], dstate_ref[hh]
                qg, kl = q * eG, k * ekl
                qmb, kmb = (q * eq).astype(BF16), (k * ek).astype(BF16)
                dob, vb, dstb = a["dob"], ri_ref[rows, ln].astype(BF16), dst.astype(BF16)
                a.update(eG=eG, eq=eq, ek=ek, ekl=ekl, qg=qg, kl=kl, qmb=qmb, kmb=kmb,
                         att=_dot(qmb, kmb, NT_DIMS), datt=_dot(dob, vb, NT_DIMS),
                         dqg=_dot(dob, st.astype(BF16), NN_DIMS), dkl=_dot(vb, dstb, NN_DIMS),
                         dri=_dot(kl.astype(BF16), dstb, NT_DIMS),
                         dlast=jnp.sum(dst * st, axis=0, keepdims=True) * elast)
                dstate_ref[hh] = dst * elast + _dot(dob, qg.astype(BF16), TN_DIMS)
            for hh, ln in enumerate(lns):
                a = A[hh]
                att = jnp.where(causal, a.pop("att"), 0.0).astype(BF16)
                datt = jnp.where(causal, a.pop("datt"), 0.0).astype(BF16)
                dqm = _dot(datt, a["kmb"], NN_DIMS)
                dkm = _dot(datt, a["qmb"], TN_DIMS)
                dri_ref[rows, ln] = (_dot(att, a["dob"], TN_DIMS) + a.pop("dri")).astype(BF16)
                dqg, dkl, kl = a.pop("dqg"), a.pop("dkl"), a.pop("kl")
                a["dq"] = dqg * a.pop("eG") + dqm * a.pop("eq")
                a["dk"] = dkm * a.pop("ek") + dkl * a.pop("ekl")
                dG = dqg * a.pop("qg") + dqm * a.pop("qmb").astype(F32) - dkm * a.pop("kmb").astype(F32) - dkl * kl
                dlast = jnp.sum(dkl * kl, axis=0, keepdims=True) + a.pop("dlast")
                a["dg"] = _tri_sum(tri_t, dG + jnp.where(last_row, dlast, 0.0))
            for hh, ln in enumerate(lns):
                a = A[hh]
                rq, sq, sf = a["rq"], a["sq"], a["sf"]
                df = a["dg"] / a["f"] - a["dk"]
                drq_ref[rows, ln] = (a["dq"] * (sq * (1.0 + rq * (1.0 - sq)))).astype(BF16)
                drf_ref[rows, ln] = (df * (1.0 - lb[:, ln]) * (sf * (1.0 - sf))).astype(BF16)
                dlb_ref[:, ln] += jnp.sum(df * (1.0 - sf), axis=0, keepdims=True)
            return carry

        lax.fori_loop(0, cpb, chunk, 0, unroll=True)
        row = pl.multiple_of(t_of(step) * tb, tb)
        col0 = 2 * D + D // 4 + pl.program_id(0) * RNN_WIDE
        corners = [(row, pl.multiple_of(col0 + i * D, LANES)) for i in range(4)]
        _flush_windows([out_buf.at[i] for i in range(4)], out_sems, dproj_ref, corners, slot, step, nt - 1)

    nh //= RNN_HB
    vec2 = pl.BlockSpec((2, RNN_WIDE), lambda h, s: (0, h))
    vec1 = pl.BlockSpec((1, RNN_WIDE), lambda h, s: (0, h))
    blk = pl.BlockSpec((tb, RNN_WIDE), lambda h, s: (t_of(s), h))
    return pl.pallas_call(
        body, name="rnn_bwd", grid=(nh, nt),
        in_specs=cols + [vec2, vec1, blk,
                         pl.BlockSpec((RNN_HB, cpb, RNN_HEAD, RNN_HEAD), lambda h, s: (h, t_of(s), 0, 0)),
                         pl.BlockSpec((tb, RNN_WIDE), lambda h, s: (t_of(s), D // RNN_WIDE + h)), ANY_SPEC],
        out_specs=[ANY_SPEC, vec1, vec1],
        out_shape=[jax.ShapeDtypeStruct(dproj.shape, BF16)] + [jax.ShapeDtypeStruct((1, D), F32)] * 2,
        scratch_shapes=[pltpu.VMEM((RNN_HB, RNN_HEAD, RNN_HEAD), F32), pltpu.VMEM((4, 2, tb, RNN_WIDE), BF16),
                        pltpu.SemaphoreType.DMA((2, 4))],
        input_output_aliases={9: 0},
        compiler_params=_params("parallel", "arbitrary"))(proj, proj, proj, proj, lb_logits, rnn_norm, o_raw, states, dmix, dproj)


def _adamw(w, g, m, v):
    m = ADAM_B1 * m + (1.0 - ADAM_B1) * g
    v = ADAM_B2 * v + (1.0 - ADAM_B2) * (g * g)
    m_hat = m / (1.0 - ADAM_B1 ** ADAM_STEP)
    v_hat = v / (1.0 - ADAM_B2 ** ADAM_STEP)
    delta = -ADAM_LR * (m_hat / (jnp.sqrt(v_hat) + ADAM_EPS) + ADAM_WD * w)
    return delta, m, v


def _adamw_big(w, m, v, parts, tr, name):
    R, C = w.shape
    n_parts = len(parts)

    def body(w_ref, m_ref, v_ref, *rest):
        part_refs = rest[:n_parts]
        g_ref, d_ref, nm_ref, nv_ref = rest[n_parts:]
        g = part_refs[0][...].astype(F32)
        for p_ref in part_refs[1:]:
            g = g + p_ref[...].astype(F32)
        delta, nm, nv = _adamw(w_ref[...], g, m_ref[...], v_ref[...])
        g_ref[...] = g
        d_ref[...] = delta
        nm_ref[...] = nm
        nv_ref[...] = nv

    blk = pl.BlockSpec((tr, C), lambda i: (i, 0))
    part_specs = [pl.BlockSpec((None, tr, C), functools.partial(lambda s, i: (s, i, 0), slot)) for _, slot in parts]
    return pl.pallas_call(
        body, name=name, grid=(R // tr,), in_specs=[blk, blk, blk] + part_specs,
        out_specs=[blk] * 4, out_shape=[jax.ShapeDtypeStruct((R, C), F32)] * 4,
        compiler_params=_params("parallel"))(w, m, v, *[a for a, _ in parts])


def _pair_sum(pa, qa):
    n, R, C = qa.shape
    tr = _pick(R, 256)

    def body(p_ref, q_ref, r_ref):
        r_ref[...] = (p_ref[...].astype(F32) + q_ref[...].astype(F32)).astype(BF16)

    return pl.pallas_call(
        body, name="pair_sum", grid=(n, R // tr),
        in_specs=[pl.BlockSpec((None, tr, C), lambda j, i: (2 * j, i, 0)), pl.BlockSpec((None, tr, C), lambda j, i: (j, i, 0))],
        out_specs=pl.BlockSpec((None, tr, C), lambda j, i: (j, i, 0)),
        out_shape=jax.ShapeDtypeStruct((n, R, C), BF16), compiler_params=_params("parallel", "parallel"))(pa, qa)


def _adamw_small(total, sinks, lb_logits, rnn_norm, pre_norm, post_norm, moments, D):
    params = [sinks, lb_logits, rnn_norm, pre_norm, post_norm]

    def body(tot_ref, *refs):
        p_refs, m_refs, v_refs = refs[0:5], refs[5:10], refs[10:15]
        loss_ref, outs = refs[15], refs[16:]
        tot = tot_ref[...]
        l0, l1 = p_refs[1][0:1, :], p_refs[1][1:2, :]
        mx = jnp.maximum(l0, l1)
        e0, e1 = jnp.exp(l0 - mx), jnp.exp(l1 - mx)
        p0, p1 = e0 / (e0 + e1), e1 / (e0 + e1)
        dlb = tot[0:1, :]
        grads = [tot[5:6, 0:LANES], jnp.concatenate([dlb * p0 * (1.0 - p0), -dlb * p0 * p1], axis=0),
                 tot[1:2, :], tot[2:3, :], tot[3:4, :]]
        loss_ref[...] = 0.5 / D * jnp.sum(tot[4:5, :], axis=-1, keepdims=True)
        for i, g in enumerate(grads):
            delta, nm, nv = _adamw(p_refs[i][...], g, m_refs[i][...], v_refs[i][...])
            outs[4 * i][...] = g
            outs[4 * i + 1][...] = delta
            outs[4 * i + 2][...] = nm
            outs[4 * i + 3][...] = nv

    out_shape = [jax.ShapeDtypeStruct((1, 1), F32)]
    for p in params:
        out_shape += [jax.ShapeDtypeStruct(p.shape, F32)] * 4
    return pl.pallas_call(body, name="adamw_small", out_shape=out_shape)(total, *params, *moments)


SIBLING = 1


def _me():
    return lax.axis_index("x"), lax.axis_index("y"), lax.axis_index("c")


def _flat(px, py, pc):
    return 4 * px + 2 * py + pc


def _role_peer(role, x, y, c):
    if not isinstance(role, int):
        j = (role - 2) // 2
        px = jnp.where((role >= 2) & (j != 1), 1 - x, x)
        py = jnp.where((role >= 2) & (j != 0), 1 - y, y)
        return (px, py, jnp.where(role % 2 == 1, 1 - c, c))
    if role < 2:
        return (x, y, (1 - c) if role else c)
    j, other = (role - 2) // 2, (role - 2) % 2
    px = (1 - x) if j in (0, 2) else x
    py = (1 - y) if j in (1, 2) else y
    return (px, py, (1 - c) if other else c)


def _wave1(x, y, c):
    return jnp.where(c == 1, 2, 4)


def _wave2(x, y, c):
    return jnp.where(c == 1, 4, 2)


def _role_ids():
    x, y, c = _me()
    w1, w2 = _wave1(x, y, c), _wave2(x, y, c)
    order = [0, 1, w1, w2 + 1, w2, w1 + 1, 6, 7]
    return jnp.stack([_flat(*_role_peer(r, x, y, c)) for r in range(N_DEV)] + order).astype(jnp.int32)


def _comm_call(name, bufs, waits=(), starts=(), after=()):
    n_buf, n_wait, n_start, n_after = len(bufs), len(waits), len(starts), len(after)

    def body(*refs):
        buf_refs = refs[:n_buf]
        sem_refs = refs[n_buf:n_buf + 2 * n_wait]
        outs = refs[n_buf + 2 * n_wait + n_after:]
        new_sems, token = outs[:2 * n_start], outs[-1]
        x, y, c = _me()

        def val(v):
            return v(x, y, c) if callable(v) else v

        def block(ref, slot, rows):
            if rows is None:
                return ref.at[val(slot)]
            first = rows[0] if isinstance(rows[0], int) else pl.multiple_of(val(rows[0]), 16)
            return ref.at[val(slot), pl.ds(first, rows[1])]

        for w, (_, kind, like, *rows) in enumerate(waits):
            shape_ref = block(buf_refs[like], 0, (0, rows[0][1]) if rows else None)
            cp = pltpu.make_async_remote_copy(
                src_ref=shape_ref, dst_ref=shape_ref, send_sem=sem_refs[2 * w], recv_sem=sem_refs[2 * w + 1],
                device_id=(x, y, c), device_id_type=MESH)
            if kind == "send":
                cp.wait_send()
            else:
                cp.wait_recv()
        for s, (sb, ss, db, ds, role, *rows) in enumerate(starts):
            rows = rows[0] if rows else None
            pltpu.make_async_remote_copy(
                src_ref=block(buf_refs[sb], ss, rows), dst_ref=block(buf_refs[db], ds, rows), send_sem=new_sems[2 * s],
                recv_sem=new_sems[2 * s + 1], device_id=_role_peer(val(role), x, y, c), device_id_type=MESH).start()
        token[...] = jnp.zeros_like(token)

    sems = [s for flight, *_ in waits for s in flight]
    out = pl.pallas_call(
        body, name=name,
        out_shape=tuple([pltpu.SemaphoreType.DMA(())] * (2 * n_start) + [pltpu.HBM(b.shape, b.dtype) for b in bufs]
                        + [jax.ShapeDtypeStruct((8, LANES), F32)]),
        in_specs=tuple([HBM_SPEC] * n_buf + [SEM_SPEC] * (2 * n_wait) + [ANY_SPEC] * n_after),
        out_specs=tuple([SEM_SPEC] * (2 * n_start) + [HBM_SPEC] * n_buf + [pl.BlockSpec(memory_space=pltpu.VMEM)]),
        input_output_aliases={i: 2 * n_start + i for i in range(n_buf)},
        compiler_params=pltpu.CompilerParams(has_side_effects=EFFECT),
    )(*[pltpu.with_memory_space_constraint(b, pltpu.HBM) for b in bufs], *sems, *after)
    flights = [(out[2 * s], out[2 * s + 1]) for s in range(n_start)]
    return list(out[2 * n_start:2 * n_start + n_buf]), flights, out[-1]


def _landing(shape):
    return lax.empty(shape, BF16)


def _both(flights, like):
    return [(f, kind, like) for f in flights for kind in ("send", "recv")]


def _allreduce_small(rows, D):
    n_rows = len(rows)

    def body(*refs):
        in_refs, out_ref, all_ref = refs[:n_rows], refs[n_rows], refs[n_rows + 1]
        send_sems, recv_sems = refs[n_rows + 2], refs[n_rows + 3]
        x, y, c = _me()
        mine = all_ref.at[_flat(x, y, c)]
        mine[...] = jnp.zeros((8, D), F32)
        for i, r in enumerate(in_refs):
            mine[i:i + 1, :] = r[...]
        copies = []
        for k in range(N_DEV - 1):
            copies.append(pltpu.make_async_remote_copy(
                src_ref=mine, dst_ref=mine, send_sem=send_sems.at[k], recv_sem=recv_sems.at[k],
                device_id=_role_peer(k + 1, x, y, c), device_id_type=MESH))
        for cp in copies:
            cp.start()
        for cp in copies:
            cp.wait_recv()
        for cp in copies:
            cp.wait_send()
        tot = all_ref[0]
        for d in range(1, N_DEV):
            tot = tot + all_ref[d]
        out_ref[...] = tot

    vm = pl.BlockSpec(memory_space=pltpu.VMEM)
    return pl.pallas_call(
        body, name="allreduce_small", in_specs=[vm] * n_rows, out_specs=vm,
        out_shape=jax.ShapeDtypeStruct((8, D), F32),
        scratch_shapes=[pltpu.VMEM((N_DEV, 8, D), F32), pltpu.SemaphoreType.DMA((7,)), pltpu.SemaphoreType.DMA((7,))],
    )(*rows)


def _step(ids, x, target, g_in, g_out, sinks, lb_logits, rnn_norm, pre_norm, post_norm, dist):
    T, D = x.shape
    n_roles, _, wd = g_in.shape
    ro = g_out.shape[1]
    assert n_roles % 2 == 0 and (not dist or n_roles == N_DEV)

    half = D // 2
    up = lambda f: (lambda x, y, c: f(x, y, c) + 1)
    rows_of = lambda f: (lambda x, y, c: jnp.where(f(x, y, c) == 2, 0, half))
    if dist:
        (g_in,), fl_in, _ = _comm_call(
            "gather_in_start", [g_in], starts=[(0, 0, 0, SIBLING, SIBLING), (0, 0, 0, _wave1, _wave1)])
    h, h_t = _prenorm(x, pre_norm)
    proj = _proj_in(ids, h, g_in, 0, 1, None, "proj_in_0")
    if dist:
        (g_in,), _, _ = _comm_call("gather_in_sibling", [g_in], waits=[(fl_in[0], "recv", 0)], after=(proj,))
    proj = _proj_in(ids, h, g_in, 1, 1, proj, "proj_in_1")
    if not dist:
        for p in range(2, n_roles, 2):
            proj = _proj_in(ids, h, g_in, p, 2, proj, f"proj_in_{p}")
    else:
        (g_in,), fl_a, _ = _comm_call(
            "gather_in_wave_1", [g_in], waits=[(fl_in[1], "recv", 0)],
            starts=[(0, 0, 0, _wave2, _wave2), (0, _wave1, 0, up(_wave1), SIBLING),
                    (0, _wave1, 0, 6, _wave2, (rows_of(_wave1), half))], after=(proj,))
        proj = _proj_in(ids, h, g_in, 2, 1, proj, "proj_in_2")
        (g_in,), _, _ = _comm_call("gather_in_passed_1", [g_in], waits=[(fl_a[1], "recv", 0)], after=(proj,))
        proj = _proj_in(ids, h, g_in, 3, 1, proj, "proj_in_3")
        (g_in,), fl_b, _ = _comm_call(
            "gather_in_wave_2", [g_in], waits=[(fl_a[0], "recv", 0)],
            starts=[(0, _wave2, 0, up(_wave2), SIBLING), (0, _wave2, 0, 6, _wave1, (rows_of(_wave2), half))],
            after=(proj,))
        proj = _proj_in(ids, h, g_in, 4, 1, proj, "proj_in_4")
        (g_in,), _, _ = _comm_call("gather_in_passed_2", [g_in], waits=[(fl_b[0], "recv", 0)], after=(proj,))
        proj = _proj_in(ids, h, g_in, 5, 1, proj, "proj_in_5")
        (g_in,), fl_c, _ = _comm_call(
            "gather_in_diagonal", [g_in], waits=[(fl_a[2], "recv", 0, (0, half)), (fl_b[1], "recv", 0, (0, half))],
            starts=[(0, 6, 0, 7, SIBLING)], after=(proj,))
        proj = _proj_in(ids, h, g_in, 6, 1, proj, "proj_in_6")
        (g_in, g_out), fl_out, token = _comm_call(
            "gather_out_start", [g_in, g_out], waits=[(fl_c[0], "recv", 0)],
            starts=[(1, 0, 1, SIBLING, SIBLING)] + [(1, 0, 1, 2 + 2 * i, 2 + 2 * i) for i in range(3)], after=(proj,))
        proj = _proj_in(ids, h, g_in, 7, 1, proj, "proj_in_7", (token,))
        send_waits = [(f, "send", 0) for f in fl_in + fl_a[:2] + fl_b[:1] + fl_c] \
            + [(f, "send", 0, (0, half)) for f in (fl_a[2], fl_b[1])]
    mixed, mixed_t = _attn_fwd(proj, sinks, D)
    if dist:
        (g_in, g_out), fl_out_fwd, _ = _comm_call(
            "gather_out_pass", [g_in, g_out],
            waits=send_waits + [(fl_out[1 + i], "recv", 1) for i in range(3)],
            starts=[(1, 2 + 2 * i, 1, 3 + 2 * i, SIBLING) for i in range(3)], after=(mixed,))
    mixed, mixed_t, o_raw, states = _rnn_fwd(proj, lb_logits, rnn_norm, mixed, mixed_t, D)
    if dist:
        (g_out,), _, _ = _comm_call(
            "gather_out_done", [g_out],
            waits=[(fl_out[0], "recv", 0)] + [(f, "recv", 0) for f in fl_out_fwd]
            + [(f, "send", 0) for f in fl_out + fl_out_fwd], after=(states,))
    y = _proj_out(ids, mixed, g_out)
    dy, dout, g_post, sq_err = _loss_head(y, x, target, post_norm)

    dmix = _dmixed(ids, dy, g_out)
    p_out, after = _dw_out(ids, mixed_t, dy, n_roles), ()
    if dist:
        (p_out, l_out), fl_so, token = _comm_call(
            "scatter_out_start", [p_out, _landing((n_roles - 1, ro, D))],
            starts=[(0, r, 1, r - 1, r) for r in range(1, n_roles)])
        after = (token,)
    dproj, d_sink = _attn_bwd(proj, sinks, dmix, D, after)
    dproj, d_lb, g_rnn = _rnn_bwd(proj, lb_logits, rnn_norm, o_raw, states, dmix, dproj, D)

    p_far = _dw_in(ids, h_t, dproj, n_roles, 2, n_roles - 2, "dw_in_far")
    if dist:
        (p_far, q_far), fl_pair, token = _comm_call(
            "scatter_in_pair_start", [p_far, _landing((3, D, wd))],
            starts=[(0, 1 + 2 * j, 1, j, SIBLING) for j in range(3)])
        p_near = _dw_in(ids, h_t, dproj, n_roles, 0, 2, "dw_in_near", after=(token,))
        (p_far, q_far), _, _ = _comm_call("scatter_in_pair_wait", [p_far, q_far], waits=_both(fl_pair, 1), after=(p_near,))
        chip_sum = _pair_sum(p_far, q_far)
        (chip_sum, z_far), fl_chip, token = _comm_call(
            "scatter_in_chip_start", [chip_sum, _landing((3, D, wd))],
            starts=[(0, j, 1, j, 2 + 2 * j) for j in range(3)])
        (p_near, q_near), fl_sib, token = _comm_call(
            "scatter_in_sibling_start", [p_near, _landing((1, D, wd))], starts=[(0, 1, 1, 0, SIBLING)], after=(token,))
        after = (token,)
    else:
        p_near, after = _dw_in(ids, h_t, dproj, n_roles, 0, 2, "dw_in_near"), ()
    dh = _dh(ids, dproj, g_in, after)
    grad_x, g_pre = _prenorm_bwd(x, dh, dout, pre_norm)
    n_q = D // ATTN_HEAD
    sink_row = jnp.pad(d_sink[:, 0, :2 * GQA].reshape(1, n_q), ((0, 0), (0, D - n_q)))
    rows = [d_lb, g_rnn, g_pre, g_post, sq_err, sink_row]
    if not dist:
        return grad_x, [(p_near, r) for r in range(2)] + [(p_far, r) for r in range(n_roles - 2)], \
            [(p_out, r) for r in range(n_roles)], rows

    (p_out, l_out), _, _ = _comm_call("scatter_out_wait", [p_out, l_out], waits=_both(fl_so, 1), after=(grad_x,))
    (chip_sum, z_far, p_near, q_near), _, _ = _comm_call(
        "scatter_in_wait", [chip_sum, z_far, p_near, q_near], waits=_both(fl_chip, 1) + _both(fl_sib, 3), after=(p_out,))
    parts_in = [(p_near, 0), (q_near, 0)] + [(z_far, j) for j in range(3)]
    parts_out = [(p_out, 0)] + [(l_out, k) for k in range(n_roles - 1)]
    return grad_x, parts_in, parts_out, rows


def kernel(x, w_in, attn_sinks, lb_logits, rnn_norm, w_out, pre_norm, post_norm, loss_target, m_w_in, m_attn_sinks, m_lb_logits, m_rnn_norm, m_w_out, m_pre_norm, m_post_norm, v_w_in, v_attn_sinks, v_lb_logits, v_rnn_norm, v_w_out, v_pre_norm, v_post_norm):
    _, T, D = x.shape
    ro = w_out.shape[1]
    n_q = attn_sinks.shape[1]
    assert lb_logits.shape[0] == 2 and n_q == D // ATTN_HEAD and n_q <= LANES

    grad_x, parts_in, parts_out, small_rows = _step(
        _role_ids(), x[0], loss_target[0], _cast_slot0(w_in[0], N_DEV, "cast_w_in"),
        _cast_slot0(w_out[0], N_DEV, "cast_w_out"), attn_sinks, lb_logits, rnn_norm, pre_norm, post_norm, True)

    g_wo, d_wo, nm_wo, nv_wo = _adamw_big(w_out[0], m_w_out[0], v_w_out[0], parts_out, _pick(ro, 64), "adamw_w_out")
    g_wi, d_wi, nm_wi, nv_wi = _adamw_big(w_in[0], m_w_in[0], v_w_in[0], parts_in, _pick(D, 128), "adamw_w_in")

    total = _allreduce_small(small_rows, D)
    pad = lambda a: jnp.pad(a, ((0, 0), (0, LANES - n_q)))
    moments = [pad(m_attn_sinks), m_lb_logits, m_rnn_norm, m_pre_norm, m_post_norm,
               pad(v_attn_sinks), v_lb_logits, v_rnn_norm, v_pre_norm, v_post_norm]
    res = _adamw_small(total, pad(attn_sinks), lb_logits, rnn_norm, pre_norm, post_norm, moments, D)
    loss = res[0][0, 0]
    small = [[res[1 + 4 * i + j] for i in range(5)] for j in range(4)]
    for j in range(4):
        small[j][0] = small[j][0][:, :n_q]

    def assemble(j, wi, wo):
        s = small[j]
        return [wi[None], s[0], s[1], s[2], wo[None], s[3], s[4]]

    return (loss, grad_x[None], *assemble(0, g_wi, g_wo), *assemble(1, d_wi, d_wo),
            *assemble(2, nm_wi, nm_wo), *assemble(3, nv_wi, nv_wo))
```
